```python
import math
import jax, jax.numpy as jnp
from jax import lax
import numpy as np

D_MODEL = 1024
BATCH = 8
SEQ = 2048
DEPTH = 2
DEC_BATCH = 128
DEC_SEQ = 8
PAST_LEN = 16384
PAGE_SIZE = 128

D_MIX = D_MODEL
D_SSM = D_MIX // 2
D_RWKV = D_MIX - D_SSM
SSM_GROUP = 16
N_SSM_GROUPS = D_SSM // SSM_GROUP
SSM_STATE = 64
RWKV_HEAD = 64
N_RWKV_HEADS = D_RWKV // RWKV_HEAD
DECAY_LORA = 64
AAA_LORA = 64
GATE_LORA = 128
N_SHIFT = 3 * D_RWKV + DECAY_LORA + AAA_LORA + GATE_LORA
N_IN = D_SSM + N_SHIFT
D_FF = 2816
CONV_W = 3
D_PLE = 256
RMS_EPS = 1e-6
GN_EPS = 64e-5
L2_EPS = 1e-12

kernel_name = 's5_rwkv7_hybrid_decode_step'


def _rmsnorm(x, g):
    xf = x.astype(jnp.float32)
    y = xf * lax.rsqrt(jnp.mean(xf * xf, axis=-1, keepdims=True) + RMS_EPS)
    return y * g.astype(jnp.float32)


def _s5(u, h0_re, h0_im, lam_re, lam_im, log_dt, b_re, b_im, c_re, c_im, d_skip, w_glu, b_glu):
    bsz, seq_len, _ = u.shape
    ug = u.reshape(bsz, seq_len, N_SSM_GROUPS, SSM_GROUP)
    dt = jnp.exp(log_dt)[:, None]
    mag = jnp.exp(lam_re * dt)
    ab_re = mag * jnp.cos(lam_im * dt)
    ab_im = mag * jnp.sin(lam_im * dt)
    den = lam_re * lam_re + lam_im * lam_im
    nr = ab_re - 1.0
    f_re = (nr * lam_re + ab_im * lam_im) / den
    f_im = (ab_im * lam_re - nr * lam_im) / den
    bb_re = f_re[..., None] * b_re - f_im[..., None] * b_im
    bb_im = f_re[..., None] * b_im + f_im[..., None] * b_re
    bu_re = jnp.einsum('blgc,gpc->blgp', ug, bb_re)
    bu_im = jnp.einsum('blgc,gpc->blgp', ug, bb_im)
    bu_re = bu_re.at[:, 0].add(ab_re * h0_re - ab_im * h0_im)
    bu_im = bu_im.at[:, 0].add(ab_re * h0_im + ab_im * h0_re)
    a_re = jnp.broadcast_to(ab_re, bu_re.shape)
    a_im = jnp.broadcast_to(ab_im, bu_im.shape)

    def combine(e1, e2):
        a1r, a1i, b1r, b1i = e1
        a2r, a2i, b2r, b2i = e2
        return (a1r * a2r - a1i * a2i, a1r * a2i + a1i * a2r,
                a2r * b1r - a2i * b1i + b2r, a2r * b1i + a2i * b1r + b2i)

    _, _, h_re, h_im = lax.associative_scan(combine, (a_re, a_im, bu_re, bu_im), axis=1)
    y = jnp.einsum('blgp,gcp->blgc', h_re, c_re) - jnp.einsum('blgp,gcp->blgc', h_im, c_im)
    y = y.reshape(bsz, seq_len, D_SSM) + d_skip * u
    y = jax.nn.gelu(y)
    y = y * jax.nn.sigmoid(y @ w_glu + b_glu)
    return y, h_re[:, -1], h_im[:, -1]


def _rwkv7(z, s0, w0, w2, a0, a2, g2, k_k, k_a, r_k, lnx_g, lnx_b):
    bsz, seq_len, _ = z.shape
    c1, c2, c3 = D_RWKV, 2 * D_RWKV, 3 * D_RWKV
    c4, c5 = c3 + DECAY_LORA, c3 + DECAY_LORA + AAA_LORA
    r, k, v = z[..., :c1], z[..., c1:c2], z[..., c2:c3]
    xw, xa, xg = z[..., c3:c4], z[..., c4:c5], z[..., c5:]
    w = -jax.nn.softplus(-(w0 + jnp.tanh(xw) @ w2)) - 0.5
    decay = jnp.exp(-jnp.exp(w))
    a = jax.nn.sigmoid(a0 + xa @ a2)
    g = jax.nn.sigmoid(xg) @ g2
    hs = (bsz, seq_len, N_RWKV_HEADS, RWKV_HEAD)
    kk = (k * k_k).reshape(hs)
    kk = kk / jnp.maximum(jnp.sqrt(jnp.sum(kk * kk, axis=-1, keepdims=True)), L2_EPS)
    k = k * (1.0 + (a - 1.0) * k_a)
    r4, k4, v4 = r.reshape(hs), k.reshape(hs), v.reshape(hs)
    w4, a4 = decay.reshape(hs), a.reshape(hs)

    def step(S, inp):
        r_t, w_t, k_t, v_t, kk_t, a_t = inp
        sa = jnp.einsum('bhvk,bhk->bhv', S, -kk_t)
        S = (S * w_t[:, :, None, :] + sa[..., None] * (kk_t * a_t)[:, :, None, :]
             + v_t[..., None] * k_t[:, :, None, :])
        return S, jnp.einsum('bhvk,bhk->bhv', S, r_t)

    tm = lambda t: jnp.moveaxis(t, 1, 0)
    s_last, y = lax.scan(step, s0, (tm(r4), tm(w4), tm(k4), tm(v4), tm(kk), tm(a4)))
    y = jnp.moveaxis(y, 0, 1)
    mu = jnp.mean(y, axis=-1, keepdims=True)
    var = jnp.mean(jnp.square(y - mu), axis=-1, keepdims=True)
    y = ((y - mu) * lax.rsqrt(var + GN_EPS)).reshape(bsz, seq_len, D_RWKV) * lnx_g + lnx_b
    bonus = jnp.sum(r4 * k4 * r_k, axis=-1, keepdims=True) * v4
    y = (y + bonus.reshape(bsz, seq_len, D_RWKV)) * g
    return y, s_last


def _layer(h, pe, st, w):
    ssm_re0, ssm_im0, rwkv0, shift0, conv0 = st
    (norm1_g, w_in, shift_mu, lam_re, lam_im, log_dt, b_re, b_im, c_re, c_im, d_skip,
     w_glu, b_glu, w0, w2, a0, a2, g2, k_k, k_a, r_k, lnx_g, lnx_b, w_out,
     norm2_g, w_up, conv_w, conv_b, w_down, w_ple, w_pg) = w
    f32 = jnp.float32
    seq_len = h.shape[1]
    x = _rmsnorm(h, norm1_g)
    proj = x @ w_in.astype(f32)
    u = proj[..., :D_SSM]
    zr = proj[..., D_SSM:]
    prev = jnp.concatenate([shift0.astype(f32)[:, None], zr[:, :-1]], axis=1)
    z = zr + (prev - zr) * shift_mu
    y_a, hre, him = _s5(u, ssm_re0.astype(f32), ssm_im0.astype(f32), lam_re, lam_im, log_dt,
                        b_re, b_im, c_re, c_im, d_skip, w_glu, b_glu)
    y_b, s_last = _rwkv7(z, rwkv0.astype(f32), w0, w2, a0, a2, g2, k_k, k_a, r_k, lnx_g, lnx_b)
    h = h + jnp.concatenate([y_a, y_b], axis=-1) @ w_out
    x2 = _rmsnorm(h, norm2_g)
    up = x2 @ w_up
    val, gate = up[..., :D_FF], up[..., D_FF:]
    gate_full = jnp.concatenate([conv0.astype(f32), gate], axis=1)
    conv = conv_b + sum(gate_full[:, j:j + seq_len] * conv_w[j] for j in range(CONV_W))
    h = h + (jax.nn.silu(conv) * val) @ w_down
    h = h + (pe.astype(f32) @ w_ple) * jax.nn.sigmoid(h @ w_pg)
    new_st = (hre, him, s_last, zr[:, -1], gate_full[:, -(CONV_W - 1):])
    return h, new_st


def setup_inputs(seed: int = 0) -> dict:
    key = jax.random.key(seed)
    ks = iter(jax.random.split(key, 64))
    f32 = jnp.float32
    nrm = lambda shape, s: jax.random.normal(next(ks), shape, f32) * s
    uni = lambda shape, lo, hi: jax.random.uniform(next(ks), shape, f32, lo, hi)
    G, P, H, N = N_SSM_GROUPS, SSM_STATE, N_RWKV_HEADS, RWKV_HEAD
    lam_im_init = jnp.broadcast_to(math.pi * jnp.arange(P, dtype=f32), (DEPTH, G, P))
    return {
        'x_prompt': nrm((BATCH, SEQ, D_MODEL), 1.0),
        'x_sample': nrm((DEC_BATCH, DEC_SEQ, D_MODEL), 1.0),
        'p_prompt': nrm((DEPTH, BATCH, SEQ, D_PLE), 1.0),
        'p_sample': nrm((DEPTH, DEC_BATCH, DEC_SEQ, D_PLE), 1.0),
        'state_ssm_re': nrm((DEPTH, DEC_BATCH, G, P), 0.1),
        'state_ssm_im': nrm((DEPTH, DEC_BATCH, G, P), 0.1),
        'state_rwkv': nrm((DEPTH, DEC_BATCH, H, N, N), 0.1),
        'state_shift': nrm((DEPTH, DEC_BATCH, N_SHIFT), 1.0),
        'state_conv': nrm((DEPTH, DEC_BATCH, CONV_W - 1, D_FF), 1.0),
        'norm1_g': 1.0 + nrm((DEPTH, D_MODEL), 0.02),
        'w_in': nrm((DEPTH, D_MODEL, N_IN), D_MODEL ** -0.5),
        'shift_mu': uni((DEPTH, N_SHIFT), 0.0, 1.0),
        'ssm_lam_re': -0.5 + nrm((DEPTH, G, P), 1e-3),
        'ssm_lam_im': lam_im_init + nrm((DEPTH, G, P), 1e-3),
        'ssm_log_dt': uni((DEPTH, G), math.log(1e-3), math.log(1e-1)),
        'ssm_b_re': nrm((DEPTH, G, P, SSM_GROUP), (2 * SSM_GROUP) ** -0.5),
        'ssm_b_im': nrm((DEPTH, G, P, SSM_GROUP), (2 * SSM_GROUP) ** -0.5),
        'ssm_c_re': nrm((DEPTH, G, SSM_GROUP, P), P ** -0.5),
        'ssm_c_im': nrm((DEPTH, G, SSM_GROUP, P), P ** -0.5),
        'ssm_d': nrm((DEPTH, D_SSM), 1.0),
        'ssm_w_glu': nrm((DEPTH, D_SSM, D_SSM), D_SSM ** -0.5),
        'ssm_b_glu': nrm((DEPTH, D_SSM), 0.01),
        'rwkv_w0': uni((DEPTH, D_RWKV), -6.0, -1.0),
        'rwkv_w2': nrm((DEPTH, DECAY_LORA, D_RWKV), 0.1 * DECAY_LORA ** -0.5),
        'rwkv_a0': nrm((DEPTH, D_RWKV), 0.1),
        'rwkv_a2': nrm((DEPTH, AAA_LORA, D_RWKV), 0.1 * AAA_LORA ** -0.5),
        'rwkv_g2': nrm((DEPTH, GATE_LORA, D_RWKV), GATE_LORA ** -0.5),
        'rwkv_k_k': 0.85 + nrm((DEPTH, D_RWKV), 0.02),
        'rwkv_k_a': 1.0 + nrm((DEPTH, D_RWKV), 0.02),
        'rwkv_r_k': nrm((DEPTH, H, N), 0.1),
        'rwkv_lnx_g': 1.0 + nrm((DEPTH, D_RWKV), 0.02),
        'rwkv_lnx_b': nrm((DEPTH, D_RWKV), 0.01),
        'w_out': nrm((DEPTH, D_MIX, D_MODEL), D_MIX ** -0.5),
        'norm2_g': 1.0 + nrm((DEPTH, D_MODEL), 0.02),
        'w_up': nrm((DEPTH, D_MODEL, 2 * D_FF), D_MODEL ** -0.5),
        'conv_w': nrm((DEPTH, CONV_W, D_FF), CONV_W ** -0.5),
        'conv_b': nrm((DEPTH, D_FF), 0.01),
        'w_down': nrm((DEPTH, D_FF, D_MODEL), D_FF ** -0.5),
        'w_ple': nrm((DEPTH, D_PLE, D_MODEL), D_PLE ** -0.5),
        'w_pg': nrm((DEPTH, D_MODEL, D_MODEL), D_MODEL ** -0.5),
        'final_g': 1.0 + nrm((D_MODEL,), 0.02),
    }


def reference(x_prompt, x_sample, p_prompt, p_sample, state_ssm_re, state_ssm_im, state_rwkv,
              state_shift, state_conv, norm1_g, w_in, shift_mu, ssm_lam_re, ssm_lam_im,
              ssm_log_dt, ssm_b_re, ssm_b_im, ssm_c_re, ssm_c_im, ssm_d, ssm_w_glu, ssm_b_glu,
              rwkv_w0, rwkv_w2, rwkv_a0, rwkv_a2, rwkv_g2, rwkv_k_k, rwkv_k_a, rwkv_r_k,
              rwkv_lnx_g, rwkv_lnx_b, w_out, norm2_g, w_up, conv_w, conv_b, w_down, w_ple,
              w_pg, final_g):
    f32 = jnp.float32
    weights = (norm1_g, w_in, shift_mu, ssm_lam_re, ssm_lam_im, ssm_log_dt, ssm_b_re, ssm_b_im,
               ssm_c_re, ssm_c_im, ssm_d, ssm_w_glu, ssm_b_glu, rwkv_w0, rwkv_w2, rwkv_a0,
               rwkv_a2, rwkv_g2, rwkv_k_k, rwkv_k_a, rwkv_r_k, rwkv_lnx_g, rwkv_lnx_b, w_out,
               norm2_g, w_up, conv_w, conv_b, w_down, w_ple, w_pg)
    bp = x_prompt.shape[0]
    zero_st = (jnp.zeros((bp, N_SSM_GROUPS, SSM_STATE), f32),
               jnp.zeros((bp, N_SSM_GROUPS, SSM_STATE), f32),
               jnp.zeros((bp, N_RWKV_HEADS, RWKV_HEAD, RWKV_HEAD), f32),
               jnp.zeros((bp, N_SHIFT), f32),
               jnp.zeros((bp, CONV_W - 1, D_FF), f32))
    hp = x_prompt.astype(f32)
    hs = x_sample.astype(f32)
    new_p = [[] for _ in range(5)]
    new_s = [[] for _ in range(5)]
    for i in range(DEPTH):
        w_i = tuple(t[i].astype(f32) for t in weights)
        hp, stp = _layer(hp, p_prompt[i], zero_st, w_i)
        st_in = (state_ssm_re[i], state_ssm_im[i], state_rwkv[i], state_shift[i], state_conv[i])
        hs, sts = _layer(hs, p_sample[i], st_in, w_i)
        for j in range(5):
            new_p[j].append(stp[j])
            new_s[j].append(sts[j])
    y_prompt = _rmsnorm(hp, final_g).astype(x_prompt.dtype)
    y_sample = _rmsnorm(hs, final_g).astype(x_sample.dtype)
    ssm_re_p = jnp.stack(new_p[0]).astype(state_ssm_re.dtype)
    ssm_im_p = jnp.stack(new_p[1]).astype(state_ssm_im.dtype)
    rwkv_p = jnp.stack(new_p[2]).astype(state_rwkv.dtype)
    shift_p = jnp.stack(new_p[3]).astype(state_shift.dtype)
    conv_p = jnp.stack(new_p[4]).astype(state_conv.dtype)
    ssm_re_s = jnp.stack(new_s[0]).astype(state_ssm_re.dtype)
    ssm_im_s = jnp.stack(new_s[1]).astype(state_ssm_im.dtype)
    rwkv_s = jnp.stack(new_s[2]).astype(state_rwkv.dtype)
    shift_s = jnp.stack(new_s[3]).astype(state_shift.dtype)
    conv_s = jnp.stack(new_s[4]).astype(state_conv.dtype)
    return (y_prompt, y_sample, ssm_re_p, ssm_im_p, rwkv_p, shift_p, conv_p,
            ssm_re_s, ssm_im_s, rwkv_s, shift_s, conv_s)
```

```python
import functools
import math

import numpy as np
import jax
import jax.numpy as jnp
from jax import lax
from jax.experimental import pallas as pl
from jax.experimental.pallas import tpu as pltpu

F32 = jnp.float32
BF16 = jnp.bfloat16
HIGHEST = lax.Precision.HIGHEST

D_MODEL = 1024
D_SSM = 512
D_RWKV = 512
SSM_GROUP = 16
N_SSM_GROUPS = 32
SSM_STATE = 64
RWKV_HEAD = 64
N_RWKV_HEADS = 8
DECAY_LORA = 64
AAA_LORA = 64
GATE_LORA = 128
N_SHIFT = 3 * D_RWKV + DECAY_LORA + AAA_LORA + GATE_LORA
N_IN = D_SSM + N_SHIFT
D_FF = 2816
D_PLE = 256
RMS_EPS = 1e-6
GN_EPS = 64e-5
L2_EPS = 1e-12

S5_STEPS = 8
S5_ROW = S5_STEPS * D_SSM
S5_LANE_BLOCKS = 4
S5_BLOCK_STATE = 2 * 8 * SSM_STATE
S5_STATE_ROW = S5_LANE_BLOCKS * S5_BLOCK_STATE
FF_CHUNK = 256
V7X_VMEM_LIMIT_BYTES = 56 * 1024 * 1024


def _mm(a, b):
    return jnp.dot(a.astype(BF16), b.astype(BF16), preferred_element_type=F32)


def _mm_nt(a, b):
    return lax.dot_general(a.astype(BF16), b.astype(BF16), (((1,), (1,)), ((), ())),
                           preferred_element_type=F32)


def _mm_tn(a, b):
    return lax.dot_general(a.astype(BF16), b.astype(BF16), (((0,), (0,)), ((), ())),
                           preferred_element_type=F32)


def _seg_sum(x, ones_bd):
    hi = x.astype(BF16)
    lo = (x - hi.astype(F32)).astype(BF16)
    return (jnp.dot(hi, ones_bd, preferred_element_type=F32)
            + jnp.dot(lo, ones_bd, preferred_element_type=F32))


def _rmsnorm(x, g):
    return x * lax.rsqrt(jnp.mean(x * x, axis=-1, keepdims=True) + RMS_EPS) * g


def _sigmoid(x):
    return 1.0 / (1.0 + jnp.exp(-x))


def _const_spec(shape):
    nd = len(shape)
    return pl.BlockSpec(shape, lambda *_: (0,) * nd, pipeline_mode=pl.Buffered(1))


def _inproj_kernel(h_ref, g_ref, w_ref, mu_ref, init_ref, u_ref, z_ref, last_ref, *scratch,
                   seq_len, tm):
    i = pl.program_id(0)
    if seq_len >= tm:
        (carry_ref,) = scratch

        @pl.when(i == 0)
        def _():
            carry_ref[...] = jnp.zeros_like(carry_ref)
    xn = _rmsnorm(h_ref[...], g_ref[...])
    proj = jnp.dot(xn.astype(BF16), w_ref[...], preferred_element_type=F32)
    u_ref[...] = proj[:, :D_SSM]
    zr = proj[:, D_SSM:]
    rolled = pltpu.roll(zr, 1, 0)
    row = lax.broadcasted_iota(jnp.int32, (tm, 1), 0)
    if seq_len >= tm:
        first = (i % (seq_len // tm)) == 0
        row0 = jnp.where(first, init_ref[0], carry_ref[...])
        prev = jnp.where(row == 0, row0, rolled)
        carry_ref[...] = zr[tm - 1:tm, :]
        last_ref[0] = zr[tm - 1:tm, :]
    else:
        prev = jnp.where(row % seq_len == 0, init_ref[...], rolled)
        last_ref[...] = zr
    z_ref[...] = zr + (prev - zr) * mu_ref[...]


def _inproj(h, g1, w_in, mu, shift0, seq_len, tm):
    n = h.shape[0]
    nseq = n // seq_len
    if seq_len >= tm:
        tps = seq_len // tm
        init = shift0.reshape(nseq, 1, N_SHIFT)
        init_spec = pl.BlockSpec((1, 1, N_SHIFT), lambda i: (i // tps, 0, 0))
        last_shape = jax.ShapeDtypeStruct((nseq, 1, N_SHIFT), F32)
        last_spec = pl.BlockSpec((1, 1, N_SHIFT), lambda i: (i // tps, 0, 0))
    else:
        init = jnp.zeros((nseq, seq_len, N_SHIFT), F32).at[:, 0].set(shift0).reshape(n, N_SHIFT)
        init_spec = pl.BlockSpec((tm, N_SHIFT), lambda i: (i, 0))
        last_shape = jax.ShapeDtypeStruct((n, N_SHIFT), F32)
        last_spec = pl.BlockSpec((tm, N_SHIFT), lambda i: (i, 0))
    u, z, last = pl.pallas_call(
        functools.partial(_inproj_kernel, seq_len=seq_len, tm=tm),
        grid=(n // tm,),
        in_specs=[pl.BlockSpec((tm, D_MODEL), lambda i: (i, 0)),
                  _const_spec((1, D_MODEL)),
                  _const_spec((D_MODEL, N_IN)),
                  _const_spec((1, N_SHIFT)),
                  init_spec],
        out_specs=[pl.BlockSpec((tm, D_SSM), lambda i: (i, 0)),
                   pl.BlockSpec((tm, N_SHIFT), lambda i: (i, 0)),
                   last_spec],
        out_shape=[jax.ShapeDtypeStruct((n, D_SSM), F32),
                   jax.ShapeDtypeStruct((n, N_SHIFT), F32),
                   last_shape],
        scratch_shapes=[pltpu.VMEM((1, N_SHIFT), F32)] if seq_len >= tm else [],
        compiler_params=pltpu.CompilerParams(dimension_semantics=("arbitrary",),
                                             vmem_limit_bytes=V7X_VMEM_LIMIT_BYTES),
        name="inproj",
    )(h, g1, w_in, mu, init)
    shift_new = last.reshape(nseq, -1, N_SHIFT)[:, -1]
    return u, z, shift_new


def _s5_prep_kernel(lr_ref, li_ref, ldt_ref, br_ref, bi_ref, cr_ref, ci_ref,
                    wxr_ref, wxi_ref, mre_ref, mim_ref, kt_ref, apr_ref, api_ref):
    lr, li = lr_ref[...], li_ref[...]
    dt = jnp.exp(ldt_ref[...])
    mag = jnp.exp(lr * dt)
    ar = mag * jnp.cos(li * dt)
    ai = mag * jnp.sin(li * dt)
    den = lr * lr + li * li
    nr = ar - 1.0
    fr = (nr * lr + ai * li) / den
    fi = (ai * lr - nr * li) / den
    br, bi = br_ref[...], bi_ref[...]
    bbr = fr * br - fi * bi
    bbi = fr * bi + fi * br
    cr, ci = cr_ref[...], ci_ref[...]
    pr = [jnp.ones_like(ar)]
    pi = [jnp.zeros_like(ar)]
    for _ in range(S5_STEPS):
        pr.append(pr[-1] * ar - pi[-1] * ai)
        pi.append(pr[-2] * ai + pi[-1] * ar)
    nrow = lr.shape[0]
    rg = lax.broadcasted_iota(jnp.int32, (nrow, nrow), 0) // SSM_GROUP
    cg = lax.broadcasted_iota(jnp.int32, (nrow, nrow), 1) // SSM_GROUP
    same_group = rg == cg
    nt = (((1,), (1,)), ((), ()))
    for s in range(S5_STEPS):
        qr, qi = pr[S5_STEPS - 1 - s], pi[S5_STEPS - 1 - s]
        wr = qr * bbr - qi * bbi
        wi = qr * bbi + qi * bbr
        wxr_ref[s] = wr
        wxi_ref[s] = wi
        kt = (lax.dot_general(wr, cr, nt, precision=HIGHEST, preferred_element_type=F32)
              - lax.dot_general(wi, ci, nt, precision=HIGHEST, preferred_element_type=F32))
        kt_ref[S5_STEPS - 1 - s] = jnp.where(same_group, kt, 0.0)
        mre_ref[s] = cr * pr[s + 1] - ci * pi[s + 1]
        mim_ref[s] = cr * pi[s + 1] + ci * pr[s + 1]
    a8r, a8i = pr[S5_STEPS], pi[S5_STEPS]
    er, ei = a8r, a8i
    for n in range(S5_STEPS):
        apr_ref[n] = er
        api_ref[n] = ei
        er, ei = er * a8r - ei * a8i, er * a8i + ei * a8r


def _s5_tables(lam_re, lam_im, log_dt, b_re, b_im, c_re, c_im):
    G, P, K = N_SSM_GROUPS, SSM_STATE, SSM_GROUP
    rep = lambda t: jnp.repeat(t, K, axis=0)
    args = (rep(lam_re), rep(lam_im), rep(jnp.broadcast_to(log_dt[:, None], (G, P))),
            jnp.swapaxes(b_re, 1, 2).reshape(G * K, P), jnp.swapaxes(b_im, 1, 2).reshape(G * K, P),
            c_re.reshape(G * K, P), c_im.reshape(G * K, P))
    t3 = jax.ShapeDtypeStruct((S5_STEPS, G * K, P), F32)
    wxr, wxi, mre, mim, kt, apr, api = pl.pallas_call(
        _s5_prep_kernel,
        out_shape=[t3, t3, t3, t3, jax.ShapeDtypeStruct((S5_STEPS, G * K, G * K), F32), t3, t3],
        name="s5_prep",
    )(*args)
    Q, GB = S5_LANE_BLOCKS, G // S5_LANE_BLOCKS
    eye = jnp.eye(GB, dtype=F32)
    five = lambda t: t.reshape(S5_STEPS, Q, GB, K, P)
    wx = jnp.concatenate(
        [jnp.einsum('sqgcp,gh->sqgchp', five(t), eye).reshape(S5_STEPS, Q, GB * K, GB * P)
         for t in (wxr, wxi)], axis=-1).astype(BF16)
    st = jnp.concatenate(
        [jnp.einsum('sqgcp,gh->sqhpgc', five(t), eye).reshape(S5_STEPS, Q, GB * P, GB * K)
         for t in (mre, -mim)], axis=2)
    ktd = jnp.stack([kt[:, q * GB * K:(q + 1) * GB * K, q * GB * K:(q + 1) * GB * K]
                     for q in range(Q)], axis=1)
    tau = np.arange(S5_STEPS)[None, :] - np.arange(S5_STEPS)[:, None]
    intra = ktd[np.clip(tau, 0, None)] * jnp.asarray(tau >= 0, F32)[:, :, None, None, None]
    intra = jnp.transpose(intra, (1, 2, 0, 3, 4))
    intra = intra.reshape(S5_STEPS, Q, S5_STEPS * GB * K, GB * K)
    wy = jnp.concatenate([st, intra], axis=2)
    wy = jnp.transpose(wy.reshape(S5_STEPS // 2, 2, Q, 2 * GB * P + S5_STEPS * GB * K, GB * K),
                       (2, 0, 3, 1, 4))
    wy = wy.reshape(Q, S5_STEPS // 2, 2 * GB * P + S5_STEPS * GB * K, 2 * GB * K).astype(BF16)
    ap = jnp.concatenate([t[:, ::K, :].reshape(S5_STEPS, Q, GB * P) for t in (apr, api)], axis=-1)
    return wx, wy, ap.reshape(S5_STEPS, S5_STATE_ROW)


def _state_to_lanes(h_re, h_im):
    n = h_re.shape[0]
    parts = [t.reshape(n, S5_LANE_BLOCKS, -1) for t in (h_re, h_im)]
    return jnp.concatenate(parts, axis=-1).reshape(n, S5_STATE_ROW)


def _lanes_to_state(h):
    n = h.shape[0]
    h = h.reshape(n, S5_LANE_BLOCKS, 2, N_SSM_GROUPS // S5_LANE_BLOCKS, SSM_STATE)
    return (h[:, :, 0].reshape(n, N_SSM_GROUPS, SSM_STATE), h[:, :, 1].reshape(n, N_SSM_GROUPS, SSM_STATE))


def _s5_kernel(u_ref, h0_ref, wx_ref, wy_ref, ap_ref, y_ref, hl_ref, hs_ref, hp_ref, *, rows, scan):
    half = S5_BLOCK_STATE // 2
    cw = D_SSM // S5_LANE_BLOCKS
    ub = u_ref[...].astype(BF16)
    sub3 = lax.broadcasted_iota(jnp.int32, (1, 8, 1), 1)
    sub2 = lax.broadcasted_iota(jnp.int32, (8, 1), 0)
    for q in range(S5_LANE_BLOCKS):
        us = [ub[:, s * D_SSM + q * cw: s * D_SSM + (q + 1) * cw] for s in range(S5_STEPS)]
        x = jnp.dot(us[0], wx_ref[0, q], preferred_element_type=F32)
        for s in range(1, S5_STEPS):
            x = x + jnp.dot(us[s], wx_ref[s, q], preferred_element_type=F32)
        lo = q * S5_BLOCK_STATE
        apq = ap_ref[:, lo:lo + S5_BLOCK_STATE]
        if scan:
            xr = x[:, :half].reshape(rows // 8, 8, half)
            xi = x[:, half:].reshape(rows // 8, 8, half)
            for k in (1, 2, 4):
                er = apq[k - 1:k, :half].reshape(1, 1, half)
                ei = apq[k - 1:k, half:].reshape(1, 1, half)
                sr = jnp.where(sub3 >= k, pltpu.roll(xr, k, 1), 0.0)
                si = jnp.where(sub3 >= k, pltpu.roll(xi, k, 1), 0.0)
                xr, xi = xr + er * sr - ei * si, xi + er * si + ei * sr
            hs_ref[:, :half] = xr.reshape(rows, half)
            hs_ref[:, half:] = xi.reshape(rows, half)
            pwr, pwi = apq[:, :half], apq[:, half:]
            c0r = h0_ref[0][:, lo:lo + half]
            c0i = h0_ref[0][:, lo + half:lo + S5_BLOCK_STATE]

            def body(j, carry):
                cr, ci = carry
                sl = pl.ds(pl.multiple_of(j * 8, 8), 8)
                hr = hs_ref[sl, :half] + pwr * cr - pwi * ci
                hi = hs_ref[sl, half:] + pwr * ci + pwi * cr
                hp_ref[sl, :half] = jnp.where(sub2 == 0, cr, pltpu.roll(hr, 1, 0))
                hp_ref[sl, half:] = jnp.where(sub2 == 0, ci, pltpu.roll(hi, 1, 0))
                return hr[7:8, :], hi[7:8, :]

            cr, ci = lax.fori_loop(0, rows // 8, body, (c0r, c0i))
            hl_ref[0, :, lo:lo + half] = cr
            hl_ref[0, :, lo + half:lo + S5_BLOCK_STATE] = ci
            hprev = hp_ref[...].astype(BF16)
        else:
            h0r = h0_ref[:, lo:lo + half]
            h0i = h0_ref[:, lo + half:lo + S5_BLOCK_STATE]
            er, ei = apq[0:1, :half], apq[0:1, half:]
            hl_ref[:, lo:lo + half] = er * h0r - ei * h0i + x[:, :half]
            hl_ref[:, lo + half:lo + S5_BLOCK_STATE] = er * h0i + ei * h0r + x[:, half:]
            hprev = jnp.concatenate([h0r, h0i], axis=1).astype(BF16)
        lhs = jnp.concatenate([hprev] + us, axis=1)
        for j in range(S5_STEPS // 2):
            kk = S5_BLOCK_STATE + cw * (2 * j + 2)
            y2 = jnp.dot(lhs[:, :kk], wy_ref[q, j, :kk, :], preferred_element_type=F32)
            c0 = (2 * j) * D_SSM + q * cw
            c1 = (2 * j + 1) * D_SSM + q * cw
            y_ref[:, c0:c0 + cw] = y2[:, :cw]
            y_ref[:, c1:c1 + cw] = y2[:, cw:]


def _s5(u, h_re, h_im, tables, seq_len):
    wx, wy, ap = tables
    n = u.shape[0]
    nseq = n // seq_len
    u8 = u.reshape(n // S5_STEPS, S5_ROW)
    h0 = _state_to_lanes(h_re, h_im)
    cps = seq_len // S5_STEPS
    scan = cps > 1
    if scan:
        rows = cps
        grid = (nseq,)
        h0 = h0.reshape(nseq, 1, S5_STATE_ROW)
        h_spec = pl.BlockSpec((1, 1, S5_STATE_ROW), lambda i: (i, 0, 0))
        h_shape = jax.ShapeDtypeStruct((nseq, 1, S5_STATE_ROW), F32)
    else:
        rows = min(nseq, 128)
        grid = (nseq // rows,)
        h_spec = pl.BlockSpec((rows, S5_STATE_ROW), lambda i: (i, 0))
        h_shape = jax.ShapeDtypeStruct((nseq, S5_STATE_ROW), F32)
    y8, hl = pl.pallas_call(
        functools.partial(_s5_kernel, rows=rows, scan=scan),
        grid=grid,
        in_specs=[pl.BlockSpec((rows, S5_ROW), lambda i: (i, 0)),
                  h_spec,
                  _const_spec(wx.shape), _const_spec(wy.shape), _const_spec(ap.shape)],
        out_specs=[pl.BlockSpec((rows, S5_ROW), lambda i: (i, 0)), h_spec],
        out_shape=[jax.ShapeDtypeStruct((n // S5_STEPS, S5_ROW), F32), h_shape],
        scratch_shapes=[pltpu.VMEM((rows, S5_BLOCK_STATE), F32),
                        pltpu.VMEM((rows, S5_BLOCK_STATE), F32)],
        compiler_params=pltpu.CompilerParams(dimension_semantics=("arbitrary",),
                                             vmem_limit_bytes=V7X_VMEM_LIMIT_BYTES),
        name="s5",
    )(u8, h0, wx, wy, ap)
    hre_new, him_new = _lanes_to_state(hl.reshape(nseq, S5_STATE_ROW))
    return y8.reshape(n, D_SSM), hre_new, him_new


def _rwkv_kernel(z_ref, s0_ref, w0_ref, w2_ref, a0_ref, a2_ref, g2_ref, kk_ref, ka_ref, rk_ref,
                 lg_ref, lb_ref, ones_ref, y_ref, sl_ref, st_ref, *, chunk, nchunks):
    C = chunk
    N = RWKV_HEAD
    c = pl.program_id(1)

    @pl.when(c == 0)
    def _():
        st_ref[...] = s0_ref[0]

    ones_bd = ones_ref[...]
    z = z_ref[...]
    r = z[:, :D_RWKV]
    k = z[:, D_RWKV:2 * D_RWKV]
    v = z[:, 2 * D_RWKV:3 * D_RWKV]
    o = 3 * D_RWKV
    xw = z[:, o:o + DECAY_LORA]
    xa = z[:, o + DECAY_LORA:o + DECAY_LORA + AAA_LORA]
    xg = z[:, o + DECAY_LORA + AAA_LORA:]
    wd = -(w0_ref[...] + _mm(jnp.tanh(xw), w2_ref[...]))
    w = -(jnp.maximum(wd, 0.0) + jnp.log1p(jnp.exp(-jnp.abs(wd)))) - 0.5
    lw = -jnp.exp(w)
    a = _sigmoid(a0_ref[...] + _mm(xa, a2_ref[...]))
    g = _mm(_sigmoid(xg), g2_ref[...])
    kk = k * kk_ref[...]
    kk = kk / jnp.maximum(jnp.sqrt(_seg_sum(kk * kk, ones_bd)), L2_EPS)
    kmod = k * (1.0 + (a - 1.0) * ka_ref[...])

    ri = lax.broadcasted_iota(jnp.int32, (C, C), 0)
    ci = lax.broadcasted_iota(jnp.int32, (C, C), 1)
    lower_incl = ri >= ci
    lower_strict = ri > ci
    cw = jnp.dot(lower_incl.astype(F32), lw, precision=HIGHEST, preferred_element_type=F32)
    cwl = cw[C - 1:C, :]
    w_in = jnp.exp(cw)
    w_tail = jnp.exp(cwl - cw)
    w_inv = jnp.exp(-cw)
    ah = -kk * jnp.exp(cw - lw)
    bh = kk * a * w_inv
    kh = kmod * w_inv
    rh = r * w_in
    bt = kk * a * w_tail
    kt = kmod * w_tail
    w_all = jnp.exp(cwl)
    eye_c = (ri == ci).astype(F32)
    rn = lax.broadcasted_iota(jnp.int32, (N, N), 0)
    cn = lax.broadcasted_iota(jnp.int32, (N, N), 1)
    eye_n = (rn == cn).astype(F32)

    ys = []
    for h in range(N_RWKV_HEADS):
        sl = slice(h * N, (h + 1) * N)
        A, B, K, R, V = ah[:, sl], bh[:, sl], kh[:, sl], rh[:, sl], v[:, sl]
        G = _mm_nt(jnp.concatenate([A, R], axis=0), jnp.concatenate([B, K], axis=0))
        AB = jnp.where(lower_strict, G[:C, :C], 0.0)
        AK = jnp.where(lower_strict, G[:C, C:], 0.0)
        RB = jnp.where(lower_incl, G[C:, :C], 0.0)
        RK = jnp.where(lower_incl, G[C:, C:], 0.0)
        P = AB
        T = eye_c + AB
        span = 2
        while span < C:
            P = _mm(P, P)
            T = T + _mm(T, P)
            span *= 2
        X = _mm(T, jnp.concatenate([A, _mm(AK, V)], axis=1))
        TA, U0 = X[:, :N], X[:, N:]
        Rt = R + _mm(RB, TA)
        Y0 = _mm(RB, U0) + _mm(RK, V)
        Phi = eye_n * w_all[:, sl] + _mm_tn(bt[:, sl], TA)
        PsiT = _mm_tn(U0, bt[:, sl]) + _mm_tn(V, kt[:, sl])
        S = st_ref[h]
        ys.append(_mm_nt(Rt, S) + Y0)
        st_ref[h] = _mm_nt(S, Phi) + PsiT
    y = jnp.concatenate(ys, axis=1)

    mu = _seg_sum(y, ones_bd) * (1.0 / N)
    d = y - mu
    var = _seg_sum(d * d, ones_bd) * (1.0 / N)
    yn = d * lax.rsqrt(var + GN_EPS) * lg_ref[...] + lb_ref[...]
    bonus = _seg_sum(r * kmod * rk_ref[...], ones_bd) * v
    y_ref[...] = (yn + bonus) * g

    @pl.when(c == nchunks - 1)
    def _():
        sl_ref[0] = st_ref[...]


def _rwkv(z, s0, p, seq_len, chunk):
    n = z.shape[0]
    nseq = n // seq_len
    nchunks = seq_len // chunk
    ones_bd = jnp.asarray(np.kron(np.eye(N_RWKV_HEADS), np.ones((RWKV_HEAD, RWKV_HEAD))), BF16)
    state_spec = pl.BlockSpec((1, N_RWKV_HEADS, RWKV_HEAD, RWKV_HEAD), lambda s, c: (s, 0, 0, 0))
    small = [p['w0'], p['w2'], p['a0'], p['a2'], p['g2'], p['k_k'], p['k_a'], p['r_k'],
             p['lnx_g'], p['lnx_b'], ones_bd]
    y, s_new = pl.pallas_call(
        functools.partial(_rwkv_kernel, chunk=chunk, nchunks=nchunks),
        grid=(nseq, nchunks),
        in_specs=[pl.BlockSpec((chunk, N_SHIFT), lambda s, c: (s * nchunks + c, 0)), state_spec]
        + [_const_spec(t.shape) for t in small],
        out_specs=[pl.BlockSpec((chunk, D_RWKV), lambda s, c: (s * nchunks + c, 0)), state_spec],
        out_shape=[jax.ShapeDtypeStruct((n, D_RWKV), F32),
                   jax.ShapeDtypeStruct((nseq, N_RWKV_HEADS, RWKV_HEAD, RWKV_HEAD), F32)],
        scratch_shapes=[pltpu.VMEM((N_RWKV_HEADS, RWKV_HEAD, RWKV_HEAD), F32)],
        compiler_params=pltpu.CompilerParams(dimension_semantics=("arbitrary", "arbitrary")),
        name="rwkv7",
    )(z, s0, *small)
    return y, s_new


def _ffn_kernel(h_ref, y8_ref, u_ref, yb_ref, pe_ref, c1_ref, c2_ref, d_ref, wglu_ref, bglu_ref,
                wout_ref, g2_ref, wup_ref, cw_ref, cb_ref, wdn_ref, wple_ref, wpg_ref, gf_ref,
                o_ref, *rest, seq_len, tm, final):
    i = pl.program_id(0)
    long_seq = seq_len >= tm
    if long_seq:
        ga_ref, gb_ref, carry_ref = rest

        @pl.when(i == 0)
        def _():
            carry_ref[...] = jnp.zeros_like(carry_ref)
    else:
        (gate_ref,) = rest
    ya = y8_ref[...] + d_ref[...] * u_ref[...]
    c_gelu = math.sqrt(2.0 / math.pi)
    ya = ya * (0.5 * (1.0 + jnp.tanh(c_gelu * (ya + 0.044715 * (ya * ya * ya)))))
    ya = ya * _sigmoid(_mm(ya, wglu_ref[...]) + bglu_ref[...])
    h1 = (h_ref[...] + _mm(ya, wout_ref[:D_SSM, :]) + _mm(yb_ref[...], wout_ref[D_SSM:, :]))
    x2 = _rmsnorm(h1, g2_ref[...]).astype(BF16)
    row = lax.broadcasted_iota(jnp.int32, (tm, 1), 0)
    if long_seq:
        first = (i % (seq_len // tm)) == 0
    else:
        t = row % seq_len
    acc = jnp.zeros((tm, D_MODEL), F32)
    for c in range(D_FF // FF_CHUNK):
        cs = slice(c * FF_CHUNK, (c + 1) * FF_CHUNK)
        gs = slice(D_FF + c * FF_CHUNK, D_FF + (c + 1) * FF_CHUNK)
        val = jnp.dot(x2, wup_ref[:, cs], preferred_element_type=F32)
        gate = jnp.dot(x2, wup_ref[:, gs], preferred_element_type=F32)
        r1 = pltpu.roll(gate, 1, 0)
        r2 = pltpu.roll(gate, 2, 0)
        if long_seq:
            m1 = jnp.where(first, c1_ref[0][:, cs], carry_ref[1:2, cs])
            m2 = jnp.where(first, c2_ref[0][:, cs], carry_ref[0:1, cs])
            p1 = jnp.where(row == 0, m1, r1)
            p2 = jnp.where(row == 0, m2, jnp.where(row == 1, m1, r2))
            carry_ref[:, cs] = gate[tm - 2:tm, :]
            ga_ref[0, :, cs] = gate[tm - 2:tm - 1, :]
            gb_ref[0, :, cs] = gate[tm - 1:tm, :]
        else:
            p1 = jnp.where(t == 0, c1_ref[:, cs], r1)
            p2 = jnp.where(t <= 1, c2_ref[:, cs], r2)
            gate_ref[:, cs] = gate
        conv = (cb_ref[:, cs] + cw_ref[2:3, cs] * gate + cw_ref[1:2, cs] * p1 + cw_ref[0:1, cs] * p2)
        act = conv * _sigmoid(conv) * val
        acc = acc + jnp.dot(act.astype(BF16), wdn_ref[cs, :], preferred_element_type=F32)
    h2 = h1 + acc
    h3 = h2 + _mm(pe_ref[...], wple_ref[...]) * _sigmoid(_mm(h2, wpg_ref[...]))
    if final:
        h3 = _rmsnorm(h3, gf_ref[...])
    o_ref[...] = h3


def _ffn(h, y8, u, yb, pe, conv0, p, seq_len, tm, final):
    n = h.shape[0]
    nseq = n // seq_len
    row_spec = lambda w: pl.BlockSpec((tm, w), lambda i: (i, 0))
    if seq_len >= tm:
        tps = seq_len // tm
        seq_spec = pl.BlockSpec((1, 1, D_FF), lambda i: (i // tps, 0, 0))
        c1 = conv0[:, 1].reshape(nseq, 1, D_FF)
        c2 = conv0[:, 0].reshape(nseq, 1, D_FF)
        c_specs = [seq_spec, seq_spec]
        g_shapes = [jax.ShapeDtypeStruct((nseq, 1, D_FF), F32)] * 2
        g_specs = [seq_spec, seq_spec]
    else:
        zeros = jnp.zeros((nseq, seq_len, D_FF), F32)
        c1 = zeros.at[:, 0].set(conv0[:, 1]).reshape(n, D_FF)
        c2 = zeros.at[:, 0].set(conv0[:, 0]).at[:, 1].set(conv0[:, 1]).reshape(n, D_FF)
        c_specs = [row_spec(D_FF), row_spec(D_FF)]
        g_shapes = [jax.ShapeDtypeStruct((n, D_FF), F32)]
        g_specs = [row_spec(D_FF)]
    weights = [p['d'], p['w_glu'], p['b_glu'], p['w_out'], p['norm2_g'], p['w_up'], p['conv_w'],
               p['conv_b'], p['w_down'], p['w_ple'], p['w_pg'], p['final_g']]
    out, *gates = pl.pallas_call(
        functools.partial(_ffn_kernel, seq_len=seq_len, tm=tm, final=final),
        grid=(n // tm,),
        in_specs=[row_spec(D_MODEL), row_spec(D_SSM), row_spec(D_SSM), row_spec(D_RWKV),
                  row_spec(D_PLE)] + c_specs + [_const_spec(t.shape) for t in weights],
        out_specs=[row_spec(D_MODEL)] + g_specs,
        out_shape=[jax.ShapeDtypeStruct((n, D_MODEL), F32)] + g_shapes,
        scratch_shapes=[pltpu.VMEM((2, D_FF), F32)] if seq_len >= tm else [],
        compiler_params=pltpu.CompilerParams(dimension_semantics=("arbitrary",),
                                             vmem_limit_bytes=V7X_VMEM_LIMIT_BYTES),
        name="ffn",
    )(h, y8, u, yb, pe, c1, c2, *weights)
    if seq_len >= tm:
        conv_new = jnp.concatenate(gates, axis=1)
    else:
        conv_new = gates[0].reshape(nseq, seq_len, D_FF)[:, seq_len - 2:]
    return out, conv_new


def _layer(h, pe, st, p, s5_tables, seq_len, tm, chunk, final):
    ssm_re0, ssm_im0, rwkv0, shift0, conv0 = st
    u, z, shift_new = _inproj(h, p['norm1_g'], p['w_in'], p['shift_mu'], shift0, seq_len, tm)
    y8, hre, him = _s5(u, ssm_re0, ssm_im0, s5_tables, seq_len)
    yb, s_last = _rwkv(z, rwkv0, p, seq_len, chunk)
    h, conv_new = _ffn(h, y8, u, yb, pe, conv0, p, seq_len, tm, final)
    return h, (hre, him, s_last, shift_new, conv_new)


def _layer_params(i, w):
    row = lambda t: t[i].reshape(1, -1).astype(F32)
    bf = lambda t: t[i].astype(BF16)
    return {
        'norm1_g': row(w['norm1_g']), 'w_in': bf(w['w_in']), 'shift_mu': row(w['shift_mu']),
        'd': row(w['ssm_d']), 'w_glu': bf(w['ssm_w_glu']), 'b_glu': row(w['ssm_b_glu']),
        'w0': row(w['rwkv_w0']), 'w2': bf(w['rwkv_w2']), 'a0': row(w['rwkv_a0']),
        'a2': bf(w['rwkv_a2']), 'g2': bf(w['rwkv_g2']), 'k_k': row(w['rwkv_k_k']),
        'k_a': row(w['rwkv_k_a']), 'r_k': row(w['rwkv_r_k']), 'lnx_g': row(w['rwkv_lnx_g']),
        'lnx_b': row(w['rwkv_lnx_b']), 'w_out': bf(w['w_out']), 'norm2_g': row(w['norm2_g']),
        'w_up': bf(w['w_up']), 'conv_w': w['conv_w'][i].astype(F32), 'conv_b': row(w['conv_b']),
        'w_down': bf(w['w_down']), 'w_ple': bf(w['w_ple']), 'w_pg': bf(w['w_pg']),
        'final_g': w['final_g'].reshape(1, -1).astype(F32),
    }


def _forward(x_prompt, x_sample, p_prompt, p_sample, state_ssm_re, state_ssm_im, state_rwkv,
             state_shift, state_conv, w, tm_prompt, tm_sample, chunk_prompt):
    depth = w['w_in'].shape[0]
    bp, lp, _ = x_prompt.shape
    bs, ls, _ = x_sample.shape
    hp = x_prompt.reshape(bp * lp, D_MODEL).astype(F32)
    hs = x_sample.reshape(bs * ls, D_MODEL).astype(F32)
    zero_st = (jnp.zeros((bp, N_SSM_GROUPS, SSM_STATE), F32),
               jnp.zeros((bp, N_SSM_GROUPS, SSM_STATE), F32),
               jnp.zeros((bp, N_RWKV_HEADS, RWKV_HEAD, RWKV_HEAD), F32),
               jnp.zeros((bp, N_SHIFT), F32),
               jnp.zeros((bp, 2, D_FF), F32))
    new_p = [[] for _ in range(5)]
    new_s = [[] for _ in range(5)]
    for i in range(depth):
        p = _layer_params(i, w)
        tables = _s5_tables(w['ssm_lam_re'][i], w['ssm_lam_im'][i], w['ssm_log_dt'][i],
                            w['ssm_b_re'][i], w['ssm_b_im'][i], w['ssm_c_re'][i], w['ssm_c_im'][i])
        final = i == depth - 1
        hp, stp = _layer(hp, p_prompt[i].reshape(bp * lp, D_PLE).astype(F32), zero_st, p, tables,
                         lp, tm_prompt, chunk_prompt, final)
        st_in = (state_ssm_re[i].astype(F32), state_ssm_im[i].astype(F32),
                 state_rwkv[i].astype(F32), state_shift[i].astype(F32), state_conv[i].astype(F32))
        hs, sts = _layer(hs, p_sample[i].reshape(bs * ls, D_PLE).astype(F32), st_in, p, tables,
                         ls, tm_sample, ls, final)
        for j in range(5):
            new_p[j].append(stp[j])
            new_s[j].append(sts[j])
    y_prompt = hp.reshape(bp, lp, D_MODEL).astype(x_prompt.dtype)
    y_sample = hs.reshape(bs, ls, D_MODEL).astype(x_sample.dtype)
    dts = (state_ssm_re.dtype, state_ssm_im.dtype, state_rwkv.dtype, state_shift.dtype,
           state_conv.dtype)
    outs_p = tuple(jnp.stack(new_p[j]).astype(dts[j]) for j in range(5))
    outs_s = tuple(jnp.stack(new_s[j]).astype(dts[j]) for j in range(5))
    return (y_prompt, y_sample) + outs_p + outs_s


def kernel(x_prompt, x_sample, p_prompt, p_sample, state_ssm_re, state_ssm_im, state_rwkv, state_shift, state_conv, norm1_g, w_in, shift_mu, ssm_lam_re, ssm_lam_im, ssm_log_dt, ssm_b_re, ssm_b_im, ssm_c_re, ssm_c_im, ssm_d, ssm_w_glu, ssm_b_glu, rwkv_w0, rwkv_w2, rwkv_a0, rwkv_a2, rwkv_g2, rwkv_k_k, rwkv_k_a, rwkv_r_k, rwkv_lnx_g, rwkv_lnx_b, w_out, norm2_g, w_up, conv_w, conv_b, w_down, w_ple, w_pg, final_g):
    w = dict(norm1_g=norm1_g, w_in=w_in, shift_mu=shift_mu, ssm_lam_re=ssm_lam_re,
             ssm_lam_im=ssm_lam_im, ssm_log_dt=ssm_log_dt, ssm_b_re=ssm_b_re, ssm_b_im=ssm_b_im,
             ssm_c_re=ssm_c_re, ssm_c_im=ssm_c_im, ssm_d=ssm_d, ssm_w_glu=ssm_w_glu,
             ssm_b_glu=ssm_b_glu, rwkv_w0=rwkv_w0, rwkv_w2=rwkv_w2, rwkv_a0=rwkv_a0,
             rwkv_a2=rwkv_a2, rwkv_g2=rwkv_g2, rwkv_k_k=rwkv_k_k, rwkv_k_a=rwkv_k_a,
             rwkv_r_k=rwkv_r_k, rwkv_lnx_g=rwkv_lnx_g, rwkv_lnx_b=rwkv_lnx_b, w_out=w_out,
             norm2_g=norm2_g, w_up=w_up, conv_w=conv_w, conv_b=conv_b, w_down=w_down,
             w_ple=w_ple, w_pg=w_pg, final_g=final_g)
    lp = x_prompt.shape[1]
    ns = x_sample.shape[0] * x_sample.shape[1]
    return _forward(x_prompt, x_sample, p_prompt, p_sample, state_ssm_re, state_ssm_im,
                    state_rwkv, state_shift, state_conv, w,
                    tm_prompt=min(512, lp), tm_sample=min(256, ns), chunk_prompt=min(64, lp))
```

```python
import functools
import math

import numpy as np
import jax
import jax.numpy as jnp
from jax import lax
from jax.experimental import pallas as pl
from jax.experimental.pallas import tpu as pltpu

F32 = jnp.float32
BF16 = jnp.bfloat16
HIGHEST = lax.Precision.HIGHEST

D_MODEL = 1024
D_SSM = 512
D_RWKV = 512
SSM_GROUP = 16
N_SSM_GROUPS = 32
SSM_STATE = 64
RWKV_HEAD = 64
N_RWKV_HEADS = 8
DECAY_LORA = 64
AAA_LORA = 64
GATE_LORA = 128
N_SHIFT = 3 * D_RWKV + DECAY_LORA + AAA_LORA + GATE_LORA
N_IN = D_SSM + N_SHIFT
D_FF = 2816
D_PLE = 256
RMS_EPS = 1e-6
GN_EPS = 64e-5
L2_EPS = 1e-12

S5_STEPS = 8
S5_ROW = S5_STEPS * D_SSM
S5_LANE_BLOCKS = 4
S5_BLOCK_STATE = 2 * 8 * SSM_STATE
S5_STATE_ROW = S5_LANE_BLOCKS * S5_BLOCK_STATE
FF_CHUNK = 256
V7X_VMEM_LIMIT_BYTES = 56 * 1024 * 1024


def _mm(a, b):
    return jnp.dot(a.astype(BF16), b.astype(BF16), preferred_element_type=F32)


def _mm_nt(a, b):
    return lax.dot_general(a.astype(BF16), b.astype(BF16), (((1,), (1,)), ((), ())),
                           preferred_element_type=F32)


def _mm_tn(a, b):
    return lax.dot_general(a.astype(BF16), b.astype(BF16), (((0,), (0,)), ((), ())),
                           preferred_element_type=F32)


def _seg_sum(x, ones_bd):
    hi = x.astype(BF16)
    lo = (x - hi.astype(F32)).astype(BF16)
    return (jnp.dot(hi, ones_bd, preferred_element_type=F32)
            + jnp.dot(lo, ones_bd, preferred_element_type=F32))


def _rmsnorm(x, g):
    return x * lax.rsqrt(jnp.mean(x * x, axis=-1, keepdims=True) + RMS_EPS) * g


def _sigmoid(x):
    return 1.0 / (1.0 + jnp.exp(-x))


def _const_spec(shape):
    nd = len(shape)
    return pl.BlockSpec(shape, lambda *_: (0,) * nd, pipeline_mode=pl.Buffered(1))


def _inproj_kernel(h_ref, g_ref, w_ref, mu_ref, init_ref, u_ref, z_ref, last_ref, *scratch,
                   seq_len, tm):
    i = pl.program_id(0)
    if seq_len >= tm:
        (carry_ref,) = scratch

        @pl.when(i == 0)
        def _():
            carry_ref[...] = jnp.zeros_like(carry_ref)
    xn = _rmsnorm(h_ref[...], g_ref[...])
    proj = jnp.dot(xn.astype(BF16), w_ref[...], preferred_element_type=F32)
    u_ref[...] = proj[:, :D_SSM]
    zr = proj[:, D_SSM:]
    rolled = pltpu.roll(zr, 1, 0)
    row = lax.broadcasted_iota(jnp.int32, (tm, 1), 0)
    if seq_len >= tm:
        first = (i % (seq_len // tm)) == 0
        row0 = jnp.where(first, init_ref[0], carry_ref[...])
        prev = jnp.where(row == 0, row0, rolled)
        carry_ref[...] = zr[tm - 1:tm, :]
        last_ref[0] = zr[tm - 1:tm, :]
    else:
        prev = jnp.where(row % seq_len == 0, init_ref[...], rolled)
        last_ref[...] = zr
    z_ref[...] = zr + (prev - zr) * mu_ref[...]


def _inproj(h, g1, w_in, mu, shift0, seq_len, tm):
    n = h.shape[0]
    nseq = n // seq_len
    if seq_len >= tm:
        tps = seq_len // tm
        init = shift0.reshape(nseq, 1, N_SHIFT)
        init_spec = pl.BlockSpec((1, 1, N_SHIFT), lambda i: (i // tps, 0, 0))
        last_shape = jax.ShapeDtypeStruct((nseq, 1, N_SHIFT), F32)
        last_spec = pl.BlockSpec((1, 1, N_SHIFT), lambda i: (i // tps, 0, 0))
    else:
        init = jnp.zeros((nseq, seq_len, N_SHIFT), F32).at[:, 0].set(shift0).reshape(n, N_SHIFT)
        init_spec = pl.BlockSpec((tm, N_SHIFT), lambda i: (i, 0))
        last_shape = jax.ShapeDtypeStruct((n, N_SHIFT), F32)
        last_spec = pl.BlockSpec((tm, N_SHIFT), lambda i: (i, 0))
    u, z, last = pl.pallas_call(
        functools.partial(_inproj_kernel, seq_len=seq_len, tm=tm),
        grid=(n // tm,),
        in_specs=[pl.BlockSpec((tm, D_MODEL), lambda i: (i, 0)),
                  _const_spec((1, D_MODEL)),
                  _const_spec((D_MODEL, N_IN)),
                  _const_spec((1, N_SHIFT)),
                  init_spec],
        out_specs=[pl.BlockSpec((tm, D_SSM), lambda i: (i, 0)),
                   pl.BlockSpec((tm, N_SHIFT), lambda i: (i, 0)),
                   last_spec],
        out_shape=[jax.ShapeDtypeStruct((n, D_SSM), F32),
                   jax.ShapeDtypeStruct((n, N_SHIFT), F32),
                   last_shape],
        scratch_shapes=[pltpu.VMEM((1, N_SHIFT), F32)] if seq_len >= tm else [],
        compiler_params=pltpu.CompilerParams(dimension_semantics=("arbitrary",),
                                             vmem_limit_bytes=V7X_VMEM_LIMIT_BYTES),
        name="inproj",
    )(h, g1, w_in, mu, init)
    shift_new = last.reshape(nseq, -1, N_SHIFT)[:, -1]
    return u, z, shift_new


def _s5_prep_kernel(lr_ref, li_ref, ldt_ref, br_ref, bi_ref, cr_ref, ci_ref,
                    wxr_ref, wxi_ref, mre_ref, mim_ref, kt_ref, apr_ref, api_ref):
    lr, li = lr_ref[...], li_ref[...]
    dt = jnp.exp(ldt_ref[...])
    mag = jnp.exp(lr * dt)
    ar = mag * jnp.cos(li * dt)
    ai = mag * jnp.sin(li * dt)
    den = lr * lr + li * li
    nr = ar - 1.0
    fr = (nr * lr + ai * li) / den
    fi = (ai * lr - nr * li) / den
    br, bi = br_ref[...], bi_ref[...]
    bbr = fr * br - fi * bi
    bbi = fr * bi + fi * br
    cr, ci = cr_ref[...], ci_ref[...]
    pr = [jnp.ones_like(ar)]
    pi = [jnp.zeros_like(ar)]
    for _ in range(S5_STEPS):
        pr.append(pr[-1] * ar - pi[-1] * ai)
        pi.append(pr[-2] * ai + pi[-1] * ar)
    nrow = lr.shape[0]
    rg = lax.broadcasted_iota(jnp.int32, (nrow, nrow), 0) // SSM_GROUP
    cg = lax.broadcasted_iota(jnp.int32, (nrow, nrow), 1) // SSM_GROUP
    same_group = rg == cg
    nt = (((1,), (1,)), ((), ()))
    for s in range(S5_STEPS):
        qr, qi = pr[S5_STEPS - 1 - s], pi[S5_STEPS - 1 - s]
        wr = qr * bbr - qi * bbi
        wi = qr * bbi + qi * bbr
        wxr_ref[s] = wr
        wxi_ref[s] = wi
        kt = (lax.dot_general(wr, cr, nt, precision=HIGHEST, preferred_element_type=F32)
              - lax.dot_general(wi, ci, nt, precision=HIGHEST, preferred_element_type=F32))
        kt_ref[S5_STEPS - 1 - s] = jnp.where(same_group, kt, 0.0)
        mre_ref[s] = cr * pr[s + 1] - ci * pi[s + 1]
        mim_ref[s] = cr * pi[s + 1] + ci * pr[s + 1]
    a8r, a8i = pr[S5_STEPS], pi[S5_STEPS]
    er, ei = a8r, a8i
    for n in range(S5_STEPS):
        apr_ref[n] = er
        api_ref[n] = ei
        er, ei = er * a8r - ei * a8i, er * a8i + ei * a8r


def _s5_tables(lam_re, lam_im, log_dt, b_re, b_im, c_re, c_im):
    G, P, K = N_SSM_GROUPS, SSM_STATE, SSM_GROUP
    rep = lambda t: jnp.repeat(t, K, axis=0)
    args = (rep(lam_re), rep(lam_im), rep(jnp.broadcast_to(log_dt[:, None], (G, P))),
            jnp.swapaxes(b_re, 1, 2).reshape(G * K, P), jnp.swapaxes(b_im, 1, 2).reshape(G * K, P),
            c_re.reshape(G * K, P), c_im.reshape(G * K, P))
    t3 = jax.ShapeDtypeStruct((S5_STEPS, G * K, P), F32)
    wxr, wxi, mre, mim, kt, apr, api = pl.pallas_call(
        _s5_prep_kernel,
        out_shape=[t3, t3, t3, t3, jax.ShapeDtypeStruct((S5_STEPS, G * K, G * K), F32), t3, t3],
        name="s5_prep",
    )(*args)
    Q, GB = S5_LANE_BLOCKS, G // S5_LANE_BLOCKS
    eye = jnp.eye(GB, dtype=F32)
    five = lambda t: t.reshape(S5_STEPS, Q, GB, K, P)
    wx = jnp.concatenate(
        [jnp.einsum('sqgcp,gh->sqgchp', five(t), eye).reshape(S5_STEPS, Q, GB * K, GB * P)
         for t in (wxr, wxi)], axis=-1).astype(BF16)
    st = jnp.concatenate(
        [jnp.einsum('sqgcp,gh->sqhpgc', five(t), eye).reshape(S5_STEPS, Q, GB * P, GB * K)
         for t in (mre, -mim)], axis=2)
    ktd = jnp.stack([kt[:, q * GB * K:(q + 1) * GB * K, q * GB * K:(q + 1) * GB * K]
                     for q in range(Q)], axis=1)
    tau = np.arange(S5_STEPS)[None, :] - np.arange(S5_STEPS)[:, None]
    intra = ktd[np.clip(tau, 0, None)] * jnp.asarray(tau >= 0, F32)[:, :, None, None, None]
    intra = jnp.transpose(intra, (1, 2, 0, 3, 4))
    intra = intra.reshape(S5_STEPS, Q, S5_STEPS * GB * K, GB * K)
    wy = jnp.concatenate([st, intra], axis=2)
    wy = jnp.transpose(wy.reshape(S5_STEPS // 2, 2, Q, 2 * GB * P + S5_STEPS * GB * K, GB * K),
                       (2, 0, 3, 1, 4))
    wy = wy.reshape(Q, S5_STEPS // 2, 2 * GB * P + S5_STEPS * GB * K, 2 * GB * K).astype(BF16)
    ap = jnp.concatenate([t[:, ::K, :].reshape(S5_STEPS, Q, GB * P) for t in (apr, api)], axis=-1)
    return wx, wy, ap.reshape(S5_STEPS, S5_STATE_ROW)


def _state_to_lanes(h_re, h_im):
    n = h_re.shape[0]
    parts = [t.reshape(n, S5_LANE_BLOCKS, -1) for t in (h_re, h_im)]
    return jnp.concatenate(parts, axis=-1).reshape(n, S5_STATE_ROW)


def _lanes_to_state(h):
    n = h.shape[0]
    h = h.reshape(n, S5_LANE_BLOCKS, 2, N_SSM_GROUPS // S5_LANE_BLOCKS, SSM_STATE)
    return (h[:, :, 0].reshape(n, N_SSM_GROUPS, SSM_STATE), h[:, :, 1].reshape(n, N_SSM_GROUPS, SSM_STATE))


def _s5_kernel(u_ref, h0_ref, wx_ref, wy_ref, ap_ref, y_ref, hl_ref, hs_ref, hp_ref, *, rows, scan):
    half = S5_BLOCK_STATE // 2
    cw = D_SSM // S5_LANE_BLOCKS
    ub = u_ref[...].astype(BF16)
    sub3 = lax.broadcasted_iota(jnp.int32, (1, 8, 1), 1)
    sub2 = lax.broadcasted_iota(jnp.int32, (8, 1), 0)
    for q in range(S5_LANE_BLOCKS):
        us = [ub[:, s * D_SSM + q * cw: s * D_SSM + (q + 1) * cw] for s in range(S5_STEPS)]
        x = jnp.dot(us[0], wx_ref[0, q], preferred_element_type=F32)
        for s in range(1, S5_STEPS):
            x = x + jnp.dot(us[s], wx_ref[s, q], preferred_element_type=F32)
        lo = q * S5_BLOCK_STATE
        apq = ap_ref[:, lo:lo + S5_BLOCK_STATE]
        if scan:
            xr = x[:, :half].reshape(rows // 8, 8, half)
            xi = x[:, half:].reshape(rows // 8, 8, half)
            for k in (1, 2, 4):
                er = apq[k - 1:k, :half].reshape(1, 1, half)
                ei = apq[k - 1:k, half:].reshape(1, 1, half)
                sr = jnp.where(sub3 >= k, pltpu.roll(xr, k, 1), 0.0)
                si = jnp.where(sub3 >= k, pltpu.roll(xi, k, 1), 0.0)
                xr, xi = xr + er * sr - ei * si, xi + er * si + ei * sr
            hs_ref[:, :half] = xr.reshape(rows, half)
            hs_ref[:, half:] = xi.reshape(rows, half)
            pwr, pwi = apq[:, :half], apq[:, half:]
            c0r = h0_ref[0][:, lo:lo + half]
            c0i = h0_ref[0][:, lo + half:lo + S5_BLOCK_STATE]

            def body(j, carry):
                cr, ci = carry
                sl = pl.ds(pl.multiple_of(j * 8, 8), 8)
                hr = hs_ref[sl, :half] + pwr * cr - pwi * ci
                hi = hs_ref[sl, half:] + pwr * ci + pwi * cr
                hp_ref[sl, :half] = jnp.where(sub2 == 0, cr, pltpu.roll(hr, 1, 0))
                hp_ref[sl, half:] = jnp.where(sub2 == 0, ci, pltpu.roll(hi, 1, 0))
                return hr[7:8, :], hi[7:8, :]

            cr, ci = lax.fori_loop(0, rows // 8, body, (c0r, c0i))
            hl_ref[0, :, lo:lo + half] = cr
            hl_ref[0, :, lo + half:lo + S5_BLOCK_STATE] = ci
            hprev = hp_ref[...].astype(BF16)
        else:
            h0r = h0_ref[:, lo:lo + half]
            h0i = h0_ref[:, lo + half:lo + S5_BLOCK_STATE]
            er, ei = apq[0:1, :half], apq[0:1, half:]
            hl_ref[:, lo:lo + half] = er * h0r - ei * h0i + x[:, :half]
            hl_ref[:, lo + half:lo + S5_BLOCK_STATE] = er * h0i + ei * h0r + x[:, half:]
            hprev = jnp.concatenate([h0r, h0i], axis=1).astype(BF16)
        lhs = jnp.concatenate([hprev] + us, axis=1)
        for j in range(S5_STEPS // 2):
            kk = S5_BLOCK_STATE + cw * (2 * j + 2)
            y2 = jnp.dot(lhs[:, :kk], wy_ref[q, j, :kk, :], preferred_element_type=F32)
            c0 = (2 * j) * D_SSM + q * cw
            c1 = (2 * j + 1) * D_SSM + q * cw
            y_ref[:, c0:c0 + cw] = y2[:, :cw]
            y_ref[:, c1:c1 + cw] = y2[:, cw:]


def _s5(u, h_re, h_im, tables, seq_len):
    wx, wy, ap = tables
    n = u.shape[0]
    nseq = n // seq_len
    u8 = u.reshape(n // S5_STEPS, S5_ROW)
    h0 = _state_to_lanes(h_re, h_im)
    cps = seq_len // S5_STEPS
    scan = cps > 1
    if scan:
        rows = cps
        grid = (nseq,)
        h0 = h0.reshape(nseq, 1, S5_STATE_ROW)
        h_spec = pl.BlockSpec((1, 1, S5_STATE_ROW), lambda i: (i, 0, 0))
        h_shape = jax.ShapeDtypeStruct((nseq, 1, S5_STATE_ROW), F32)
    else:
        rows = min(nseq, 128)
        grid = (nseq // rows,)
        h_spec = pl.BlockSpec((rows, S5_STATE_ROW), lambda i: (i, 0))
        h_shape = jax.ShapeDtypeStruct((nseq, S5_STATE_ROW), F32)
    y8, hl = pl.pallas_call(
        functools.partial(_s5_kernel, rows=rows, scan=scan),
        grid=grid,
        in_specs=[pl.BlockSpec((rows, S5_ROW), lambda i: (i, 0)),
                  h_spec,
                  _const_spec(wx.shape), _const_spec(wy.shape), _const_spec(ap.shape)],
        out_specs=[pl.BlockSpec((rows, S5_ROW), lambda i: (i, 0)), h_spec],
        out_shape=[jax.ShapeDtypeStruct((n // S5_STEPS, S5_ROW), F32), h_shape],
        scratch_shapes=[pltpu.VMEM((rows, S5_BLOCK_STATE), F32),
                        pltpu.VMEM((rows, S5_BLOCK_STATE), F32)],
        compiler_params=pltpu.CompilerParams(dimension_semantics=("arbitrary",),
                                             vmem_limit_bytes=V7X_VMEM_LIMIT_BYTES),
        name="s5",
    )(u8, h0, wx, wy, ap)
    hre_new, him_new = _lanes_to_state(hl.reshape(nseq, S5_STATE_ROW))
    return y8.reshape(n, D_SSM), hre_new, him_new


RW_CHUNK = 64
RW_STEP_CHUNKS = 4
RW_GROUPS = 2
RW_GROUP_HEADS = N_RWKV_HEADS // RW_GROUPS
RW_GROUP_LANES = D_RWKV // RW_GROUPS


def _split3(x):
    hi = x.astype(BF16)
    r1 = x - hi.astype(F32)
    mid = r1.astype(BF16)
    lo = (r1 - mid.astype(F32)).astype(BF16)
    return hi, mid, lo


def _bd(x, hm_ref):
    xb = x.astype(BF16)
    return jnp.concatenate([xb * hm_ref[h] for h in range(RW_GROUP_HEADS)], axis=0)


def _unbd(f, hm_ref):
    n = RWKV_HEAD
    out = f[:n] * hm_ref[0].astype(F32)
    for h in range(1, RW_GROUP_HEADS):
        out = out + f[h * n:(h + 1) * n] * hm_ref[h].astype(F32)
    return out


def _rwkv_kernel(z_ref, s0_ref, w0_ref, w2_ref, a0_ref, a2_ref, g2_ref, kk_ref, ka_ref, rk_ref,
                 lg_ref, lb_ref, ones_ref, hm_ref, cm_ref, tri_ref, seg_ref, y_ref, sl_ref, ypre_ref,
                 *scratch, seg_len, step_chunks):
    C = RW_CHUNK
    N = RWKV_HEAD
    GL = RW_GROUP_LANES
    chained = seg_len == C
    if chained:
        (st_ref,) = scratch
        b = pl.program_id(1)

        @pl.when(b == 0)
        def _():
            st_ref[...] = s0_ref[0]

    ones_bd = ones_ref[...]
    z = z_ref[...]
    r = z[:, :D_RWKV]
    k = z[:, D_RWKV:2 * D_RWKV]
    v = z[:, 2 * D_RWKV:3 * D_RWKV]
    o = 3 * D_RWKV
    xw = z[:, o:o + DECAY_LORA]
    xa = z[:, o + DECAY_LORA:o + DECAY_LORA + AAA_LORA]
    xg = z[:, o + DECAY_LORA + AAA_LORA:]
    wd = -(w0_ref[...] + _mm(jnp.tanh(xw), w2_ref[...]))
    w = -(jnp.maximum(wd, 0.0) + jnp.log1p(jnp.exp(-jnp.abs(wd)))) - 0.5
    lw = -jnp.exp(w)
    a = _sigmoid(a0_ref[...] + _mm(xa, a2_ref[...]))
    out_gate = _mm(_sigmoid(xg), g2_ref[...])
    kk = k * kk_ref[...]
    kk = kk / jnp.maximum(jnp.sqrt(_seg_sum(kk * kk, ones_bd)), L2_EPS)
    kmod = k * (1.0 + (a - 1.0) * ka_ref[...])

    terms = _split3(lw)
    tri, seg = tri_ref[...], seg_ref[...]
    cw = sum(jnp.dot(tri, t, preferred_element_type=F32) for t in terms)
    tot = sum(jnp.dot(seg, t, preferred_element_type=F32) for t in terms)
    w_in = jnp.exp(cw)
    w_tail = jnp.exp(tot - cw)
    w_inv = jnp.exp(-cw)
    w_all = jnp.exp(tot)
    ah = -kk * jnp.exp(cw - lw)
    bh = kk * a * w_inv
    kh = kmod * w_inv
    rh = r * w_in
    bt = kk * a * w_tail
    kt = kmod * w_tail
    strict = cm_ref[0] > 0.0
    incl = cm_ref[1] > 0.0
    eye_cat = cm_ref[2]
    n_fac = seg_len.bit_length() - 1

    items = [(c, g) for c in range(step_chunks) for g in range(RW_GROUPS)]
    sub = lambda t, c, g: t[c * C:(c + 1) * C, g * GL:(g + 1) * GL]
    bd = lambda t: _bd(t, hm_ref)
    A = [sub(ah, c, g) for c, g in items]
    R = [sub(rh, c, g) for c, g in items]
    V = [sub(v, c, g) for c, g in items]
    G = [_mm_nt(jnp.concatenate([A[i], R[i]], axis=0),
                jnp.concatenate([bd(sub(bh, c, g)), bd(sub(kh, c, g))], axis=0))
         for i, (c, g) in enumerate(items)]
    AB = [jnp.where(strict, t[:C, :GL], 0.0) for t in G]
    AK = [jnp.where(strict, t[:C, GL:], 0.0) for t in G]
    RB = [jnp.where(incl, t[C:, :GL], 0.0) for t in G]
    RK = [jnp.where(incl, t[C:, GL:], 0.0) for t in G]
    KV = [_mm(jnp.concatenate([AK[i], RK[i]], axis=0), bd(V[i])) for i in range(len(items))]
    T = [eye_cat + t for t in AB]
    P = [_mm(t, bd(t)) for t in AB]
    for _ in range(n_fac - 2):
        PT = [_mm(jnp.concatenate([P[i], T[i]], axis=0), bd(P[i])) for i in range(len(items))]
        P = [t[:C] for t in PT]
        T = [T[i] + PT[i][C:] for i in range(len(items))]
    T = [T[i] + _mm(T[i], bd(P[i])) for i in range(len(items))]
    X = [_mm(T[i], jnp.concatenate([bd(A[i]), bd(KV[i][:C])], axis=1)) for i in range(len(items))]
    TA = [t[:, :GL] for t in X]
    U0 = [t[:, GL:] for t in X]
    W2 = [_mm(RB[i], jnp.concatenate([bd(TA[i]), bd(U0[i])], axis=1)) for i in range(len(items))]
    Rt = [R[i] + W2[i][:, :GL] for i in range(len(items))]
    Y0 = [W2[i][:, GL:] + KV[i][C:] for i in range(len(items))]
    nseg = C // seg_len
    Phi, Psi = {}, {}
    for i, (c, g) in enumerate(items):
        Bt, Kt, Wa = sub(bt, c, g), sub(kt, c, g), sub(w_all, c, g)
        for s in range(nseg):
            ss = slice(s * seg_len, (s + 1) * seg_len)
            Phi[i, s] = (eye_cat * Wa[s * seg_len:s * seg_len + 1, :]
                         + _unbd(_mm_tn(Bt[ss], TA[i][ss]), hm_ref))
            Psi[i, s] = _unbd(_mm_tn(jnp.concatenate([Bt[ss], Kt[ss]], axis=0),
                                     jnp.concatenate([U0[i][ss], V[i][ss]], axis=0)), hm_ref)
    if chained:
        states = [st_ref[g] for g in range(RW_GROUPS)]
    for i, (c, g) in enumerate(items):
        for s in range(nseg):
            ss = slice(s * seg_len, (s + 1) * seg_len)
            St = states[g] if chained else s0_ref[c * nseg + s, g]
            out = _mm(jnp.concatenate([Rt[i][ss], Phi[i, s]], axis=0), bd(St))
            ypre_ref[c * C + s * seg_len:c * C + (s + 1) * seg_len, g * GL:(g + 1) * GL] = (
                out[:seg_len] + Y0[i][ss])
            St = out[seg_len:] + Psi[i, s]
            if chained:
                states[g] = St
            else:
                sl_ref[c * nseg + s, g] = St
    if chained:
        for g in range(RW_GROUPS):
            st_ref[g] = states[g]
            sl_ref[0, g] = states[g]
    y = ypre_ref[...]

    mu = _seg_sum(y, ones_bd) * (1.0 / N)
    d = y - mu
    var = _seg_sum(d * d, ones_bd) * (1.0 / N)
    yn = d * lax.rsqrt(var + GN_EPS) * lg_ref[...] + lb_ref[...]
    bonus = _seg_sum(r * kmod * rk_ref[...], ones_bd) * v
    y_ref[...] = (yn + bonus) * out_gate


def _rwkv_constants(seg_len, rows):
    n, gh, gl = RWKV_HEAD, RW_GROUP_HEADS, RW_GROUP_LANES
    lane_head = np.arange(gl) // n
    hm = (lane_head[None, None, :] == np.arange(gh)[:, None, None]) * np.ones((gh, n, gl))
    i = np.arange(RW_CHUNK)[:, None]
    j = (np.arange(gl) % n)[None, :]
    same = (i // seg_len) == (j // seg_len)
    cm = np.stack([(j < i) & same, (j <= i) & same, j == i]).astype(np.float32)
    ri = np.arange(rows)[:, None]
    rj = np.arange(rows)[None, :]
    same_r = (ri // seg_len) == (rj // seg_len)
    tri = ((rj <= ri) & same_r).astype(np.float32)
    ones_bd = np.kron(np.eye(N_RWKV_HEADS), np.ones((n, n)))
    return (jnp.asarray(ones_bd, BF16), jnp.asarray(hm, BF16), jnp.asarray(cm, F32),
            jnp.asarray(tri, BF16), jnp.asarray(same_r.astype(np.float32), BF16))


def _rwkv(z, s0, p, seq_len):
    n = z.shape[0]
    step_chunks = min(RW_STEP_CHUNKS, (seq_len if seq_len >= RW_CHUNK else n) // RW_CHUNK)
    nseq = n // seq_len
    nh, gh, hd = N_RWKV_HEADS, RW_GROUP_HEADS, RWKV_HEAD
    seg_len = min(seq_len, RW_CHUNK)
    rows = RW_CHUNK * step_chunks
    chained = seq_len >= RW_CHUNK
    s0 = jnp.transpose(s0.reshape(nseq, RW_GROUPS, gh, hd, hd), (0, 1, 4, 2, 3))
    s0 = s0.reshape(nseq, RW_GROUPS, hd, RW_GROUP_LANES)
    if chained:
        steps = seq_len // rows
        grid = (nseq, steps)
        row_map = lambda s, b: (s * steps + b, 0)
        state_spec = pl.BlockSpec((1, RW_GROUPS, hd, RW_GROUP_LANES), lambda s, b: (s, 0, 0, 0))
        scratch = [pltpu.VMEM((rows, D_RWKV), F32), pltpu.VMEM((RW_GROUPS, hd, RW_GROUP_LANES), F32)]
    else:
        grid = (n // rows, 1)
        row_map = lambda s, b: (s, 0)
        state_spec = pl.BlockSpec((rows // seq_len, RW_GROUPS, hd, RW_GROUP_LANES),
                                  lambda s, b: (s, 0, 0, 0))
        scratch = [pltpu.VMEM((rows, D_RWKV), F32)]
    small = [p['w0'], p['w2'], p['a0'], p['a2'], p['g2'], p['k_k'], p['k_a'], p['r_k'],
             p['lnx_g'], p['lnx_b'], *_rwkv_constants(seg_len, rows)]
    y, s_new = pl.pallas_call(
        functools.partial(_rwkv_kernel, seg_len=seg_len, step_chunks=step_chunks),
        grid=grid,
        in_specs=[pl.BlockSpec((rows, N_SHIFT), row_map), state_spec]
        + [_const_spec(t.shape) for t in small],
        out_specs=[pl.BlockSpec((rows, D_RWKV), row_map), state_spec],
        out_shape=[jax.ShapeDtypeStruct((n, D_RWKV), F32),
                   jax.ShapeDtypeStruct((nseq, RW_GROUPS, hd, RW_GROUP_LANES), F32)],
        scratch_shapes=scratch,
        compiler_params=pltpu.CompilerParams(dimension_semantics=("arbitrary", "arbitrary"),
                                             vmem_limit_bytes=V7X_VMEM_LIMIT_BYTES),
        name="rwkv7",
    )(z, s0, *small)
    s_new = jnp.transpose(s_new.reshape(nseq, RW_GROUPS, hd, gh, hd), (0, 1, 3, 4, 2))
    return y, s_new.reshape(nseq, nh, hd, hd)


def _ffn_kernel(h_ref, y8_ref, u_ref, yb_ref, pe_ref, c1_ref, c2_ref, d_ref, wglu_ref, bglu_ref,
                wout_ref, g2_ref, wup_ref, cw_ref, cb_ref, wdn_ref, wple_ref, wpg_ref, gf_ref,
                o_ref, *rest, seq_len, tm, final):
    i = pl.program_id(0)
    long_seq = seq_len >= tm
    if long_seq:
        ga_ref, gb_ref, carry_ref = rest

        @pl.when(i == 0)
        def _():
            carry_ref[...] = jnp.zeros_like(carry_ref)
    else:
        (gate_ref,) = rest
    ya = y8_ref[...] + d_ref[...] * u_ref[...]
    c_gelu = math.sqrt(2.0 / math.pi)
    ya = ya * (0.5 * (1.0 + jnp.tanh(c_gelu * (ya + 0.044715 * (ya * ya * ya)))))
    ya = ya * _sigmoid(_mm(ya, wglu_ref[...]) + bglu_ref[...])
    h1 = (h_ref[...] + _mm(ya, wout_ref[:D_SSM, :]) + _mm(yb_ref[...], wout_ref[D_SSM:, :]))
    x2 = _rmsnorm(h1, g2_ref[...]).astype(BF16)
    row = lax.broadcasted_iota(jnp.int32, (tm, 1), 0)
    if long_seq:
        first = (i % (seq_len // tm)) == 0
    else:
        t = row % seq_len
    acc = jnp.zeros((tm, D_MODEL), F32)
    for c in range(D_FF // FF_CHUNK):
        cs = slice(c * FF_CHUNK, (c + 1) * FF_CHUNK)
        gs = slice(D_FF + c * FF_CHUNK, D_FF + (c + 1) * FF_CHUNK)
        val = jnp.dot(x2, wup_ref[:, cs], preferred_element_type=F32)
        gate = jnp.dot(x2, wup_ref[:, gs], preferred_element_type=F32)
        r1 = pltpu.roll(gate, 1, 0)
        r2 = pltpu.roll(gate, 2, 0)
        if long_seq:
            m1 = jnp.where(first, c1_ref[0][:, cs], carry_ref[1:2, cs])
            m2 = jnp.where(first, c2_ref[0][:, cs], carry_ref[0:1, cs])
            p1 = jnp.where(row == 0, m1, r1)
            p2 = jnp.where(row == 0, m2, jnp.where(row == 1, m1, r2))
            carry_ref[:, cs] = gate[tm - 2:tm, :]
            ga_ref[0, :, cs] = gate[tm - 2:tm - 1, :]
            gb_ref[0, :, cs] = gate[tm - 1:tm, :]
        else:
            p1 = jnp.where(t == 0, c1_ref[:, cs], r1)
            p2 = jnp.where(t <= 1, c2_ref[:, cs], r2)
            gate_ref[:, cs] = gate
        conv = (cb_ref[:, cs] + cw_ref[2:3, cs] * gate + cw_ref[1:2, cs] * p1 + cw_ref[0:1, cs] * p2)
        act = conv * _sigmoid(conv) * val
        acc = acc + jnp.dot(act.astype(BF16), wdn_ref[cs, :], preferred_element_type=F32)
    h2 = h1 + acc
    h3 = h2 + _mm(pe_ref[...], wple_ref[...]) * _sigmoid(_mm(h2, wpg_ref[...]))
    if final:
        h3 = _rmsnorm(h3, gf_ref[...])
    o_ref[...] = h3


def _ffn(h, y8, u, yb, pe, conv0, p, seq_len, tm, final):
    n = h.shape[0]
    nseq = n // seq_len
    row_spec = lambda w: pl.BlockSpec((tm, w), lambda i: (i, 0))
    if seq_len >= tm:
        tps = seq_len // tm
        seq_spec = pl.BlockSpec((1, 1, D_FF), lambda i: (i // tps, 0, 0))
        c1 = conv0[:, 1].reshape(nseq, 1, D_FF)
        c2 = conv0[:, 0].reshape(nseq, 1, D_FF)
        c_specs = [seq_spec, seq_spec]
        g_shapes = [jax.ShapeDtypeStruct((nseq, 1, D_FF), F32)] * 2
        g_specs = [seq_spec, seq_spec]
    else:
        zeros = jnp.zeros((nseq, seq_len, D_FF), F32)
        c1 = zeros.at[:, 0].set(conv0[:, 1]).reshape(n, D_FF)
        c2 = zeros.at[:, 0].set(conv0[:, 0]).at[:, 1].set(conv0[:, 1]).reshape(n, D_FF)
        c_specs = [row_spec(D_FF), row_spec(D_FF)]
        g_shapes = [jax.ShapeDtypeStruct((n, D_FF), F32)]
        g_specs = [row_spec(D_FF)]
    weights = [p['d'], p['w_glu'], p['b_glu'], p['w_out'], p['norm2_g'], p['w_up'], p['conv_w'],
               p['conv_b'], p['w_down'], p['w_ple'], p['w_pg'], p['final_g']]
    out, *gates = pl.pallas_call(
        functools.partial(_ffn_kernel, seq_len=seq_len, tm=tm, final=final),
        grid=(n // tm,),
        in_specs=[row_spec(D_MODEL), row_spec(D_SSM), row_spec(D_SSM), row_spec(D_RWKV),
                  row_spec(D_PLE)] + c_specs + [_const_spec(t.shape) for t in weights],
        out_specs=[row_spec(D_MODEL)] + g_specs,
        out_shape=[jax.ShapeDtypeStruct((n, D_MODEL), F32)] + g_shapes,
        scratch_shapes=[pltpu.VMEM((2, D_FF), F32)] if seq_len >= tm else [],
        compiler_params=pltpu.CompilerParams(dimension_semantics=("arbitrary",),
                                             vmem_limit_bytes=V7X_VMEM_LIMIT_BYTES),
        name="ffn",
    )(h, y8, u, yb, pe, c1, c2, *weights)
    if seq_len >= tm:
        conv_new = jnp.concatenate(gates, axis=1)
    else:
        conv_new = gates[0].reshape(nseq, seq_len, D_FF)[:, seq_len - 2:]
    return out, conv_new


def _layer(h, pe, st, p, s5_tables, seq_len, tm, final):
    ssm_re0, ssm_im0, rwkv0, shift0, conv0 = st
    u, z, shift_new = _inproj(h, p['norm1_g'], p['w_in'], p['shift_mu'], shift0, seq_len, tm)
    y8, hre, him = _s5(u, ssm_re0, ssm_im0, s5_tables, seq_len)
    yb, s_last = _rwkv(z, rwkv0, p, seq_len)
    h, conv_new = _ffn(h, y8, u, yb, pe, conv0, p, seq_len, tm, final)
    return h, (hre, him, s_last, shift_new, conv_new)


def _layer_params(i, w):
    row = lambda t: t[i].reshape(1, -1).astype(F32)
    bf = lambda t: t[i].astype(BF16)
    return {
        'norm1_g': row(w['norm1_g']), 'w_in': bf(w['w_in']), 'shift_mu': row(w['shift_mu']),
        'd': row(w['ssm_d']), 'w_glu': bf(w['ssm_w_glu']), 'b_glu': row(w['ssm_b_glu']),
        'w0': row(w['rwkv_w0']), 'w2': bf(w['rwkv_w2']), 'a0': row(w['rwkv_a0']),
        'a2': bf(w['rwkv_a2']), 'g2': bf(w['rwkv_g2']), 'k_k': row(w['rwkv_k_k']),
        'k_a': row(w['rwkv_k_a']), 'r_k': row(w['rwkv_r_k']), 'lnx_g': row(w['rwkv_lnx_g']),
        'lnx_b': row(w['rwkv_lnx_b']), 'w_out': bf(w['w_out']), 'norm2_g': row(w['norm2_g']),
        'w_up': bf(w['w_up']), 'conv_w': w['conv_w'][i].astype(F32), 'conv_b': row(w['conv_b']),
        'w_down': bf(w['w_down']), 'w_ple': bf(w['w_ple']), 'w_pg': bf(w['w_pg']),
        'final_g': w['final_g'].reshape(1, -1).astype(F32),
    }


def _forward(x_prompt, x_sample, p_prompt, p_sample, state_ssm_re, state_ssm_im, state_rwkv,
             state_shift, state_conv, w, tm_prompt, tm_sample):
    depth = w['w_in'].shape[0]
    bp, lp, _ = x_prompt.shape
    bs, ls, _ = x_sample.shape
    hp = x_prompt.reshape(bp * lp, D_MODEL).astype(F32)
    hs = x_sample.reshape(bs * ls, D_MODEL).astype(F32)
    zero_st = (jnp.zeros((bp, N_SSM_GROUPS, SSM_STATE), F32),
               jnp.zeros((bp, N_SSM_GROUPS, SSM_STATE), F32),
               jnp.zeros((bp, N_RWKV_HEADS, RWKV_HEAD, RWKV_HEAD), F32),
               jnp.zeros((bp, N_SHIFT), F32),
               jnp.zeros((bp, 2, D_FF), F32))
    new_p = [[] for _ in range(5)]
    new_s = [[] for _ in range(5)]
    for i in range(depth):
        p = _layer_params(i, w)
        tables = _s5_tables(w['ssm_lam_re'][i], w['ssm_lam_im'][i], w['ssm_log_dt'][i],
                            w['ssm_b_re'][i], w['ssm_b_im'][i], w['ssm_c_re'][i], w['ssm_c_im'][i])
        final = i == depth - 1
        hp, stp = _layer(hp, p_prompt[i].reshape(bp * lp, D_PLE).astype(F32), zero_st, p, tables,
                         lp, tm_prompt, final)
        st_in = (state_ssm_re[i].astype(F32), state_ssm_im[i].astype(F32),
                 state_rwkv[i].astype(F32), state_shift[i].astype(F32), state_conv[i].astype(F32))
        hs, sts = _layer(hs, p_sample[i].reshape(bs * ls, D_PLE).astype(F32), st_in, p, tables,
                         ls, tm_sample, final)
        for j in range(5):
            new_p[j].append(stp[j])
            new_s[j].append(sts[j])
    y_prompt = hp.reshape(bp, lp, D_MODEL).astype(x_prompt.dtype)
    y_sample = hs.reshape(bs, ls, D_MODEL).astype(x_sample.dtype)
    dts = (state_ssm_re.dtype, state_ssm_im.dtype, state_rwkv.dtype, state_shift.dtype,
           state_conv.dtype)
    outs_p = tuple(jnp.stack(new_p[j]).astype(dts[j]) for j in range(5))
    outs_s = tuple(jnp.stack(new_s[j]).astype(dts[j]) for j in range(5))
    return (y_prompt, y_sample) + outs_p + outs_s


def kernel(x_prompt, x_sample, p_prompt, p_sample, state_ssm_re, state_ssm_im, state_rwkv, state_shift, state_conv, norm1_g, w_in, shift_mu, ssm_lam_re, ssm_lam_im, ssm_log_dt, ssm_b_re, ssm_b_im, ssm_c_re, ssm_c_im, ssm_d, ssm_w_glu, ssm_b_glu, rwkv_w0, rwkv_w2, rwkv_a0, rwkv_a2, rwkv_g2, rwkv_k_k, rwkv_k_a, rwkv_r_k, rwkv_lnx_g, rwkv_lnx_b, w_out, norm2_g, w_up, conv_w, conv_b, w_down, w_ple, w_pg, final_g):
    w = dict(norm1_g=norm1_g, w_in=w_in, shift_mu=shift_mu, ssm_lam_re=ssm_lam_re,
             ssm_lam_im=ssm_lam_im, ssm_log_dt=ssm_log_dt, ssm_b_re=ssm_b_re, ssm_b_im=ssm_b_im,
             ssm_c_re=ssm_c_re, ssm_c_im=ssm_c_im, ssm_d=ssm_d, ssm_w_glu=ssm_w_glu,
             ssm_b_glu=ssm_b_glu, rwkv_w0=rwkv_w0, rwkv_w2=rwkv_w2, rwkv_a0=rwkv_a0,
             rwkv_a2=rwkv_a2, rwkv_g2=rwkv_g2, rwkv_k_k=rwkv_k_k, rwkv_k_a=rwkv_k_a,
             rwkv_r_k=rwkv_r_k, rwkv_lnx_g=rwkv_lnx_g, rwkv_lnx_b=rwkv_lnx_b, w_out=w_out,
             norm2_g=norm2_g, w_up=w_up, conv_w=conv_w, conv_b=conv_b, w_down=w_down,
             w_ple=w_ple, w_pg=w_pg, final_g=final_g)
    lp = x_prompt.shape[1]
    ns = x_sample.shape[0] * x_sample.shape[1]
    return _forward(x_prompt, x_sample, p_prompt, p_sample, state_ssm_re, state_ssm_im,
                    state_rwkv, state_shift, state_conv, w,
                    tm_prompt=min(512, lp), tm_sample=min(256, ns))
```

```python
import functools
import math

import numpy as np
import jax
import jax.numpy as jnp
from jax import lax
from jax.experimental import pallas as pl
from jax.experimental.pallas import tpu as pltpu

F32 = jnp.float32
BF16 = jnp.bfloat16
HIGHEST = lax.Precision.HIGHEST

D_MODEL = 1024
D_SSM = 512
D_RWKV = 512
SSM_GROUP = 16
N_SSM_GROUPS = 32
SSM_STATE = 64
RWKV_HEAD = 64
N_RWKV_HEADS = 8
DECAY_LORA = 64
AAA_LORA = 64
GATE_LORA = 128
N_SHIFT = 3 * D_RWKV + DECAY_LORA + AAA_LORA + GATE_LORA
N_IN = D_SSM + N_SHIFT
D_FF = 2816
D_PLE = 256
RMS_EPS = 1e-6
GN_EPS = 64e-5
L2_EPS = 1e-12

LANES = 128
FOLD_LANE_BLOCKS = D_SSM // LANES
S5_STEPS = 8
S5_ROW = S5_STEPS * D_SSM
S5_LANE_BLOCKS = 4
S5_BLOCK_STATE = 2 * 8 * SSM_STATE
S5_STATE_ROW = S5_LANE_BLOCKS * S5_BLOCK_STATE
FF_CHUNK = 256
V7X_VMEM_LIMIT_BYTES = 56 * 1024 * 1024


def _mm(a, b):
    return jnp.dot(a.astype(BF16), b.astype(BF16), preferred_element_type=F32)


def _mm_nt(a, b):
    return lax.dot_general(a.astype(BF16), b.astype(BF16), (((1,), (1,)), ((), ())),
                           preferred_element_type=F32)


def _mm_tn(a, b):
    return lax.dot_general(a.astype(BF16), b.astype(BF16), (((0,), (0,)), ((), ())),
                           preferred_element_type=F32)


def _seg_sum(x, ones_bd):
    hi = x.astype(BF16)
    lo = (x - hi.astype(F32)).astype(BF16)
    return (jnp.dot(hi, ones_bd, preferred_element_type=F32)
            + jnp.dot(lo, ones_bd, preferred_element_type=F32))


def _rmsnorm(x, g):
    return x * lax.rsqrt(jnp.mean(x * x, axis=-1, keepdims=True) + RMS_EPS) * g


def _sigmoid(x):
    return 1.0 / (1.0 + jnp.exp(-x))


def _const_spec(shape):
    nd = len(shape)
    return pl.BlockSpec(shape, lambda *_: (0,) * nd, pipeline_mode=pl.Buffered(1))


def _exact_rows(sel, x):
    return sum(jnp.dot(sel, t, preferred_element_type=F32) for t in _split3(x))


def _fold_rows(x, scr_ref, out_ref):
    tm = x.shape[0]
    for q in range(FOLD_LANE_BLOCKS):
        scr_ref[q] = x[:, q * LANES:(q + 1) * LANES]
    for s in range(S5_STEPS):
        for q in range(FOLD_LANE_BLOCKS):
            c0 = s * D_SSM + q * LANES
            out_ref[:, c0:c0 + LANES] = scr_ref[q, pl.ds(s, tm // S5_STEPS, stride=S5_STEPS), :]


def _unfold_rows(x8, scr_ref):
    tm = x8.shape[0] * S5_STEPS
    for s in range(S5_STEPS):
        for q in range(FOLD_LANE_BLOCKS):
            c0 = s * D_SSM + q * LANES
            scr_ref[q, pl.ds(s, tm // S5_STEPS, stride=S5_STEPS), :] = x8[:, c0:c0 + LANES]
    return jnp.concatenate([scr_ref[q] for q in range(FOLD_LANE_BLOCKS)], axis=1)


def _seq_selectors(seq_len, tm):
    r = np.arange(tm)
    b = np.arange(tm // seq_len)
    expand = (r[:, None] // seq_len == b[None, :]).astype(np.float32)
    last = (r[None, :] == (b[:, None] + 1) * seq_len - 1).astype(np.float32)
    prev = (r[None, :] == (b[:, None] + 1) * seq_len - 2).astype(np.float32)
    return jnp.asarray(expand, BF16), jnp.asarray(last, BF16), jnp.asarray(prev, BF16)


def _inproj_kernel(h_ref, g_ref, w_ref, mu_ref, init_ref, *rest, seq_len, tm):
    i = pl.program_id(0)
    long_seq = seq_len >= tm
    if long_seq:
        u8_ref, z_ref, last_ref, fold_ref, carry_ref = rest

        @pl.when(i == 0)
        def _():
            carry_ref[...] = jnp.zeros_like(carry_ref)
    else:
        expand_ref, sel_last_ref, u8_ref, z_ref, last_ref, fold_ref = rest
    xn = _rmsnorm(h_ref[...], g_ref[...])
    proj = jnp.dot(xn.astype(BF16), w_ref[...], preferred_element_type=F32)
    _fold_rows(proj[:, :D_SSM], fold_ref, u8_ref)
    zr = proj[:, D_SSM:]
    rolled = pltpu.roll(zr, 1, 0)
    row = lax.broadcasted_iota(jnp.int32, (tm, 1), 0)
    if long_seq:
        first = (i % (seq_len // tm)) == 0
        row0 = jnp.where(first, init_ref[0], carry_ref[...])
        prev = jnp.where(row == 0, row0, rolled)
        carry_ref[...] = zr[tm - 1:tm, :]
        last_ref[0] = zr[tm - 1:tm, :]
    else:
        prev = jnp.where(row % seq_len == 0, _exact_rows(expand_ref[...], init_ref[...]), rolled)
        last_ref[...] = _exact_rows(sel_last_ref[...], zr)
    z_ref[...] = zr + (prev - zr) * mu_ref[...]


def _inproj(h, g1, w_in, mu, shift0, seq_len, tm):
    n = h.shape[0]
    nseq = n // seq_len
    fold_scratch = pltpu.VMEM((FOLD_LANE_BLOCKS, tm, LANES), F32)
    if seq_len >= tm:
        tps = seq_len // tm
        seq_spec = pl.BlockSpec((1, 1, N_SHIFT), lambda i: (i // tps, 0, 0))
        extra, extra_specs = [], []
        init = shift0.reshape(nseq, 1, N_SHIFT)
        last_shape = jax.ShapeDtypeStruct((nseq, 1, N_SHIFT), F32)
        scratch = [fold_scratch, pltpu.VMEM((1, N_SHIFT), F32)]
    else:
        spt = tm // seq_len
        seq_spec = pl.BlockSpec((spt, N_SHIFT), lambda i: (i, 0))
        expand, sel_last, _ = _seq_selectors(seq_len, tm)
        extra = [expand, sel_last]
        extra_specs = [_const_spec(expand.shape), _const_spec(sel_last.shape)]
        init = shift0
        last_shape = jax.ShapeDtypeStruct((nseq, N_SHIFT), F32)
        scratch = [fold_scratch]
    u8, z, last = pl.pallas_call(
        functools.partial(_inproj_kernel, seq_len=seq_len, tm=tm),
        grid=(n // tm,),
        in_specs=[pl.BlockSpec((tm, D_MODEL), lambda i: (i, 0)),
                  _const_spec((1, D_MODEL)),
                  _const_spec((D_MODEL, N_IN)),
                  _const_spec((1, N_SHIFT)),
                  seq_spec] + extra_specs,
        out_specs=[pl.BlockSpec((tm // S5_STEPS, S5_ROW), lambda i: (i, 0)),
                   pl.BlockSpec((tm, N_SHIFT), lambda i: (i, 0)),
                   seq_spec],
        out_shape=[jax.ShapeDtypeStruct((n // S5_STEPS, S5_ROW), F32),
                   jax.ShapeDtypeStruct((n, N_SHIFT), F32),
                   last_shape],
        scratch_shapes=scratch,
        compiler_params=pltpu.CompilerParams(dimension_semantics=("arbitrary",),
                                             vmem_limit_bytes=V7X_VMEM_LIMIT_BYTES),
        name="inproj",
    )(h, g1, w_in, mu, init, *extra)
    return u8, z, last.reshape(nseq, N_SHIFT)


def _s5_prep_kernel(lr_ref, li_ref, ldt_ref, br_ref, bi_ref, cr_ref, ci_ref,
                    wxr_ref, wxi_ref, mre_ref, mim_ref, kt_ref, apr_ref, api_ref):
    lr, li = lr_ref[...], li_ref[...]
    dt = jnp.exp(ldt_ref[...])
    mag = jnp.exp(lr * dt)
    ar = mag * jnp.cos(li * dt)
    ai = mag * jnp.sin(li * dt)
    den = lr * lr + li * li
    nr = ar - 1.0
    fr = (nr * lr + ai * li) / den
    fi = (ai * lr - nr * li) / den
    br, bi = br_ref[...], bi_ref[...]
    bbr = fr * br - fi * bi
    bbi = fr * bi + fi * br
    cr, ci = cr_ref[...], ci_ref[...]
    pr = [jnp.ones_like(ar)]
    pi = [jnp.zeros_like(ar)]
    for _ in range(S5_STEPS):
        pr.append(pr[-1] * ar - pi[-1] * ai)
        pi.append(pr[-2] * ai + pi[-1] * ar)
    nrow = lr.shape[0]
    rg = lax.broadcasted_iota(jnp.int32, (nrow, nrow), 0) // SSM_GROUP
    cg = lax.broadcasted_iota(jnp.int32, (nrow, nrow), 1) // SSM_GROUP
    same_group = rg == cg
    nt = (((1,), (1,)), ((), ()))
    for s in range(S5_STEPS):
        qr, qi = pr[S5_STEPS - 1 - s], pi[S5_STEPS - 1 - s]
        wr = qr * bbr - qi * bbi
        wi = qr * bbi + qi * bbr
        wxr_ref[s] = wr
        wxi_ref[s] = wi
        kt = (lax.dot_general(wr, cr, nt, precision=HIGHEST, preferred_element_type=F32)
              - lax.dot_general(wi, ci, nt, precision=HIGHEST, preferred_element_type=F32))
        kt_ref[S5_STEPS - 1 - s] = jnp.where(same_group, kt, 0.0)
        mre_ref[s] = cr * pr[s + 1] - ci * pi[s + 1]
        mim_ref[s] = cr * pi[s + 1] + ci * pr[s + 1]
    a8r, a8i = pr[S5_STEPS], pi[S5_STEPS]
    er, ei = a8r, a8i
    for n in range(S5_STEPS):
        apr_ref[n] = er
        api_ref[n] = ei
        er, ei = er * a8r - ei * a8i, er * a8i + ei * a8r


def _s5_tables(lam_re, lam_im, log_dt, b_re, b_im, c_re, c_im):
    G, P, K = N_SSM_GROUPS, SSM_STATE, SSM_GROUP
    rep = lambda t: jnp.repeat(t, K, axis=0)
    args = (rep(lam_re), rep(lam_im), rep(jnp.broadcast_to(log_dt[:, None], (G, P))),
            jnp.swapaxes(b_re, 1, 2).reshape(G * K, P), jnp.swapaxes(b_im, 1, 2).reshape(G * K, P),
            c_re.reshape(G * K, P), c_im.reshape(G * K, P))
    t3 = jax.ShapeDtypeStruct((S5_STEPS, G * K, P), F32)
    wxr, wxi, mre, mim, kt, apr, api = pl.pallas_call(
        _s5_prep_kernel,
        out_shape=[t3, t3, t3, t3, jax.ShapeDtypeStruct((S5_STEPS, G * K, G * K), F32), t3, t3],
        name="s5_prep",
    )(*args)
    Q, GB = S5_LANE_BLOCKS, G // S5_LANE_BLOCKS
    eye = jnp.eye(GB, dtype=F32)
    five = lambda t: t.reshape(S5_STEPS, Q, GB, K, P)
    wx = jnp.concatenate(
        [jnp.einsum('sqgcp,gh->sqgchp', five(t), eye).reshape(S5_STEPS, Q, GB * K, GB * P)
         for t in (wxr, wxi)], axis=-1).astype(BF16)
    st = jnp.concatenate(
        [jnp.einsum('sqgcp,gh->sqhpgc', five(t), eye).reshape(S5_STEPS, Q, GB * P, GB * K)
         for t in (mre, -mim)], axis=2)
    ktd = jnp.stack([kt[:, q * GB * K:(q + 1) * GB * K, q * GB * K:(q + 1) * GB * K]
                     for q in range(Q)], axis=1)
    tau = np.arange(S5_STEPS)[None, :] - np.arange(S5_STEPS)[:, None]
    intra = ktd[np.clip(tau, 0, None)] * jnp.asarray(tau >= 0, F32)[:, :, None, None, None]
    intra = jnp.transpose(intra, (1, 2, 0, 3, 4))
    intra = intra.reshape(S5_STEPS, Q, S5_STEPS * GB * K, GB * K)
    wy = jnp.concatenate([st, intra], axis=2)
    wy = jnp.transpose(wy.reshape(S5_STEPS // 2, 2, Q, 2 * GB * P + S5_STEPS * GB * K, GB * K),
                       (2, 0, 3, 1, 4))
    wy = wy.reshape(Q, S5_STEPS // 2, 2 * GB * P + S5_STEPS * GB * K, 2 * GB * K).astype(BF16)
    ap = jnp.concatenate([t[:, ::K, :].reshape(S5_STEPS, Q, GB * P) for t in (apr, api)], axis=-1)
    return wx, wy, ap.reshape(S5_STEPS, S5_STATE_ROW)


def _state_to_lanes(h_re, h_im):
    n = h_re.shape[0]
    parts = [t.reshape(n, S5_LANE_BLOCKS, -1) for t in (h_re, h_im)]
    return jnp.concatenate(parts, axis=-1).reshape(n, S5_STATE_ROW)


def _lanes_to_state(h):
    n = h.shape[0]
    h = h.reshape(n, S5_LANE_BLOCKS, 2, N_SSM_GROUPS // S5_LANE_BLOCKS, SSM_STATE)
    return (h[:, :, 0].reshape(n, N_SSM_GROUPS, SSM_STATE), h[:, :, 1].reshape(n, N_SSM_GROUPS, SSM_STATE))


def _s5_kernel(u_ref, h0_ref, wx_ref, wy_ref, ap_ref, y_ref, hl_ref, hs_ref, hp_ref, *, rows, scan):
    half = S5_BLOCK_STATE // 2
    cw = D_SSM // S5_LANE_BLOCKS
    ub = u_ref[...].astype(BF16)
    sub3 = lax.broadcasted_iota(jnp.int32, (1, 8, 1), 1)
    sub2 = lax.broadcasted_iota(jnp.int32, (8, 1), 0)
    for q in range(S5_LANE_BLOCKS):
        us = [ub[:, s * D_SSM + q * cw: s * D_SSM + (q + 1) * cw] for s in range(S5_STEPS)]
        x = jnp.dot(us[0], wx_ref[0, q], preferred_element_type=F32)
        for s in range(1, S5_STEPS):
            x = x + jnp.dot(us[s], wx_ref[s, q], preferred_element_type=F32)
        lo = q * S5_BLOCK_STATE
        apq = ap_ref[:, lo:lo + S5_BLOCK_STATE]
        if scan:
            xr = x[:, :half].reshape(rows // 8, 8, half)
            xi = x[:, half:].reshape(rows // 8, 8, half)
            for k in (1, 2, 4):
                er = apq[k - 1:k, :half].reshape(1, 1, half)
                ei = apq[k - 1:k, half:].reshape(1, 1, half)
                sr = jnp.where(sub3 >= k, pltpu.roll(xr, k, 1), 0.0)
                si = jnp.where(sub3 >= k, pltpu.roll(xi, k, 1), 0.0)
                xr, xi = xr + er * sr - ei * si, xi + er * si + ei * sr
            hs_ref[:, :half] = xr.reshape(rows, half)
            hs_ref[:, half:] = xi.reshape(rows, half)
            pwr, pwi = apq[:, :half], apq[:, half:]
            c0r = h0_ref[0][:, lo:lo + half]
            c0i = h0_ref[0][:, lo + half:lo + S5_BLOCK_STATE]

            def body(j, carry):
                cr, ci = carry
                sl = pl.ds(pl.multiple_of(j * 8, 8), 8)
                hr = hs_ref[sl, :half] + pwr * cr - pwi * ci
                hi = hs_ref[sl, half:] + pwr * ci + pwi * cr
                hp_ref[sl, :half] = jnp.where(sub2 == 0, cr, pltpu.roll(hr, 1, 0))
                hp_ref[sl, half:] = jnp.where(sub2 == 0, ci, pltpu.roll(hi, 1, 0))
                return hr[7:8, :], hi[7:8, :]

            cr, ci = lax.fori_loop(0, rows // 8, body, (c0r, c0i))
            hl_ref[0, :, lo:lo + half] = cr
            hl_ref[0, :, lo + half:lo + S5_BLOCK_STATE] = ci
            hprev = hp_ref[...].astype(BF16)
        else:
            h0r = h0_ref[:, lo:lo + half]
            h0i = h0_ref[:, lo + half:lo + S5_BLOCK_STATE]
            er, ei = apq[0:1, :half], apq[0:1, half:]
            hl_ref[:, lo:lo + half] = er * h0r - ei * h0i + x[:, :half]
            hl_ref[:, lo + half:lo + S5_BLOCK_STATE] = er * h0i + ei * h0r + x[:, half:]
            hprev = jnp.concatenate([h0r, h0i], axis=1).astype(BF16)
        lhs = jnp.concatenate([hprev] + us, axis=1)
        for j in range(S5_STEPS // 2):
            kk = S5_BLOCK_STATE + cw * (2 * j + 2)
            y2 = jnp.dot(lhs[:, :kk], wy_ref[q, j, :kk, :], preferred_element_type=F32)
            c0 = (2 * j) * D_SSM + q * cw
            c1 = (2 * j + 1) * D_SSM + q * cw
            y_ref[:, c0:c0 + cw] = y2[:, :cw]
            y_ref[:, c1:c1 + cw] = y2[:, cw:]


def _s5(u8, h_re, h_im, tables, seq_len):
    wx, wy, ap = tables
    n = u8.shape[0] * S5_STEPS
    nseq = n // seq_len
    h0 = _state_to_lanes(h_re, h_im)
    cps = seq_len // S5_STEPS
    scan = cps > 1
    if scan:
        rows = cps
        grid = (nseq,)
        h0 = h0.reshape(nseq, 1, S5_STATE_ROW)
        h_spec = pl.BlockSpec((1, 1, S5_STATE_ROW), lambda i: (i, 0, 0))
        h_shape = jax.ShapeDtypeStruct((nseq, 1, S5_STATE_ROW), F32)
    else:
        rows = min(nseq, 128)
        grid = (nseq // rows,)
        h_spec = pl.BlockSpec((rows, S5_STATE_ROW), lambda i: (i, 0))
        h_shape = jax.ShapeDtypeStruct((nseq, S5_STATE_ROW), F32)
    y8, hl = pl.pallas_call(
        functools.partial(_s5_kernel, rows=rows, scan=scan),
        grid=grid,
        in_specs=[pl.BlockSpec((rows, S5_ROW), lambda i: (i, 0)),
                  h_spec,
                  _const_spec(wx.shape), _const_spec(wy.shape), _const_spec(ap.shape)],
        out_specs=[pl.BlockSpec((rows, S5_ROW), lambda i: (i, 0)), h_spec],
        out_shape=[jax.ShapeDtypeStruct((n // S5_STEPS, S5_ROW), F32), h_shape],
        scratch_shapes=[pltpu.VMEM((rows, S5_BLOCK_STATE), F32),
                        pltpu.VMEM((rows, S5_BLOCK_STATE), F32)],
        compiler_params=pltpu.CompilerParams(dimension_semantics=("arbitrary",),
                                             vmem_limit_bytes=V7X_VMEM_LIMIT_BYTES),
        name="s5",
    )(u8, h0, wx, wy, ap)
    hre_new, him_new = _lanes_to_state(hl.reshape(nseq, S5_STATE_ROW))
    return y8, hre_new, him_new


RW_CHUNK = 64
RW_STEP_CHUNKS = 4
RW_GROUPS = 2
RW_GROUP_HEADS = N_RWKV_HEADS // RW_GROUPS
RW_GROUP_LANES = D_RWKV // RW_GROUPS


def _split3(x):
    hi = x.astype(BF16)
    r1 = x - hi.astype(F32)
    mid = r1.astype(BF16)
    lo = (r1 - mid.astype(F32)).astype(BF16)
    return hi, mid, lo


def _bd(x, hm_ref):
    xb = x.astype(BF16)
    return jnp.concatenate([xb * hm_ref[h] for h in range(RW_GROUP_HEADS)], axis=0)


def _unbd(f, hm_ref):
    n = RWKV_HEAD
    out = f[:n] * hm_ref[0].astype(F32)
    for h in range(1, RW_GROUP_HEADS):
        out = out + f[h * n:(h + 1) * n] * hm_ref[h].astype(F32)
    return out


def _rwkv_kernel(z_ref, s0_ref, w0_ref, w2_ref, a0_ref, a2_ref, g2_ref, kk_ref, ka_ref, rk_ref,
                 lg_ref, lb_ref, ones_ref, hm_ref, cm_ref, tri_ref, seg_ref, y_ref, sl_ref, ypre_ref,
                 *scratch, seg_len, step_chunks):
    C = RW_CHUNK
    N = RWKV_HEAD
    GL = RW_GROUP_LANES
    chained = seg_len == C
    if chained:
        (st_ref,) = scratch
        b = pl.program_id(1)

        @pl.when(b == 0)
        def _():
            for g in range(RW_GROUPS):
                st_ref[g] = jnp.concatenate(
                    [s0_ref[0, h] for h in range(g * RW_GROUP_HEADS, (g + 1) * RW_GROUP_HEADS)], axis=1)

    ones_bd = ones_ref[...]
    z = z_ref[...]
    r = z[:, :D_RWKV]
    k = z[:, D_RWKV:2 * D_RWKV]
    v = z[:, 2 * D_RWKV:3 * D_RWKV]
    o = 3 * D_RWKV
    xw = z[:, o:o + DECAY_LORA]
    xa = z[:, o + DECAY_LORA:o + DECAY_LORA + AAA_LORA]
    xg = z[:, o + DECAY_LORA + AAA_LORA:]
    wd = -(w0_ref[...] + _mm(jnp.tanh(xw), w2_ref[...]))
    w = -(jnp.maximum(wd, 0.0) + jnp.log1p(jnp.exp(-jnp.abs(wd)))) - 0.5
    lw = -jnp.exp(w)
    a = _sigmoid(a0_ref[...] + _mm(xa, a2_ref[...]))
    out_gate = _mm(_sigmoid(xg), g2_ref[...])
    kk = k * kk_ref[...]
    kk = kk / jnp.maximum(jnp.sqrt(_seg_sum(kk * kk, ones_bd)), L2_EPS)
    kmod = k * (1.0 + (a - 1.0) * ka_ref[...])

    terms = _split3(lw)
    tri, seg = tri_ref[...], seg_ref[...]
    cw = sum(jnp.dot(tri, t, preferred_element_type=F32) for t in terms)
    tot = sum(jnp.dot(seg, t, preferred_element_type=F32) for t in terms)
    w_in = jnp.exp(cw)
    w_tail = jnp.exp(tot - cw)
    w_inv = jnp.exp(-cw)
    w_all = jnp.exp(tot)
    ah = -kk * jnp.exp(cw - lw)
    bh = kk * a * w_inv
    kh = kmod * w_inv
    rh = r * w_in
    bt = kk * a * w_tail
    kt = kmod * w_tail
    strict = cm_ref[0] > 0.0
    incl = cm_ref[1] > 0.0
    eye_cat = cm_ref[2]
    n_fac = seg_len.bit_length() - 1

    items = [(c, g) for c in range(step_chunks) for g in range(RW_GROUPS)]
    sub = lambda t, c, g: t[c * C:(c + 1) * C, g * GL:(g + 1) * GL]
    bd = lambda t: _bd(t, hm_ref)
    A = [sub(ah, c, g) for c, g in items]
    R = [sub(rh, c, g) for c, g in items]
    V = [sub(v, c, g) for c, g in items]
    G = [_mm_nt(jnp.concatenate([A[i], R[i]], axis=0),
                jnp.concatenate([bd(sub(bh, c, g)), bd(sub(kh, c, g))], axis=0))
         for i, (c, g) in enumerate(items)]
    AB = [jnp.where(strict, t[:C, :GL], 0.0) for t in G]
    AK = [jnp.where(strict, t[:C, GL:], 0.0) for t in G]
    RB = [jnp.where(incl, t[C:, :GL], 0.0) for t in G]
    RK = [jnp.where(incl, t[C:, GL:], 0.0) for t in G]
    KV = [_mm(jnp.concatenate([AK[i], RK[i]], axis=0), bd(V[i])) for i in range(len(items))]
    T = [eye_cat + t for t in AB]
    P = [_mm(t, bd(t)) for t in AB]
    for _ in range(n_fac - 2):
        PT = [_mm(jnp.concatenate([P[i], T[i]], axis=0), bd(P[i])) for i in range(len(items))]
        P = [t[:C] for t in PT]
        T = [T[i] + PT[i][C:] for i in range(len(items))]
    T = [T[i] + _mm(T[i], bd(P[i])) for i in range(len(items))]
    X = [_mm(T[i], jnp.concatenate([bd(A[i]), bd(KV[i][:C])], axis=1)) for i in range(len(items))]
    TA = [t[:, :GL] for t in X]
    U0 = [t[:, GL:] for t in X]
    W2 = [_mm(RB[i], jnp.concatenate([bd(TA[i]), bd(U0[i])], axis=1)) for i in range(len(items))]
    Rt = [R[i] + W2[i][:, :GL] for i in range(len(items))]
    Y0 = [W2[i][:, GL:] + KV[i][C:] for i in range(len(items))]
    nseg = C // seg_len
    PhiT, PsiT = {}, {}
    for i, (c, g) in enumerate(items):
        Bt, Kt, Wa = sub(bt, c, g), sub(kt, c, g), sub(w_all, c, g)
        for s in range(nseg):
            ss = slice(s * seg_len, (s + 1) * seg_len)
            PhiT[i, s] = bd(eye_cat * Wa[s * seg_len:s * seg_len + 1, :]
                            + _unbd(_mm_tn(TA[i][ss], Bt[ss]), hm_ref))
            PsiT[i, s] = _unbd(_mm_tn(jnp.concatenate([U0[i][ss], V[i][ss]], axis=0),
                                      jnp.concatenate([Bt[ss], Kt[ss]], axis=0)), hm_ref)
    heads = lambda g: range(g * RW_GROUP_HEADS, (g + 1) * RW_GROUP_HEADS)
    load_state = lambda ref, j, g: jnp.concatenate([ref[j, h] for h in heads(g)], axis=1)
    if chained:
        states = [st_ref[g] for g in range(RW_GROUPS)]
    for i, (c, g) in enumerate(items):
        for s in range(nseg):
            ss = slice(s * seg_len, (s + 1) * seg_len)
            S = states[g] if chained else load_state(s0_ref, c * nseg + s, g)
            ypre_ref[c * C + s * seg_len:c * C + (s + 1) * seg_len, g * GL:(g + 1) * GL] = (
                _mm_nt(Rt[i][ss], bd(S)) + Y0[i][ss])
            S = _mm(S, PhiT[i, s]) + PsiT[i, s]
            if chained:
                states[g] = S
            else:
                for h in heads(g):
                    sl_ref[c * nseg + s, h] = S[:, (h % RW_GROUP_HEADS) * N:(h % RW_GROUP_HEADS + 1) * N]
    if chained:
        for g in range(RW_GROUPS):
            st_ref[g] = states[g]
            for h in heads(g):
                sl_ref[0, h] = states[g][:, (h % RW_GROUP_HEADS) * N:(h % RW_GROUP_HEADS + 1) * N]
    y = ypre_ref[...]

    mu = _seg_sum(y, ones_bd) * (1.0 / N)
    d = y - mu
    var = _seg_sum(d * d, ones_bd) * (1.0 / N)
    yn = d * lax.rsqrt(var + GN_EPS) * lg_ref[...] + lb_ref[...]
    bonus = _seg_sum(r * kmod * rk_ref[...], ones_bd) * v
    y_ref[...] = (yn + bonus) * out_gate


def _rwkv_constants(seg_len, rows):
    n, gh, gl = RWKV_HEAD, RW_GROUP_HEADS, RW_GROUP_LANES
    lane_head = np.arange(gl) // n
    hm = (lane_head[None, None, :] == np.arange(gh)[:, None, None]) * np.ones((gh, n, gl))
    i = np.arange(RW_CHUNK)[:, None]
    j = (np.arange(gl) % n)[None, :]
    same = (i // seg_len) == (j // seg_len)
    cm = np.stack([(j < i) & same, (j <= i) & same, j == i]).astype(np.float32)
    ri = np.arange(rows)[:, None]
    rj = np.arange(rows)[None, :]
    same_r = (ri // seg_len) == (rj // seg_len)
    tri = ((rj <= ri) & same_r).astype(np.float32)
    ones_bd = np.kron(np.eye(N_RWKV_HEADS), np.ones((n, n)))
    return (jnp.asarray(ones_bd, BF16), jnp.asarray(hm, BF16), jnp.asarray(cm, F32),
            jnp.asarray(tri, BF16), jnp.asarray(same_r.astype(np.float32), BF16))


def _rwkv(z, s0, p, seq_len):
    n = z.shape[0]
    step_chunks = min(RW_STEP_CHUNKS, (seq_len if seq_len >= RW_CHUNK else n) // RW_CHUNK)
    nseq = n // seq_len
    nh, hd = N_RWKV_HEADS, RWKV_HEAD
    seg_len = min(seq_len, RW_CHUNK)
    rows = RW_CHUNK * step_chunks
    chained = seq_len >= RW_CHUNK
    if chained:
        steps = seq_len // rows
        grid = (nseq, steps)
        row_map = lambda s, b: (s * steps + b, 0)
        state_spec = pl.BlockSpec((1, nh, hd, hd), lambda s, b: (s, 0, 0, 0))
        scratch = [pltpu.VMEM((rows, D_RWKV), F32), pltpu.VMEM((RW_GROUPS, hd, RW_GROUP_LANES), F32)]
    else:
        grid = (n // rows, 1)
        row_map = lambda s, b: (s, 0)
        state_spec = pl.BlockSpec((rows // seq_len, nh, hd, hd), lambda s, b: (s, 0, 0, 0))
        scratch = [pltpu.VMEM((rows, D_RWKV), F32)]
    small = [p['w0'], p['w2'], p['a0'], p['a2'], p['g2'], p['k_k'], p['k_a'], p['r_k'],
             p['lnx_g'], p['lnx_b'], *_rwkv_constants(seg_len, rows)]
    y, s_new = pl.pallas_call(
        functools.partial(_rwkv_kernel, seg_len=seg_len, step_chunks=step_chunks),
        grid=grid,
        in_specs=[pl.BlockSpec((rows, N_SHIFT), row_map), state_spec]
        + [_const_spec(t.shape) for t in small],
        out_specs=[pl.BlockSpec((rows, D_RWKV), row_map), state_spec],
        out_shape=[jax.ShapeDtypeStruct((n, D_RWKV), F32),
                   jax.ShapeDtypeStruct((nseq, nh, hd, hd), F32)],
        scratch_shapes=scratch,
        compiler_params=pltpu.CompilerParams(dimension_semantics=("arbitrary", "arbitrary"),
                                             vmem_limit_bytes=V7X_VMEM_LIMIT_BYTES),
        name="rwkv7",
    )(z, s0, *small)
    return y, s_new


def _ffn_kernel(h_ref, y8_ref, u8_ref, yb_ref, pe_ref, c1_ref, c2_ref, d8_ref, wglu_ref, bglu_ref,
                wout_ref, g2_ref, wup_ref, cw_ref, cb_ref, wdn_ref, wple_ref, wpg_ref, gf_ref,
                *rest, seq_len, tm, final):
    i = pl.program_id(0)
    long_seq = seq_len >= tm
    if long_seq:
        o_ref, ga_ref, gb_ref, fold_ref, carry_ref = rest

        @pl.when(i == 0)
        def _():
            carry_ref[...] = jnp.zeros_like(carry_ref)
    else:
        expand_ref, sel_last_ref, sel_prev_ref, o_ref, ga_ref, gb_ref, fold_ref = rest
        expand = expand_ref[...]
    ya = y8_ref[...] + d8_ref[...] * u8_ref[...]
    c_gelu = math.sqrt(2.0 / math.pi)
    ya = ya * (0.5 * (1.0 + jnp.tanh(c_gelu * (ya + 0.044715 * (ya * ya * ya)))))
    ya = _unfold_rows(ya, fold_ref)
    ya = ya * _sigmoid(_mm(ya, wglu_ref[...]) + bglu_ref[...])
    h1 = (h_ref[...] + _mm(ya, wout_ref[:D_SSM, :]) + _mm(yb_ref[...], wout_ref[D_SSM:, :]))
    x2 = _rmsnorm(h1, g2_ref[...]).astype(BF16)
    row = lax.broadcasted_iota(jnp.int32, (tm, 1), 0)
    if long_seq:
        first = (i % (seq_len // tm)) == 0
    else:
        t = row % seq_len
    acc = jnp.zeros((tm, D_MODEL), F32)
    for c in range(D_FF // FF_CHUNK):
        cs = slice(c * FF_CHUNK, (c + 1) * FF_CHUNK)
        gs = slice(D_FF + c * FF_CHUNK, D_FF + (c + 1) * FF_CHUNK)
        val = jnp.dot(x2, wup_ref[:, cs], preferred_element_type=F32)
        gate = jnp.dot(x2, wup_ref[:, gs], preferred_element_type=F32)
        r1 = pltpu.roll(gate, 1, 0)
        r2 = pltpu.roll(gate, 2, 0)
        if long_seq:
            m1 = jnp.where(first, c1_ref[0][:, cs], carry_ref[1:2, cs])
            m2 = jnp.where(first, c2_ref[0][:, cs], carry_ref[0:1, cs])
            p1 = jnp.where(row == 0, m1, r1)
            p2 = jnp.where(row == 0, m2, jnp.where(row == 1, m1, r2))
            carry_ref[:, cs] = gate[tm - 2:tm, :]
            ga_ref[0, :, cs] = gate[tm - 2:tm - 1, :]
            gb_ref[0, :, cs] = gate[tm - 1:tm, :]
        else:
            m1 = _exact_rows(expand, c1_ref[:, cs])
            m2 = _exact_rows(expand, c2_ref[:, cs])
            p1 = jnp.where(t == 0, m1, r1)
            p2 = jnp.where(t == 0, m2, jnp.where(t == 1, m1, r2))
            ga_ref[:, cs] = _exact_rows(sel_prev_ref[...], gate)
            gb_ref[:, cs] = _exact_rows(sel_last_ref[...], gate)
        conv = (cb_ref[:, cs] + cw_ref[2:3, cs] * gate + cw_ref[1:2, cs] * p1 + cw_ref[0:1, cs] * p2)
        act = conv * _sigmoid(conv) * val
        acc = acc + jnp.dot(act.astype(BF16), wdn_ref[cs, :], preferred_element_type=F32)
    h2 = h1 + acc
    h3 = h2 + _mm(pe_ref[...], wple_ref[...]) * _sigmoid(_mm(h2, wpg_ref[...]))
    if final:
        h3 = _rmsnorm(h3, gf_ref[...])
    o_ref[...] = h3


def _ffn(h, y8, u8, yb, pe, conv0, p, seq_len, tm, final):
    n = h.shape[0]
    nseq = n // seq_len
    row_spec = lambda w: pl.BlockSpec((tm, w), lambda i: (i, 0))
    fold_spec = pl.BlockSpec((tm // S5_STEPS, S5_ROW), lambda i: (i, 0))
    scratch = [pltpu.VMEM((FOLD_LANE_BLOCKS, tm, LANES), F32)]
    if seq_len >= tm:
        tps = seq_len // tm
        seq_spec = pl.BlockSpec((1, 1, D_FF), lambda i: (i // tps, 0, 0))
        seq_shape = (nseq, 1, D_FF)
        extra = []
        scratch.append(pltpu.VMEM((2, D_FF), F32))
    else:
        seq_spec = pl.BlockSpec((tm // seq_len, D_FF), lambda i: (i, 0))
        seq_shape = (nseq, D_FF)
        extra = list(_seq_selectors(seq_len, tm))
    weights = [jnp.tile(p['d'], (1, S5_STEPS)), p['w_glu'], p['b_glu'], p['w_out'], p['norm2_g'],
               p['w_up'], p['conv_w'], p['conv_b'], p['w_down'], p['w_ple'], p['w_pg'],
               p['final_g']] + extra
    out, ga, gb = pl.pallas_call(
        functools.partial(_ffn_kernel, seq_len=seq_len, tm=tm, final=final),
        grid=(n // tm,),
        in_specs=[row_spec(D_MODEL), fold_spec, fold_spec, row_spec(D_RWKV), row_spec(D_PLE),
                  seq_spec, seq_spec] + [_const_spec(t.shape) for t in weights],
        out_specs=[row_spec(D_MODEL), seq_spec, seq_spec],
        out_shape=[jax.ShapeDtypeStruct((n, D_MODEL), F32)]
        + [jax.ShapeDtypeStruct(seq_shape, F32)] * 2,
        scratch_shapes=scratch,
        compiler_params=pltpu.CompilerParams(dimension_semantics=("arbitrary",),
                                             vmem_limit_bytes=V7X_VMEM_LIMIT_BYTES),
        name="ffn",
    )(h, y8, u8, yb, pe, conv0[:, 1].reshape(seq_shape), conv0[:, 0].reshape(seq_shape), *weights)
    conv_new = jnp.stack([ga.reshape(nseq, D_FF), gb.reshape(nseq, D_FF)], axis=1)
    return out, conv_new


def _layer(h, pe, st, p, s5_tables, seq_len, tm, final):
    ssm_re0, ssm_im0, rwkv0, shift0, conv0 = st
    u8, z, shift_new = _inproj(h, p['norm1_g'], p['w_in'], p['shift_mu'], shift0, seq_len, tm)
    y8, hre, him = _s5(u8, ssm_re0, ssm_im0, s5_tables, seq_len)
    yb, s_last = _rwkv(z, rwkv0, p, seq_len)
    h, conv_new = _ffn(h, y8, u8, yb, pe, conv0, p, seq_len, tm, final)
    return h, (hre, him, s_last, shift_new, conv_new)


def _layer_params(i, w):
    row = lambda t: t[i].reshape(1, -1).astype(F32)
    bf = lambda t: t[i].astype(BF16)
    return {
        'norm1_g': row(w['norm1_g']), 'w_in': bf(w['w_in']), 'shift_mu': row(w['shift_mu']),
        'd': row(w['ssm_d']), 'w_glu': bf(w['ssm_w_glu']), 'b_glu': row(w['ssm_b_glu']),
        'w0': row(w['rwkv_w0']), 'w2': bf(w['rwkv_w2']), 'a0': row(w['rwkv_a0']),
        'a2': bf(w['rwkv_a2']), 'g2': bf(w['rwkv_g2']), 'k_k': row(w['rwkv_k_k']),
        'k_a': row(w['rwkv_k_a']), 'r_k': row(w['rwkv_r_k']), 'lnx_g': row(w['rwkv_lnx_g']),
        'lnx_b': row(w['rwkv_lnx_b']), 'w_out': bf(w['w_out']), 'norm2_g': row(w['norm2_g']),
        'w_up': bf(w['w_up']), 'conv_w': w['conv_w'][i].astype(F32), 'conv_b': row(w['conv_b']),
        'w_down': bf(w['w_down']), 'w_ple': bf(w['w_ple']), 'w_pg': bf(w['w_pg']),
        'final_g': w['final_g'].reshape(1, -1).astype(F32),
    }


def _forward(x_prompt, x_sample, p_prompt, p_sample, state_ssm_re, state_ssm_im, state_rwkv,
             state_shift, state_conv, w, tm_prompt, tm_sample):
    depth = w['w_in'].shape[0]
    bp, lp, _ = x_prompt.shape
    bs, ls, _ = x_sample.shape
    hp = x_prompt.reshape(bp * lp, D_MODEL).astype(F32)
    hs = x_sample.reshape(bs * ls, D_MODEL).astype(F32)
    zero_st = (jnp.zeros((bp, N_SSM_GROUPS, SSM_STATE), F32),
               jnp.zeros((bp, N_SSM_GROUPS, SSM_STATE), F32),
               jnp.zeros((bp, N_RWKV_HEADS, RWKV_HEAD, RWKV_HEAD), F32),
               jnp.zeros((bp, N_SHIFT), F32),
               jnp.zeros((bp, 2, D_FF), F32))
    new_p = [[] for _ in range(5)]
    new_s = [[] for _ in range(5)]
    for i in range(depth):
        p = _layer_params(i, w)
        tables = _s5_tables(w['ssm_lam_re'][i], w['ssm_lam_im'][i], w['ssm_log_dt'][i],
                            w['ssm_b_re'][i], w['ssm_b_im'][i], w['ssm_c_re'][i], w['ssm_c_im'][i])
        final = i == depth - 1
        hp, stp = _layer(hp, p_prompt[i].reshape(bp * lp, D_PLE).astype(F32), zero_st, p, tables,
                         lp, tm_prompt, final)
        st_in = (state_ssm_re[i].astype(F32), state_ssm_im[i].astype(F32),
                 state_rwkv[i].astype(F32), state_shift[i].astype(F32), state_conv[i].astype(F32))
        hs, sts = _layer(hs, p_sample[i].reshape(bs * ls, D_PLE).astype(F32), st_in, p, tables,
                         ls, tm_sample, final)
        for j in range(5):
            new_p[j].append(stp[j])
            new_s[j].append(sts[j])
    y_prompt = hp.reshape(bp, lp, D_MODEL).astype(x_prompt.dtype)
    y_sample = hs.reshape(bs, ls, D_MODEL).astype(x_sample.dtype)
    dts = (state_ssm_re.dtype, state_ssm_im.dtype, state_rwkv.dtype, state_shift.dtype,
           state_conv.dtype)
    outs_p = tuple(jnp.stack(new_p[j]).astype(dts[j]) for j in range(5))
    outs_s = tuple(jnp.stack(new_s[j]).astype(dts[j]) for j in range(5))
    return (y_prompt, y_sample) + outs_p + outs_s


def kernel(x_prompt, x_sample, p_prompt, p_sample, state_ssm_re, state_ssm_im, state_rwkv, state_shift, state_conv, norm1_g, w_in, shift_mu, ssm_lam_re, ssm_lam_im, ssm_log_dt, ssm_b_re, ssm_b_im, ssm_c_re, ssm_c_im, ssm_d, ssm_w_glu, ssm_b_glu, rwkv_w0, rwkv_w2, rwkv_a0, rwkv_a2, rwkv_g2, rwkv_k_k, rwkv_k_a, rwkv_r_k, rwkv_lnx_g, rwkv_lnx_b, w_out, norm2_g, w_up, conv_w, conv_b, w_down, w_ple, w_pg, final_g):
    w = dict(norm1_g=norm1_g, w_in=w_in, shift_mu=shift_mu, ssm_lam_re=ssm_lam_re,
             ssm_lam_im=ssm_lam_im, ssm_log_dt=ssm_log_dt, ssm_b_re=ssm_b_re, ssm_b_im=ssm_b_im,
             ssm_c_re=ssm_c_re, ssm_c_im=ssm_c_im, ssm_d=ssm_d, ssm_w_glu=ssm_w_glu,
             ssm_b_glu=ssm_b_glu, rwkv_w0=rwkv_w0, rwkv_w2=rwkv_w2, rwkv_a0=rwkv_a0,
             rwkv_a2=rwkv_a2, rwkv_g2=rwkv_g2, rwkv_k_k=rwkv_k_k, rwkv_k_a=rwkv_k_a,
             rwkv_r_k=rwkv_r_k, rwkv_lnx_g=rwkv_lnx_g, rwkv_lnx_b=rwkv_lnx_b, w_out=w_out,
             norm2_g=norm2_g, w_up=w_up, conv_w=conv_w, conv_b=conv_b, w_down=w_down,
             w_ple=w_ple, w_pg=w_pg, final_g=final_g)
    lp = x_prompt.shape[1]
    ns = x_sample.shape[0] * x_sample.shape[1]
    return _forward(x_prompt, x_sample, p_prompt, p_sample, state_ssm_re, state_ssm_im,
                    state_rwkv, state_shift, state_conv, w,
                    tm_prompt=min(512, lp), tm_sample=min(256, ns))
```

```python
import functools
import math

import numpy as np
import jax
import jax.numpy as jnp
from jax import lax
from jax.experimental import pallas as pl
from jax.experimental.pallas import tpu as pltpu

F32 = jnp.float32
BF16 = jnp.bfloat16
HIGHEST = lax.Precision.HIGHEST

D_MODEL = 1024
D_SSM = 512
D_RWKV = 512
SSM_GROUP = 16
N_SSM_GROUPS = 32
SSM_STATE = 64
RWKV_HEAD = 64
N_RWKV_HEADS = 8
DECAY_LORA = 64
AAA_LORA = 64
GATE_LORA = 128
N_SHIFT = 3 * D_RWKV + DECAY_LORA + AAA_LORA + GATE_LORA
N_IN = D_SSM + N_SHIFT
D_FF = 2816
D_PLE = 256
RMS_EPS = 1e-6
GN_EPS = 64e-5
L2_EPS = 1e-12

LANES = 128
FOLD_LANE_BLOCKS = D_SSM // LANES
S5_STEPS = 8
S5_ROW = S5_STEPS * D_SSM
S5_LANE_BLOCKS = 4
S5_BLOCK_STATE = 2 * 8 * SSM_STATE
S5_STATE_ROW = S5_LANE_BLOCKS * S5_BLOCK_STATE
FF_CHUNK = 256
V7X_VMEM_LIMIT_BYTES = 56 * 1024 * 1024


def _mm(a, b):
    return jnp.dot(a.astype(BF16), b.astype(BF16), preferred_element_type=F32)


def _mm_nt(a, b):
    return lax.dot_general(a.astype(BF16), b.astype(BF16), (((1,), (1,)), ((), ())),
                           preferred_element_type=F32)


def _mm_tn(a, b):
    return lax.dot_general(a.astype(BF16), b.astype(BF16), (((0,), (0,)), ((), ())),
                           preferred_element_type=F32)


def _seg_sum(x, ones_bd):
    hi = x.astype(BF16)
    lo = (x - hi.astype(F32)).astype(BF16)
    return (jnp.dot(hi, ones_bd, preferred_element_type=F32)
            + jnp.dot(lo, ones_bd, preferred_element_type=F32))


def _rmsnorm(x, g):
    return x * lax.rsqrt(jnp.mean(x * x, axis=-1, keepdims=True) + RMS_EPS) * g


def _sigmoid(x):
    return 1.0 / (1.0 + jnp.exp(-x))


def _const_spec(shape):
    nd = len(shape)
    return pl.BlockSpec(shape, lambda *_: (0,) * nd, pipeline_mode=pl.Buffered(1))


def _exact_rows(sel, x):
    return sum(jnp.dot(sel, t, preferred_element_type=F32) for t in _split3(x))


def _fold_rows(x, scr_ref, out_ref):
    tm = x.shape[0]
    for q in range(FOLD_LANE_BLOCKS):
        scr_ref[q] = x[:, q * LANES:(q + 1) * LANES]
    for s in range(S5_STEPS):
        for q in range(FOLD_LANE_BLOCKS):
            c0 = s * D_SSM + q * LANES
            out_ref[:, c0:c0 + LANES] = scr_ref[q, pl.ds(s, tm // S5_STEPS, stride=S5_STEPS), :]


def _unfold_rows(x8, scr_ref):
    tm = x8.shape[0] * S5_STEPS
    for s in range(S5_STEPS):
        for q in range(FOLD_LANE_BLOCKS):
            c0 = s * D_SSM + q * LANES
            scr_ref[q, pl.ds(s, tm // S5_STEPS, stride=S5_STEPS), :] = x8[:, c0:c0 + LANES]
    return jnp.concatenate([scr_ref[q] for q in range(FOLD_LANE_BLOCKS)], axis=1)


def _seq_selectors(seq_len, tm):
    r = np.arange(tm)
    b = np.arange(tm // seq_len)
    expand = (r[:, None] // seq_len == b[None, :]).astype(np.float32)
    last = (r[None, :] == (b[:, None] + 1) * seq_len - 1).astype(np.float32)
    prev = (r[None, :] == (b[:, None] + 1) * seq_len - 2).astype(np.float32)
    return jnp.asarray(expand, BF16), jnp.asarray(last, BF16), jnp.asarray(prev, BF16)


def _inproj_kernel(h_ref, g_ref, w_ref, mu_ref, init_ref, *rest, seq_len, tm):
    i = pl.program_id(0)
    long_seq = seq_len >= tm
    if long_seq:
        u8_ref, z_ref, last_ref, fold_ref, carry_ref = rest

        @pl.when(i == 0)
        def _():
            carry_ref[...] = jnp.zeros_like(carry_ref)
    else:
        expand_ref, sel_last_ref, u8_ref, z_ref, last_ref, fold_ref = rest
    xn = _rmsnorm(h_ref[...], g_ref[...])
    proj = jnp.dot(xn.astype(BF16), w_ref[...], preferred_element_type=F32)
    _fold_rows(proj[:, :D_SSM], fold_ref, u8_ref)
    zr = proj[:, D_SSM:]
    rolled = pltpu.roll(zr, 1, 0)
    row = lax.broadcasted_iota(jnp.int32, (tm, 1), 0)
    if long_seq:
        first = (i % (seq_len // tm)) == 0
        row0 = jnp.where(first, init_ref[0], carry_ref[...])
        prev = jnp.where(row == 0, row0, rolled)
        carry_ref[...] = zr[tm - 1:tm, :]
        last_ref[0] = zr[tm - 1:tm, :]
    else:
        prev = jnp.where(row % seq_len == 0, _exact_rows(expand_ref[...], init_ref[...]), rolled)
        last_ref[...] = _exact_rows(sel_last_ref[...], zr)
    z_ref[...] = zr + (prev - zr) * mu_ref[...]


def _inproj(h, g1, w_in, mu, shift0, seq_len, tm):
    n = h.shape[0]
    nseq = n // seq_len
    fold_scratch = pltpu.VMEM((FOLD_LANE_BLOCKS, tm, LANES), F32)
    if seq_len >= tm:
        tps = seq_len // tm
        seq_spec = pl.BlockSpec((1, 1, N_SHIFT), lambda i: (i // tps, 0, 0))
        extra, extra_specs = [], []
        init = shift0.reshape(nseq, 1, N_SHIFT)
        last_shape = jax.ShapeDtypeStruct((nseq, 1, N_SHIFT), F32)
        scratch = [fold_scratch, pltpu.VMEM((1, N_SHIFT), F32)]
    else:
        spt = tm // seq_len
        seq_spec = pl.BlockSpec((spt, N_SHIFT), lambda i: (i, 0))
        expand, sel_last, _ = _seq_selectors(seq_len, tm)
        extra = [expand, sel_last]
        extra_specs = [_const_spec(expand.shape), _const_spec(sel_last.shape)]
        init = shift0
        last_shape = jax.ShapeDtypeStruct((nseq, N_SHIFT), F32)
        scratch = [fold_scratch]
    u8, z, last = pl.pallas_call(
        functools.partial(_inproj_kernel, seq_len=seq_len, tm=tm),
        grid=(n // tm,),
        in_specs=[pl.BlockSpec((tm, D_MODEL), lambda i: (i, 0)),
                  _const_spec((1, D_MODEL)),
                  _const_spec((D_MODEL, N_IN)),
                  _const_spec((1, N_SHIFT)),
                  seq_spec] + extra_specs,
        out_specs=[pl.BlockSpec((tm // S5_STEPS, S5_ROW), lambda i: (i, 0)),
                   pl.BlockSpec((tm, N_SHIFT), lambda i: (i, 0)),
                   seq_spec],
        out_shape=[jax.ShapeDtypeStruct((n // S5_STEPS, S5_ROW), F32),
                   jax.ShapeDtypeStruct((n, N_SHIFT), F32),
                   last_shape],
        scratch_shapes=scratch,
        compiler_params=pltpu.CompilerParams(dimension_semantics=("arbitrary",),
                                             vmem_limit_bytes=V7X_VMEM_LIMIT_BYTES),
        name="inproj",
    )(h, g1, w_in, mu, init, *extra)
    return u8, z, last.reshape(nseq, N_SHIFT)


def _s5_prep_kernel(lr_ref, li_ref, ldt_ref, br_ref, bi_ref, cr_ref, ci_ref, tile_ref, sel_ref,
                    wx_ref, wy_ref, apr_ref, api_ref):
    lr, li = lr_ref[...], li_ref[...]
    dt = jnp.exp(ldt_ref[...])
    mag = jnp.exp(lr * dt)
    ar = mag * jnp.cos(li * dt)
    ai = mag * jnp.sin(li * dt)
    den = lr * lr + li * li
    nr = ar - 1.0
    fr = (nr * lr + ai * li) / den
    fi = (ai * lr - nr * li) / den
    br, bi = br_ref[...], bi_ref[...]
    bbr = fr * br - fi * bi
    bbi = fr * bi + fi * br
    cr, ci = cr_ref[...], ci_ref[...]
    pr = [jnp.ones_like(ar)]
    pi = [jnp.zeros_like(ar)]
    for _ in range(S5_STEPS):
        pr.append(pr[-1] * ar - pi[-1] * ai)
        pi.append(pr[-2] * ai + pi[-1] * ar)
    nrow = lr.shape[0]
    blk = nrow // S5_LANE_BLOCKS
    half = S5_BLOCK_STATE // 2
    ri = lax.broadcasted_iota(jnp.int32, (nrow, nrow), 0)
    ci_ = lax.broadcasted_iota(jnp.int32, (nrow, nrow), 1)
    same_group = (ri // SSM_GROUP) == (ci_ // SSM_GROUP)
    keep_in = ((ri % blk) // SSM_GROUP) == (ci_ // SSM_STATE)
    keep_out = (ri // SSM_STATE) == ((ci_ % blk) // SSM_GROUP)
    tile, sel = tile_ref[...], sel_ref[...]
    nt = (((1,), (1,)), ((), ()))
    wy_ref[...] = jnp.zeros(wy_ref.shape, wy_ref.dtype)
    for s in range(S5_STEPS):
        qr, qi = pr[S5_STEPS - 1 - s], pi[S5_STEPS - 1 - s]
        wr = qr * bbr - qi * bbi
        wi = qr * bbi + qi * bbr
        for h, w_ in enumerate((wr, wi)):
            t = jnp.dot(w_.astype(BF16), tile, preferred_element_type=F32)
            t = jnp.where(keep_in, t, 0.0).astype(BF16)
            for q in range(S5_LANE_BLOCKS):
                wx_ref[s, q, :, h * half:(h + 1) * half] = t[q * blk:(q + 1) * blk, :]
        tau = S5_STEPS - 1 - s
        kt = (lax.dot_general(wr, cr, nt, precision=HIGHEST, preferred_element_type=F32)
              - lax.dot_general(wi, ci, nt, precision=HIGHEST, preferred_element_type=F32))
        kt = jnp.where(same_group, kt, 0.0).astype(BF16)
        for q in range(S5_LANE_BLOCKS):
            kq = kt[q * blk:(q + 1) * blk, q * blk:(q + 1) * blk]
            for s_in in range(S5_STEPS - tau):
                s_out = s_in + tau
                r0 = S5_BLOCK_STATE + s_in * blk
                wy_ref[q, s_out // 2, r0:r0 + blk, (s_out % 2) * blk:(s_out % 2 + 1) * blk] = kq
        mr = cr * pr[s + 1] - ci * pi[s + 1]
        mi = cr * pi[s + 1] + ci * pr[s + 1]
        for h, m_ in enumerate((mr, -mi)):
            t = lax.dot_general(sel, m_.astype(BF16), nt, preferred_element_type=F32)
            t = jnp.where(keep_out, t, 0.0).astype(BF16)
            for q in range(S5_LANE_BLOCKS):
                wy_ref[q, s // 2, h * half:(h + 1) * half, (s % 2) * blk:(s % 2 + 1) * blk] = (
                    t[:, q * blk:(q + 1) * blk])
    a8r, a8i = pr[S5_STEPS], pi[S5_STEPS]
    er, ei = a8r, a8i
    for n in range(S5_STEPS):
        apr_ref[n] = er
        api_ref[n] = ei
        er, ei = er * a8r - ei * a8i, er * a8i + ei * a8r


def _s5_tables(lam_re, lam_im, log_dt, b_re, b_im, c_re, c_im):
    G, P, K = N_SSM_GROUPS, SSM_STATE, SSM_GROUP
    Q, GB = S5_LANE_BLOCKS, G // S5_LANE_BLOCKS
    rep = lambda t: jnp.repeat(t, K, axis=0)
    tile = np.tile(np.eye(P, dtype=np.float32), (1, GB))
    args = (rep(lam_re), rep(lam_im), rep(jnp.broadcast_to(log_dt[:, None], (G, P))),
            jnp.swapaxes(b_re, 1, 2).reshape(G * K, P), jnp.swapaxes(b_im, 1, 2).reshape(G * K, P),
            c_re.reshape(G * K, P), c_im.reshape(G * K, P),
            jnp.asarray(tile, BF16), jnp.asarray(tile.T, BF16))
    t3 = jax.ShapeDtypeStruct((S5_STEPS, G * K, P), F32)
    wx, wy, apr, api = pl.pallas_call(
        _s5_prep_kernel,
        out_shape=[jax.ShapeDtypeStruct((S5_STEPS, Q, GB * K, S5_BLOCK_STATE), BF16),
                   jax.ShapeDtypeStruct((Q, S5_STEPS // 2, S5_BLOCK_STATE + S5_STEPS * GB * K,
                                         2 * GB * K), BF16),
                   t3, t3],
        compiler_params=pltpu.CompilerParams(vmem_limit_bytes=V7X_VMEM_LIMIT_BYTES),
        name="s5_prep",
    )(*args)
    ap = jnp.concatenate([t[:, ::K, :].reshape(S5_STEPS, Q, GB * P) for t in (apr, api)], axis=-1)
    return wx, wy, ap.reshape(S5_STEPS, S5_STATE_ROW)


def _state_to_lanes(h_re, h_im):
    n = h_re.shape[0]
    parts = [t.reshape(n, S5_LANE_BLOCKS, -1) for t in (h_re, h_im)]
    return jnp.concatenate(parts, axis=-1).reshape(n, S5_STATE_ROW)


def _lanes_to_state(h):
    n = h.shape[0]
    h = h.reshape(n, S5_LANE_BLOCKS, 2, N_SSM_GROUPS // S5_LANE_BLOCKS, SSM_STATE)
    return (h[:, :, 0].reshape(n, N_SSM_GROUPS, SSM_STATE), h[:, :, 1].reshape(n, N_SSM_GROUPS, SSM_STATE))


def _s5_kernel(u_ref, h0_ref, wx_ref, wy_ref, ap_ref, y_ref, hl_ref, hs_ref, hp_ref, *, rows, scan):
    half = S5_BLOCK_STATE // 2
    cw = D_SSM // S5_LANE_BLOCKS
    ub = u_ref[...].astype(BF16)
    sub3 = lax.broadcasted_iota(jnp.int32, (1, 8, 1), 1)
    sub2 = lax.broadcasted_iota(jnp.int32, (8, 1), 0)
    for q in range(S5_LANE_BLOCKS):
        us = [ub[:, s * D_SSM + q * cw: s * D_SSM + (q + 1) * cw] for s in range(S5_STEPS)]
        x = jnp.dot(us[0], wx_ref[0, q], preferred_element_type=F32)
        for s in range(1, S5_STEPS):
            x = x + jnp.dot(us[s], wx_ref[s, q], preferred_element_type=F32)
        lo = q * S5_BLOCK_STATE
        apq = ap_ref[:, lo:lo + S5_BLOCK_STATE]
        if scan:
            xr = x[:, :half].reshape(rows // 8, 8, half)
            xi = x[:, half:].reshape(rows // 8, 8, half)
            for k in (1, 2, 4):
                er = apq[k - 1:k, :half].reshape(1, 1, half)
                ei = apq[k - 1:k, half:].reshape(1, 1, half)
                sr = jnp.where(sub3 >= k, pltpu.roll(xr, k, 1), 0.0)
                si = jnp.where(sub3 >= k, pltpu.roll(xi, k, 1), 0.0)
                xr, xi = xr + er * sr - ei * si, xi + er * si + ei * sr
            hs_ref[:, :half] = xr.reshape(rows, half)
            hs_ref[:, half:] = xi.reshape(rows, half)
            pwr, pwi = apq[:, :half], apq[:, half:]
            c0r = h0_ref[0][:, lo:lo + half]
            c0i = h0_ref[0][:, lo + half:lo + S5_BLOCK_STATE]

            def body(j, carry):
                cr, ci = carry
                sl = pl.ds(pl.multiple_of(j * 8, 8), 8)
                hr = hs_ref[sl, :half] + pwr * cr - pwi * ci
                hi = hs_ref[sl, half:] + pwr * ci + pwi * cr
                hp_ref[sl, :half] = jnp.where(sub2 == 0, cr, pltpu.roll(hr, 1, 0))
                hp_ref[sl, half:] = jnp.where(sub2 == 0, ci, pltpu.roll(hi, 1, 0))
                return hr[7:8, :], hi[7:8, :]

            cr, ci = lax.fori_loop(0, rows // 8, body, (c0r, c0i))
            hl_ref[0, :, lo:lo + half] = cr
            hl_ref[0, :, lo + half:lo + S5_BLOCK_STATE] = ci
            hprev = hp_ref[...].astype(BF16)
        else:
            h0r = h0_ref[:, lo:lo + half]
            h0i = h0_ref[:, lo + half:lo + S5_BLOCK_STATE]
            er, ei = apq[0:1, :half], apq[0:1, half:]
            hl_ref[:, lo:lo + half] = er * h0r - ei * h0i + x[:, :half]
            hl_ref[:, lo + half:lo + S5_BLOCK_STATE] = er * h0i + ei * h0r + x[:, half:]
            hprev = jnp.concatenate([h0r, h0i], axis=1).astype(BF16)
        lhs = jnp.concatenate([hprev] + us, axis=1)
        for j in range(S5_STEPS // 2):
            kk = S5_BLOCK_STATE + cw * (2 * j + 2)
            y2 = jnp.dot(lhs[:, :kk], wy_ref[q, j, :kk, :], preferred_element_type=F32)
            c0 = (2 * j) * D_SSM + q * cw
            c1 = (2 * j + 1) * D_SSM + q * cw
            y_ref[:, c0:c0 + cw] = y2[:, :cw]
            y_ref[:, c1:c1 + cw] = y2[:, cw:]


def _s5(u8, h_re, h_im, tables, seq_len):
    wx, wy, ap = tables
    n = u8.shape[0] * S5_STEPS
    nseq = n // seq_len
    h0 = _state_to_lanes(h_re, h_im)
    cps = seq_len // S5_STEPS
    scan = cps > 1
    if scan:
        rows = cps
        grid = (nseq,)
        h0 = h0.reshape(nseq, 1, S5_STATE_ROW)
        h_spec = pl.BlockSpec((1, 1, S5_STATE_ROW), lambda i: (i, 0, 0))
        h_shape = jax.ShapeDtypeStruct((nseq, 1, S5_STATE_ROW), F32)
    else:
        rows = min(nseq, 128)
        grid = (nseq // rows,)
        h_spec = pl.BlockSpec((rows, S5_STATE_ROW), lambda i: (i, 0))
        h_shape = jax.ShapeDtypeStruct((nseq, S5_STATE_ROW), F32)
    y8, hl = pl.pallas_call(
        functools.partial(_s5_kernel, rows=rows, scan=scan),
        grid=grid,
        in_specs=[pl.BlockSpec((rows, S5_ROW), lambda i: (i, 0)),
                  h_spec,
                  _const_spec(wx.shape), _const_spec(wy.shape), _const_spec(ap.shape)],
        out_specs=[pl.BlockSpec((rows, S5_ROW), lambda i: (i, 0)), h_spec],
        out_shape=[jax.ShapeDtypeStruct((n // S5_STEPS, S5_ROW), F32), h_shape],
        scratch_shapes=[pltpu.VMEM((rows, S5_BLOCK_STATE), F32),
                        pltpu.VMEM((rows, S5_BLOCK_STATE), F32)],
        compiler_params=pltpu.CompilerParams(dimension_semantics=("arbitrary",),
                                             vmem_limit_bytes=V7X_VMEM_LIMIT_BYTES),
        name="s5",
    )(u8, h0, wx, wy, ap)
    hre_new, him_new = _lanes_to_state(hl.reshape(nseq, S5_STATE_ROW))
    return y8, hre_new, him_new


RW_CHUNK = 64
RW_STEP_CHUNKS = 4
RW_GROUPS = 2
RW_GROUP_HEADS = N_RWKV_HEADS // RW_GROUPS
RW_GROUP_LANES = D_RWKV // RW_GROUPS


def _split3(x):
    hi = x.astype(BF16)
    r1 = x - hi.astype(F32)
    mid = r1.astype(BF16)
    lo = (r1 - mid.astype(F32)).astype(BF16)
    return hi, mid, lo


def _bd(x, hm_ref):
    xb = x.astype(BF16)
    return jnp.concatenate([xb * hm_ref[h] for h in range(RW_GROUP_HEADS)], axis=0)


def _unbd(f, hm_ref):
    n = RWKV_HEAD
    out = f[:n] * hm_ref[0].astype(F32)
    for h in range(1, RW_GROUP_HEADS):
        out = out + f[h * n:(h + 1) * n] * hm_ref[h].astype(F32)
    return out


def _rwkv_kernel(z_ref, s0_ref, w0_ref, w2_ref, a0_ref, a2_ref, g2_ref, kk_ref, ka_ref, rk_ref,
                 lg_ref, lb_ref, ones_ref, hm_ref, cm_ref, tri_ref, seg_ref, y_ref, sl_ref, ypre_ref,
                 *scratch, seg_len, step_chunks):
    C = RW_CHUNK
    N = RWKV_HEAD
    GL = RW_GROUP_LANES
    chained = seg_len == C
    if chained:
        (st_ref,) = scratch
        b = pl.program_id(1)

        @pl.when(b == 0)
        def _():
            for g in range(RW_GROUPS):
                st_ref[g] = jnp.concatenate(
                    [s0_ref[0, h] for h in range(g * RW_GROUP_HEADS, (g + 1) * RW_GROUP_HEADS)], axis=1)

    ones_bd = ones_ref[...]
    z = z_ref[...]
    r = z[:, :D_RWKV]
    k = z[:, D_RWKV:2 * D_RWKV]
    v = z[:, 2 * D_RWKV:3 * D_RWKV]
    o = 3 * D_RWKV
    xw = z[:, o:o + DECAY_LORA]
    xa = z[:, o + DECAY_LORA:o + DECAY_LORA + AAA_LORA]
    xg = z[:, o + DECAY_LORA + AAA_LORA:]
    wd = -(w0_ref[...] + _mm(jnp.tanh(xw), w2_ref[...]))
    w = -(jnp.maximum(wd, 0.0) + jnp.log1p(jnp.exp(-jnp.abs(wd)))) - 0.5
    lw = -jnp.exp(w)
    a = _sigmoid(a0_ref[...] + _mm(xa, a2_ref[...]))
    out_gate = _mm(_sigmoid(xg), g2_ref[...])
    kk = k * kk_ref[...]
    kk = kk / jnp.maximum(jnp.sqrt(_seg_sum(kk * kk, ones_bd)), L2_EPS)
    kmod = k * (1.0 + (a - 1.0) * ka_ref[...])

    terms = _split3(lw)
    tri, seg = tri_ref[...], seg_ref[...]
    cw = sum(jnp.dot(tri, t, preferred_element_type=F32) for t in terms)
    tot = sum(jnp.dot(seg, t, preferred_element_type=F32) for t in terms)
    w_in = jnp.exp(cw)
    w_tail = jnp.exp(tot - cw)
    w_inv = jnp.exp(-cw)
    w_all = jnp.exp(tot)
    ah = -kk * jnp.exp(cw - lw)
    bh = kk * a * w_inv
    kh = kmod * w_inv
    rh = r * w_in
    bt = kk * a * w_tail
    kt = kmod * w_tail
    strict = cm_ref[0] > 0.0
    incl = cm_ref[1] > 0.0
    eye_cat = cm_ref[2]
    n_fac = seg_len.bit_length() - 1

    items = [(c, g) for c in range(step_chunks) for g in range(RW_GROUPS)]
    sub = lambda t, c, g: t[c * C:(c + 1) * C, g * GL:(g + 1) * GL]
    bd = lambda t: _bd(t, hm_ref)
    A = [sub(ah, c, g) for c, g in items]
    R = [sub(rh, c, g) for c, g in items]
    V = [sub(v, c, g) for c, g in items]
    G = [_mm_nt(jnp.concatenate([A[i], R[i]], axis=0),
                jnp.concatenate([bd(sub(bh, c, g)), bd(sub(kh, c, g))], axis=0))
         for i, (c, g) in enumerate(items)]
    AB = [jnp.where(strict, t[:C, :GL], 0.0) for t in G]
    AK = [jnp.where(strict, t[:C, GL:], 0.0) for t in G]
    RB = [jnp.where(incl, t[C:, :GL], 0.0) for t in G]
    RK = [jnp.where(incl, t[C:, GL:], 0.0) for t in G]
    KV = [_mm(jnp.concatenate([AK[i], RK[i]], axis=0), bd(V[i])) for i in range(len(items))]
    T = [eye_cat + t for t in AB]
    P = [_mm(t, bd(t)) for t in AB]
    for _ in range(n_fac - 2):
        PT = [_mm(jnp.concatenate([P[i], T[i]], axis=0), bd(P[i])) for i in range(len(items))]
        P = [t[:C] for t in PT]
        T = [T[i] + PT[i][C:] for i in range(len(items))]
    T = [T[i] + _mm(T[i], bd(P[i])) for i in range(len(items))]
    X = [_mm(T[i], jnp.concatenate([bd(A[i]), bd(KV[i][:C])], axis=1)) for i in range(len(items))]
    TA = [t[:, :GL] for t in X]
    U0 = [t[:, GL:] for t in X]
    W2 = [_mm(RB[i], jnp.concatenate([bd(TA[i]), bd(U0[i])], axis=1)) for i in range(len(items))]
    Rt = [R[i] + W2[i][:, :GL] for i in range(len(items))]
    Y0 = [W2[i][:, GL:] + KV[i][C:] for i in range(len(items))]
    nseg = C // seg_len
    PhiT, PsiT = {}, {}
    for i, (c, g) in enumerate(items):
        Bt, Kt, Wa = sub(bt, c, g), sub(kt, c, g), sub(w_all, c, g)
        for s in range(nseg):
            ss = slice(s * seg_len, (s + 1) * seg_len)
            PhiT[i, s] = bd(eye_cat * Wa[s * seg_len:s * seg_len + 1, :]
                            + _unbd(_mm_tn(TA[i][ss], Bt[ss]), hm_ref))
            PsiT[i, s] = _unbd(_mm_tn(jnp.concatenate([U0[i][ss], V[i][ss]], axis=0),
                                      jnp.concatenate([Bt[ss], Kt[ss]], axis=0)), hm_ref)
    heads = lambda g: range(g * RW_GROUP_HEADS, (g + 1) * RW_GROUP_HEADS)
    load_state = lambda ref, j, g: jnp.concatenate([ref[j, h] for h in heads(g)], axis=1)
    if chained:
        states = [st_ref[g] for g in range(RW_GROUPS)]
    for i, (c, g) in enumerate(items):
        for s in range(nseg):
            ss = slice(s * seg_len, (s + 1) * seg_len)
            S = states[g] if chained else load_state(s0_ref, c * nseg + s, g)
            ypre_ref[c * C + s * seg_len:c * C + (s + 1) * seg_len, g * GL:(g + 1) * GL] = (
                _mm_nt(Rt[i][ss], bd(S)) + Y0[i][ss])
            S = _mm(S, PhiT[i, s]) + PsiT[i, s]
            if chained:
                states[g] = S
            else:
                for h in heads(g):
                    sl_ref[c * nseg + s, h] = S[:, (h % RW_GROUP_HEADS) * N:(h % RW_GROUP_HEADS + 1) * N]
    if chained:
        for g in range(RW_GROUPS):
            st_ref[g] = states[g]
            for h in heads(g):
                sl_ref[0, h] = states[g][:, (h % RW_GROUP_HEADS) * N:(h % RW_GROUP_HEADS + 1) * N]
    y = ypre_ref[...]

    mu = _seg_sum(y, ones_bd) * (1.0 / N)
    d = y - mu
    var = _seg_sum(d * d, ones_bd) * (1.0 / N)
    yn = d * lax.rsqrt(var + GN_EPS) * lg_ref[...] + lb_ref[...]
    bonus = _seg_sum(r * kmod * rk_ref[...], ones_bd) * v
    y_ref[...] = (yn + bonus) * out_gate


def _rwkv_constants(seg_len, rows):
    n, gh, gl = RWKV_HEAD, RW_GROUP_HEADS, RW_GROUP_LANES
    lane_head = np.arange(gl) // n
    hm = (lane_head[None, None, :] == np.arange(gh)[:, None, None]) * np.ones((gh, n, gl))
    i = np.arange(RW_CHUNK)[:, None]
    j = (np.arange(gl) % n)[None, :]
    same = (i // seg_len) == (j // seg_len)
    cm = np.stack([(j < i) & same, (j <= i) & same, j == i]).astype(np.float32)
    ri = np.arange(rows)[:, None]
    rj = np.arange(rows)[None, :]
    same_r = (ri // seg_len) == (rj // seg_len)
    tri = ((rj <= ri) & same_r).astype(np.float32)
    ones_bd = np.kron(np.eye(N_RWKV_HEADS), np.ones((n, n)))
    return (jnp.asarray(ones_bd, BF16), jnp.asarray(hm, BF16), jnp.asarray(cm, F32),
            jnp.asarray(tri, BF16), jnp.asarray(same_r.astype(np.float32), BF16))


def _rwkv(z, s0, p, seq_len):
    n = z.shape[0]
    step_chunks = min(RW_STEP_CHUNKS, (seq_len if seq_len >= RW_CHUNK else n) // RW_CHUNK)
    nseq = n // seq_len
    nh, hd = N_RWKV_HEADS, RWKV_HEAD
    seg_len = min(seq_len, RW_CHUNK)
    rows = RW_CHUNK * step_chunks
    chained = seq_len >= RW_CHUNK
    if chained:
        steps = seq_len // rows
        grid = (nseq, steps)
        row_map = lambda s, b: (s * steps + b, 0)
        state_spec = pl.BlockSpec((1, nh, hd, hd), lambda s, b: (s, 0, 0, 0))
        scratch = [pltpu.VMEM((rows, D_RWKV), F32), pltpu.VMEM((RW_GROUPS, hd, RW_GROUP_LANES), F32)]
    else:
        grid = (n // rows, 1)
        row_map = lambda s, b: (s, 0)
        state_spec = pl.BlockSpec((rows // seq_len, nh, hd, hd), lambda s, b: (s, 0, 0, 0))
        scratch = [pltpu.VMEM((rows, D_RWKV), F32)]
    small = [p['w0'], p['w2'], p['a0'], p['a2'], p['g2'], p['k_k'], p['k_a'], p['r_k'],
             p['lnx_g'], p['lnx_b'], *_rwkv_constants(seg_len, rows)]
    y, s_new = pl.pallas_call(
        functools.partial(_rwkv_kernel, seg_len=seg_len, step_chunks=step_chunks),
        grid=grid,
        in_specs=[pl.BlockSpec((rows, N_SHIFT), row_map), state_spec]
        + [_const_spec(t.shape) for t in small],
        out_specs=[pl.BlockSpec((rows, D_RWKV), row_map), state_spec],
        out_shape=[jax.ShapeDtypeStruct((n, D_RWKV), F32),
                   jax.ShapeDtypeStruct((nseq, nh, hd, hd), F32)],
        scratch_shapes=scratch,
        compiler_params=pltpu.CompilerParams(dimension_semantics=("arbitrary", "arbitrary"),
                                             vmem_limit_bytes=V7X_VMEM_LIMIT_BYTES),
        name="rwkv7",
    )(z, s0, *small)
    return y, s_new


def _ffn_kernel(h_ref, y8_ref, u8_ref, yb_ref, pe_ref, c1_ref, c2_ref, d8_ref, wglu_ref, bglu_ref,
                wout_ref, g2_ref, wup_ref, cw_ref, cb_ref, wdn_ref, wple_ref, wpg_ref, gf_ref,
                *rest, seq_len, tm, final):
    i = pl.program_id(0)
    long_seq = seq_len >= tm
    if long_seq:
        o_ref, ga_ref, gb_ref, fold_ref, carry_ref = rest

        @pl.when(i == 0)
        def _():
            carry_ref[...] = jnp.zeros_like(carry_ref)
    else:
        expand_ref, sel_last_ref, sel_prev_ref, o_ref, ga_ref, gb_ref, fold_ref = rest
        expand = expand_ref[...]
    ya = y8_ref[...] + d8_ref[...] * u8_ref[...]
    c_gelu = math.sqrt(2.0 / math.pi)
    ya = ya * (0.5 * (1.0 + jnp.tanh(c_gelu * (ya + 0.044715 * (ya * ya * ya)))))
    ya = _unfold_rows(ya, fold_ref)
    ya = ya * _sigmoid(_mm(ya, wglu_ref[...]) + bglu_ref[...])
    h1 = (h_ref[...] + _mm(ya, wout_ref[:D_SSM, :]) + _mm(yb_ref[...], wout_ref[D_SSM:, :]))
    x2 = _rmsnorm(h1, g2_ref[...]).astype(BF16)
    row = lax.broadcasted_iota(jnp.int32, (tm, 1), 0)
    if long_seq:
        first = (i % (seq_len // tm)) == 0
    else:
        t = row % seq_len
    def up(c):
        cs = slice(c * FF_CHUNK, (c + 1) * FF_CHUNK)
        gs = slice(D_FF + c * FF_CHUNK, D_FF + (c + 1) * FF_CHUNK)
        return (jnp.dot(x2, wup_ref[:, cs], preferred_element_type=F32),
                jnp.dot(x2, wup_ref[:, gs], preferred_element_type=F32))

    n_chunks = D_FF // FF_CHUNK
    acc = jnp.zeros((tm, D_MODEL), F32)
    nxt = up(0)
    for c in range(n_chunks):
        cs = slice(c * FF_CHUNK, (c + 1) * FF_CHUNK)
        val, gate = nxt
        if c + 1 < n_chunks:
            nxt = up(c + 1)
        r1 = pltpu.roll(gate, 1, 0)
        r2 = pltpu.roll(gate, 2, 0)
        if long_seq:
            m1 = jnp.where(first, c1_ref[0][:, cs], carry_ref[1:2, cs])
            m2 = jnp.where(first, c2_ref[0][:, cs], carry_ref[0:1, cs])
            p1 = jnp.where(row == 0, m1, r1)
            p2 = jnp.where(row == 0, m2, jnp.where(row == 1, m1, r2))
            carry_ref[:, cs] = gate[tm - 2:tm, :]
            ga_ref[0, :, cs] = gate[tm - 2:tm - 1, :]
            gb_ref[0, :, cs] = gate[tm - 1:tm, :]
        else:
            m1 = _exact_rows(expand, c1_ref[:, cs])
            m2 = _exact_rows(expand, c2_ref[:, cs])
            p1 = jnp.where(t == 0, m1, r1)
            p2 = jnp.where(t == 0, m2, jnp.where(t == 1, m1, r2))
            ga_ref[:, cs] = _exact_rows(sel_prev_ref[...], gate)
            gb_ref[:, cs] = _exact_rows(sel_last_ref[...], gate)
        conv = (cb_ref[:, cs] + cw_ref[2:3, cs] * gate + cw_ref[1:2, cs] * p1 + cw_ref[0:1, cs] * p2)
        act = conv * _sigmoid(conv) * val
        acc = acc + jnp.dot(act.astype(BF16), wdn_ref[cs, :], preferred_element_type=F32)
    h2 = h1 + acc
    h3 = h2 + _mm(pe_ref[...], wple_ref[...]) * _sigmoid(_mm(h2, wpg_ref[...]))
    if final:
        h3 = _rmsnorm(h3, gf_ref[...])
    o_ref[...] = h3


def _ffn(h, y8, u8, yb, pe, conv0, p, seq_len, tm, final):
    n = h.shape[0]
    nseq = n // seq_len
    row_spec = lambda w: pl.BlockSpec((tm, w), lambda i: (i, 0))
    fold_spec = pl.BlockSpec((tm // S5_STEPS, S5_ROW), lambda i: (i, 0))
    scratch = [pltpu.VMEM((FOLD_LANE_BLOCKS, tm, LANES), F32)]
    if seq_len >= tm:
        tps = seq_len // tm
        seq_spec = pl.BlockSpec((1, 1, D_FF), lambda i: (i // tps, 0, 0))
        seq_shape = (nseq, 1, D_FF)
        extra = []
        scratch.append(pltpu.VMEM((2, D_FF), F32))
    else:
        seq_spec = pl.BlockSpec((tm // seq_len, D_FF), lambda i: (i, 0))
        seq_shape = (nseq, D_FF)
        extra = list(_seq_selectors(seq_len, tm))
    weights = [jnp.tile(p['d'], (1, S5_STEPS)), p['w_glu'], p['b_glu'], p['w_out'], p['norm2_g'],
               p['w_up'], p['conv_w'], p['conv_b'], p['w_down'], p['w_ple'], p['w_pg'],
               p['final_g']] + extra
    out, ga, gb = pl.pallas_call(
        functools.partial(_ffn_kernel, seq_len=seq_len, tm=tm, final=final),
        grid=(n // tm,),
        in_specs=[row_spec(D_MODEL), fold_spec, fold_spec, row_spec(D_RWKV), row_spec(D_PLE),
                  seq_spec, seq_spec] + [_const_spec(t.shape) for t in weights],
        out_specs=[row_spec(D_MODEL), seq_spec, seq_spec],
        out_shape=[jax.ShapeDtypeStruct((n, D_MODEL), F32)]
        + [jax.ShapeDtypeStruct(seq_shape, F32)] * 2,
        scratch_shapes=scratch,
        compiler_params=pltpu.CompilerParams(dimension_semantics=("arbitrary",),
                                             vmem_limit_bytes=V7X_VMEM_LIMIT_BYTES),
        name="ffn",
    )(h, y8, u8, yb, pe, conv0[:, 1].reshape(seq_shape), conv0[:, 0].reshape(seq_shape), *weights)
    conv_new = jnp.stack([ga.reshape(nseq, D_FF), gb.reshape(nseq, D_FF)], axis=1)
    return out, conv_new


def _layer(h, pe, st, p, s5_tables, seq_len, tm, final):
    ssm_re0, ssm_im0, rwkv0, shift0, conv0 = st
    u8, z, shift_new = _inproj(h, p['norm1_g'], p['w_in'], p['shift_mu'], shift0, seq_len, tm)
    y8, hre, him = _s5(u8, ssm_re0, ssm_im0, s5_tables, seq_len)
    yb, s_last = _rwkv(z, rwkv0, p, seq_len)
    h, conv_new = _ffn(h, y8, u8, yb, pe, conv0, p, seq_len, tm, final)
    return h, (hre, him, s_last, shift_new, conv_new)


def _layer_params(i, w):
    row = lambda t: t[i].reshape(1, -1).astype(F32)
    bf = lambda t: t[i].astype(BF16)
    return {
        'norm1_g': row(w['norm1_g']), 'w_in': bf(w['w_in']), 'shift_mu': row(w['shift_mu']),
        'd': row(w['ssm_d']), 'w_glu': bf(w['ssm_w_glu']), 'b_glu': row(w['ssm_b_glu']),
        'w0': row(w['rwkv_w0']), 'w2': bf(w['rwkv_w2']), 'a0': row(w['rwkv_a0']),
        'a2': bf(w['rwkv_a2']), 'g2': bf(w['rwkv_g2']), 'k_k': row(w['rwkv_k_k']),
        'k_a': row(w['rwkv_k_a']), 'r_k': row(w['rwkv_r_k']), 'lnx_g': row(w['rwkv_lnx_g']),
        'lnx_b': row(w['rwkv_lnx_b']), 'w_out': bf(w['w_out']), 'norm2_g': row(w['norm2_g']),
        'w_up': bf(w['w_up']), 'conv_w': w['conv_w'][i].astype(F32), 'conv_b': row(w['conv_b']),
        'w_down': bf(w['w_down']), 'w_ple': bf(w['w_ple']), 'w_pg': bf(w['w_pg']),
        'final_g': w['final_g'].reshape(1, -1).astype(F32),
    }


def _forward(x_prompt, x_sample, p_prompt, p_sample, state_ssm_re, state_ssm_im, state_rwkv,
             state_shift, state_conv, w, tm_prompt, tm_sample):
    depth = w['w_in'].shape[0]
    bp, lp, _ = x_prompt.shape
    bs, ls, _ = x_sample.shape
    hp = x_prompt.reshape(bp * lp, D_MODEL).astype(F32)
    hs = x_sample.reshape(bs * ls, D_MODEL).astype(F32)
    zero_st = (jnp.zeros((bp, N_SSM_GROUPS, SSM_STATE), F32),
               jnp.zeros((bp, N_SSM_GROUPS, SSM_STATE), F32),
               jnp.zeros((bp, N_RWKV_HEADS, RWKV_HEAD, RWKV_HEAD), F32),
               jnp.zeros((bp, N_SHIFT), F32),
               jnp.zeros((bp, 2, D_FF), F32))
    new_p = [[] for _ in range(5)]
    new_s = [[] for _ in range(5)]
    for i in range(depth):
        p = _layer_params(i, w)
        tables = _s5_tables(w['ssm_lam_re'][i], w['ssm_lam_im'][i], w['ssm_log_dt'][i],
                            w['ssm_b_re'][i], w['ssm_b_im'][i], w['ssm_c_re'][i], w['ssm_c_im'][i])
        final = i == depth - 1
        hp, stp = _layer(hp, p_prompt[i].reshape(bp * lp, D_PLE).astype(F32), zero_st, p, tables,
                         lp, tm_prompt, final)
        st_in = (state_ssm_re[i].astype(F32), state_ssm_im[i].astype(F32),
                 state_rwkv[i].astype(F32), state_shift[i].astype(F32), state_conv[i].astype(F32))
        hs, sts = _layer(hs, p_sample[i].reshape(bs * ls, D_PLE).astype(F32), st_in, p, tables,
                         ls, tm_sample, final)
        for j in range(5):
            new_p[j].append(stp[j])
            new_s[j].append(sts[j])
    y_prompt = hp.reshape(bp, lp, D_MODEL).astype(x_prompt.dtype)
    y_sample = hs.reshape(bs, ls, D_MODEL).astype(x_sample.dtype)
    dts = (state_ssm_re.dtype, state_ssm_im.dtype, state_rwkv.dtype, state_shift.dtype,
           state_conv.dtype)
    outs_p = tuple(jnp.stack(new_p[j]).astype(dts[j]) for j in range(5))
    outs_s = tuple(jnp.stack(new_s[j]).astype(dts[j]) for j in range(5))
    return (y_prompt, y_sample) + outs_p + outs_s


def kernel(x_prompt, x_sample, p_prompt, p_sample, state_ssm_re, state_ssm_im, state_rwkv, state_shift, state_conv, norm1_g, w_in, shift_mu, ssm_lam_re, ssm_lam_im, ssm_log_dt, ssm_b_re, ssm_b_im, ssm_c_re, ssm_c_im, ssm_d, ssm_w_glu, ssm_b_glu, rwkv_w0, rwkv_w2, rwkv_a0, rwkv_a2, rwkv_g2, rwkv_k_k, rwkv_k_a, rwkv_r_k, rwkv_lnx_g, rwkv_lnx_b, w_out, norm2_g, w_up, conv_w, conv_b, w_down, w_ple, w_pg, final_g):
    w = dict(norm1_g=norm1_g, w_in=w_in, shift_mu=shift_mu, ssm_lam_re=ssm_lam_re,
             ssm_lam_im=ssm_lam_im, ssm_log_dt=ssm_log_dt, ssm_b_re=ssm_b_re, ssm_b_im=ssm_b_im,
             ssm_c_re=ssm_c_re, ssm_c_im=ssm_c_im, ssm_d=ssm_d, ssm_w_glu=ssm_w_glu,
             ssm_b_glu=ssm_b_glu, rwkv_w0=rwkv_w0, rwkv_w2=rwkv_w2, rwkv_a0=rwkv_a0,
             rwkv_a2=rwkv_a2, rwkv_g2=rwkv_g2, rwkv_k_k=rwkv_k_k, rwkv_k_a=rwkv_k_a,
             rwkv_r_k=rwkv_r_k, rwkv_lnx_g=rwkv_lnx_g, rwkv_lnx_b=rwkv_lnx_b, w_out=w_out,
             norm2_g=norm2_g, w_up=w_up, conv_w=conv_w, conv_b=conv_b, w_down=w_down,
             w_ple=w_ple, w_pg=w_pg, final_g=final_g)
    lp = x_prompt.shape[1]
    ns = x_sample.shape[0] * x_sample.shape[1]
    return _forward(x_prompt, x_sample, p_prompt, p_sample, state_ssm_re, state_ssm_im,
                    state_rwkv, state_shift, state_conv, w,
                    tm_prompt=min(512, lp), tm_sample=min(256, ns))
```

```python
import functools
import math

import numpy as np
import jax
import jax.numpy as jnp
from jax import lax
from jax.experimental import pallas as pl
from jax.experimental.pallas import tpu as pltpu

F32 = jnp.float32
BF16 = jnp.bfloat16
HIGHEST = lax.Precision.HIGHEST

D_MODEL = 1024
D_SSM = 512
D_RWKV = 512
SSM_GROUP = 16
N_SSM_GROUPS = 32
SSM_STATE = 64
RWKV_HEAD = 64
N_RWKV_HEADS = 8
DECAY_LORA = 64
AAA_LORA = 64
GATE_LORA = 128
N_SHIFT = 3 * D_RWKV + DECAY_LORA + AAA_LORA + GATE_LORA
N_IN = D_SSM + N_SHIFT
D_FF = 2816
D_PLE = 256
RMS_EPS = 1e-6
GN_EPS = 64e-5
L2_EPS = 1e-12

LANES = 128
FOLD_LANE_BLOCKS = D_SSM // LANES
S5_STEPS = 8
S5_ROW = S5_STEPS * D_SSM
S5_LANE_BLOCKS = 4
S5_BLOCK_STATE = 2 * 8 * SSM_STATE
S5_STATE_ROW = S5_LANE_BLOCKS * S5_BLOCK_STATE
FF_CHUNK = 256
V7X_VMEM_LIMIT_BYTES = 56 * 1024 * 1024


def _mm(a, b):
    return jnp.dot(a.astype(BF16), b.astype(BF16), preferred_element_type=F32)


def _mm_nt(a, b):
    return lax.dot_general(a.astype(BF16), b.astype(BF16), (((1,), (1,)), ((), ())),
                           preferred_element_type=F32)


def _mm_tn(a, b):
    return lax.dot_general(a.astype(BF16), b.astype(BF16), (((0,), (0,)), ((), ())),
                           preferred_element_type=F32)


def _seg_sum(x, ones_bd):
    hi = x.astype(BF16)
    lo = (x - hi.astype(F32)).astype(BF16)
    return (jnp.dot(hi, ones_bd, preferred_element_type=F32)
            + jnp.dot(lo, ones_bd, preferred_element_type=F32))


def _rmsnorm(x, g):
    return x * lax.rsqrt(jnp.mean(x * x, axis=-1, keepdims=True) + RMS_EPS) * g


def _sigmoid(x):
    return 1.0 / (1.0 + jnp.exp(-x))


def _const_spec(shape):
    nd = len(shape)
    return pl.BlockSpec(shape, lambda *_: (0,) * nd, pipeline_mode=pl.Buffered(1))


def _layer_spec(t, layer):
    nd = t.ndim - 1
    return pl.BlockSpec((None,) + t.shape[1:], lambda *_: (layer,) + (0,) * nd,
                        pipeline_mode=pl.Buffered(1))


def _exact_rows(sel, x):
    return sum(jnp.dot(sel, t, preferred_element_type=F32) for t in _split3(x))


def _fold_rows(x, scr_ref, out_ref):
    tm = x.shape[0]
    for q in range(FOLD_LANE_BLOCKS):
        scr_ref[q] = x[:, q * LANES:(q + 1) * LANES]
    for s in range(S5_STEPS):
        for q in range(FOLD_LANE_BLOCKS):
            c0 = s * D_SSM + q * LANES
            out_ref[:, c0:c0 + LANES] = scr_ref[q, pl.ds(s, tm // S5_STEPS, stride=S5_STEPS), :]


def _unfold_rows(x8, scr_ref):
    tm = x8.shape[0] * S5_STEPS
    for s in range(S5_STEPS):
        for q in range(FOLD_LANE_BLOCKS):
            c0 = s * D_SSM + q * LANES
            scr_ref[q, pl.ds(s, tm // S5_STEPS, stride=S5_STEPS), :] = x8[:, c0:c0 + LANES]
    return jnp.concatenate([scr_ref[q] for q in range(FOLD_LANE_BLOCKS)], axis=1)


def _seq_selectors(seq_len, tm):
    r = np.arange(tm)
    b = np.arange(tm // seq_len)
    expand = (r[:, None] // seq_len == b[None, :]).astype(np.float32)
    last = (r[None, :] == (b[:, None] + 1) * seq_len - 1).astype(np.float32)
    prev = (r[None, :] == (b[:, None] + 1) * seq_len - 2).astype(np.float32)
    return jnp.asarray(expand, BF16), jnp.asarray(last, BF16), jnp.asarray(prev, BF16)


def _inproj_kernel(h_ref, g_ref, w_ref, mu_ref, init_ref, *rest, seq_len, tm):
    i = pl.program_id(0)
    long_seq = seq_len >= tm
    if long_seq:
        u8_ref, z_ref, last_ref, fold_ref, carry_ref = rest

        @pl.when(i == 0)
        def _():
            carry_ref[...] = jnp.zeros_like(carry_ref)
    else:
        expand_ref, sel_last_ref, u8_ref, z_ref, last_ref, fold_ref = rest
    xn = _rmsnorm(h_ref[...], g_ref[...])
    proj = jnp.dot(xn.astype(BF16), w_ref[...], preferred_element_type=F32)
    _fold_rows(proj[:, :D_SSM], fold_ref, u8_ref)
    zr = proj[:, D_SSM:]
    rolled = pltpu.roll(zr, 1, 0)
    row = lax.broadcasted_iota(jnp.int32, (tm, 1), 0)
    if long_seq:
        first = (i % (seq_len // tm)) == 0
        row0 = jnp.where(first, init_ref[0], carry_ref[...])
        prev = jnp.where(row == 0, row0, rolled)
        carry_ref[...] = zr[tm - 1:tm, :]
        last_ref[0] = zr[tm - 1:tm, :]
    else:
        prev = jnp.where(row % seq_len == 0, _exact_rows(expand_ref[...], init_ref[...]), rolled)
        last_ref[...] = _exact_rows(sel_last_ref[...], zr)
    z_ref[...] = zr + (prev - zr) * mu_ref[...]


def _inproj(h, p, layer, shift0, seq_len, tm):
    n = h.shape[0]
    nseq = n // seq_len
    fold_scratch = pltpu.VMEM((FOLD_LANE_BLOCKS, tm, LANES), F32)
    if seq_len >= tm:
        tps = seq_len // tm
        seq_spec = pl.BlockSpec((1, 1, N_SHIFT), lambda i: (i // tps, 0, 0))
        init_spec = seq_spec
        extra, extra_specs = [], []
        init = shift0.reshape(nseq, 1, N_SHIFT)
        last_shape = jax.ShapeDtypeStruct((nseq, 1, N_SHIFT), F32)
        scratch = [fold_scratch, pltpu.VMEM((1, N_SHIFT), F32)]
    else:
        spt = tm // seq_len
        seq_spec = pl.BlockSpec((spt, N_SHIFT), lambda i: (i, 0))
        init_spec = pl.BlockSpec((None, spt, N_SHIFT), lambda i: (layer, i, 0))
        expand, sel_last, _ = _seq_selectors(seq_len, tm)
        extra = [expand, sel_last]
        extra_specs = [_const_spec(expand.shape), _const_spec(sel_last.shape)]
        init = p['state_shift']
        last_shape = jax.ShapeDtypeStruct((nseq, N_SHIFT), F32)
        scratch = [fold_scratch]
    g1, w_in, mu = p['norm1_g'], p['w_in'], p['shift_mu']
    u8, z, last = pl.pallas_call(
        functools.partial(_inproj_kernel, seq_len=seq_len, tm=tm),
        grid=(n // tm,),
        in_specs=[pl.BlockSpec((tm, D_MODEL), lambda i: (i, 0)),
                  _layer_spec(g1, layer), _layer_spec(w_in, layer), _layer_spec(mu, layer),
                  init_spec] + extra_specs,
        out_specs=[pl.BlockSpec((tm // S5_STEPS, S5_ROW), lambda i: (i, 0)),
                   pl.BlockSpec((tm, N_SHIFT), lambda i: (i, 0)),
                   seq_spec],
        out_shape=[jax.ShapeDtypeStruct((n // S5_STEPS, S5_ROW), F32),
                   jax.ShapeDtypeStruct((n, N_SHIFT), F32),
                   last_shape],
        scratch_shapes=scratch,
        compiler_params=pltpu.CompilerParams(dimension_semantics=("arbitrary",),
                                             vmem_limit_bytes=V7X_VMEM_LIMIT_BYTES),
        name="inproj",
    )(h, g1, w_in, mu, init, *extra)
    return u8, z, last.reshape(nseq, N_SHIFT)


def _s5_prep_kernel(lr_ref, li_ref, ldt_ref, br_ref, bi_ref, cr_ref, ci_ref, tile_ref, sel_ref,
                    wx_ref, wy_ref, apr_ref, api_ref):
    lr, li = lr_ref[...], li_ref[...]
    dt = jnp.exp(ldt_ref[...])
    mag = jnp.exp(lr * dt)
    ar = mag * jnp.cos(li * dt)
    ai = mag * jnp.sin(li * dt)
    den = lr * lr + li * li
    nr = ar - 1.0
    fr = (nr * lr + ai * li) / den
    fi = (ai * lr - nr * li) / den
    br, bi = br_ref[...], bi_ref[...]
    bbr = fr * br - fi * bi
    bbi = fr * bi + fi * br
    cr, ci = cr_ref[...], ci_ref[...]
    pr = [jnp.ones_like(ar)]
    pi = [jnp.zeros_like(ar)]
    for _ in range(S5_STEPS):
        pr.append(pr[-1] * ar - pi[-1] * ai)
        pi.append(pr[-2] * ai + pi[-1] * ar)
    nrow = lr.shape[0]
    blk = nrow // S5_LANE_BLOCKS
    half = S5_BLOCK_STATE // 2
    ri = lax.broadcasted_iota(jnp.int32, (nrow, nrow), 0)
    ci_ = lax.broadcasted_iota(jnp.int32, (nrow, nrow), 1)
    same_group = (ri // SSM_GROUP) == (ci_ // SSM_GROUP)
    keep_in = ((ri % blk) // SSM_GROUP) == (ci_ // SSM_STATE)
    keep_out = (ri // SSM_STATE) == ((ci_ % blk) // SSM_GROUP)
    tile, sel = tile_ref[...], sel_ref[...]
    nt = (((1,), (1,)), ((), ()))
    wy_ref[...] = jnp.zeros(wy_ref.shape, wy_ref.dtype)
    for s in range(S5_STEPS):
        qr, qi = pr[S5_STEPS - 1 - s], pi[S5_STEPS - 1 - s]
        wr = qr * bbr - qi * bbi
        wi = qr * bbi + qi * bbr
        for h, w_ in enumerate((wr, wi)):
            t = jnp.dot(w_.astype(BF16), tile, preferred_element_type=F32)
            t = jnp.where(keep_in, t, 0.0).astype(BF16)
            for q in range(S5_LANE_BLOCKS):
                wx_ref[q, s * blk:(s + 1) * blk, h * half:(h + 1) * half] = t[q * blk:(q + 1) * blk, :]
        tau = S5_STEPS - 1 - s
        kt = (lax.dot_general(wr, cr, nt, precision=HIGHEST, preferred_element_type=F32)
              - lax.dot_general(wi, ci, nt, precision=HIGHEST, preferred_element_type=F32))
        kt = jnp.where(same_group, kt, 0.0).astype(BF16)
        for q in range(S5_LANE_BLOCKS):
            kq = kt[q * blk:(q + 1) * blk, q * blk:(q + 1) * blk]
            for s_in in range(S5_STEPS - tau):
                s_out = s_in + tau
                r0 = S5_BLOCK_STATE + s_in * blk
                wy_ref[q, s_out // 2, r0:r0 + blk, (s_out % 2) * blk:(s_out % 2 + 1) * blk] = kq
        mr = cr * pr[s + 1] - ci * pi[s + 1]
        mi = cr * pi[s + 1] + ci * pr[s + 1]
        for h, m_ in enumerate((mr, -mi)):
            t = lax.dot_general(sel, m_.astype(BF16), nt, preferred_element_type=F32)
            t = jnp.where(keep_out, t, 0.0).astype(BF16)
            for q in range(S5_LANE_BLOCKS):
                wy_ref[q, s // 2, h * half:(h + 1) * half, (s % 2) * blk:(s % 2 + 1) * blk] = (
                    t[:, q * blk:(q + 1) * blk])
    a8r, a8i = pr[S5_STEPS], pi[S5_STEPS]
    er, ei = a8r, a8i
    for n in range(S5_STEPS):
        apr_ref[n] = er
        api_ref[n] = ei
        er, ei = er * a8r - ei * a8i, er * a8i + ei * a8r


def _s5_tables(lam_re, lam_im, log_dt, b_re, b_im, c_re, c_im):
    G, P, K = N_SSM_GROUPS, SSM_STATE, SSM_GROUP
    Q, GB = S5_LANE_BLOCKS, G // S5_LANE_BLOCKS
    rep = lambda t: jnp.repeat(t, K, axis=0)
    tile = np.tile(np.eye(P, dtype=np.float32), (1, GB))
    args = (rep(lam_re), rep(lam_im), rep(jnp.broadcast_to(log_dt[:, None], (G, P))),
            jnp.swapaxes(b_re, 1, 2).reshape(G * K, P), jnp.swapaxes(b_im, 1, 2).reshape(G * K, P),
            c_re.reshape(G * K, P), c_im.reshape(G * K, P),
            jnp.asarray(tile, BF16), jnp.asarray(tile.T, BF16))
    t3 = jax.ShapeDtypeStruct((S5_STEPS, G * K, P), F32)
    wx, wy, apr, api = pl.pallas_call(
        _s5_prep_kernel,
        out_shape=[jax.ShapeDtypeStruct((Q, S5_STEPS * GB * K, S5_BLOCK_STATE), BF16),
                   jax.ShapeDtypeStruct((Q, S5_STEPS // 2, S5_BLOCK_STATE + S5_STEPS * GB * K,
                                         2 * GB * K), BF16),
                   t3, t3],
        compiler_params=pltpu.CompilerParams(vmem_limit_bytes=V7X_VMEM_LIMIT_BYTES),
        name="s5_prep",
    )(*args)
    ap = jnp.concatenate([t[:, ::K, :].reshape(S5_STEPS, Q, GB * P) for t in (apr, api)], axis=-1)
    return wx, wy, ap.reshape(S5_STEPS, S5_STATE_ROW)


def _state_to_lanes(h_re, h_im):
    n = h_re.shape[0]
    parts = [t.reshape(n, S5_LANE_BLOCKS, -1) for t in (h_re, h_im)]
    return jnp.concatenate(parts, axis=-1).reshape(n, S5_STATE_ROW)


def _lanes_to_state(h):
    n = h.shape[0]
    h = h.reshape(n, S5_LANE_BLOCKS, 2, N_SSM_GROUPS // S5_LANE_BLOCKS, SSM_STATE)
    return (h[:, :, 0].reshape(n, N_SSM_GROUPS, SSM_STATE), h[:, :, 1].reshape(n, N_SSM_GROUPS, SSM_STATE))


def _s5_kernel(u_ref, h0_ref, wx_ref, wy_ref, ap_ref, y_ref, hl_ref, hs_ref, hp_ref, *, rows, scan):
    half = S5_BLOCK_STATE // 2
    cw = D_SSM // S5_LANE_BLOCKS
    ub = u_ref[...].astype(BF16)
    sub3 = lax.broadcasted_iota(jnp.int32, (1, 8, 1), 1)
    sub2 = lax.broadcasted_iota(jnp.int32, (8, 1), 0)
    for q in range(S5_LANE_BLOCKS):
        us = [ub[:, s * D_SSM + q * cw: s * D_SSM + (q + 1) * cw] for s in range(S5_STEPS)]
        ucat = jnp.concatenate(us, axis=1)
        x = jnp.dot(ucat, wx_ref[q], preferred_element_type=F32)
        lo = q * S5_BLOCK_STATE
        apq = ap_ref[:, lo:lo + S5_BLOCK_STATE]
        if scan:
            xr = x[:, :half].reshape(rows // 8, 8, half)
            xi = x[:, half:].reshape(rows // 8, 8, half)
            for k in (1, 2, 4):
                er = apq[k - 1:k, :half].reshape(1, 1, half)
                ei = apq[k - 1:k, half:].reshape(1, 1, half)
                sr = jnp.where(sub3 >= k, pltpu.roll(xr, k, 1), 0.0)
                si = jnp.where(sub3 >= k, pltpu.roll(xi, k, 1), 0.0)
                xr, xi = xr + er * sr - ei * si, xi + er * si + ei * sr
            hs_ref[:, :half] = xr.reshape(rows, half)
            hs_ref[:, half:] = xi.reshape(rows, half)
            pwr, pwi = apq[:, :half], apq[:, half:]
            c0r = h0_ref[0][:, lo:lo + half]
            c0i = h0_ref[0][:, lo + half:lo + S5_BLOCK_STATE]

            def body(j, carry):
                cr, ci = carry
                sl = pl.ds(pl.multiple_of(j * 8, 8), 8)
                hr = hs_ref[sl, :half] + pwr * cr - pwi * ci
                hi = hs_ref[sl, half:] + pwr * ci + pwi * cr
                hp_ref[sl, :half] = jnp.where(sub2 == 0, cr, pltpu.roll(hr, 1, 0))
                hp_ref[sl, half:] = jnp.where(sub2 == 0, ci, pltpu.roll(hi, 1, 0))
                return hr[7:8, :], hi[7:8, :]

            cr, ci = lax.fori_loop(0, rows // 8, body, (c0r, c0i))
            hl_ref[0, :, lo:lo + half] = cr
            hl_ref[0, :, lo + half:lo + S5_BLOCK_STATE] = ci
            hprev = hp_ref[...].astype(BF16)
        else:
            h0r = h0_ref[:, lo:lo + half]
            h0i = h0_ref[:, lo + half:lo + S5_BLOCK_STATE]
            er, ei = apq[0:1, :half], apq[0:1, half:]
            hl_ref[:, lo:lo + half] = er * h0r - ei * h0i + x[:, :half]
            hl_ref[:, lo + half:lo + S5_BLOCK_STATE] = er * h0i + ei * h0r + x[:, half:]
            hprev = jnp.concatenate([h0r, h0i], axis=1).astype(BF16)
        lhs = jnp.concatenate([hprev, ucat], axis=1)
        for j in range(S5_STEPS // 2):
            kk = S5_BLOCK_STATE + cw * (2 * j + 2)
            y2 = jnp.dot(lhs[:, :kk], wy_ref[q, j, :kk, :], preferred_element_type=F32)
            c0 = (2 * j) * D_SSM + q * cw
            c1 = (2 * j + 1) * D_SSM + q * cw
            y_ref[:, c0:c0 + cw] = y2[:, :cw]
            y_ref[:, c1:c1 + cw] = y2[:, cw:]


def _s5(u8, h_re, h_im, tables, seq_len):
    wx, wy, ap = tables
    n = u8.shape[0] * S5_STEPS
    nseq = n // seq_len
    h0 = _state_to_lanes(h_re, h_im)
    cps = seq_len // S5_STEPS
    scan = cps > 1
    if scan:
        rows = cps
        grid = (nseq,)
        h0 = h0.reshape(nseq, 1, S5_STATE_ROW)
        h_spec = pl.BlockSpec((1, 1, S5_STATE_ROW), lambda i: (i, 0, 0))
        h_shape = jax.ShapeDtypeStruct((nseq, 1, S5_STATE_ROW), F32)
    else:
        rows = min(nseq, 128)
        grid = (nseq // rows,)
        h_spec = pl.BlockSpec((rows, S5_STATE_ROW), lambda i: (i, 0))
        h_shape = jax.ShapeDtypeStruct((nseq, S5_STATE_ROW), F32)
    y8, hl = pl.pallas_call(
        functools.partial(_s5_kernel, rows=rows, scan=scan),
        grid=grid,
        in_specs=[pl.BlockSpec((rows, S5_ROW), lambda i: (i, 0)),
                  h_spec,
                  _const_spec(wx.shape), _const_spec(wy.shape), _const_spec(ap.shape)],
        out_specs=[pl.BlockSpec((rows, S5_ROW), lambda i: (i, 0)), h_spec],
        out_shape=[jax.ShapeDtypeStruct((n // S5_STEPS, S5_ROW), F32), h_shape],
        scratch_shapes=[pltpu.VMEM((rows, S5_BLOCK_STATE), F32),
                        pltpu.VMEM((rows, S5_BLOCK_STATE), F32)],
        compiler_params=pltpu.CompilerParams(dimension_semantics=("arbitrary",),
                                             vmem_limit_bytes=V7X_VMEM_LIMIT_BYTES),
        name="s5",
    )(u8, h0, wx, wy, ap)
    hre_new, him_new = _lanes_to_state(hl.reshape(nseq, S5_STATE_ROW))
    return y8, hre_new, him_new


RW_CHUNK = 64
RW_STEP_CHUNKS = 4
RW_GROUPS = 2
RW_GROUP_HEADS = N_RWKV_HEADS // RW_GROUPS
RW_GROUP_LANES = D_RWKV // RW_GROUPS


def _split3(x):
    hi = x.astype(BF16)
    r1 = x - hi.astype(F32)
    mid = r1.astype(BF16)
    lo = (r1 - mid.astype(F32)).astype(BF16)
    return hi, mid, lo


def _bd(x, hm_ref):
    xb = x.astype(BF16)
    return jnp.concatenate([xb * hm_ref[h] for h in range(RW_GROUP_HEADS)], axis=0)


def _unbd(f, hm_ref):
    n = RWKV_HEAD
    out = f[:n] * hm_ref[0].astype(F32)
    for h in range(1, RW_GROUP_HEADS):
        out = out + f[h * n:(h + 1) * n] * hm_ref[h].astype(F32)
    return out


def _rwkv_kernel(z_ref, s0_ref, w0_ref, w2_ref, a0_ref, a2_ref, g2_ref, kk_ref, ka_ref, rk_ref,
                 lg_ref, lb_ref, ones_ref, hm_ref, cm_ref, tri_ref, seg_ref, y_ref, sl_ref, ypre_ref,
                 *scratch, seg_len, step_chunks):
    C = RW_CHUNK
    N = RWKV_HEAD
    GL = RW_GROUP_LANES
    chained = seg_len == C
    if chained:
        (st_ref,) = scratch
        b = pl.program_id(1)

        @pl.when(b == 0)
        def _():
            for g in range(RW_GROUPS):
                st_ref[g] = jnp.concatenate(
                    [s0_ref[0, h] for h in range(g * RW_GROUP_HEADS, (g + 1) * RW_GROUP_HEADS)], axis=1)

    ones_bd = ones_ref[...]
    z = z_ref[...]
    r = z[:, :D_RWKV]
    k = z[:, D_RWKV:2 * D_RWKV]
    v = z[:, 2 * D_RWKV:3 * D_RWKV]
    o = 3 * D_RWKV
    xw = z[:, o:o + DECAY_LORA]
    xa = z[:, o + DECAY_LORA:o + DECAY_LORA + AAA_LORA]
    xg = z[:, o + DECAY_LORA + AAA_LORA:]
    wd = -(w0_ref[...] + _mm(jnp.tanh(xw), w2_ref[...]))
    w = -(jnp.maximum(wd, 0.0) + jnp.log1p(jnp.exp(-jnp.abs(wd)))) - 0.5
    lw = -jnp.exp(w)
    a = _sigmoid(a0_ref[...] + _mm(xa, a2_ref[...]))
    out_gate = _mm(_sigmoid(xg), g2_ref[...])
    kk = k * kk_ref[...]
    kk = kk / jnp.maximum(jnp.sqrt(_seg_sum(kk * kk, ones_bd)), L2_EPS)
    kmod = k * (1.0 + (a - 1.0) * ka_ref[...])

    terms = _split3(lw)
    tri, seg = tri_ref[...], seg_ref[...]
    cw = sum(jnp.dot(tri, t, preferred_element_type=F32) for t in terms)
    tot = sum(jnp.dot(seg, t, preferred_element_type=F32) for t in terms)
    w_in = jnp.exp(cw)
    w_tail = jnp.exp(tot - cw)
    w_inv = jnp.exp(-cw)
    w_all = jnp.exp(tot)
    ah = -kk * jnp.exp(cw - lw)
    bh = kk * a * w_inv
    kh = kmod * w_inv
    rh = r * w_in
    bt = kk * a * w_tail
    kt = kmod * w_tail
    strict = cm_ref[0] > 0.0
    incl = cm_ref[1] > 0.0
    eye_cat = cm_ref[2]
    n_fac = seg_len.bit_length() - 1

    items = [(c, g) for c in range(step_chunks) for g in range(RW_GROUPS)]
    sub = lambda t, c, g: t[c * C:(c + 1) * C, g * GL:(g + 1) * GL]
    bd = lambda t: _bd(t, hm_ref)
    A = [sub(ah, c, g) for c, g in items]
    R = [sub(rh, c, g) for c, g in items]
    V = [sub(v, c, g) for c, g in items]
    G = [_mm_nt(jnp.concatenate([A[i], R[i]], axis=0),
                jnp.concatenate([bd(sub(bh, c, g)), bd(sub(kh, c, g))], axis=0))
         for i, (c, g) in enumerate(items)]
    AB = [jnp.where(strict, t[:C, :GL], 0.0) for t in G]
    AK = [jnp.where(strict, t[:C, GL:], 0.0) for t in G]
    RB = [jnp.where(incl, t[C:, :GL], 0.0) for t in G]
    RK = [jnp.where(incl, t[C:, GL:], 0.0) for t in G]
    KV = [_mm(jnp.concatenate([AK[i], RK[i]], axis=0), bd(V[i])) for i in range(len(items))]
    T = [eye_cat + t for t in AB]
    P = [_mm(t, bd(t)) for t in AB]
    for _ in range(n_fac - 2):
        PT = [_mm(jnp.concatenate([P[i], T[i]], axis=0), bd(P[i])) for i in range(len(items))]
        P = [t[:C] for t in PT]
        T = [T[i] + PT[i][C:] for i in range(len(items))]
    T = [T[i] + _mm(T[i], bd(P[i])) for i in range(len(items))]
    X = [_mm(T[i], jnp.concatenate([bd(A[i]), bd(KV[i][:C])], axis=1)) for i in range(len(items))]
    TA = [t[:, :GL] for t in X]
    U0 = [t[:, GL:] for t in X]
    W2 = [_mm(RB[i], jnp.concatenate([bd(TA[i]), bd(U0[i])], axis=1)) for i in range(len(items))]
    Rt = [R[i] + W2[i][:, :GL] for i in range(len(items))]
    Y0 = [W2[i][:, GL:] + KV[i][C:] for i in range(len(items))]
    nseg = C // seg_len
    PhiT, PsiT = {}, {}
    for i, (c, g) in enumerate(items):
        Bt, Kt, Wa = sub(bt, c, g), sub(kt, c, g), sub(w_all, c, g)
        for s in range(nseg):
            ss = slice(s * seg_len, (s + 1) * seg_len)
            PhiT[i, s] = bd(eye_cat * Wa[s * seg_len:s * seg_len + 1, :]
                            + _unbd(_mm_tn(TA[i][ss], Bt[ss]), hm_ref))
            PsiT[i, s] = _unbd(_mm_tn(jnp.concatenate([U0[i][ss], V[i][ss]], axis=0),
                                      jnp.concatenate([Bt[ss], Kt[ss]], axis=0)), hm_ref)
    heads = lambda g: range(g * RW_GROUP_HEADS, (g + 1) * RW_GROUP_HEADS)
    load_state = lambda ref, j, g: jnp.concatenate([ref[j, h] for h in heads(g)], axis=1)
    if chained:
        states = [st_ref[g] for g in range(RW_GROUPS)]
    for i, (c, g) in enumerate(items):
        for s in range(nseg):
            ss = slice(s * seg_len, (s + 1) * seg_len)
            S = states[g] if chained else load_state(s0_ref, c * nseg + s, g)
            ypre_ref[c * C + s * seg_len:c * C + (s + 1) * seg_len, g * GL:(g + 1) * GL] = (
                _mm_nt(Rt[i][ss], bd(S)) + Y0[i][ss])
            S = _mm(S, PhiT[i, s]) + PsiT[i, s]
            if chained:
                states[g] = S
            else:
                for h in heads(g):
                    sl_ref[c * nseg + s, h] = S[:, (h % RW_GROUP_HEADS) * N:(h % RW_GROUP_HEADS + 1) * N]
    if chained:
        for g in range(RW_GROUPS):
            st_ref[g] = states[g]
            for h in heads(g):
                sl_ref[0, h] = states[g][:, (h % RW_GROUP_HEADS) * N:(h % RW_GROUP_HEADS + 1) * N]
    y = ypre_ref[...]

    mu = _seg_sum(y, ones_bd) * (1.0 / N)
    d = y - mu
    var = _seg_sum(d * d, ones_bd) * (1.0 / N)
    yn = d * lax.rsqrt(var + GN_EPS) * lg_ref[...] + lb_ref[...]
    bonus = _seg_sum(r * kmod * rk_ref[...], ones_bd) * v
    y_ref[...] = (yn + bonus) * out_gate


def _rwkv_constants(seg_len, rows):
    n, gh, gl = RWKV_HEAD, RW_GROUP_HEADS, RW_GROUP_LANES
    lane_head = np.arange(gl) // n
    hm = (lane_head[None, None, :] == np.arange(gh)[:, None, None]) * np.ones((gh, n, gl))
    i = np.arange(RW_CHUNK)[:, None]
    j = (np.arange(gl) % n)[None, :]
    same = (i // seg_len) == (j // seg_len)
    cm = np.stack([(j < i) & same, (j <= i) & same, j == i]).astype(np.float32)
    ri = np.arange(rows)[:, None]
    rj = np.arange(rows)[None, :]
    same_r = (ri // seg_len) == (rj // seg_len)
    tri = ((rj <= ri) & same_r).astype(np.float32)
    ones_bd = np.kron(np.eye(N_RWKV_HEADS), np.ones((n, n)))
    return (jnp.asarray(ones_bd, BF16), jnp.asarray(hm, BF16), jnp.asarray(cm, F32),
            jnp.asarray(tri, BF16), jnp.asarray(same_r.astype(np.float32), BF16))


def _rwkv(z, s0, p, layer, seq_len):
    n = z.shape[0]
    step_chunks = min(RW_STEP_CHUNKS, (seq_len if seq_len >= RW_CHUNK else n) // RW_CHUNK)
    nseq = n // seq_len
    nh, hd = N_RWKV_HEADS, RWKV_HEAD
    seg_len = min(seq_len, RW_CHUNK)
    rows = RW_CHUNK * step_chunks
    chained = seq_len >= RW_CHUNK
    if chained:
        steps = seq_len // rows
        grid = (nseq, steps)
        row_map = lambda s, b: (s * steps + b, 0)
        state_spec = pl.BlockSpec((1, nh, hd, hd), lambda s, b: (s, 0, 0, 0))
        scratch = [pltpu.VMEM((rows, D_RWKV), F32), pltpu.VMEM((RW_GROUPS, hd, RW_GROUP_LANES), F32)]
    else:
        grid = (n // rows, 1)
        row_map = lambda s, b: (s, 0)
        state_spec = pl.BlockSpec((rows // seq_len, nh, hd, hd), lambda s, b: (s, 0, 0, 0))
        scratch = [pltpu.VMEM((rows, D_RWKV), F32)]
    if s0 is None:
        s0 = p['state_rwkv']
        s0_spec = pl.BlockSpec((None,) + state_spec.block_shape,
                               lambda s, b: (layer, s, 0, 0, 0))
    else:
        s0_spec = state_spec
    stacked = [p[k] for k in ('w0', 'w2', 'a0', 'a2', 'g2', 'k_k', 'k_a', 'r_k', 'lnx_g', 'lnx_b')]
    consts = list(_rwkv_constants(seg_len, rows))
    y, s_new = pl.pallas_call(
        functools.partial(_rwkv_kernel, seg_len=seg_len, step_chunks=step_chunks),
        grid=grid,
        in_specs=[pl.BlockSpec((rows, N_SHIFT), row_map), s0_spec]
        + [_layer_spec(t, layer) for t in stacked] + [_const_spec(t.shape) for t in consts],
        out_specs=[pl.BlockSpec((rows, D_RWKV), row_map), state_spec],
        out_shape=[jax.ShapeDtypeStruct((n, D_RWKV), F32),
                   jax.ShapeDtypeStruct((nseq, nh, hd, hd), F32)],
        scratch_shapes=scratch,
        compiler_params=pltpu.CompilerParams(dimension_semantics=("arbitrary", "arbitrary"),
                                             vmem_limit_bytes=V7X_VMEM_LIMIT_BYTES),
        name="rwkv7",
    )(z, s0, *stacked, *consts)
    return y, s_new


def _ffn_kernel(h_ref, y8_ref, u8_ref, yb_ref, pe_ref, c1_ref, c2_ref, d8_ref, wglu_ref, bglu_ref,
                wout_ref, g2_ref, wup_ref, cw_ref, cb_ref, wdn_ref, wple_ref, wpg_ref, gf_ref,
                *rest, seq_len, tm, final):
    i = pl.program_id(0)
    long_seq = seq_len >= tm
    if long_seq:
        o_ref, ga_ref, gb_ref, fold_ref, act_ref, carry_ref = rest

        @pl.when(i == 0)
        def _():
            carry_ref[...] = jnp.zeros_like(carry_ref)
    else:
        expand_ref, sel_last_ref, sel_prev_ref, o_ref, ga_ref, gb_ref, fold_ref, act_ref = rest
        expand = expand_ref[...]
    ya = y8_ref[...] + d8_ref[...] * u8_ref[...]
    c_gelu = math.sqrt(2.0 / math.pi)
    ya = ya * (0.5 * (1.0 + jnp.tanh(c_gelu * (ya + 0.044715 * (ya * ya * ya)))))
    ya = _unfold_rows(ya, fold_ref)
    ya = ya * _sigmoid(_mm(ya, wglu_ref[...]) + bglu_ref[...])
    h1 = (h_ref[...] + _mm(ya, wout_ref[:D_SSM, :]) + _mm(yb_ref[...], wout_ref[D_SSM:, :]))
    x2 = _rmsnorm(h1, g2_ref[...]).astype(BF16)
    row = lax.broadcasted_iota(jnp.int32, (tm, 1), 0)
    if long_seq:
        first = (i % (seq_len // tm)) == 0
    else:
        t = row % seq_len
    def up(c):
        cs = slice(c * FF_CHUNK, (c + 1) * FF_CHUNK)
        gs = slice(D_FF + c * FF_CHUNK, D_FF + (c + 1) * FF_CHUNK)
        return (jnp.dot(x2, wup_ref[:, cs], preferred_element_type=F32),
                jnp.dot(x2, wup_ref[:, gs], preferred_element_type=F32))

    n_chunks = D_FF // FF_CHUNK
    nxt = up(0)
    for c in range(n_chunks):
        cs = slice(c * FF_CHUNK, (c + 1) * FF_CHUNK)
        val, gate = nxt
        if c + 1 < n_chunks:
            nxt = up(c + 1)
        r1 = pltpu.roll(gate, 1, 0)
        r2 = pltpu.roll(gate, 2, 0)
        if long_seq:
            m1 = jnp.where(first, c1_ref[0][:, cs], carry_ref[1:2, cs])
            m2 = jnp.where(first, c2_ref[0][:, cs], carry_ref[0:1, cs])
            p1 = jnp.where(row == 0, m1, r1)
            p2 = jnp.where(row == 0, m2, jnp.where(row == 1, m1, r2))
            carry_ref[:, cs] = gate[tm - 2:tm, :]
            ga_ref[0, :, cs] = gate[tm - 2:tm - 1, :]
            gb_ref[0, :, cs] = gate[tm - 1:tm, :]
        else:
            m1 = _exact_rows(expand, c1_ref[:, cs])
            m2 = _exact_rows(expand, c2_ref[:, cs])
            p1 = jnp.where(t == 0, m1, r1)
            p2 = jnp.where(t == 0, m2, jnp.where(t == 1, m1, r2))
            ga_ref[:, cs] = _exact_rows(sel_prev_ref[...], gate)
            gb_ref[:, cs] = _exact_rows(sel_last_ref[...], gate)
        conv = (cb_ref[:, cs] + cw_ref[2:3, cs] * gate + cw_ref[1:2, cs] * p1 + cw_ref[0:1, cs] * p2)
        act_ref[:, cs] = (conv * _sigmoid(conv) * val).astype(BF16)
    h2 = h1 + jnp.dot(act_ref[...], wdn_ref[...], preferred_element_type=F32)
    h3 = h2 + _mm(pe_ref[...], wple_ref[...]) * _sigmoid(_mm(h2, wpg_ref[...]))
    if final:
        h3 = _rmsnorm(h3, gf_ref[...])
    o_ref[...] = h3


def _ffn(h, y8, u8, yb, pe, conv0, p, layer, seq_len, tm, final):
    n = h.shape[0]
    nseq = n // seq_len
    row_spec = lambda w: pl.BlockSpec((tm, w), lambda i: (i, 0))
    pe_spec = pl.BlockSpec((None, tm, D_PLE), lambda i: (layer, i, 0))
    fold_spec = pl.BlockSpec((tm // S5_STEPS, S5_ROW), lambda i: (i, 0))
    scratch = [pltpu.VMEM((FOLD_LANE_BLOCKS, tm, LANES), F32), pltpu.VMEM((tm, D_FF), BF16)]
    if seq_len >= tm:
        tps = seq_len // tm
        seq_spec = pl.BlockSpec((1, 1, D_FF), lambda i: (i // tps, 0, 0))
        seq_shape = (nseq, 1, D_FF)
        extra = []
        scratch.append(pltpu.VMEM((2, D_FF), F32))
    else:
        seq_spec = pl.BlockSpec((tm // seq_len, D_FF), lambda i: (i, 0))
        seq_shape = (nseq, D_FF)
        extra = list(_seq_selectors(seq_len, tm))
    stacked = [p[k] for k in ('d8', 'w_glu', 'b_glu', 'w_out', 'norm2_g', 'w_up', 'conv_w',
                              'conv_b', 'w_down', 'w_ple', 'w_pg')]
    consts = [p['final_g']] + extra
    weights = stacked + consts
    out, ga, gb = pl.pallas_call(
        functools.partial(_ffn_kernel, seq_len=seq_len, tm=tm, final=final),
        grid=(n // tm,),
        in_specs=[row_spec(D_MODEL), fold_spec, fold_spec, row_spec(D_RWKV), pe_spec,
                  seq_spec, seq_spec] + [_layer_spec(t, layer) for t in stacked]
        + [_const_spec(t.shape) for t in consts],
        out_specs=[row_spec(D_MODEL), seq_spec, seq_spec],
        out_shape=[jax.ShapeDtypeStruct((n, D_MODEL), F32)]
        + [jax.ShapeDtypeStruct(seq_shape, F32)] * 2,
        scratch_shapes=scratch,
        compiler_params=pltpu.CompilerParams(dimension_semantics=("arbitrary",),
                                             vmem_limit_bytes=V7X_VMEM_LIMIT_BYTES),
        name="ffn",
    )(h, y8, u8, yb, pe, conv0[:, 1].reshape(seq_shape), conv0[:, 0].reshape(seq_shape), *weights)
    conv_new = jnp.stack([ga.reshape(nseq, D_FF), gb.reshape(nseq, D_FF)], axis=1)
    return out, conv_new


def _layer(h, pe, st, p, layer, s5_tables, seq_len, tm, final):
    ssm_re0, ssm_im0, rwkv0, shift0, conv0 = st
    u8, z, shift_new = _inproj(h, p, layer, shift0, seq_len, tm)
    y8, hre, him = _s5(u8, ssm_re0, ssm_im0, s5_tables, seq_len)
    yb, s_last = _rwkv(z, rwkv0, p, layer, seq_len)
    h, conv_new = _ffn(h, y8, u8, yb, pe, conv0, p, layer, seq_len, tm, final)
    return h, (hre, him, s_last, shift_new, conv_new)


def _stacked_params(w, state_rwkv, state_shift):
    depth = w['w_in'].shape[0]
    row = lambda t: t.reshape(depth, 1, -1).astype(F32)
    bf = lambda t: t.astype(BF16)
    return {
        'norm1_g': row(w['norm1_g']), 'w_in': bf(w['w_in']), 'shift_mu': row(w['shift_mu']),
        'd8': jnp.tile(row(w['ssm_d']), (1, 1, S5_STEPS)), 'w_glu': bf(w['ssm_w_glu']),
        'b_glu': row(w['ssm_b_glu']),
        'w0': row(w['rwkv_w0']), 'w2': bf(w['rwkv_w2']), 'a0': row(w['rwkv_a0']),
        'a2': bf(w['rwkv_a2']), 'g2': bf(w['rwkv_g2']), 'k_k': row(w['rwkv_k_k']),
        'k_a': row(w['rwkv_k_a']), 'r_k': row(w['rwkv_r_k']), 'lnx_g': row(w['rwkv_lnx_g']),
        'lnx_b': row(w['rwkv_lnx_b']), 'w_out': bf(w['w_out']), 'norm2_g': row(w['norm2_g']),
        'w_up': bf(w['w_up']), 'conv_w': w['conv_w'].astype(F32), 'conv_b': row(w['conv_b']),
        'w_down': bf(w['w_down']), 'w_ple': bf(w['w_ple']), 'w_pg': bf(w['w_pg']),
        'final_g': w['final_g'].reshape(1, -1).astype(F32),
        'state_rwkv': state_rwkv.astype(F32), 'state_shift': state_shift.astype(F32),
    }


def _forward(x_prompt, x_sample, p_prompt, p_sample, state_ssm_re, state_ssm_im, state_rwkv,
             state_shift, state_conv, w, tm_prompt, tm_sample):
    depth = w['w_in'].shape[0]
    bp, lp, _ = x_prompt.shape
    bs, ls, _ = x_sample.shape
    hp = x_prompt.reshape(bp * lp, D_MODEL).astype(F32)
    hs = x_sample.reshape(bs * ls, D_MODEL).astype(F32)
    pe_p = p_prompt.reshape(depth, bp * lp, D_PLE).astype(F32)
    pe_s = p_sample.reshape(depth, bs * ls, D_PLE).astype(F32)
    zero_st = (jnp.zeros((bp, N_SSM_GROUPS, SSM_STATE), F32),
               jnp.zeros((bp, N_SSM_GROUPS, SSM_STATE), F32),
               jnp.zeros((bp, N_RWKV_HEADS, RWKV_HEAD, RWKV_HEAD), F32),
               jnp.zeros((bp, N_SHIFT), F32),
               jnp.zeros((bp, 2, D_FF), F32))
    p = _stacked_params(w, state_rwkv, state_shift)
    new_p = [[] for _ in range(5)]
    new_s = [[] for _ in range(5)]
    for i in range(depth):
        tables = _s5_tables(w['ssm_lam_re'][i], w['ssm_lam_im'][i], w['ssm_log_dt'][i],
                            w['ssm_b_re'][i], w['ssm_b_im'][i], w['ssm_c_re'][i], w['ssm_c_im'][i])
        final = i == depth - 1
        hp, stp = _layer(hp, pe_p, zero_st, p, i, tables, lp, tm_prompt, final)
        st_in = (state_ssm_re[i].astype(F32), state_ssm_im[i].astype(F32), None, None,
                 state_conv[i].astype(F32))
        hs, sts = _layer(hs, pe_s, st_in, p, i, tables, ls, tm_sample, final)
        for j in range(5):
            new_p[j].append(stp[j])
            new_s[j].append(sts[j])
    y_prompt = hp.reshape(bp, lp, D_MODEL).astype(x_prompt.dtype)
    y_sample = hs.reshape(bs, ls, D_MODEL).astype(x_sample.dtype)
    dts = (state_ssm_re.dtype, state_ssm_im.dtype, state_rwkv.dtype, state_shift.dtype,
           state_conv.dtype)
    outs_p = tuple(jnp.stack(new_p[j]).astype(dts[j]) for j in range(5))
    outs_s = tuple(jnp.stack(new_s[j]).astype(dts[j]) for j in range(5))
    return (y_prompt, y_sample) + outs_p + outs_s


def kernel(x_prompt, x_sample, p_prompt, p_sample, state_ssm_re, state_ssm_im, state_rwkv, state_shift, state_conv, norm1_g, w_in, shift_mu, ssm_lam_re, ssm_lam_im, ssm_log_dt, ssm_b_re, ssm_b_im, ssm_c_re, ssm_c_im, ssm_d, ssm_w_glu, ssm_b_glu, rwkv_w0, rwkv_w2, rwkv_a0, rwkv_a2, rwkv_g2, rwkv_k_k, rwkv_k_a, rwkv_r_k, rwkv_lnx_g, rwkv_lnx_b, w_out, norm2_g, w_up, conv_w, conv_b, w_down, w_ple, w_pg, final_g):
    w = dict(norm1_g=norm1_g, w_in=w_in, shift_mu=shift_mu, ssm_lam_re=ssm_lam_re,
             ssm_lam_im=ssm_lam_im, ssm_log_dt=ssm_log_dt, ssm_b_re=ssm_b_re, ssm_b_im=ssm_b_im,
             ssm_c_re=ssm_c_re, ssm_c_im=ssm_c_im, ssm_d=ssm_d, ssm_w_glu=ssm_w_glu,
             ssm_b_glu=ssm_b_glu, rwkv_w0=rwkv_w0, rwkv_w2=rwkv_w2, rwkv_a0=rwkv_a0,
             rwkv_a2=rwkv_a2, rwkv_g2=rwkv_g2, rwkv_k_k=rwkv_k_k, rwkv_k_a=rwkv_k_a,
             rwkv_r_k=rwkv_r_k, rwkv_lnx_g=rwkv_lnx_g, rwkv_lnx_b=rwkv_lnx_b, w_out=w_out,
             norm2_g=norm2_g, w_up=w_up, conv_w=conv_w, conv_b=conv_b, w_down=w_down,
             w_ple=w_ple, w_pg=w_pg, final_g=final_g)
    lp = x_prompt.shape[1]
    ns = x_sample.shape[0] * x_sample.shape[1]
    return _forward(x_prompt, x_sample, p_prompt, p_sample, state_ssm_re, state_ssm_im,
                    state_rwkv, state_shift, state_conv, w,
                    tm_prompt=min(512, lp), tm_sample=min(256, ns))
```

```python
import functools
import math

import numpy as np
import jax
import jax.numpy as jnp
from jax import lax
from jax.experimental import pallas as pl
from jax.experimental.pallas import tpu as pltpu

F32 = jnp.float32
BF16 = jnp.bfloat16
HIGHEST = lax.Precision.HIGHEST

D_MODEL = 1024
D_SSM = 512
D_RWKV = 512
SSM_GROUP = 16
N_SSM_GROUPS = 32
SSM_STATE = 64
RWKV_HEAD = 64
N_RWKV_HEADS = 8
DECAY_LORA = 64
AAA_LORA = 64
GATE_LORA = 128
N_SHIFT = 3 * D_RWKV + DECAY_LORA + AAA_LORA + GATE_LORA
N_IN = D_SSM + N_SHIFT
D_FF = 2816
D_PLE = 256
RMS_EPS = 1e-6
GN_EPS = 64e-5
L2_EPS = 1e-12

LANES = 128
FOLD_LANE_BLOCKS = D_SSM // LANES
S5_STEPS = 8
S5_ROW = S5_STEPS * D_SSM
S5_LANE_BLOCKS = 4
S5_BLOCK_STATE = 2 * 8 * SSM_STATE
S5_STATE_ROW = S5_LANE_BLOCKS * S5_BLOCK_STATE
FF_CHUNK = 256
V7X_VMEM_LIMIT_BYTES = 56 * 1024 * 1024


def _mm(a, b):
    return jnp.dot(a.astype(BF16), b.astype(BF16), preferred_element_type=F32)


def _mm_nt(a, b):
    return lax.dot_general(a.astype(BF16), b.astype(BF16), (((1,), (1,)), ((), ())),
                           preferred_element_type=F32)


def _mm_tn(a, b):
    return lax.dot_general(a.astype(BF16), b.astype(BF16), (((0,), (0,)), ((), ())),
                           preferred_element_type=F32)


def _seg_sum(x, ones_bd):
    hi = x.astype(BF16)
    lo = (x - hi.astype(F32)).astype(BF16)
    return (jnp.dot(hi, ones_bd, preferred_element_type=F32)
            + jnp.dot(lo, ones_bd, preferred_element_type=F32))


def _rmsnorm(x, g):
    return x * lax.rsqrt(jnp.mean(x * x, axis=-1, keepdims=True) + RMS_EPS) * g


def _sigmoid(x):
    return 1.0 / (1.0 + jnp.exp(-x))


def _const_spec(shape):
    nd = len(shape)
    return pl.BlockSpec(shape, lambda *_: (0,) * nd, pipeline_mode=pl.Buffered(1))


def _layer_spec(t, layer):
    nd = t.ndim - 1
    return pl.BlockSpec((None,) + t.shape[1:], lambda *_: (layer,) + (0,) * nd,
                        pipeline_mode=pl.Buffered(1))


def _expand_rows(x, seq_len):
    nseq, w = x.shape
    return jnp.broadcast_to(x[:, None, :], (nseq, seq_len, w)).reshape(nseq * seq_len, w)


def _step_rows(x, seq_len, t):
    return x.reshape(x.shape[0] // seq_len, seq_len, x.shape[1])[:, t, :]


def _fold_rows(x, scr_ref, out_ref):
    tm = x.shape[0]
    for q in range(FOLD_LANE_BLOCKS):
        scr_ref[q] = x[:, q * LANES:(q + 1) * LANES]
    for s in range(S5_STEPS):
        for q in range(FOLD_LANE_BLOCKS):
            c0 = s * D_SSM + q * LANES
            out_ref[:, c0:c0 + LANES] = scr_ref[q, pl.ds(s, tm // S5_STEPS, stride=S5_STEPS), :]


def _unfold_rows(x8, scr_ref):
    tm = x8.shape[0] * S5_STEPS
    for s in range(S5_STEPS):
        for q in range(FOLD_LANE_BLOCKS):
            c0 = s * D_SSM + q * LANES
            scr_ref[q, pl.ds(s, tm // S5_STEPS, stride=S5_STEPS), :] = x8[:, c0:c0 + LANES]
    return jnp.concatenate([scr_ref[q] for q in range(FOLD_LANE_BLOCKS)], axis=1)


def _inproj_kernel(h_ref, g_ref, w_ref, mu_ref, init_ref, *rest, seq_len, tm):
    i = pl.program_id(0)
    long_seq = seq_len >= tm
    if long_seq:
        u8_ref, z_ref, last_ref, fold_ref, carry_ref = rest

        @pl.when(i == 0)
        def _():
            carry_ref[...] = jnp.zeros_like(carry_ref)
    else:
        u8_ref, z_ref, last_ref, fold_ref = rest
    xn = _rmsnorm(h_ref[...], g_ref[...])
    proj = jnp.dot(xn.astype(BF16), w_ref[...], preferred_element_type=F32)
    _fold_rows(proj[:, :D_SSM], fold_ref, u8_ref)
    zr = proj[:, D_SSM:]
    rolled = pltpu.roll(zr, 1, 0)
    row = lax.broadcasted_iota(jnp.int32, (tm, 1), 0)
    if long_seq:
        first = (i % (seq_len // tm)) == 0
        row0 = jnp.where(first, init_ref[0], carry_ref[...])
        prev = jnp.where(row == 0, row0, rolled)
        carry_ref[...] = zr[tm - 1:tm, :]
        last_ref[0] = zr[tm - 1:tm, :]
    else:
        prev = jnp.where(row % seq_len == 0, _expand_rows(init_ref[...], seq_len), rolled)
        last_ref[...] = _step_rows(zr, seq_len, seq_len - 1)
    z_ref[...] = zr + (prev - zr) * mu_ref[...]


def _inproj(h, p, layer, shift0, seq_len, tm):
    n = h.shape[0]
    nseq = n // seq_len
    fold_scratch = pltpu.VMEM((FOLD_LANE_BLOCKS, tm, LANES), F32)
    if seq_len >= tm:
        tps = seq_len // tm
        seq_spec = pl.BlockSpec((1, 1, N_SHIFT), lambda i: (i // tps, 0, 0))
        init_spec = seq_spec
        init = shift0.reshape(nseq, 1, N_SHIFT)
        last_shape = jax.ShapeDtypeStruct((nseq, 1, N_SHIFT), F32)
        scratch = [fold_scratch, pltpu.VMEM((1, N_SHIFT), F32)]
    else:
        spt = tm // seq_len
        seq_spec = pl.BlockSpec((spt, N_SHIFT), lambda i: (i, 0))
        init_spec = pl.BlockSpec((None, spt, N_SHIFT), lambda i: (layer, i, 0))
        init = p['state_shift']
        last_shape = jax.ShapeDtypeStruct((nseq, N_SHIFT), F32)
        scratch = [fold_scratch]
    g1, w_in, mu = p['norm1_g'], p['w_in'], p['shift_mu']
    u8, z, last = pl.pallas_call(
        functools.partial(_inproj_kernel, seq_len=seq_len, tm=tm),
        grid=(n // tm,),
        in_specs=[pl.BlockSpec((tm, D_MODEL), lambda i: (i, 0)),
                  _layer_spec(g1, layer), _layer_spec(w_in, layer), _layer_spec(mu, layer),
                  init_spec],
        out_specs=[pl.BlockSpec((tm // S5_STEPS, S5_ROW), lambda i: (i, 0)),
                   pl.BlockSpec((tm, N_SHIFT), lambda i: (i, 0)),
                   seq_spec],
        out_shape=[jax.ShapeDtypeStruct((n // S5_STEPS, S5_ROW), F32),
                   jax.ShapeDtypeStruct((n, N_SHIFT), F32),
                   last_shape],
        scratch_shapes=scratch,
        compiler_params=pltpu.CompilerParams(dimension_semantics=("arbitrary",),
                                             vmem_limit_bytes=V7X_VMEM_LIMIT_BYTES),
        name="inproj",
    )(h, g1, w_in, mu, init)
    return u8, z, last.reshape(nseq, N_SHIFT)


def _s5_prep_kernel(lr_ref, li_ref, ldt_ref, br_ref, bi_ref, cr_ref, ci_ref, tile_ref, sel_ref,
                    wx_ref, wy_ref, apr_ref, api_ref):
    lr, li = lr_ref[...], li_ref[...]
    dt = jnp.exp(ldt_ref[...])
    mag = jnp.exp(lr * dt)
    ar = mag * jnp.cos(li * dt)
    ai = mag * jnp.sin(li * dt)
    den = lr * lr + li * li
    nr = ar - 1.0
    fr = (nr * lr + ai * li) / den
    fi = (ai * lr - nr * li) / den
    br, bi = br_ref[...], bi_ref[...]
    bbr = fr * br - fi * bi
    bbi = fr * bi + fi * br
    cr, ci = cr_ref[...], ci_ref[...]
    pr = [jnp.ones_like(ar)]
    pi = [jnp.zeros_like(ar)]
    for _ in range(S5_STEPS):
        pr.append(pr[-1] * ar - pi[-1] * ai)
        pi.append(pr[-2] * ai + pi[-1] * ar)
    nrow = lr.shape[0]
    blk = nrow // S5_LANE_BLOCKS
    half = S5_BLOCK_STATE // 2
    ri = lax.broadcasted_iota(jnp.int32, (nrow, nrow), 0)
    ci_ = lax.broadcasted_iota(jnp.int32, (nrow, nrow), 1)
    same_group = (ri // SSM_GROUP) == (ci_ // SSM_GROUP)
    keep_in = ((ri % blk) // SSM_GROUP) == (ci_ // SSM_STATE)
    keep_out = (ri // SSM_STATE) == ((ci_ % blk) // SSM_GROUP)
    tile, sel = tile_ref[...], sel_ref[...]
    nt = (((1,), (1,)), ((), ()))
    wy_ref[...] = jnp.zeros(wy_ref.shape, wy_ref.dtype)
    for s in range(S5_STEPS):
        qr, qi = pr[S5_STEPS - 1 - s], pi[S5_STEPS - 1 - s]
        wr = qr * bbr - qi * bbi
        wi = qr * bbi + qi * bbr
        for h, w_ in enumerate((wr, wi)):
            t = jnp.dot(w_.astype(BF16), tile, preferred_element_type=F32)
            t = jnp.where(keep_in, t, 0.0).astype(BF16)
            for q in range(S5_LANE_BLOCKS):
                wx_ref[q, s * blk:(s + 1) * blk, h * half:(h + 1) * half] = t[q * blk:(q + 1) * blk, :]
        tau = S5_STEPS - 1 - s
        kt = (lax.dot_general(wr, cr, nt, precision=HIGHEST, preferred_element_type=F32)
              - lax.dot_general(wi, ci, nt, precision=HIGHEST, preferred_element_type=F32))
        kt = jnp.where(same_group, kt, 0.0).astype(BF16)
        for q in range(S5_LANE_BLOCKS):
            kq = kt[q * blk:(q + 1) * blk, q * blk:(q + 1) * blk]
            for s_in in range(S5_STEPS - tau):
                s_out = s_in + tau
                r0 = S5_BLOCK_STATE + s_in * blk
                wy_ref[q, s_out // 2, r0:r0 + blk, (s_out % 2) * blk:(s_out % 2 + 1) * blk] = kq
        mr = cr * pr[s + 1] - ci * pi[s + 1]
        mi = cr * pi[s + 1] + ci * pr[s + 1]
        for h, m_ in enumerate((mr, -mi)):
            t = lax.dot_general(sel, m_.astype(BF16), nt, preferred_element_type=F32)
            t = jnp.where(keep_out, t, 0.0).astype(BF16)
            for q in range(S5_LANE_BLOCKS):
                wy_ref[q, s // 2, h * half:(h + 1) * half, (s % 2) * blk:(s % 2 + 1) * blk] = (
                    t[:, q * blk:(q + 1) * blk])
    a8r, a8i = pr[S5_STEPS], pi[S5_STEPS]
    er, ei = a8r, a8i
    for n in range(S5_STEPS):
        apr_ref[n] = er
        api_ref[n] = ei
        er, ei = er * a8r - ei * a8i, er * a8i + ei * a8r


def _s5_tables(lam_re, lam_im, log_dt, b_re, b_im, c_re, c_im):
    G, P, K = N_SSM_GROUPS, SSM_STATE, SSM_GROUP
    Q, GB = S5_LANE_BLOCKS, G // S5_LANE_BLOCKS
    rep = lambda t: jnp.repeat(t, K, axis=0)
    tile = np.tile(np.eye(P, dtype=np.float32), (1, GB))
    args = (rep(lam_re), rep(lam_im), rep(jnp.broadcast_to(log_dt[:, None], (G, P))),
            jnp.swapaxes(b_re, 1, 2).reshape(G * K, P), jnp.swapaxes(b_im, 1, 2).reshape(G * K, P),
            c_re.reshape(G * K, P), c_im.reshape(G * K, P),
            jnp.asarray(tile, BF16), jnp.asarray(tile.T, BF16))
    t3 = jax.ShapeDtypeStruct((S5_STEPS, G * K, P), F32)
    wx, wy, apr, api = pl.pallas_call(
        _s5_prep_kernel,
        out_shape=[jax.ShapeDtypeStruct((Q, S5_STEPS * GB * K, S5_BLOCK_STATE), BF16),
                   jax.ShapeDtypeStruct((Q, S5_STEPS // 2, S5_BLOCK_STATE + S5_STEPS * GB * K,
                                         2 * GB * K), BF16),
                   t3, t3],
        compiler_params=pltpu.CompilerParams(vmem_limit_bytes=V7X_VMEM_LIMIT_BYTES),
        name="s5_prep",
    )(*args)
    ap = jnp.concatenate([t[:, ::K, :].reshape(S5_STEPS, Q, GB * P) for t in (apr, api)], axis=-1)
    return wx, wy, ap.reshape(S5_STEPS, S5_STATE_ROW)


def _state_to_lanes(h_re, h_im):
    n = h_re.shape[0]
    parts = [t.reshape(n, S5_LANE_BLOCKS, -1) for t in (h_re, h_im)]
    return jnp.concatenate(parts, axis=-1).reshape(n, S5_STATE_ROW)


def _lanes_to_state(h):
    n = h.shape[0]
    h = h.reshape(n, S5_LANE_BLOCKS, 2, N_SSM_GROUPS // S5_LANE_BLOCKS, SSM_STATE)
    return (h[:, :, 0].reshape(n, N_SSM_GROUPS, SSM_STATE), h[:, :, 1].reshape(n, N_SSM_GROUPS, SSM_STATE))


def _s5_kernel(u_ref, h0_ref, wx_ref, wy_ref, ap_ref, y_ref, hl_ref, hs_ref, hp_ref, *, rows, scan):
    half = S5_BLOCK_STATE // 2
    cw = D_SSM // S5_LANE_BLOCKS
    ub = u_ref[...].astype(BF16)
    sub3 = lax.broadcasted_iota(jnp.int32, (1, 8, 1), 1)
    sub2 = lax.broadcasted_iota(jnp.int32, (8, 1), 0)
    for q in range(S5_LANE_BLOCKS):
        us = [ub[:, s * D_SSM + q * cw: s * D_SSM + (q + 1) * cw] for s in range(S5_STEPS)]
        ucat = jnp.concatenate(us, axis=1)
        x = jnp.dot(ucat, wx_ref[q], preferred_element_type=F32)
        lo = q * S5_BLOCK_STATE
        apq = ap_ref[:, lo:lo + S5_BLOCK_STATE]
        if scan:
            xr = x[:, :half].reshape(rows // 8, 8, half)
            xi = x[:, half:].reshape(rows // 8, 8, half)
            for k in (1, 2, 4):
                er = apq[k - 1:k, :half].reshape(1, 1, half)
                ei = apq[k - 1:k, half:].reshape(1, 1, half)
                sr = jnp.where(sub3 >= k, pltpu.roll(xr, k, 1), 0.0)
                si = jnp.where(sub3 >= k, pltpu.roll(xi, k, 1), 0.0)
                xr, xi = xr + er * sr - ei * si, xi + er * si + ei * sr
            hs_ref[:, :half] = xr.reshape(rows, half)
            hs_ref[:, half:] = xi.reshape(rows, half)
            pwr, pwi = apq[:, :half], apq[:, half:]
            c0r = h0_ref[0][:, lo:lo + half]
            c0i = h0_ref[0][:, lo + half:lo + S5_BLOCK_STATE]

            def body(j, carry):
                cr, ci = carry
                sl = pl.ds(pl.multiple_of(j * 8, 8), 8)
                hr = hs_ref[sl, :half] + pwr * cr - pwi * ci
                hi = hs_ref[sl, half:] + pwr * ci + pwi * cr
                hp_ref[sl, :half] = jnp.where(sub2 == 0, cr, pltpu.roll(hr, 1, 0))
                hp_ref[sl, half:] = jnp.where(sub2 == 0, ci, pltpu.roll(hi, 1, 0))
                return hr[7:8, :], hi[7:8, :]

            cr, ci = lax.fori_loop(0, rows // 8, body, (c0r, c0i))
            hl_ref[0, :, lo:lo + half] = cr
            hl_ref[0, :, lo + half:lo + S5_BLOCK_STATE] = ci
            hprev = hp_ref[...].astype(BF16)
        else:
            h0r = h0_ref[:, lo:lo + half]
            h0i = h0_ref[:, lo + half:lo + S5_BLOCK_STATE]
            er, ei = apq[0:1, :half], apq[0:1, half:]
            hl_ref[:, lo:lo + half] = er * h0r - ei * h0i + x[:, :half]
            hl_ref[:, lo + half:lo + S5_BLOCK_STATE] = er * h0i + ei * h0r + x[:, half:]
            hprev = jnp.concatenate([h0r, h0i], axis=1).astype(BF16)
        lhs = jnp.concatenate([hprev, ucat], axis=1)
        for j in range(S5_STEPS // 2):
            kk = S5_BLOCK_STATE + cw * (2 * j + 2)
            y2 = jnp.dot(lhs[:, :kk], wy_ref[q, j, :kk, :], preferred_element_type=F32)
            c0 = (2 * j) * D_SSM + q * cw
            c1 = (2 * j + 1) * D_SSM + q * cw
            y_ref[:, c0:c0 + cw] = y2[:, :cw]
            y_ref[:, c1:c1 + cw] = y2[:, cw:]


def _s5(u8, h_re, h_im, tables, seq_len):
    wx, wy, ap = tables
    n = u8.shape[0] * S5_STEPS
    nseq = n // seq_len
    h0 = _state_to_lanes(h_re, h_im)
    cps = seq_len // S5_STEPS
    scan = cps > 1
    if scan:
        rows = cps
        grid = (nseq,)
        h0 = h0.reshape(nseq, 1, S5_STATE_ROW)
        h_spec = pl.BlockSpec((1, 1, S5_STATE_ROW), lambda i: (i, 0, 0))
        h_shape = jax.ShapeDtypeStruct((nseq, 1, S5_STATE_ROW), F32)
    else:
        rows = min(nseq, 128)
        grid = (nseq // rows,)
        h_spec = pl.BlockSpec((rows, S5_STATE_ROW), lambda i: (i, 0))
        h_shape = jax.ShapeDtypeStruct((nseq, S5_STATE_ROW), F32)
    y8, hl = pl.pallas_call(
        functools.partial(_s5_kernel, rows=rows, scan=scan),
        grid=grid,
        in_specs=[pl.BlockSpec((rows, S5_ROW), lambda i: (i, 0)),
                  h_spec,
                  _const_spec(wx.shape), _const_spec(wy.shape), _const_spec(ap.shape)],
        out_specs=[pl.BlockSpec((rows, S5_ROW), lambda i: (i, 0)), h_spec],
        out_shape=[jax.ShapeDtypeStruct((n // S5_STEPS, S5_ROW), F32), h_shape],
        scratch_shapes=[pltpu.VMEM((rows, S5_BLOCK_STATE), F32),
                        pltpu.VMEM((rows, S5_BLOCK_STATE), F32)],
        compiler_params=pltpu.CompilerParams(dimension_semantics=("arbitrary",),
                                             vmem_limit_bytes=V7X_VMEM_LIMIT_BYTES),
        name="s5",
    )(u8, h0, wx, wy, ap)
    hre_new, him_new = _lanes_to_state(hl.reshape(nseq, S5_STATE_ROW))
    return y8, hre_new, him_new


RW_CHUNK = 64
RW_STEP_CHUNKS_LONG = 4
RW_STEP_CHUNKS_SHORT = 4
RW_GROUPS = 2
RW_GROUP_HEADS = N_RWKV_HEADS // RW_GROUPS
RW_GROUP_LANES = D_RWKV // RW_GROUPS


def _split3(x):
    hi = x.astype(BF16)
    r1 = x - hi.astype(F32)
    mid = r1.astype(BF16)
    lo = (r1 - mid.astype(F32)).astype(BF16)
    return hi, mid, lo


def _bd(x, hm_ref):
    xb = x.astype(BF16)
    return jnp.concatenate([xb * hm_ref[h] for h in range(RW_GROUP_HEADS)], axis=0)


def _unbd(f, hm_ref):
    n = RWKV_HEAD
    out = f[:n] * hm_ref[0].astype(F32)
    for h in range(1, RW_GROUP_HEADS):
        out = out + f[h * n:(h + 1) * n] * hm_ref[h].astype(F32)
    return out


def _rwkv_kernel(z_ref, s0_ref, w0_ref, w2_ref, a0_ref, a2_ref, g2_ref, kk_ref, ka_ref, rk_ref,
                 lg_ref, lb_ref, ones_ref, hm_ref, cm_ref, tri_ref, seg_ref, y_ref, sl_ref, ypre_ref,
                 *scratch, seg_len, step_chunks):
    C = RW_CHUNK
    N = RWKV_HEAD
    GL = RW_GROUP_LANES
    chained = seg_len == C
    if chained:
        (st_ref,) = scratch
        b = pl.program_id(1)

        @pl.when(b == 0)
        def _():
            for g in range(RW_GROUPS):
                st_ref[g] = jnp.concatenate(
                    [s0_ref[0, h] for h in range(g * RW_GROUP_HEADS, (g + 1) * RW_GROUP_HEADS)], axis=1)

    ones_bd = ones_ref[...]
    z = z_ref[...]
    r = z[:, :D_RWKV]
    k = z[:, D_RWKV:2 * D_RWKV]
    v = z[:, 2 * D_RWKV:3 * D_RWKV]
    o = 3 * D_RWKV
    xw = z[:, o:o + DECAY_LORA]
    xa = z[:, o + DECAY_LORA:o + DECAY_LORA + AAA_LORA]
    xg = z[:, o + DECAY_LORA + AAA_LORA:]
    wd = -(w0_ref[...] + _mm(jnp.tanh(xw), w2_ref[...]))
    w = -(jnp.maximum(wd, 0.0) + jnp.log1p(jnp.exp(-jnp.abs(wd)))) - 0.5
    lw = -jnp.exp(w)
    a = _sigmoid(a0_ref[...] + _mm(xa, a2_ref[...]))
    out_gate = _mm(_sigmoid(xg), g2_ref[...])
    kk = k * kk_ref[...]
    kk = kk / jnp.maximum(jnp.sqrt(_seg_sum(kk * kk, ones_bd)), L2_EPS)
    kmod = k * (1.0 + (a - 1.0) * ka_ref[...])

    terms = _split3(lw)
    tri, seg = tri_ref[...], seg_ref[...]
    cw = sum(jnp.dot(tri, t, preferred_element_type=F32) for t in terms)
    tot = sum(jnp.dot(seg, t, preferred_element_type=F32) for t in terms)
    w_in = jnp.exp(cw)
    w_tail = jnp.exp(tot - cw)
    w_inv = jnp.exp(-cw)
    w_all = jnp.exp(tot)
    ah = -kk * jnp.exp(cw - lw)
    bh = kk * a * w_inv
    kh = kmod * w_inv
    rh = r * w_in
    bt = kk * a * w_tail
    kt = kmod * w_tail
    strict = cm_ref[0] > 0.0
    incl = cm_ref[1] > 0.0
    eye_cat = cm_ref[2]
    n_fac = seg_len.bit_length() - 1

    items = [(c, g) for c in range(step_chunks) for g in range(RW_GROUPS)]
    sub = lambda t, c, g: t[c * C:(c + 1) * C, g * GL:(g + 1) * GL]
    bd = lambda t: _bd(t, hm_ref)
    A = [sub(ah, c, g) for c, g in items]
    R = [sub(rh, c, g) for c, g in items]
    V = [sub(v, c, g) for c, g in items]
    G = [_mm_nt(jnp.concatenate([A[i], R[i]], axis=0),
                jnp.concatenate([bd(sub(bh, c, g)), bd(sub(kh, c, g))], axis=0))
         for i, (c, g) in enumerate(items)]
    AB = [jnp.where(strict, t[:C, :GL], 0.0) for t in G]
    AK = [jnp.where(strict, t[:C, GL:], 0.0) for t in G]
    RB = [jnp.where(incl, t[C:, :GL], 0.0) for t in G]
    RK = [jnp.where(incl, t[C:, GL:], 0.0) for t in G]
    KV = [_mm(jnp.concatenate([AK[i], RK[i]], axis=0), bd(V[i])) for i in range(len(items))]
    T = [eye_cat + t for t in AB]
    P = [_mm(t, bd(t)) for t in AB]
    for _ in range(n_fac - 2):
        PT = [_mm(jnp.concatenate([P[i], T[i]], axis=0), bd(P[i])) for i in range(len(items))]
        P = [t[:C] for t in PT]
        T = [T[i] + PT[i][C:] for i in range(len(items))]
    T = [T[i] + _mm(T[i], bd(P[i])) for i in range(len(items))]
    RBT = [_mm(RB[i], bd(T[i])) for i in range(len(items))]
    X = [_mm(jnp.concatenate([T[i], RBT[i]], axis=0),
             jnp.concatenate([bd(A[i]), bd(KV[i][:C])], axis=1)) for i in range(len(items))]
    TA = [t[:C, :GL] for t in X]
    U0 = [t[:C, GL:] for t in X]
    Rt = [R[i] + X[i][C:, :GL] for i in range(len(items))]
    Y0 = [X[i][C:, GL:] + KV[i][C:] for i in range(len(items))]
    nseg = C // seg_len
    PhiT, PsiT = {}, {}
    for i, (c, g) in enumerate(items):
        Bt, Kt, Wa = sub(bt, c, g), sub(kt, c, g), sub(w_all, c, g)
        for s in range(nseg):
            ss = slice(s * seg_len, (s + 1) * seg_len)
            PhiT[i, s] = bd(eye_cat * Wa[s * seg_len:s * seg_len + 1, :]
                            + _unbd(_mm_tn(TA[i][ss], Bt[ss]), hm_ref))
            PsiT[i, s] = _unbd(_mm_tn(jnp.concatenate([U0[i][ss], V[i][ss]], axis=0),
                                      jnp.concatenate([Bt[ss], Kt[ss]], axis=0)), hm_ref)
    heads = lambda g: range(g * RW_GROUP_HEADS, (g + 1) * RW_GROUP_HEADS)
    load_state = lambda ref, j, g: jnp.concatenate([ref[j, h] for h in heads(g)], axis=1)
    if chained:
        states = [st_ref[g] for g in range(RW_GROUPS)]
    for i, (c, g) in enumerate(items):
        for s in range(nseg):
            ss = slice(s * seg_len, (s + 1) * seg_len)
            S = states[g] if chained else load_state(s0_ref, c * nseg + s, g)
            ypre_ref[c * C + s * seg_len:c * C + (s + 1) * seg_len, g * GL:(g + 1) * GL] = (
                _mm_nt(Rt[i][ss], bd(S)) + Y0[i][ss])
            S = _mm(S, PhiT[i, s]) + PsiT[i, s]
            if chained:
                states[g] = S
            else:
                for h in heads(g):
                    sl_ref[c * nseg + s, h] = S[:, (h % RW_GROUP_HEADS) * N:(h % RW_GROUP_HEADS + 1) * N]
    if chained:
        for g in range(RW_GROUPS):
            st_ref[g] = states[g]
            for h in heads(g):
                sl_ref[0, h] = states[g][:, (h % RW_GROUP_HEADS) * N:(h % RW_GROUP_HEADS + 1) * N]
    y = ypre_ref[...]

    mu = _seg_sum(y, ones_bd) * (1.0 / N)
    d = y - mu
    var = _seg_sum(d * d, ones_bd) * (1.0 / N)
    yn = d * lax.rsqrt(var + GN_EPS) * lg_ref[...] + lb_ref[...]
    bonus = _seg_sum(r * kmod * rk_ref[...], ones_bd) * v
    y_ref[...] = (yn + bonus) * out_gate


def _rwkv_constants(seg_len, rows):
    n, gh, gl = RWKV_HEAD, RW_GROUP_HEADS, RW_GROUP_LANES
    lane_head = np.arange(gl) // n
    hm = (lane_head[None, None, :] == np.arange(gh)[:, None, None]) * np.ones((gh, n, gl))
    i = np.arange(RW_CHUNK)[:, None]
    j = (np.arange(gl) % n)[None, :]
    same = (i // seg_len) == (j // seg_len)
    cm = np.stack([(j < i) & same, (j <= i) & same, j == i]).astype(np.float32)
    ri = np.arange(rows)[:, None]
    rj = np.arange(rows)[None, :]
    same_r = (ri // seg_len) == (rj // seg_len)
    tri = ((rj <= ri) & same_r).astype(np.float32)
    ones_bd = np.kron(np.eye(N_RWKV_HEADS), np.ones((n, n)))
    return (jnp.asarray(ones_bd, BF16), jnp.asarray(hm, BF16), jnp.asarray(cm, F32),
            jnp.asarray(tri, BF16), jnp.asarray(same_r.astype(np.float32), BF16))


def _rwkv(z, s0, p, layer, seq_len):
    n = z.shape[0]
    if seq_len >= RW_CHUNK:
        step_chunks = min(RW_STEP_CHUNKS_LONG, seq_len // RW_CHUNK)
    else:
        step_chunks = min(RW_STEP_CHUNKS_SHORT, n // RW_CHUNK)
    nseq = n // seq_len
    nh, hd = N_RWKV_HEADS, RWKV_HEAD
    seg_len = min(seq_len, RW_CHUNK)
    rows = RW_CHUNK * step_chunks
    chained = seq_len >= RW_CHUNK
    if chained:
        steps = seq_len // rows
        grid = (nseq, steps)
        row_map = lambda s, b: (s * steps + b, 0)
        state_spec = pl.BlockSpec((1, nh, hd, hd), lambda s, b: (s, 0, 0, 0))
        scratch = [pltpu.VMEM((rows, D_RWKV), F32), pltpu.VMEM((RW_GROUPS, hd, RW_GROUP_LANES), F32)]
    else:
        grid = (n // rows, 1)
        row_map = lambda s, b: (s, 0)
        state_spec = pl.BlockSpec((rows // seq_len, nh, hd, hd), lambda s, b: (s, 0, 0, 0))
        scratch = [pltpu.VMEM((rows, D_RWKV), F32)]
    if s0 is None:
        s0 = p['state_rwkv']
        s0_spec = pl.BlockSpec((None,) + state_spec.block_shape,
                               lambda s, b: (layer, s, 0, 0, 0))
    else:
        s0_spec = state_spec
    stacked = [p[k] for k in ('w0', 'w2', 'a0', 'a2', 'g2', 'k_k', 'k_a', 'r_k', 'lnx_g', 'lnx_b')]
    consts = list(_rwkv_constants(seg_len, rows))
    y, s_new = pl.pallas_call(
        functools.partial(_rwkv_kernel, seg_len=seg_len, step_chunks=step_chunks),
        grid=grid,
        in_specs=[pl.BlockSpec((rows, N_SHIFT), row_map), s0_spec]
        + [_layer_spec(t, layer) for t in stacked] + [_const_spec(t.shape) for t in consts],
        out_specs=[pl.BlockSpec((rows, D_RWKV), row_map), state_spec],
        out_shape=[jax.ShapeDtypeStruct((n, D_RWKV), F32),
                   jax.ShapeDtypeStruct((nseq, nh, hd, hd), F32)],
        scratch_shapes=scratch,
        compiler_params=pltpu.CompilerParams(dimension_semantics=("arbitrary", "arbitrary"),
                                             vmem_limit_bytes=V7X_VMEM_LIMIT_BYTES),
        name="rwkv7",
    )(z, s0, *stacked, *consts)
    return y, s_new


def _ffn_kernel(h_ref, y8_ref, u8_ref, yb_ref, pe_ref, c1_ref, c2_ref, d8_ref, wglu_ref, bglu_ref,
                wout_ref, g2_ref, wup_ref, cw_ref, cb_ref, wdn_ref, wple_ref, wpg_ref, gf_ref,
                *rest, seq_len, tm, final):
    i = pl.program_id(0)
    long_seq = seq_len >= tm
    if long_seq:
        o_ref, ga_ref, gb_ref, fold_ref, act_ref, carry_ref = rest

        @pl.when(i == 0)
        def _():
            carry_ref[...] = jnp.zeros_like(carry_ref)
    else:
        o_ref, ga_ref, gb_ref, fold_ref, act_ref = rest
    ya = y8_ref[...] + d8_ref[...] * u8_ref[...]
    c_gelu = math.sqrt(2.0 / math.pi)
    ya = ya * (0.5 * (1.0 + jnp.tanh(c_gelu * (ya + 0.044715 * (ya * ya * ya)))))
    ya = _unfold_rows(ya, fold_ref)
    ya = ya * _sigmoid(_mm(ya, wglu_ref[...]) + bglu_ref[...])
    h1 = (h_ref[...] + _mm(ya, wout_ref[:D_SSM, :]) + _mm(yb_ref[...], wout_ref[D_SSM:, :]))
    x2 = _rmsnorm(h1, g2_ref[...]).astype(BF16)
    row = lax.broadcasted_iota(jnp.int32, (tm, 1), 0)
    if long_seq:
        first = (i % (seq_len // tm)) == 0
    else:
        t = row % seq_len
    def up(c):
        cs = slice(c * FF_CHUNK, (c + 1) * FF_CHUNK)
        gs = slice(D_FF + c * FF_CHUNK, D_FF + (c + 1) * FF_CHUNK)
        return (jnp.dot(x2, wup_ref[:, cs], preferred_element_type=F32),
                jnp.dot(x2, wup_ref[:, gs], preferred_element_type=F32))

    n_chunks = D_FF // FF_CHUNK
    nxt = up(0)
    for c in range(n_chunks):
        cs = slice(c * FF_CHUNK, (c + 1) * FF_CHUNK)
        val, gate = nxt
        if c + 1 < n_chunks:
            nxt = up(c + 1)
        r1 = pltpu.roll(gate, 1, 0)
        r2 = pltpu.roll(gate, 2, 0)
        if long_seq:
            m1 = jnp.where(first, c1_ref[0][:, cs], carry_ref[1:2, cs])
            m2 = jnp.where(first, c2_ref[0][:, cs], carry_ref[0:1, cs])
            p1 = jnp.where(row == 0, m1, r1)
            p2 = jnp.where(row == 0, m2, jnp.where(row == 1, m1, r2))
            carry_ref[:, cs] = gate[tm - 2:tm, :]
            ga_ref[0, :, cs] = gate[tm - 2:tm - 1, :]
            gb_ref[0, :, cs] = gate[tm - 1:tm, :]
        else:
            m1 = _expand_rows(c1_ref[:, cs], seq_len)
            m2 = _expand_rows(c2_ref[:, cs], seq_len)
            p1 = jnp.where(t == 0, m1, r1)
            p2 = jnp.where(t == 0, m2, jnp.where(t == 1, m1, r2))
            ga_ref[:, cs] = _step_rows(gate, seq_len, seq_len - 2)
            gb_ref[:, cs] = _step_rows(gate, seq_len, seq_len - 1)
        conv = (cb_ref[:, cs] + cw_ref[2:3, cs] * gate + cw_ref[1:2, cs] * p1 + cw_ref[0:1, cs] * p2)
        act_ref[:, cs] = (conv * _sigmoid(conv) * val).astype(BF16)
    h2 = h1 + jnp.dot(act_ref[...], wdn_ref[...], preferred_element_type=F32)
    h3 = h2 + _mm(pe_ref[...], wple_ref[...]) * _sigmoid(_mm(h2, wpg_ref[...]))
    if final:
        h3 = _rmsnorm(h3, gf_ref[...])
    o_ref[...] = h3


def _ffn(h, y8, u8, yb, pe, conv0, p, layer, seq_len, tm, final):
    n = h.shape[0]
    nseq = n // seq_len
    row_spec = lambda w: pl.BlockSpec((tm, w), lambda i: (i, 0))
    pe_spec = pl.BlockSpec((None, tm, D_PLE), lambda i: (layer, i, 0))
    fold_spec = pl.BlockSpec((tm // S5_STEPS, S5_ROW), lambda i: (i, 0))
    scratch = [pltpu.VMEM((FOLD_LANE_BLOCKS, tm, LANES), F32), pltpu.VMEM((tm, D_FF), BF16)]
    if seq_len >= tm:
        tps = seq_len // tm
        seq_spec = pl.BlockSpec((1, 1, D_FF), lambda i: (i // tps, 0, 0))
        seq_shape = (nseq, 1, D_FF)
        scratch.append(pltpu.VMEM((2, D_FF), F32))
    else:
        seq_spec = pl.BlockSpec((tm // seq_len, D_FF), lambda i: (i, 0))
        seq_shape = (nseq, D_FF)
    stacked = [p[k] for k in ('d8', 'w_glu', 'b_glu', 'w_out', 'norm2_g', 'w_up', 'conv_w',
                              'conv_b', 'w_down', 'w_ple', 'w_pg')]
    consts = [p['final_g']]
    weights = stacked + consts
    out, ga, gb = pl.pallas_call(
        functools.partial(_ffn_kernel, seq_len=seq_len, tm=tm, final=final),
        grid=(n // tm,),
        in_specs=[row_spec(D_MODEL), fold_spec, fold_spec, row_spec(D_RWKV), pe_spec,
                  seq_spec, seq_spec] + [_layer_spec(t, layer) for t in stacked]
        + [_const_spec(t.shape) for t in consts],
        out_specs=[row_spec(D_MODEL), seq_spec, seq_spec],
        out_shape=[jax.ShapeDtypeStruct((n, D_MODEL), F32)]
        + [jax.ShapeDtypeStruct(seq_shape, F32)] * 2,
        scratch_shapes=scratch,
        compiler_params=pltpu.CompilerParams(dimension_semantics=("arbitrary",),
                                             vmem_limit_bytes=V7X_VMEM_LIMIT_BYTES),
        name="ffn",
    )(h, y8, u8, yb, pe, conv0[:, 1].reshape(seq_shape), conv0[:, 0].reshape(seq_shape), *weights)
    conv_new = jnp.stack([ga.reshape(nseq, D_FF), gb.reshape(nseq, D_FF)], axis=1)
    return out, conv_new


def _layer(h, pe, st, p, layer, s5_tables, seq_len, tm, final):
    ssm_re0, ssm_im0, rwkv0, shift0, conv0 = st
    u8, z, shift_new = _inproj(h, p, layer, shift0, seq_len, tm)
    y8, hre, him = _s5(u8, ssm_re0, ssm_im0, s5_tables, seq_len)
    yb, s_last = _rwkv(z, rwkv0, p, layer, seq_len)
    h, conv_new = _ffn(h, y8, u8, yb, pe, conv0, p, layer, seq_len, tm, final)
    return h, (hre, him, s_last, shift_new, conv_new)


def _stacked_params(w, state_rwkv, state_shift):
    depth = w['w_in'].shape[0]
    row = lambda t: t.reshape(depth, 1, -1).astype(F32)
    bf = lambda t: t.astype(BF16)
    return {
        'norm1_g': row(w['norm1_g']), 'w_in': bf(w['w_in']), 'shift_mu': row(w['shift_mu']),
        'd8': jnp.tile(row(w['ssm_d']), (1, 1, S5_STEPS)), 'w_glu': bf(w['ssm_w_glu']),
        'b_glu': row(w['ssm_b_glu']),
        'w0': row(w['rwkv_w0']), 'w2': bf(w['rwkv_w2']), 'a0': row(w['rwkv_a0']),
        'a2': bf(w['rwkv_a2']), 'g2': bf(w['rwkv_g2']), 'k_k': row(w['rwkv_k_k']),
        'k_a': row(w['rwkv_k_a']), 'r_k': row(w['rwkv_r_k']), 'lnx_g': row(w['rwkv_lnx_g']),
        'lnx_b': row(w['rwkv_lnx_b']), 'w_out': bf(w['w_out']), 'norm2_g': row(w['norm2_g']),
        'w_up': bf(w['w_up']), 'conv_w': w['conv_w'].astype(F32), 'conv_b': row(w['conv_b']),
        'w_down': bf(w['w_down']), 'w_ple': bf(w['w_ple']), 'w_pg': bf(w['w_pg']),
        'final_g': w['final_g'].reshape(1, -1).astype(F32),
        'state_rwkv': state_rwkv.astype(F32), 'state_shift': state_shift.astype(F32),
    }


def _forward(x_prompt, x_sample, p_prompt, p_sample, state_ssm_re, state_ssm_im, state_rwkv,
             state_shift, state_conv, w, tm_prompt, tm_sample):
    depth = w['w_in'].shape[0]
    bp, lp, _ = x_prompt.shape
    bs, ls, _ = x_sample.shape
    hp = x_prompt.reshape(bp * lp, D_MODEL).astype(F32)
    hs = x_sample.reshape(bs * ls, D_MODEL).astype(F32)
    pe_p = p_prompt.reshape(depth, bp * lp, D_PLE).astype(F32)
    pe_s = p_sample.reshape(depth, bs * ls, D_PLE).astype(F32)
    zero_st = (jnp.zeros((bp, N_SSM_GROUPS, SSM_STATE), F32),
               jnp.zeros((bp, N_SSM_GROUPS, SSM_STATE), F32),
               jnp.zeros((bp, N_RWKV_HEADS, RWKV_HEAD, RWKV_HEAD), F32),
               jnp.zeros((bp, N_SHIFT), F32),
               jnp.zeros((bp, 2, D_FF), F32))
    p = _stacked_params(w, state_rwkv, state_shift)
    new_p = [[] for _ in range(5)]
    new_s = [[] for _ in range(5)]
    for i in range(depth):
        tables = _s5_tables(w['ssm_lam_re'][i], w['ssm_lam_im'][i], w['ssm_log_dt'][i],
                            w['ssm_b_re'][i], w['ssm_b_im'][i], w['ssm_c_re'][i], w['ssm_c_im'][i])
        final = i == depth - 1
        hp, stp = _layer(hp, pe_p, zero_st, p, i, tables, lp, tm_prompt, final)
        st_in = (state_ssm_re[i].astype(F32), state_ssm_im[i].astype(F32), None, None,
                 state_conv[i].astype(F32))
        hs, sts = _layer(hs, pe_s, st_in, p, i, tables, ls, tm_sample, final)
        for j in range(5):
            new_p[j].append(stp[j])
            new_s[j].append(sts[j])
    y_prompt = hp.reshape(bp, lp, D_MODEL).astype(x_prompt.dtype)
    y_sample = hs.reshape(bs, ls, D_MODEL).astype(x_sample.dtype)
    dts = (state_ssm_re.dtype, state_ssm_im.dtype, state_rwkv.dtype, state_shift.dtype,
           state_conv.dtype)
    outs_p = tuple(jnp.stack(new_p[j]).astype(dts[j]) for j in range(5))
    outs_s = tuple(jnp.stack(new_s[j]).astype(dts[j]) for j in range(5))
    return (y_prompt, y_sample) + outs_p + outs_s


def kernel(x_prompt, x_sample, p_prompt, p_sample, state_ssm_re, state_ssm_im, state_rwkv, state_shift, state_conv, norm1_g, w_in, shift_mu, ssm_lam_re, ssm_lam_im, ssm_log_dt, ssm_b_re, ssm_b_im, ssm_c_re, ssm_c_im, ssm_d, ssm_w_glu, ssm_b_glu, rwkv_w0, rwkv_w2, rwkv_a0, rwkv_a2, rwkv_g2, rwkv_k_k, rwkv_k_a, rwkv_r_k, rwkv_lnx_g, rwkv_lnx_b, w_out, norm2_g, w_up, conv_w, conv_b, w_down, w_ple, w_pg, final_g):
    w = dict(norm1_g=norm1_g, w_in=w_in, shift_mu=shift_mu, ssm_lam_re=ssm_lam_re,
             ssm_lam_im=ssm_lam_im, ssm_log_dt=ssm_log_dt, ssm_b_re=ssm_b_re, ssm_b_im=ssm_b_im,
             ssm_c_re=ssm_c_re, ssm_c_im=ssm_c_im, ssm_d=ssm_d, ssm_w_glu=ssm_w_glu,
             ssm_b_glu=ssm_b_glu, rwkv_w0=rwkv_w0, rwkv_w2=rwkv_w2, rwkv_a0=rwkv_a0,
             rwkv_a2=rwkv_a2, rwkv_g2=rwkv_g2, rwkv_k_k=rwkv_k_k, rwkv_k_a=rwkv_k_a,
             rwkv_r_k=rwkv_r_k, rwkv_lnx_g=rwkv_lnx_g, rwkv_lnx_b=rwkv_lnx_b, w_out=w_out,
             norm2_g=norm2_g, w_up=w_up, conv_w=conv_w, conv_b=conv_b, w_down=w_down,
             w_ple=w_ple, w_pg=w_pg, final_g=final_g)
    lp = x_prompt.shape[1]
    ns = x_sample.shape[0] * x_sample.shape[1]
    return _forward(x_prompt, x_sample, p_prompt, p_sample, state_ssm_re, state_ssm_im,
                    state_rwkv, state_shift, state_conv, w,
                    tm_prompt=min(512, lp), tm_sample=min(256, ns))
```

```python
import functools
import math

import numpy as np
import jax
import jax.numpy as jnp
from jax import lax
from jax.experimental import pallas as pl
from jax.experimental.pallas import tpu as pltpu

F32 = jnp.float32
BF16 = jnp.bfloat16
HIGHEST = lax.Precision.HIGHEST

D_MODEL = 1024
D_SSM = 512
D_RWKV = 512
SSM_GROUP = 16
N_SSM_GROUPS = 32
SSM_STATE = 64
RWKV_HEAD = 64
N_RWKV_HEADS = 8
DECAY_LORA = 64
AAA_LORA = 64
GATE_LORA = 128
N_SHIFT = 3 * D_RWKV + DECAY_LORA + AAA_LORA + GATE_LORA
N_IN = D_SSM + N_SHIFT
D_FF = 2816
D_PLE = 256
RMS_EPS = 1e-6
GN_EPS = 64e-5
L2_EPS = 1e-12

LANES = 128
FOLD_LANE_BLOCKS = D_SSM // LANES
S5_STEPS = 8
S5_ROW = S5_STEPS * D_SSM
S5_LANE_BLOCKS = 4
S5_BLOCK_STATE = 2 * 8 * SSM_STATE
S5_STATE_ROW = S5_LANE_BLOCKS * S5_BLOCK_STATE
FF_CHUNK = 256
V7X_VMEM_LIMIT_BYTES = 56 * 1024 * 1024


def _mm(a, b):
    return jnp.dot(a.astype(BF16), b.astype(BF16), preferred_element_type=F32)


def _mm_nt(a, b):
    return lax.dot_general(a.astype(BF16), b.astype(BF16), (((1,), (1,)), ((), ())),
                           preferred_element_type=F32)


def _mm_tn(a, b):
    return lax.dot_general(a.astype(BF16), b.astype(BF16), (((0,), (0,)), ((), ())),
                           preferred_element_type=F32)


def _seg_sum(x, ones_bd):
    hi = x.astype(BF16)
    lo = (x - hi.astype(F32)).astype(BF16)
    return (jnp.dot(hi, ones_bd, preferred_element_type=F32)
            + jnp.dot(lo, ones_bd, preferred_element_type=F32))


def _rmsnorm(x, g):
    return x * lax.rsqrt(jnp.mean(x * x, axis=-1, keepdims=True) + RMS_EPS) * g


def _sigmoid(x):
    return 1.0 / (1.0 + jnp.exp(-x))


def _const_spec(shape):
    nd = len(shape)
    return pl.BlockSpec(shape, lambda *_: (0,) * nd, pipeline_mode=pl.Buffered(1))


def _layer_spec(t, layer):
    nd = t.ndim - 1
    return pl.BlockSpec((None,) + t.shape[1:], lambda *_: (layer,) + (0,) * nd,
                        pipeline_mode=pl.Buffered(1))


def _expand_rows(x, seq_len):
    nseq, w = x.shape
    return jnp.broadcast_to(x[:, None, :], (nseq, seq_len, w)).reshape(nseq * seq_len, w)


def _step_rows(x, seq_len, t):
    return x.reshape(x.shape[0] // seq_len, seq_len, x.shape[1])[:, t, :]


def _fold_rows(x, scr_ref, out_ref):
    tm = x.shape[0]
    for q in range(FOLD_LANE_BLOCKS):
        scr_ref[q] = x[:, q * LANES:(q + 1) * LANES]
    for s in range(S5_STEPS):
        for q in range(FOLD_LANE_BLOCKS):
            c0 = s * D_SSM + q * LANES
            out_ref[:, c0:c0 + LANES] = scr_ref[q, pl.ds(s, tm // S5_STEPS, stride=S5_STEPS), :]


def _unfold_rows(x8, scr_ref):
    tm = x8.shape[0] * S5_STEPS
    for s in range(S5_STEPS):
        for q in range(FOLD_LANE_BLOCKS):
            c0 = s * D_SSM + q * LANES
            scr_ref[q, pl.ds(s, tm // S5_STEPS, stride=S5_STEPS), :] = x8[:, c0:c0 + LANES]
    return jnp.concatenate([scr_ref[q] for q in range(FOLD_LANE_BLOCKS)], axis=1)


def _inproj_kernel(h_ref, g_ref, w_ref, mu_ref, init_ref, *rest, seq_len, tm):
    i = pl.program_id(0)
    long_seq = seq_len >= tm
    if long_seq:
        u8_ref, z_ref, last_ref, fold_ref, carry_ref = rest

        @pl.when(i == 0)
        def _():
            carry_ref[...] = jnp.zeros_like(carry_ref)
    else:
        u8_ref, z_ref, last_ref, fold_ref = rest
    xn = _rmsnorm(h_ref[...], g_ref[...])
    proj = jnp.dot(xn.astype(BF16), w_ref[...], preferred_element_type=F32)
    _fold_rows(proj[:, :D_SSM], fold_ref, u8_ref)
    zr = proj[:, D_SSM:]
    rolled = pltpu.roll(zr, 1, 0)
    row = lax.broadcasted_iota(jnp.int32, (tm, 1), 0)
    if long_seq:
        first = (i % (seq_len // tm)) == 0
        row0 = jnp.where(first, init_ref[0], carry_ref[...])
        prev = jnp.where(row == 0, row0, rolled)
        carry_ref[...] = zr[tm - 1:tm, :]
        last_ref[0] = zr[tm - 1:tm, :]
    else:
        prev = jnp.where(row % seq_len == 0, _expand_rows(init_ref[...], seq_len), rolled)
        last_ref[...] = _step_rows(zr, seq_len, seq_len - 1)
    z_ref[...] = (zr + (prev - zr) * mu_ref[...]).astype(z_ref.dtype)


def _inproj(h, p, layer, shift0, seq_len, tm):
    n = h.shape[0]
    nseq = n // seq_len
    fold_scratch = pltpu.VMEM((FOLD_LANE_BLOCKS, tm, LANES), F32)
    if seq_len >= tm:
        tps = seq_len // tm
        seq_spec = pl.BlockSpec((1, 1, N_SHIFT), lambda i: (i // tps, 0, 0))
        init_spec = seq_spec
        init = shift0.reshape(nseq, 1, N_SHIFT)
        last_shape = jax.ShapeDtypeStruct((nseq, 1, N_SHIFT), F32)
        scratch = [fold_scratch, pltpu.VMEM((1, N_SHIFT), F32)]
    else:
        spt = tm // seq_len
        seq_spec = pl.BlockSpec((spt, N_SHIFT), lambda i: (i, 0))
        init_spec = pl.BlockSpec((None, spt, N_SHIFT), lambda i: (layer, i, 0))
        init = p['state_shift']
        last_shape = jax.ShapeDtypeStruct((nseq, N_SHIFT), F32)
        scratch = [fold_scratch]
    g1, w_in, mu = p['norm1_g'], p['w_in'], p['shift_mu']
    u8, z, last = pl.pallas_call(
        functools.partial(_inproj_kernel, seq_len=seq_len, tm=tm),
        grid=(n // tm,),
        in_specs=[pl.BlockSpec((tm, D_MODEL), lambda i: (i, 0)),
                  _layer_spec(g1, layer), _layer_spec(w_in, layer), _layer_spec(mu, layer),
                  init_spec],
        out_specs=[pl.BlockSpec((tm // S5_STEPS, S5_ROW), lambda i: (i, 0)),
                   pl.BlockSpec((tm, N_SHIFT), lambda i: (i, 0)),
                   seq_spec],
        out_shape=[jax.ShapeDtypeStruct((n // S5_STEPS, S5_ROW), F32),
                   jax.ShapeDtypeStruct((n, N_SHIFT), BF16),
                   last_shape],
        scratch_shapes=scratch,
        compiler_params=pltpu.CompilerParams(dimension_semantics=("arbitrary",),
                                             vmem_limit_bytes=V7X_VMEM_LIMIT_BYTES),
        name="inproj",
    )(h, g1, w_in, mu, init)
    return u8, z, last.reshape(nseq, N_SHIFT)


def _s5_prep_kernel(lr_ref, li_ref, ldt_ref, br_ref, bi_ref, cr_ref, ci_ref, tile_ref, sel_ref,
                    wx_ref, wy_ref, apr_ref, api_ref):
    lr, li = lr_ref[...], li_ref[...]
    dt = jnp.exp(ldt_ref[...])
    mag = jnp.exp(lr * dt)
    ar = mag * jnp.cos(li * dt)
    ai = mag * jnp.sin(li * dt)
    den = lr * lr + li * li
    nr = ar - 1.0
    fr = (nr * lr + ai * li) / den
    fi = (ai * lr - nr * li) / den
    br, bi = br_ref[...], bi_ref[...]
    bbr = fr * br - fi * bi
    bbi = fr * bi + fi * br
    cr, ci = cr_ref[...], ci_ref[...]
    pr = [jnp.ones_like(ar)]
    pi = [jnp.zeros_like(ar)]
    for _ in range(S5_STEPS):
        pr.append(pr[-1] * ar - pi[-1] * ai)
        pi.append(pr[-2] * ai + pi[-1] * ar)
    nrow = lr.shape[0]
    blk = nrow // S5_LANE_BLOCKS
    half = S5_BLOCK_STATE // 2
    ri = lax.broadcasted_iota(jnp.int32, (nrow, nrow), 0)
    ci_ = lax.broadcasted_iota(jnp.int32, (nrow, nrow), 1)
    same_group = (ri // SSM_GROUP) == (ci_ // SSM_GROUP)
    keep_in = ((ri % blk) // SSM_GROUP) == (ci_ // SSM_STATE)
    keep_out = (ri // SSM_STATE) == ((ci_ % blk) // SSM_GROUP)
    tile, sel = tile_ref[...], sel_ref[...]
    nt = (((1,), (1,)), ((), ()))
    wy_ref[...] = jnp.zeros(wy_ref.shape, wy_ref.dtype)
    for s in range(S5_STEPS):
        qr, qi = pr[S5_STEPS - 1 - s], pi[S5_STEPS - 1 - s]
        wr = qr * bbr - qi * bbi
        wi = qr * bbi + qi * bbr
        for h, w_ in enumerate((wr, wi)):
            t = jnp.dot(w_.astype(BF16), tile, preferred_element_type=F32)
            t = jnp.where(keep_in, t, 0.0).astype(BF16)
            for q in range(S5_LANE_BLOCKS):
                wx_ref[q, s * blk:(s + 1) * blk, h * half:(h + 1) * half] = t[q * blk:(q + 1) * blk, :]
        tau = S5_STEPS - 1 - s
        kt = (lax.dot_general(wr, cr, nt, precision=HIGHEST, preferred_element_type=F32)
              - lax.dot_general(wi, ci, nt, precision=HIGHEST, preferred_element_type=F32))
        kt = jnp.where(same_group, kt, 0.0).astype(BF16)
        for q in range(S5_LANE_BLOCKS):
            kq = kt[q * blk:(q + 1) * blk, q * blk:(q + 1) * blk]
            for s_in in range(S5_STEPS - tau):
                s_out = s_in + tau
                r0 = S5_BLOCK_STATE + s_in * blk
                wy_ref[q, s_out // 2, r0:r0 + blk, (s_out % 2) * blk:(s_out % 2 + 1) * blk] = kq
        mr = cr * pr[s + 1] - ci * pi[s + 1]
        mi = cr * pi[s + 1] + ci * pr[s + 1]
        for h, m_ in enumerate((mr, -mi)):
            t = lax.dot_general(sel, m_.astype(BF16), nt, preferred_element_type=F32)
            t = jnp.where(keep_out, t, 0.0).astype(BF16)
            for q in range(S5_LANE_BLOCKS):
                wy_ref[q, s // 2, h * half:(h + 1) * half, (s % 2) * blk:(s % 2 + 1) * blk] = (
                    t[:, q * blk:(q + 1) * blk])
    a8r, a8i = pr[S5_STEPS], pi[S5_STEPS]
    er, ei = a8r, a8i
    for n in range(S5_STEPS):
        apr_ref[n] = er
        api_ref[n] = ei
        er, ei = er * a8r - ei * a8i, er * a8i + ei * a8r


def _s5_tables(lam_re, lam_im, log_dt, b_re, b_im, c_re, c_im):
    G, P, K = N_SSM_GROUPS, SSM_STATE, SSM_GROUP
    Q, GB = S5_LANE_BLOCKS, G // S5_LANE_BLOCKS
    rep = lambda t: jnp.repeat(t, K, axis=0)
    tile = np.tile(np.eye(P, dtype=np.float32), (1, GB))
    args = (rep(lam_re), rep(lam_im), rep(jnp.broadcast_to(log_dt[:, None], (G, P))),
            jnp.swapaxes(b_re, 1, 2).reshape(G * K, P), jnp.swapaxes(b_im, 1, 2).reshape(G * K, P),
            c_re.reshape(G * K, P), c_im.reshape(G * K, P),
            jnp.asarray(tile, BF16), jnp.asarray(tile.T, BF16))
    t3 = jax.ShapeDtypeStruct((S5_STEPS, G * K, P), F32)
    wx, wy, apr, api = pl.pallas_call(
        _s5_prep_kernel,
        out_shape=[jax.ShapeDtypeStruct((Q, S5_STEPS * GB * K, S5_BLOCK_STATE), BF16),
                   jax.ShapeDtypeStruct((Q, S5_STEPS // 2, S5_BLOCK_STATE + S5_STEPS * GB * K,
                                         2 * GB * K), BF16),
                   t3, t3],
        compiler_params=pltpu.CompilerParams(vmem_limit_bytes=V7X_VMEM_LIMIT_BYTES),
        name="s5_prep",
    )(*args)
    ap = jnp.concatenate([t[:, ::K, :].reshape(S5_STEPS, Q, GB * P) for t in (apr, api)], axis=-1)
    return wx, wy, ap.reshape(S5_STEPS, S5_STATE_ROW)


def _state_to_lanes(h_re, h_im):
    n = h_re.shape[0]
    parts = [t.reshape(n, S5_LANE_BLOCKS, -1) for t in (h_re, h_im)]
    return jnp.concatenate(parts, axis=-1).reshape(n, S5_STATE_ROW)


def _lanes_to_state(h):
    n = h.shape[0]
    h = h.reshape(n, S5_LANE_BLOCKS, 2, N_SSM_GROUPS // S5_LANE_BLOCKS, SSM_STATE)
    return (h[:, :, 0].reshape(n, N_SSM_GROUPS, SSM_STATE), h[:, :, 1].reshape(n, N_SSM_GROUPS, SSM_STATE))


def _s5_kernel(u_ref, h0_ref, wx_ref, wy_ref, ap_ref, y_ref, hl_ref, *scratch, rows, scan):
    half = S5_BLOCK_STATE // 2
    cw = D_SSM // S5_LANE_BLOCKS
    blocks = range(S5_LANE_BLOCKS)
    ub = u_ref[...].astype(BF16)
    ucat = [jnp.concatenate([ub[:, s * D_SSM + q * cw: s * D_SSM + (q + 1) * cw]
                             for s in range(S5_STEPS)], axis=1) for q in blocks]
    x = [jnp.dot(ucat[q], wx_ref[q], preferred_element_type=F32) for q in blocks]
    ap = [ap_ref[:, q * S5_BLOCK_STATE:(q + 1) * S5_BLOCK_STATE] for q in blocks]
    if scan:
        hs_ref, hp_ref = scratch
        sub3 = lax.broadcasted_iota(jnp.int32, (1, 8, 1), 1)
        sub2 = lax.broadcasted_iota(jnp.int32, (8, 1), 0)
        for q in blocks:
            xr = x[q][:, :half].reshape(rows // 8, 8, half)
            xi = x[q][:, half:].reshape(rows // 8, 8, half)
            for k in (1, 2, 4):
                er = ap[q][k - 1:k, :half].reshape(1, 1, half)
                ei = ap[q][k - 1:k, half:].reshape(1, 1, half)
                sr = jnp.where(sub3 >= k, pltpu.roll(xr, k, 1), 0.0)
                si = jnp.where(sub3 >= k, pltpu.roll(xi, k, 1), 0.0)
                xr, xi = xr + er * sr - ei * si, xi + er * si + ei * sr
            hs_ref[q, :, :half] = xr.reshape(rows, half)
            hs_ref[q, :, half:] = xi.reshape(rows, half)
        h0 = h0_ref[0]
        init = tuple(h0[:, q * S5_BLOCK_STATE + o:q * S5_BLOCK_STATE + o + half]
                     for q in blocks for o in (0, half))

        def body(j, carry):
            sl = pl.ds(pl.multiple_of(j * 8, 8), 8)
            out = []
            for q in blocks:
                cr, ci = carry[2 * q], carry[2 * q + 1]
                pwr, pwi = ap[q][:, :half], ap[q][:, half:]
                hr = hs_ref[q, sl, :half] + pwr * cr - pwi * ci
                hi = hs_ref[q, sl, half:] + pwr * ci + pwi * cr
                hp_ref[q, sl, :half] = jnp.where(sub2 == 0, cr, pltpu.roll(hr, 1, 0))
                hp_ref[q, sl, half:] = jnp.where(sub2 == 0, ci, pltpu.roll(hi, 1, 0))
                out += [hr[7:8, :], hi[7:8, :]]
            return tuple(out)

        last = lax.fori_loop(0, rows // 8, body, init)
        for q in blocks:
            lo = q * S5_BLOCK_STATE
            hl_ref[0, :, lo:lo + half] = last[2 * q]
            hl_ref[0, :, lo + half:lo + S5_BLOCK_STATE] = last[2 * q + 1]
        hprev = [hp_ref[q].astype(BF16) for q in blocks]
    else:
        hprev = []
        for q in blocks:
            lo = q * S5_BLOCK_STATE
            h0r = h0_ref[:, lo:lo + half]
            h0i = h0_ref[:, lo + half:lo + S5_BLOCK_STATE]
            er, ei = ap[q][0:1, :half], ap[q][0:1, half:]
            hl_ref[:, lo:lo + half] = er * h0r - ei * h0i + x[q][:, :half]
            hl_ref[:, lo + half:lo + S5_BLOCK_STATE] = er * h0i + ei * h0r + x[q][:, half:]
            hprev.append(jnp.concatenate([h0r, h0i], axis=1).astype(BF16))
    for q in blocks:
        lhs = jnp.concatenate([hprev[q], ucat[q]], axis=1)
        for j in range(S5_STEPS // 2):
            kk = S5_BLOCK_STATE + cw * (2 * j + 2)
            y2 = jnp.dot(lhs[:, :kk], wy_ref[q, j, :kk, :], preferred_element_type=F32)
            c0 = (2 * j) * D_SSM + q * cw
            c1 = (2 * j + 1) * D_SSM + q * cw
            y_ref[:, c0:c0 + cw] = y2[:, :cw]
            y_ref[:, c1:c1 + cw] = y2[:, cw:]


def _s5(u8, h_re, h_im, tables, seq_len):
    wx, wy, ap = tables
    n = u8.shape[0] * S5_STEPS
    nseq = n // seq_len
    h0 = _state_to_lanes(h_re, h_im)
    cps = seq_len // S5_STEPS
    scan = cps > 1
    if scan:
        rows = cps
        grid = (nseq,)
        h0 = h0.reshape(nseq, 1, S5_STATE_ROW)
        h_spec = pl.BlockSpec((1, 1, S5_STATE_ROW), lambda i: (i, 0, 0))
        h_shape = jax.ShapeDtypeStruct((nseq, 1, S5_STATE_ROW), F32)
    else:
        rows = min(nseq, 128)
        grid = (nseq // rows,)
        h_spec = pl.BlockSpec((rows, S5_STATE_ROW), lambda i: (i, 0))
        h_shape = jax.ShapeDtypeStruct((nseq, S5_STATE_ROW), F32)
    y8, hl = pl.pallas_call(
        functools.partial(_s5_kernel, rows=rows, scan=scan),
        grid=grid,
        in_specs=[pl.BlockSpec((rows, S5_ROW), lambda i: (i, 0)),
                  h_spec,
                  _const_spec(wx.shape), _const_spec(wy.shape), _const_spec(ap.shape)],
        out_specs=[pl.BlockSpec((rows, S5_ROW), lambda i: (i, 0)), h_spec],
        out_shape=[jax.ShapeDtypeStruct((n // S5_STEPS, S5_ROW), F32), h_shape],
        scratch_shapes=[pltpu.VMEM((S5_LANE_BLOCKS, rows, S5_BLOCK_STATE), F32)] * 2 if scan else [],
        compiler_params=pltpu.CompilerParams(dimension_semantics=("arbitrary",),
                                             vmem_limit_bytes=V7X_VMEM_LIMIT_BYTES),
        name="s5",
    )(u8, h0, wx, wy, ap)
    hre_new, him_new = _lanes_to_state(hl.reshape(nseq, S5_STATE_ROW))
    return y8, hre_new, him_new


RW_CHUNK = 64
RW_STEP_CHUNKS_LONG = 4
RW_STEP_CHUNKS_SHORT = 4
RW_GROUPS = 2
RW_GROUP_HEADS = N_RWKV_HEADS // RW_GROUPS
RW_GROUP_LANES = D_RWKV // RW_GROUPS


def _split3(x):
    hi = x.astype(BF16)
    r1 = x - hi.astype(F32)
    mid = r1.astype(BF16)
    lo = (r1 - mid.astype(F32)).astype(BF16)
    return hi, mid, lo


def _bd(x, hm_ref):
    xb = x.astype(BF16)
    return jnp.concatenate([xb * hm_ref[h] for h in range(RW_GROUP_HEADS)], axis=0)


def _unbd(f, hm_ref):
    n = RWKV_HEAD
    out = f[:n] * hm_ref[0].astype(F32)
    for h in range(1, RW_GROUP_HEADS):
        out = out + f[h * n:(h + 1) * n] * hm_ref[h].astype(F32)
    return out


def _rwkv_kernel(z_ref, s0_ref, w0_ref, w2_ref, a0_ref, a2_ref, g2_ref, kk_ref, ka_ref, rk_ref,
                 lg_ref, lb_ref, ones_ref, hm_ref, cm_ref, tri_ref, seg_ref, y_ref, sl_ref, ypre_ref,
                 *scratch, seg_len, step_chunks):
    C = RW_CHUNK
    N = RWKV_HEAD
    GL = RW_GROUP_LANES
    chained = seg_len == C
    if chained:
        (st_ref,) = scratch
        b = pl.program_id(1)

        @pl.when(b == 0)
        def _():
            for g in range(RW_GROUPS):
                st_ref[g] = jnp.concatenate(
                    [s0_ref[0, h] for h in range(g * RW_GROUP_HEADS, (g + 1) * RW_GROUP_HEADS)], axis=1)

    ones_bd = ones_ref[...]
    z = z_ref[...].astype(F32)
    r = z[:, :D_RWKV]
    k = z[:, D_RWKV:2 * D_RWKV]
    v = z[:, 2 * D_RWKV:3 * D_RWKV]
    o = 3 * D_RWKV
    xw = z[:, o:o + DECAY_LORA]
    xa = z[:, o + DECAY_LORA:o + DECAY_LORA + AAA_LORA]
    xg = z[:, o + DECAY_LORA + AAA_LORA:]
    wd = -(w0_ref[...] + _mm(jnp.tanh(xw), w2_ref[...]))
    w = -(jnp.maximum(wd, 0.0) + jnp.log1p(jnp.exp(-jnp.abs(wd)))) - 0.5
    lw = -jnp.exp(w)
    a = _sigmoid(a0_ref[...] + _mm(xa, a2_ref[...]))
    out_gate = _mm(_sigmoid(xg), g2_ref[...])
    kk = k * kk_ref[...]
    kk = kk / jnp.maximum(jnp.sqrt(_seg_sum(kk * kk, ones_bd)), L2_EPS)
    kmod = k * (1.0 + (a - 1.0) * ka_ref[...])

    terms = _split3(lw)
    tri, seg = tri_ref[...], seg_ref[...]
    cw = sum(jnp.dot(tri, t, preferred_element_type=F32) for t in terms)
    tot = sum(jnp.dot(seg, t, preferred_element_type=F32) for t in terms)
    w_in = jnp.exp(cw)
    w_tail = jnp.exp(tot - cw)
    w_inv = jnp.exp(-cw)
    w_all = jnp.exp(tot)
    ah = -kk * jnp.exp(cw - lw)
    bh = kk * a * w_inv
    kh = kmod * w_inv
    rh = r * w_in
    bt = kk * a * w_tail
    kt = kmod * w_tail
    strict = cm_ref[0] > 0.0
    incl = cm_ref[1] > 0.0
    eye_cat = cm_ref[2]
    n_fac = seg_len.bit_length() - 1

    items = [(c, g) for c in range(step_chunks) for g in range(RW_GROUPS)]
    sub = lambda t, c, g: t[c * C:(c + 1) * C, g * GL:(g + 1) * GL]
    bd = lambda t: _bd(t, hm_ref)
    A = [sub(ah, c, g) for c, g in items]
    R = [sub(rh, c, g) for c, g in items]
    V = [sub(v, c, g) for c, g in items]
    G = [_mm_nt(jnp.concatenate([A[i], R[i]], axis=0),
                jnp.concatenate([bd(sub(bh, c, g)), bd(sub(kh, c, g))], axis=0))
         for i, (c, g) in enumerate(items)]
    AB = [jnp.where(strict, t[:C, :GL], 0.0) for t in G]
    AK = [jnp.where(strict, t[:C, GL:], 0.0) for t in G]
    RB = [jnp.where(incl, t[C:, :GL], 0.0) for t in G]
    RK = [jnp.where(incl, t[C:, GL:], 0.0) for t in G]
    KV = [_mm(jnp.concatenate([AK[i], RK[i]], axis=0), bd(V[i])) for i in range(len(items))]
    T = [eye_cat + t for t in AB]
    P = [_mm(t, bd(t)) for t in AB]
    for _ in range(n_fac - 2):
        PT = [_mm(jnp.concatenate([P[i], T[i]], axis=0), bd(P[i])) for i in range(len(items))]
        P = [t[:C] for t in PT]
        T = [T[i] + PT[i][C:] for i in range(len(items))]
    T = [T[i] + _mm(T[i], bd(P[i])) for i in range(len(items))]
    RBT = [_mm(RB[i], bd(T[i])) for i in range(len(items))]
    X = [_mm(jnp.concatenate([T[i], RBT[i]], axis=0),
             jnp.concatenate([bd(A[i]), bd(KV[i][:C])], axis=1)) for i in range(len(items))]
    TA = [t[:C, :GL] for t in X]
    U0 = [t[:C, GL:] for t in X]
    Rt = [R[i] + X[i][C:, :GL] for i in range(len(items))]
    Y0 = [X[i][C:, GL:] + KV[i][C:] for i in range(len(items))]
    nseg = C // seg_len
    PhiT, PsiT = {}, {}
    for i, (c, g) in enumerate(items):
        Bt, Kt, Wa = sub(bt, c, g), sub(kt, c, g), sub(w_all, c, g)
        for s in range(nseg):
            ss = slice(s * seg_len, (s + 1) * seg_len)
            PhiT[i, s] = bd(eye_cat * Wa[s * seg_len:s * seg_len + 1, :]
                            + _unbd(_mm_tn(TA[i][ss], Bt[ss]), hm_ref))
            PsiT[i, s] = _unbd(_mm_tn(jnp.concatenate([U0[i][ss], V[i][ss]], axis=0),
                                      jnp.concatenate([Bt[ss], Kt[ss]], axis=0)), hm_ref)
    heads = lambda g: range(g * RW_GROUP_HEADS, (g + 1) * RW_GROUP_HEADS)
    load_state = lambda ref, j, g: jnp.concatenate([ref[j, h] for h in heads(g)], axis=1)
    if chained:
        states = [st_ref[g] for g in range(RW_GROUPS)]
    for i, (c, g) in enumerate(items):
        for s in range(nseg):
            ss = slice(s * seg_len, (s + 1) * seg_len)
            S = states[g] if chained else load_state(s0_ref, c * nseg + s, g)
            ypre_ref[c * C + s * seg_len:c * C + (s + 1) * seg_len, g * GL:(g + 1) * GL] = (
                _mm_nt(Rt[i][ss], bd(S)) + Y0[i][ss])
            S = _mm(S, PhiT[i, s]) + PsiT[i, s]
            if chained:
                states[g] = S
            else:
                for h in heads(g):
                    sl_ref[c * nseg + s, h] = S[:, (h % RW_GROUP_HEADS) * N:(h % RW_GROUP_HEADS + 1) * N]
    if chained:
        for g in range(RW_GROUPS):
            st_ref[g] = states[g]
            for h in heads(g):
                sl_ref[0, h] = states[g][:, (h % RW_GROUP_HEADS) * N:(h % RW_GROUP_HEADS + 1) * N]
    y = ypre_ref[...]

    mu = _seg_sum(y, ones_bd) * (1.0 / N)
    d = y - mu
    var = _seg_sum(d * d, ones_bd) * (1.0 / N)
    yn = d * lax.rsqrt(var + GN_EPS) * lg_ref[...] + lb_ref[...]
    bonus = _seg_sum(r * kmod * rk_ref[...], ones_bd) * v
    y_ref[...] = (yn + bonus) * out_gate


def _rwkv_constants(seg_len, rows):
    n, gh, gl = RWKV_HEAD, RW_GROUP_HEADS, RW_GROUP_LANES
    lane_head = np.arange(gl) // n
    hm = (lane_head[None, None, :] == np.arange(gh)[:, None, None]) * np.ones((gh, n, gl))
    i = np.arange(RW_CHUNK)[:, None]
    j = (np.arange(gl) % n)[None, :]
    same = (i // seg_len) == (j // seg_len)
    cm = np.stack([(j < i) & same, (j <= i) & same, j == i]).astype(np.float32)
    ri = np.arange(rows)[:, None]
    rj = np.arange(rows)[None, :]
    same_r = (ri // seg_len) == (rj // seg_len)
    tri = ((rj <= ri) & same_r).astype(np.float32)
    ones_bd = np.kron(np.eye(N_RWKV_HEADS), np.ones((n, n)))
    return (jnp.asarray(ones_bd, BF16), jnp.asarray(hm, BF16), jnp.asarray(cm, F32),
            jnp.asarray(tri, BF16), jnp.asarray(same_r.astype(np.float32), BF16))


def _rwkv(z, s0, p, layer, seq_len):
    n = z.shape[0]
    if seq_len >= RW_CHUNK:
        step_chunks = min(RW_STEP_CHUNKS_LONG, seq_len // RW_CHUNK)
    else:
        step_chunks = min(RW_STEP_CHUNKS_SHORT, n // RW_CHUNK)
    nseq = n // seq_len
    nh, hd = N_RWKV_HEADS, RWKV_HEAD
    seg_len = min(seq_len, RW_CHUNK)
    rows = RW_CHUNK * step_chunks
    chained = seq_len >= RW_CHUNK
    if chained:
        steps = seq_len // rows
        grid = (nseq, steps)
        row_map = lambda s, b: (s * steps + b, 0)
        state_spec = pl.BlockSpec((1, nh, hd, hd), lambda s, b: (s, 0, 0, 0))
        scratch = [pltpu.VMEM((rows, D_RWKV), F32), pltpu.VMEM((RW_GROUPS, hd, RW_GROUP_LANES), F32)]
    else:
        grid = (n // rows, 1)
        row_map = lambda s, b: (s, 0)
        state_spec = pl.BlockSpec((rows // seq_len, nh, hd, hd), lambda s, b: (s, 0, 0, 0))
        scratch = [pltpu.VMEM((rows, D_RWKV), F32)]
    if s0 is None:
        s0 = p['state_rwkv']
        s0_spec = pl.BlockSpec((None,) + state_spec.block_shape,
                               lambda s, b: (layer, s, 0, 0, 0))
    else:
        s0_spec = state_spec
    stacked = [p[k] for k in ('w0', 'w2', 'a0', 'a2', 'g2', 'k_k', 'k_a', 'r_k', 'lnx_g', 'lnx_b')]
    consts = list(_rwkv_constants(seg_len, rows))
    y, s_new = pl.pallas_call(
        functools.partial(_rwkv_kernel, seg_len=seg_len, step_chunks=step_chunks),
        grid=grid,
        in_specs=[pl.BlockSpec((rows, N_SHIFT), row_map), s0_spec]
        + [_layer_spec(t, layer) for t in stacked] + [_const_spec(t.shape) for t in consts],
        out_specs=[pl.BlockSpec((rows, D_RWKV), row_map), state_spec],
        out_shape=[jax.ShapeDtypeStruct((n, D_RWKV), F32),
                   jax.ShapeDtypeStruct((nseq, nh, hd, hd), F32)],
        scratch_shapes=scratch,
        compiler_params=pltpu.CompilerParams(dimension_semantics=("arbitrary", "arbitrary"),
                                             vmem_limit_bytes=V7X_VMEM_LIMIT_BYTES),
        name="rwkv7",
    )(z, s0, *stacked, *consts)
    return y, s_new


def _ffn_kernel(h_ref, y8_ref, u8_ref, yb_ref, pe_ref, c1_ref, c2_ref, d8_ref, wglu_ref, bglu_ref,
                wout_ref, g2_ref, wup_ref, cw_ref, cb_ref, wdn_ref, wple_ref, wpg_ref, gf_ref,
                *rest, seq_len, tm, final):
    i = pl.program_id(0)
    long_seq = seq_len >= tm
    if long_seq:
        o_ref, ga_ref, gb_ref, fold_ref, act_ref, carry_ref = rest

        @pl.when(i == 0)
        def _():
            carry_ref[...] = jnp.zeros_like(carry_ref)
    else:
        o_ref, ga_ref, gb_ref, fold_ref, act_ref = rest
    ya = y8_ref[...] + d8_ref[...] * u8_ref[...]
    c_gelu = math.sqrt(2.0 / math.pi)
    ya = ya * (0.5 * (1.0 + jnp.tanh(c_gelu * (ya + 0.044715 * (ya * ya * ya)))))
    ya = _unfold_rows(ya, fold_ref)
    ya = ya * _sigmoid(_mm(ya, wglu_ref[...]) + bglu_ref[...])
    h1 = (h_ref[...] + _mm(ya, wout_ref[:D_SSM, :]) + _mm(yb_ref[...], wout_ref[D_SSM:, :]))
    x2 = _rmsnorm(h1, g2_ref[...]).astype(BF16)
    row = lax.broadcasted_iota(jnp.int32, (tm, 1), 0)
    if long_seq:
        first = (i % (seq_len // tm)) == 0
    else:
        t = row % seq_len
    def up(c):
        cs = slice(c * FF_CHUNK, (c + 1) * FF_CHUNK)
        gs = slice(D_FF + c * FF_CHUNK, D_FF + (c + 1) * FF_CHUNK)
        return (jnp.dot(x2, wup_ref[:, cs], preferred_element_type=F32),
                jnp.dot(x2, wup_ref[:, gs], preferred_element_type=F32))

    n_chunks = D_FF // FF_CHUNK
    nxt = up(0)
    for c in range(n_chunks):
        cs = slice(c * FF_CHUNK, (c + 1) * FF_CHUNK)
        val, gate = nxt
        if c + 1 < n_chunks:
            nxt = up(c + 1)
        r1 = pltpu.roll(gate, 1, 0)
        r2 = pltpu.roll(gate, 2, 0)
        if long_seq:
            m1 = jnp.where(first, c1_ref[0][:, cs], carry_ref[1:2, cs])
            m2 = jnp.where(first, c2_ref[0][:, cs], carry_ref[0:1, cs])
            p1 = jnp.where(row == 0, m1, r1)
            p2 = jnp.where(row == 0, m2, jnp.where(row == 1, m1, r2))
            carry_ref[:, cs] = gate[tm - 2:tm, :]
            ga_ref[0, :, cs] = gate[tm - 2:tm - 1, :]
            gb_ref[0, :, cs] = gate[tm - 1:tm, :]
        else:
            m1 = _expand_rows(c1_ref[:, cs], seq_len)
            m2 = _expand_rows(c2_ref[:, cs], seq_len)
            p1 = jnp.where(t == 0, m1, r1)
            p2 = jnp.where(t == 0, m2, jnp.where(t == 1, m1, r2))
            ga_ref[:, cs] = _step_rows(gate, seq_len, seq_len - 2)
            gb_ref[:, cs] = _step_rows(gate, seq_len, seq_len - 1)
        conv = (cb_ref[:, cs] + cw_ref[2:3, cs] * gate + cw_ref[1:2, cs] * p1 + cw_ref[0:1, cs] * p2)
        act_ref[:, cs] = (conv * _sigmoid(conv) * val).astype(BF16)
    h2 = h1 + jnp.dot(act_ref[...], wdn_ref[...], preferred_element_type=F32)
    h3 = h2 + _mm(pe_ref[...], wple_ref[...]) * _sigmoid(_mm(h2, wpg_ref[...]))
    if final:
        h3 = _rmsnorm(h3, gf_ref[...])
    o_ref[...] = h3


def _ffn(h, y8, u8, yb, pe, conv0, p, layer, seq_len, tm, final):
    n = h.shape[0]
    nseq = n // seq_len
    row_spec = lambda w: pl.BlockSpec((tm, w), lambda i: (i, 0))
    pe_spec = pl.BlockSpec((None, tm, D_PLE), lambda i: (layer, i, 0))
    fold_spec = pl.BlockSpec((tm // S5_STEPS, S5_ROW), lambda i: (i, 0))
    scratch = [pltpu.VMEM((FOLD_LANE_BLOCKS, tm, LANES), F32), pltpu.VMEM((tm, D_FF), BF16)]
    if seq_len >= tm:
        tps = seq_len // tm
        seq_spec = pl.BlockSpec((1, 1, D_FF), lambda i: (i // tps, 0, 0))
        seq_shape = (nseq, 1, D_FF)
        scratch.append(pltpu.VMEM((2, D_FF), F32))
    else:
        seq_spec = pl.BlockSpec((tm // seq_len, D_FF), lambda i: (i, 0))
        seq_shape = (nseq, D_FF)
    stacked = [p[k] for k in ('d8', 'w_glu', 'b_glu', 'w_out', 'norm2_g', 'w_up', 'conv_w',
                              'conv_b', 'w_down', 'w_ple', 'w_pg')]
    consts = [p['final_g']]
    weights = stacked + consts
    out, ga, gb = pl.pallas_call(
        functools.partial(_ffn_kernel, seq_len=seq_len, tm=tm, final=final),
        grid=(n // tm,),
        in_specs=[row_spec(D_MODEL), fold_spec, fold_spec, row_spec(D_RWKV), pe_spec,
                  seq_spec, seq_spec] + [_layer_spec(t, layer) for t in stacked]
        + [_const_spec(t.shape) for t in consts],
        out_specs=[row_spec(D_MODEL), seq_spec, seq_spec],
        out_shape=[jax.ShapeDtypeStruct((n, D_MODEL), F32)]
        + [jax.ShapeDtypeStruct(seq_shape, F32)] * 2,
        scratch_shapes=scratch,
        compiler_params=pltpu.CompilerParams(dimension_semantics=("arbitrary",),
                                             vmem_limit_bytes=V7X_VMEM_LIMIT_BYTES),
        name="ffn",
    )(h, y8, u8, yb, pe, conv0[:, 1].reshape(seq_shape), conv0[:, 0].reshape(seq_shape), *weights)
    conv_new = jnp.stack([ga.reshape(nseq, D_FF), gb.reshape(nseq, D_FF)], axis=1)
    return out, conv_new


def _layer(h, pe, st, p, layer, s5_tables, seq_len, tm, final):
    ssm_re0, ssm_im0, rwkv0, shift0, conv0 = st
    u8, z, shift_new = _inproj(h, p, layer, shift0, seq_len, tm)
    y8, hre, him = _s5(u8, ssm_re0, ssm_im0, s5_tables, seq_len)
    yb, s_last = _rwkv(z, rwkv0, p, layer, seq_len)
    h, conv_new = _ffn(h, y8, u8, yb, pe, conv0, p, layer, seq_len, tm, final)
    return h, (hre, him, s_last, shift_new, conv_new)


def _stacked_params(w, state_rwkv, state_shift):
    depth = w['w_in'].shape[0]
    row = lambda t: t.reshape(depth, 1, -1).astype(F32)
    bf = lambda t: t.astype(BF16)
    return {
        'norm1_g': row(w['norm1_g']), 'w_in': bf(w['w_in']), 'shift_mu': row(w['shift_mu']),
        'd8': jnp.tile(row(w['ssm_d']), (1, 1, S5_STEPS)), 'w_glu': bf(w['ssm_w_glu']),
        'b_glu': row(w['ssm_b_glu']),
        'w0': row(w['rwkv_w0']), 'w2': bf(w['rwkv_w2']), 'a0': row(w['rwkv_a0']),
        'a2': bf(w['rwkv_a2']), 'g2': bf(w['rwkv_g2']), 'k_k': row(w['rwkv_k_k']),
        'k_a': row(w['rwkv_k_a']), 'r_k': row(w['rwkv_r_k']), 'lnx_g': row(w['rwkv_lnx_g']),
        'lnx_b': row(w['rwkv_lnx_b']), 'w_out': bf(w['w_out']), 'norm2_g': row(w['norm2_g']),
        'w_up': bf(w['w_up']), 'conv_w': w['conv_w'].astype(F32), 'conv_b': row(w['conv_b']),
        'w_down': bf(w['w_down']), 'w_ple': bf(w['w_ple']), 'w_pg': bf(w['w_pg']),
        'final_g': w['final_g'].reshape(1, -1).astype(F32),
        'state_rwkv': state_rwkv.astype(F32), 'state_shift': state_shift.astype(F32),
    }


def _forward(x_prompt, x_sample, p_prompt, p_sample, state_ssm_re, state_ssm_im, state_rwkv,
             state_shift, state_conv, w, tm_prompt, tm_sample):
    depth = w['w_in'].shape[0]
    bp, lp, _ = x_prompt.shape
    bs, ls, _ = x_sample.shape
    hp = x_prompt.reshape(bp * lp, D_MODEL).astype(F32)
    hs = x_sample.reshape(bs * ls, D_MODEL).astype(F32)
    pe_p = p_prompt.reshape(depth, bp * lp, D_PLE).astype(F32)
    pe_s = p_sample.reshape(depth, bs * ls, D_PLE).astype(F32)
    zero_st = (jnp.zeros((bp, N_SSM_GROUPS, SSM_STATE), F32),
               jnp.zeros((bp, N_SSM_GROUPS, SSM_STATE), F32),
               jnp.zeros((bp, N_RWKV_HEADS, RWKV_HEAD, RWKV_HEAD), F32),
               jnp.zeros((bp, N_SHIFT), F32),
               jnp.zeros((bp, 2, D_FF), F32))
    p = _stacked_params(w, state_rwkv, state_shift)
    new_p = [[] for _ in range(5)]
    new_s = [[] for _ in range(5)]
    for i in range(depth):
        tables = _s5_tables(w['ssm_lam_re'][i], w['ssm_lam_im'][i], w['ssm_log_dt'][i],
                            w['ssm_b_re'][i], w['ssm_b_im'][i], w['ssm_c_re'][i], w['ssm_c_im'][i])
        final = i == depth - 1
        hp, stp = _layer(hp, pe_p, zero_st, p, i, tables, lp, tm_prompt, final)
        st_in = (state_ssm_re[i].astype(F32), state_ssm_im[i].astype(F32), None, None,
                 state_conv[i].astype(F32))
        hs, sts = _layer(hs, pe_s, st_in, p, i, tables, ls, tm_sample, final)
        for j in range(5):
            new_p[j].append(stp[j])
            new_s[j].append(sts[j])
    y_prompt = hp.reshape(bp, lp, D_MODEL).astype(x_prompt.dtype)
    y_sample = hs.reshape(bs, ls, D_MODEL).astype(x_sample.dtype)
    dts = (state_ssm_re.dtype, state_ssm_im.dtype, state_rwkv.dtype, state_shift.dtype,
           state_conv.dtype)
    outs_p = tuple(jnp.stack(new_p[j]).astype(dts[j]) for j in range(5))
    outs_s = tuple(jnp.stack(new_s[j]).astype(dts[j]) for j in range(5))
    return (y_prompt, y_sample) + outs_p + outs_s


def kernel(x_prompt, x_sample, p_prompt, p_sample, state_ssm_re, state_ssm_im, state_rwkv, state_shift, state_conv, norm1_g, w_in, shift_mu, ssm_lam_re, ssm_lam_im, ssm_log_dt, ssm_b_re, ssm_b_im, ssm_c_re, ssm_c_im, ssm_d, ssm_w_glu, ssm_b_glu, rwkv_w0, rwkv_w2, rwkv_a0, rwkv_a2, rwkv_g2, rwkv_k_k, rwkv_k_a, rwkv_r_k, rwkv_lnx_g, rwkv_lnx_b, w_out, norm2_g, w_up, conv_w, conv_b, w_down, w_ple, w_pg, final_g):
    w = dict(norm1_g=norm1_g, w_in=w_in, shift_mu=shift_mu, ssm_lam_re=ssm_lam_re,
             ssm_lam_im=ssm_lam_im, ssm_log_dt=ssm_log_dt, ssm_b_re=ssm_b_re, ssm_b_im=ssm_b_im,
             ssm_c_re=ssm_c_re, ssm_c_im=ssm_c_im, ssm_d=ssm_d, ssm_w_glu=ssm_w_glu,
             ssm_b_glu=ssm_b_glu, rwkv_w0=rwkv_w0, rwkv_w2=rwkv_w2, rwkv_a0=rwkv_a0,
             rwkv_a2=rwkv_a2, rwkv_g2=rwkv_g2, rwkv_k_k=rwkv_k_k, rwkv_k_a=rwkv_k_a,
             rwkv_r_k=rwkv_r_k, rwkv_lnx_g=rwkv_lnx_g, rwkv_lnx_b=rwkv_lnx_b, w_out=w_out,
             norm2_g=norm2_g, w_up=w_up, conv_w=conv_w, conv_b=conv_b, w_down=w_down,
             w_ple=w_ple, w_pg=w_pg, final_g=final_g)
    lp = x_prompt.shape[1]
    ns = x_sample.shape[0] * x_sample.shape[1]
    return _forward(x_prompt, x_sample, p_prompt, p_sample, state_ssm_re, state_ssm_im,
                    state_rwkv, state_shift, state_conv, w,
                    tm_prompt=min(512, lp), tm_sample=min(256, ns))
```

```python
import functools
import math

import numpy as np
import jax
import jax.numpy as jnp
from jax import lax
from jax.experimental import pallas as pl
from jax.experimental.pallas import tpu as pltpu

F32 = jnp.float32
BF16 = jnp.bfloat16
HIGHEST = lax.Precision.HIGHEST

D_MODEL = 1024
D_SSM = 512
D_RWKV = 512
SSM_GROUP = 16
N_SSM_GROUPS = 32
SSM_STATE = 64
RWKV_HEAD = 64
N_RWKV_HEADS = 8
DECAY_LORA = 64
AAA_LORA = 64
GATE_LORA = 128
N_SHIFT = 3 * D_RWKV + DECAY_LORA + AAA_LORA + GATE_LORA
N_IN = D_SSM + N_SHIFT
D_FF = 2816
D_PLE = 256
RMS_EPS = 1e-6
GN_EPS = 64e-5
L2_EPS = 1e-12

LANES = 128
FOLD_LANE_BLOCKS = D_SSM // LANES
S5_STEPS = 8
S5_ROW = S5_STEPS * D_SSM
S5_LANE_BLOCKS = 4
S5_BLOCK_STATE = 2 * 8 * SSM_STATE
S5_STATE_ROW = S5_LANE_BLOCKS * S5_BLOCK_STATE
FF_CHUNK = 256
V7X_VMEM_LIMIT_BYTES = 56 * 1024 * 1024


def _mm(a, b):
    return jnp.dot(a.astype(BF16), b.astype(BF16), preferred_element_type=F32)


def _mm_nt(a, b):
    return lax.dot_general(a.astype(BF16), b.astype(BF16), (((1,), (1,)), ((), ())),
                           preferred_element_type=F32)


def _mm_tn(a, b):
    return lax.dot_general(a.astype(BF16), b.astype(BF16), (((0,), (0,)), ((), ())),
                           preferred_element_type=F32)


def _seg_sum(x, ones_bd):
    hi = x.astype(BF16)
    lo = (x - hi.astype(F32)).astype(BF16)
    gl = ones_bd.shape[0]
    return jnp.concatenate(
        [jnp.dot(hi[:, g:g + gl], ones_bd, preferred_element_type=F32)
         + jnp.dot(lo[:, g:g + gl], ones_bd, preferred_element_type=F32)
         for g in range(0, x.shape[1], gl)], axis=1)


def _rmsnorm(x, g):
    return x * lax.rsqrt(jnp.mean(x * x, axis=-1, keepdims=True) + RMS_EPS) * g


def _sigmoid(x):
    return 1.0 / (1.0 + jnp.exp(-x))


def _const_spec(shape):
    nd = len(shape)
    return pl.BlockSpec(shape, lambda *_: (0,) * nd, pipeline_mode=pl.Buffered(1))


def _layer_spec(t, layer):
    nd = t.ndim - 1
    return pl.BlockSpec((None,) + t.shape[1:], lambda *_: (layer,) + (0,) * nd,
                        pipeline_mode=pl.Buffered(1))


def _expand_rows(x, seq_len):
    nseq, w = x.shape
    return jnp.broadcast_to(x[:, None, :], (nseq, seq_len, w)).reshape(nseq * seq_len, w)


def _step_rows(x, seq_len, t):
    return x.reshape(x.shape[0] // seq_len, seq_len, x.shape[1])[:, t, :]


def _fold_rows(x, scr_ref, out_ref):
    tm = x.shape[0]
    for q in range(FOLD_LANE_BLOCKS):
        scr_ref[q] = x[:, q * LANES:(q + 1) * LANES]
    for s in range(S5_STEPS):
        for q in range(FOLD_LANE_BLOCKS):
            c0 = s * D_SSM + q * LANES
            out_ref[:, c0:c0 + LANES] = scr_ref[q, pl.ds(s, tm // S5_STEPS, stride=S5_STEPS), :]


def _unfold_rows(x8, scr_ref):
    tm = x8.shape[0] * S5_STEPS
    for s in range(S5_STEPS):
        for q in range(FOLD_LANE_BLOCKS):
            c0 = s * D_SSM + q * LANES
            scr_ref[q, pl.ds(s, tm // S5_STEPS, stride=S5_STEPS), :] = x8[:, c0:c0 + LANES]
    return jnp.concatenate([scr_ref[q] for q in range(FOLD_LANE_BLOCKS)], axis=1)


def _inproj_kernel(h_ref, g_ref, w_ref, mu_ref, init_ref, *rest, seq_len, tm):
    i = pl.program_id(0)
    long_seq = seq_len >= tm
    if long_seq:
        u8_ref, z_ref, last_ref, fold_ref, carry_ref = rest

        @pl.when(i == 0)
        def _():
            carry_ref[...] = jnp.zeros_like(carry_ref)
    else:
        u8_ref, z_ref, last_ref, fold_ref = rest
    xn = _rmsnorm(h_ref[...], g_ref[...])
    proj = jnp.dot(xn.astype(BF16), w_ref[...], preferred_element_type=F32)
    _fold_rows(proj[:, :D_SSM], fold_ref, u8_ref)
    zr = proj[:, D_SSM:]
    rolled = pltpu.roll(zr, 1, 0)
    row = lax.broadcasted_iota(jnp.int32, (tm, 1), 0)
    if long_seq:
        first = (i % (seq_len // tm)) == 0
        row0 = jnp.where(first, init_ref[0], carry_ref[...])
        prev = jnp.where(row == 0, row0, rolled)
        carry_ref[...] = zr[tm - 1:tm, :]
        last_ref[0] = zr[tm - 1:tm, :]
    else:
        prev = jnp.where(row % seq_len == 0, _expand_rows(init_ref[...], seq_len), rolled)
        last_ref[...] = _step_rows(zr, seq_len, seq_len - 1)
    z_ref[...] = (zr + (prev - zr) * mu_ref[...]).astype(z_ref.dtype)


def _inproj(h, p, layer, shift0, seq_len, tm):
    n = h.shape[0]
    nseq = n // seq_len
    fold_scratch = pltpu.VMEM((FOLD_LANE_BLOCKS, tm, LANES), F32)
    if seq_len >= tm:
        tps = seq_len // tm
        seq_spec = pl.BlockSpec((1, 1, N_SHIFT), lambda i: (i // tps, 0, 0))
        init_spec = seq_spec
        init = shift0.reshape(nseq, 1, N_SHIFT)
        last_shape = jax.ShapeDtypeStruct((nseq, 1, N_SHIFT), F32)
        scratch = [fold_scratch, pltpu.VMEM((1, N_SHIFT), F32)]
    else:
        spt = tm // seq_len
        seq_spec = pl.BlockSpec((spt, N_SHIFT), lambda i: (i, 0))
        init_spec = pl.BlockSpec((None, spt, N_SHIFT), lambda i: (layer, i, 0))
        init = p['state_shift']
        last_shape = jax.ShapeDtypeStruct((nseq, N_SHIFT), F32)
        scratch = [fold_scratch]
    g1, w_in, mu = p['norm1_g'], p['w_in'], p['shift_mu']
    u8, z, last = pl.pallas_call(
        functools.partial(_inproj_kernel, seq_len=seq_len, tm=tm),
        grid=(n // tm,),
        in_specs=[pl.BlockSpec((tm, D_MODEL), lambda i: (i, 0)),
                  _layer_spec(g1, layer), _layer_spec(w_in, layer), _layer_spec(mu, layer),
                  init_spec],
        out_specs=[pl.BlockSpec((tm // S5_STEPS, S5_ROW), lambda i: (i, 0)),
                   pl.BlockSpec((tm, N_SHIFT), lambda i: (i, 0)),
                   seq_spec],
        out_shape=[jax.ShapeDtypeStruct((n // S5_STEPS, S5_ROW), F32),
                   jax.ShapeDtypeStruct((n, N_SHIFT), BF16),
                   last_shape],
        scratch_shapes=scratch,
        compiler_params=pltpu.CompilerParams(dimension_semantics=("arbitrary",),
                                             vmem_limit_bytes=V7X_VMEM_LIMIT_BYTES),
        name="inproj",
    )(h, g1, w_in, mu, init)
    return u8, z, last.reshape(nseq, N_SHIFT)


def _s5_prep_kernel(lr_ref, li_ref, ldt_ref, br_ref, bi_ref, cr_ref, ci_ref, tile_ref, sel_ref,
                    wx_ref, wy_ref, apr_ref, api_ref):
    lr, li = lr_ref[...], li_ref[...]
    dt = jnp.exp(ldt_ref[...])
    mag = jnp.exp(lr * dt)
    ar = mag * jnp.cos(li * dt)
    ai = mag * jnp.sin(li * dt)
    den = lr * lr + li * li
    nr = ar - 1.0
    fr = (nr * lr + ai * li) / den
    fi = (ai * lr - nr * li) / den
    br, bi = br_ref[...], bi_ref[...]
    bbr = fr * br - fi * bi
    bbi = fr * bi + fi * br
    cr, ci = cr_ref[...], ci_ref[...]
    pr = [jnp.ones_like(ar)]
    pi = [jnp.zeros_like(ar)]
    for _ in range(S5_STEPS):
        pr.append(pr[-1] * ar - pi[-1] * ai)
        pi.append(pr[-2] * ai + pi[-1] * ar)
    nrow = lr.shape[0]
    blk = nrow // S5_LANE_BLOCKS
    half = S5_BLOCK_STATE // 2
    ri = lax.broadcasted_iota(jnp.int32, (nrow, nrow), 0)
    ci_ = lax.broadcasted_iota(jnp.int32, (nrow, nrow), 1)
    same_group = (ri // SSM_GROUP) == (ci_ // SSM_GROUP)
    keep_in = ((ri % blk) // SSM_GROUP) == (ci_ // SSM_STATE)
    keep_out = (ri // SSM_STATE) == ((ci_ % blk) // SSM_GROUP)
    tile, sel = tile_ref[...], sel_ref[...]
    nt = (((1,), (1,)), ((), ()))
    wy_ref[...] = jnp.zeros(wy_ref.shape, wy_ref.dtype)
    for s in range(S5_STEPS):
        qr, qi = pr[S5_STEPS - 1 - s], pi[S5_STEPS - 1 - s]
        wr = qr * bbr - qi * bbi
        wi = qr * bbi + qi * bbr
        for h, w_ in enumerate((wr, wi)):
            t = jnp.dot(w_.astype(BF16), tile, preferred_element_type=F32)
            t = jnp.where(keep_in, t, 0.0).astype(BF16)
            for q in range(S5_LANE_BLOCKS):
                wx_ref[q, s * blk:(s + 1) * blk, h * half:(h + 1) * half] = t[q * blk:(q + 1) * blk, :]
        tau = S5_STEPS - 1 - s
        kt = (lax.dot_general(wr, cr, nt, precision=HIGHEST, preferred_element_type=F32)
              - lax.dot_general(wi, ci, nt, precision=HIGHEST, preferred_element_type=F32))
        kt = jnp.where(same_group, kt, 0.0).astype(BF16)
        for q in range(S5_LANE_BLOCKS):
            kq = kt[q * blk:(q + 1) * blk, q * blk:(q + 1) * blk]
            for s_in in range(S5_STEPS - tau):
                s_out = s_in + tau
                r0 = S5_BLOCK_STATE + s_in * blk
                wy_ref[q, s_out // 2, r0:r0 + blk, (s_out % 2) * blk:(s_out % 2 + 1) * blk] = kq
        mr = cr * pr[s + 1] - ci * pi[s + 1]
        mi = cr * pi[s + 1] + ci * pr[s + 1]
        for h, m_ in enumerate((mr, -mi)):
            t = lax.dot_general(sel, m_.astype(BF16), nt, preferred_element_type=F32)
            t = jnp.where(keep_out, t, 0.0).astype(BF16)
            for q in range(S5_LANE_BLOCKS):
                wy_ref[q, s // 2, h * half:(h + 1) * half, (s % 2) * blk:(s % 2 + 1) * blk] = (
                    t[:, q * blk:(q + 1) * blk])
    a8r, a8i = pr[S5_STEPS], pi[S5_STEPS]
    er, ei = a8r, a8i
    for n in range(S5_STEPS):
        apr_ref[n] = er
        api_ref[n] = ei
        er, ei = er * a8r - ei * a8i, er * a8i + ei * a8r


def _s5_tables(lam_re, lam_im, log_dt, b_re, b_im, c_re, c_im):
    G, P, K = N_SSM_GROUPS, SSM_STATE, SSM_GROUP
    Q, GB = S5_LANE_BLOCKS, G // S5_LANE_BLOCKS
    rep = lambda t: jnp.repeat(t, K, axis=0)
    tile = np.tile(np.eye(P, dtype=np.float32), (1, GB))
    args = (rep(lam_re), rep(lam_im), rep(jnp.broadcast_to(log_dt[:, None], (G, P))),
            jnp.swapaxes(b_re, 1, 2).reshape(G * K, P), jnp.swapaxes(b_im, 1, 2).reshape(G * K, P),
            c_re.reshape(G * K, P), c_im.reshape(G * K, P),
            jnp.asarray(tile, BF16), jnp.asarray(tile.T, BF16))
    t3 = jax.ShapeDtypeStruct((S5_STEPS, G * K, P), F32)
    wx, wy, apr, api = pl.pallas_call(
        _s5_prep_kernel,
        out_shape=[jax.ShapeDtypeStruct((Q, S5_STEPS * GB * K, S5_BLOCK_STATE), BF16),
                   jax.ShapeDtypeStruct((Q, S5_STEPS // 2, S5_BLOCK_STATE + S5_STEPS * GB * K,
                                         2 * GB * K), BF16),
                   t3, t3],
        compiler_params=pltpu.CompilerParams(vmem_limit_bytes=V7X_VMEM_LIMIT_BYTES),
        name="s5_prep",
    )(*args)
    ap = jnp.concatenate([t[:, ::K, :].reshape(S5_STEPS, Q, GB * P) for t in (apr, api)], axis=-1)
    return wx, wy, ap.reshape(S5_STEPS, S5_STATE_ROW)


def _state_to_lanes(h_re, h_im):
    n = h_re.shape[0]
    parts = [t.reshape(n, S5_LANE_BLOCKS, -1) for t in (h_re, h_im)]
    return jnp.concatenate(parts, axis=-1).reshape(n, S5_STATE_ROW)


def _lanes_to_state(h):
    n = h.shape[0]
    h = h.reshape(n, S5_LANE_BLOCKS, 2, N_SSM_GROUPS // S5_LANE_BLOCKS, SSM_STATE)
    return (h[:, :, 0].reshape(n, N_SSM_GROUPS, SSM_STATE), h[:, :, 1].reshape(n, N_SSM_GROUPS, SSM_STATE))


def _s5_kernel(u_ref, h0_ref, wx_ref, wy_ref, ap_ref, y_ref, hl_ref, *scratch, rows, scan):
    half = S5_BLOCK_STATE // 2
    cw = D_SSM // S5_LANE_BLOCKS
    blocks = range(S5_LANE_BLOCKS)
    ub = u_ref[...].astype(BF16)
    ucat = [jnp.concatenate([ub[:, s * D_SSM + q * cw: s * D_SSM + (q + 1) * cw]
                             for s in range(S5_STEPS)], axis=1) for q in blocks]
    x = [jnp.dot(ucat[q], wx_ref[q], preferred_element_type=F32) for q in blocks]
    ap = [ap_ref[:, q * S5_BLOCK_STATE:(q + 1) * S5_BLOCK_STATE] for q in blocks]
    if scan:
        hs_ref, hp_ref = scratch
        sub3 = lax.broadcasted_iota(jnp.int32, (1, 8, 1), 1)
        sub2 = lax.broadcasted_iota(jnp.int32, (8, 1), 0)
        for q in blocks:
            xr = x[q][:, :half].reshape(rows // 8, 8, half)
            xi = x[q][:, half:].reshape(rows // 8, 8, half)
            for k in (1, 2, 4):
                er = ap[q][k - 1:k, :half].reshape(1, 1, half)
                ei = ap[q][k - 1:k, half:].reshape(1, 1, half)
                sr = jnp.where(sub3 >= k, pltpu.roll(xr, k, 1), 0.0)
                si = jnp.where(sub3 >= k, pltpu.roll(xi, k, 1), 0.0)
                xr, xi = xr + er * sr - ei * si, xi + er * si + ei * sr
            hs_ref[q, :, :half] = xr.reshape(rows, half)
            hs_ref[q, :, half:] = xi.reshape(rows, half)
        h0 = h0_ref[0]
        init = tuple(h0[:, q * S5_BLOCK_STATE + o:q * S5_BLOCK_STATE + o + half]
                     for q in blocks for o in (0, half))

        def body(j, carry):
            sl = pl.ds(pl.multiple_of(j * 8, 8), 8)
            out = []
            for q in blocks:
                cr, ci = carry[2 * q], carry[2 * q + 1]
                pwr, pwi = ap[q][:, :half], ap[q][:, half:]
                hr = hs_ref[q, sl, :half] + pwr * cr - pwi * ci
                hi = hs_ref[q, sl, half:] + pwr * ci + pwi * cr
                hp_ref[q, sl, :half] = jnp.where(sub2 == 0, cr, pltpu.roll(hr, 1, 0))
                hp_ref[q, sl, half:] = jnp.where(sub2 == 0, ci, pltpu.roll(hi, 1, 0))
                out += [hr[7:8, :], hi[7:8, :]]
            return tuple(out)

        last = lax.fori_loop(0, rows // 8, body, init)
        for q in blocks:
            lo = q * S5_BLOCK_STATE
            hl_ref[0, :, lo:lo + half] = last[2 * q]
            hl_ref[0, :, lo + half:lo + S5_BLOCK_STATE] = last[2 * q + 1]
        hprev = [hp_ref[q].astype(BF16) for q in blocks]
    else:
        hprev = []
        for q in blocks:
            lo = q * S5_BLOCK_STATE
            h0r = h0_ref[:, lo:lo + half]
            h0i = h0_ref[:, lo + half:lo + S5_BLOCK_STATE]
            er, ei = ap[q][0:1, :half], ap[q][0:1, half:]
            hl_ref[:, lo:lo + half] = er * h0r - ei * h0i + x[q][:, :half]
            hl_ref[:, lo + half:lo + S5_BLOCK_STATE] = er * h0i + ei * h0r + x[q][:, half:]
            hprev.append(jnp.concatenate([h0r, h0i], axis=1).astype(BF16))
    for q in blocks:
        lhs = jnp.concatenate([hprev[q], ucat[q]], axis=1)
        for j in range(S5_STEPS // 2):
            kk = S5_BLOCK_STATE + cw * (2 * j + 2)
            y2 = jnp.dot(lhs[:, :kk], wy_ref[q, j, :kk, :], preferred_element_type=F32)
            c0 = (2 * j) * D_SSM + q * cw
            c1 = (2 * j + 1) * D_SSM + q * cw
            y_ref[:, c0:c0 + cw] = y2[:, :cw]
            y_ref[:, c1:c1 + cw] = y2[:, cw:]


def _s5(u8, h_re, h_im, tables, seq_len):
    wx, wy, ap = tables
    n = u8.shape[0] * S5_STEPS
    nseq = n // seq_len
    h0 = _state_to_lanes(h_re, h_im)
    cps = seq_len // S5_STEPS
    scan = cps > 1
    if scan:
        rows = cps
        grid = (nseq,)
        h0 = h0.reshape(nseq, 1, S5_STATE_ROW)
        h_spec = pl.BlockSpec((1, 1, S5_STATE_ROW), lambda i: (i, 0, 0))
        h_shape = jax.ShapeDtypeStruct((nseq, 1, S5_STATE_ROW), F32)
    else:
        rows = min(nseq, 128)
        grid = (nseq // rows,)
        h_spec = pl.BlockSpec((rows, S5_STATE_ROW), lambda i: (i, 0))
        h_shape = jax.ShapeDtypeStruct((nseq, S5_STATE_ROW), F32)
    y8, hl = pl.pallas_call(
        functools.partial(_s5_kernel, rows=rows, scan=scan),
        grid=grid,
        in_specs=[pl.BlockSpec((rows, S5_ROW), lambda i: (i, 0)),
                  h_spec,
                  _const_spec(wx.shape), _const_spec(wy.shape), _const_spec(ap.shape)],
        out_specs=[pl.BlockSpec((rows, S5_ROW), lambda i: (i, 0)), h_spec],
        out_shape=[jax.ShapeDtypeStruct((n // S5_STEPS, S5_ROW), F32), h_shape],
        scratch_shapes=[pltpu.VMEM((S5_LANE_BLOCKS, rows, S5_BLOCK_STATE), F32)] * 2 if scan else [],
        compiler_params=pltpu.CompilerParams(dimension_semantics=("arbitrary",),
                                             vmem_limit_bytes=V7X_VMEM_LIMIT_BYTES),
        name="s5",
    )(u8, h0, wx, wy, ap)
    hre_new, him_new = _lanes_to_state(hl.reshape(nseq, S5_STATE_ROW))
    return y8, hre_new, him_new


RW_CHUNK = 64
RW_STEP_CHUNKS_LONG = 4
RW_STEP_CHUNKS_SHORT = 4
RW_GROUPS = 2
RW_GROUP_HEADS = N_RWKV_HEADS // RW_GROUPS
RW_GROUP_LANES = D_RWKV // RW_GROUPS


def _split3(x):
    hi = x.astype(BF16)
    r1 = x - hi.astype(F32)
    mid = r1.astype(BF16)
    lo = (r1 - mid.astype(F32)).astype(BF16)
    return hi, mid, lo


def _bd(x, hm_ref):
    xb = x.astype(BF16)
    return jnp.concatenate([xb * hm_ref[h] for h in range(RW_GROUP_HEADS)], axis=0)


def _unbd(f, hm_ref):
    n = RWKV_HEAD
    out = f[:n] * hm_ref[0].astype(F32)
    for h in range(1, RW_GROUP_HEADS):
        out = out + f[h * n:(h + 1) * n] * hm_ref[h].astype(F32)
    return out


def _rwkv_kernel(z_ref, s0_ref, w0_ref, w2_ref, a0_ref, a2_ref, g2_ref, kk_ref, ka_ref, rk_ref,
                 lg_ref, lb_ref, ones_ref, hm_ref, cm_ref, tri_ref, y_ref, sl_ref, ypre_ref,
                 *scratch, seg_len, step_chunks):
    C = RW_CHUNK
    N = RWKV_HEAD
    GL = RW_GROUP_LANES
    chained = seg_len == C
    if chained:
        (st_ref,) = scratch
        b = pl.program_id(1)

        @pl.when(b == 0)
        def _():
            for g in range(RW_GROUPS):
                st_ref[g] = jnp.concatenate(
                    [s0_ref[0, h] for h in range(g * RW_GROUP_HEADS, (g + 1) * RW_GROUP_HEADS)], axis=1)

    ones_bd = ones_ref[...]
    z = z_ref[...].astype(F32)
    r = z[:, :D_RWKV]
    k = z[:, D_RWKV:2 * D_RWKV]
    v = z[:, 2 * D_RWKV:3 * D_RWKV]
    o = 3 * D_RWKV
    xw = z[:, o:o + DECAY_LORA]
    xa = z[:, o + DECAY_LORA:o + DECAY_LORA + AAA_LORA]
    xg = z[:, o + DECAY_LORA + AAA_LORA:]
    wd = -(w0_ref[...] + _mm(jnp.tanh(xw), w2_ref[...]))
    w = -(jnp.maximum(wd, 0.0) + jnp.log1p(jnp.exp(-jnp.abs(wd)))) - 0.5
    lw = -jnp.exp(w)
    a = _sigmoid(a0_ref[...] + _mm(xa, a2_ref[...]))
    out_gate = _mm(_sigmoid(xg), g2_ref[...])
    kk = k * kk_ref[...]
    kk = kk / jnp.maximum(jnp.sqrt(_seg_sum(kk * kk, ones_bd)), L2_EPS)
    kmod = k * (1.0 + (a - 1.0) * ka_ref[...])

    tri = tri_ref[...]
    cw = sum(jnp.dot(tri, t, preferred_element_type=F32) for t in _split3(lw))
    tot = _expand_rows(_step_rows(cw, seg_len, seg_len - 1), seg_len)
    w_in = jnp.exp(cw)
    w_tail = jnp.exp(tot - cw)
    w_inv = jnp.exp(-cw)
    w_all = jnp.exp(tot)
    ah = -kk * jnp.exp(cw - lw)
    bh = kk * a * w_inv
    kh = kmod * w_inv
    rh = r * w_in
    bt = kk * a * w_tail
    kt = kmod * w_tail
    strict = cm_ref[0] > 0.0
    incl = cm_ref[1] > 0.0
    eye_cat = cm_ref[2]
    n_fac = seg_len.bit_length() - 1

    items = [(c, g) for c in range(step_chunks) for g in range(RW_GROUPS)]
    sub = lambda t, c, g: t[c * C:(c + 1) * C, g * GL:(g + 1) * GL]
    bd = lambda t: _bd(t, hm_ref)
    A = [sub(ah, c, g) for c, g in items]
    R = [sub(rh, c, g) for c, g in items]
    V = [sub(v, c, g) for c, g in items]
    G = [_mm_nt(jnp.concatenate([A[i], R[i]], axis=0),
                jnp.concatenate([bd(sub(bh, c, g)), bd(sub(kh, c, g))], axis=0))
         for i, (c, g) in enumerate(items)]
    AB = [jnp.where(strict, t[:C, :GL], 0.0) for t in G]
    AK = [jnp.where(strict, t[:C, GL:], 0.0) for t in G]
    RB = [jnp.where(incl, t[C:, :GL], 0.0) for t in G]
    RK = [jnp.where(incl, t[C:, GL:], 0.0) for t in G]
    KV = [_mm(jnp.concatenate([AK[i], RK[i]], axis=0), bd(V[i])) for i in range(len(items))]
    T = [eye_cat + t for t in AB]
    P = [_mm(t, bd(t)) for t in AB]
    for _ in range(n_fac - 2):
        PT = [_mm(jnp.concatenate([P[i], T[i]], axis=0), bd(P[i])) for i in range(len(items))]
        P = [t[:C] for t in PT]
        T = [T[i] + PT[i][C:] for i in range(len(items))]
    T = [T[i] + _mm(T[i], bd(P[i])) for i in range(len(items))]
    RBT = [_mm(RB[i], bd(T[i])) for i in range(len(items))]
    X = [_mm(jnp.concatenate([T[i], RBT[i]], axis=0),
             jnp.concatenate([bd(A[i]), bd(KV[i][:C])], axis=1)) for i in range(len(items))]
    TA = [t[:C, :GL] for t in X]
    U0 = [t[:C, GL:] for t in X]
    Rt = [R[i] + X[i][C:, :GL] for i in range(len(items))]
    Y0 = [X[i][C:, GL:] + KV[i][C:] for i in range(len(items))]
    nseg = C // seg_len
    PhiT, PsiT = {}, {}
    for i, (c, g) in enumerate(items):
        Bt, Kt, Wa = sub(bt, c, g), sub(kt, c, g), sub(w_all, c, g)
        for s in range(nseg):
            ss = slice(s * seg_len, (s + 1) * seg_len)
            PhiT[i, s] = bd(eye_cat * Wa[s * seg_len:s * seg_len + 1, :]
                            + _unbd(_mm_tn(TA[i][ss], Bt[ss]), hm_ref))
            PsiT[i, s] = _unbd(_mm_tn(jnp.concatenate([U0[i][ss], V[i][ss]], axis=0),
                                      jnp.concatenate([Bt[ss], Kt[ss]], axis=0)), hm_ref)
    heads = lambda g: range(g * RW_GROUP_HEADS, (g + 1) * RW_GROUP_HEADS)
    load_state = lambda ref, j, g: jnp.concatenate([ref[j, h] for h in heads(g)], axis=1)
    if chained:
        states = [st_ref[g] for g in range(RW_GROUPS)]
    for i, (c, g) in enumerate(items):
        for s in range(nseg):
            ss = slice(s * seg_len, (s + 1) * seg_len)
            S = states[g] if chained else load_state(s0_ref, c * nseg + s, g)
            ypre_ref[c * C + s * seg_len:c * C + (s + 1) * seg_len, g * GL:(g + 1) * GL] = (
                _mm_nt(Rt[i][ss], bd(S)) + Y0[i][ss])
            S = _mm(S, PhiT[i, s]) + PsiT[i, s]
            if chained:
                states[g] = S
            else:
                for h in heads(g):
                    sl_ref[c * nseg + s, h] = S[:, (h % RW_GROUP_HEADS) * N:(h % RW_GROUP_HEADS + 1) * N]
    if chained:
        for g in range(RW_GROUPS):
            st_ref[g] = states[g]
            for h in heads(g):
                sl_ref[0, h] = states[g][:, (h % RW_GROUP_HEADS) * N:(h % RW_GROUP_HEADS + 1) * N]
    y = ypre_ref[...]

    mu = _seg_sum(y, ones_bd) * (1.0 / N)
    d = y - mu
    var = _seg_sum(d * d, ones_bd) * (1.0 / N)
    yn = d * lax.rsqrt(var + GN_EPS) * lg_ref[...] + lb_ref[...]
    bonus = _seg_sum(r * kmod * rk_ref[...], ones_bd) * v
    y_ref[...] = (yn + bonus) * out_gate


def _rwkv_constants(seg_len, rows):
    n, gh, gl = RWKV_HEAD, RW_GROUP_HEADS, RW_GROUP_LANES
    lane_head = np.arange(gl) // n
    hm = (lane_head[None, None, :] == np.arange(gh)[:, None, None]) * np.ones((gh, n, gl))
    i = np.arange(RW_CHUNK)[:, None]
    j = (np.arange(gl) % n)[None, :]
    same = (i // seg_len) == (j // seg_len)
    cm = np.stack([(j < i) & same, (j <= i) & same, j == i]).astype(np.float32)
    ri = np.arange(rows)[:, None]
    rj = np.arange(rows)[None, :]
    same_r = (ri // seg_len) == (rj // seg_len)
    tri = ((rj <= ri) & same_r).astype(np.float32)
    ones_bd = np.kron(np.eye(gh), np.ones((n, n)))
    return (jnp.asarray(ones_bd, BF16), jnp.asarray(hm, BF16), jnp.asarray(cm, F32),
            jnp.asarray(tri, BF16))


def _rwkv(z, s0, p, layer, seq_len):
    n = z.shape[0]
    if seq_len >= RW_CHUNK:
        step_chunks = min(RW_STEP_CHUNKS_LONG, seq_len // RW_CHUNK)
    else:
        step_chunks = min(RW_STEP_CHUNKS_SHORT, n // RW_CHUNK)
    nseq = n // seq_len
    nh, hd = N_RWKV_HEADS, RWKV_HEAD
    seg_len = min(seq_len, RW_CHUNK)
    rows = RW_CHUNK * step_chunks
    chained = seq_len >= RW_CHUNK
    if chained:
        steps = seq_len // rows
        grid = (nseq, steps)
        row_map = lambda s, b: (s * steps + b, 0)
        state_spec = pl.BlockSpec((1, nh, hd, hd), lambda s, b: (s, 0, 0, 0))
        scratch = [pltpu.VMEM((rows, D_RWKV), F32), pltpu.VMEM((RW_GROUPS, hd, RW_GROUP_LANES), F32)]
    else:
        grid = (n // rows, 1)
        row_map = lambda s, b: (s, 0)
        state_spec = pl.BlockSpec((rows // seq_len, nh, hd, hd), lambda s, b: (s, 0, 0, 0))
        scratch = [pltpu.VMEM((rows, D_RWKV), F32)]
    if s0 is None:
        s0 = p['state_rwkv']
        s0_spec = pl.BlockSpec((None,) + state_spec.block_shape,
                               lambda s, b: (layer, s, 0, 0, 0))
    else:
        s0_spec = state_spec
    stacked = [p[k] for k in ('w0', 'w2', 'a0', 'a2', 'g2', 'k_k', 'k_a', 'r_k', 'lnx_g', 'lnx_b')]
    consts = list(_rwkv_constants(seg_len, rows))
    y, s_new = pl.pallas_call(
        functools.partial(_rwkv_kernel, seg_len=seg_len, step_chunks=step_chunks),
        grid=grid,
        in_specs=[pl.BlockSpec((rows, N_SHIFT), row_map), s0_spec]
        + [_layer_spec(t, layer) for t in stacked] + [_const_spec(t.shape) for t in consts],
        out_specs=[pl.BlockSpec((rows, D_RWKV), row_map), state_spec],
        out_shape=[jax.ShapeDtypeStruct((n, D_RWKV), F32),
                   jax.ShapeDtypeStruct((nseq, nh, hd, hd), F32)],
        scratch_shapes=scratch,
        compiler_params=pltpu.CompilerParams(dimension_semantics=("arbitrary", "arbitrary"),
                                             vmem_limit_bytes=V7X_VMEM_LIMIT_BYTES),
        name="rwkv7",
    )(z, s0, *stacked, *consts)
    return y, s_new


def _ffn_kernel(h_ref, y8_ref, u8_ref, yb_ref, pe_ref, c1_ref, c2_ref, d8_ref, wglu_ref, bglu_ref,
                wout_ref, g2_ref, wup_ref, cw_ref, cb_ref, wdn_ref, wple_ref, wpg_ref, gf_ref,
                *rest, seq_len, tm, final):
    i = pl.program_id(0)
    long_seq = seq_len >= tm
    if long_seq:
        o_ref, ga_ref, gb_ref, fold_ref, act_ref, carry_ref = rest

        @pl.when(i == 0)
        def _():
            carry_ref[...] = jnp.zeros_like(carry_ref)
    else:
        o_ref, ga_ref, gb_ref, fold_ref, act_ref = rest
    ya = y8_ref[...] + d8_ref[...] * u8_ref[...]
    c_gelu = math.sqrt(2.0 / math.pi)
    ya = ya * (0.5 * (1.0 + jnp.tanh(c_gelu * (ya + 0.044715 * (ya * ya * ya)))))
    ya = _unfold_rows(ya, fold_ref)
    ya = ya * _sigmoid(_mm(ya, wglu_ref[...]) + bglu_ref[...])
    h1 = (h_ref[...] + _mm(ya, wout_ref[:D_SSM, :]) + _mm(yb_ref[...], wout_ref[D_SSM:, :]))
    x2 = _rmsnorm(h1, g2_ref[...]).astype(BF16)
    row = lax.broadcasted_iota(jnp.int32, (tm, 1), 0)
    if long_seq:
        first = (i % (seq_len // tm)) == 0
    else:
        t = row % seq_len
    def up(c):
        cs = slice(c * FF_CHUNK, (c + 1) * FF_CHUNK)
        gs = slice(D_FF + c * FF_CHUNK, D_FF + (c + 1) * FF_CHUNK)
        return (jnp.dot(x2, wup_ref[:, cs], preferred_element_type=F32),
                jnp.dot(x2, wup_ref[:, gs], preferred_element_type=F32))

    n_chunks = D_FF // FF_CHUNK
    nxt = up(0)
    for c in range(n_chunks):
        cs = slice(c * FF_CHUNK, (c + 1) * FF_CHUNK)
        val, gate = nxt
        if c + 1 < n_chunks:
            nxt = up(c + 1)
        r1 = pltpu.roll(gate, 1, 0)
        r2 = pltpu.roll(gate, 2, 0)
        if long_seq:
            m1 = jnp.where(first, c1_ref[0][:, cs], carry_ref[1:2, cs])
            m2 = jnp.where(first, c2_ref[0][:, cs], carry_ref[0:1, cs])
            p1 = jnp.where(row == 0, m1, r1)
            p2 = jnp.where(row == 0, m2, jnp.where(row == 1, m1, r2))
            carry_ref[:, cs] = gate[tm - 2:tm, :]
            ga_ref[0, :, cs] = gate[tm - 2:tm - 1, :]
            gb_ref[0, :, cs] = gate[tm - 1:tm, :]
        else:
            m1 = _expand_rows(c1_ref[:, cs], seq_len)
            m2 = _expand_rows(c2_ref[:, cs], seq_len)
            p1 = jnp.where(t == 0, m1, r1)
            p2 = jnp.where(t == 0, m2, jnp.where(t == 1, m1, r2))
            ga_ref[:, cs] = _step_rows(gate, seq_len, seq_len - 2)
            gb_ref[:, cs] = _step_rows(gate, seq_len, seq_len - 1)
        conv = (cb_ref[:, cs] + cw_ref[2:3, cs] * gate + cw_ref[1:2, cs] * p1 + cw_ref[0:1, cs] * p2)
        act_ref[:, cs] = (conv * _sigmoid(conv) * val).astype(BF16)
    h2 = h1 + jnp.dot(act_ref[...], wdn_ref[...], preferred_element_type=F32)
    h3 = h2 + _mm(pe_ref[...], wple_ref[...]) * _sigmoid(_mm(h2, wpg_ref[...]))
    if final:
        h3 = _rmsnorm(h3, gf_ref[...])
    o_ref[...] = h3


def _ffn(h, y8, u8, yb, pe, conv0, p, layer, seq_len, tm, final):
    n = h.shape[0]
    nseq = n // seq_len
    row_spec = lambda w: pl.BlockSpec((tm, w), lambda i: (i, 0))
    pe_spec = pl.BlockSpec((None, tm, D_PLE), lambda i: (layer, i, 0))
    fold_spec = pl.BlockSpec((tm // S5_STEPS, S5_ROW), lambda i: (i, 0))
    scratch = [pltpu.VMEM((FOLD_LANE_BLOCKS, tm, LANES), F32), pltpu.VMEM((tm, D_FF), BF16)]
    if seq_len >= tm:
        tps = seq_len // tm
        seq_spec = pl.BlockSpec((1, 1, D_FF), lambda i: (i // tps, 0, 0))
        seq_shape = (nseq, 1, D_FF)
        scratch.append(pltpu.VMEM((2, D_FF), F32))
    else:
        seq_spec = pl.BlockSpec((tm // seq_len, D_FF), lambda i: (i, 0))
        seq_shape = (nseq, D_FF)
    stacked = [p[k] for k in ('d8', 'w_glu', 'b_glu', 'w_out', 'norm2_g', 'w_up', 'conv_w',
                              'conv_b', 'w_down', 'w_ple', 'w_pg')]
    consts = [p['final_g']]
    weights = stacked + consts
    out, ga, gb = pl.pallas_call(
        functools.partial(_ffn_kernel, seq_len=seq_len, tm=tm, final=final),
        grid=(n // tm,),
        in_specs=[row_spec(D_MODEL), fold_spec, fold_spec, row_spec(D_RWKV), pe_spec,
                  seq_spec, seq_spec] + [_layer_spec(t, layer) for t in stacked]
        + [_const_spec(t.shape) for t in consts],
        out_specs=[row_spec(D_MODEL), seq_spec, seq_spec],
        out_shape=[jax.ShapeDtypeStruct((n, D_MODEL), F32)]
        + [jax.ShapeDtypeStruct(seq_shape, F32)] * 2,
        scratch_shapes=scratch,
        compiler_params=pltpu.CompilerParams(dimension_semantics=("arbitrary",),
                                             vmem_limit_bytes=V7X_VMEM_LIMIT_BYTES),
        name="ffn",
    )(h, y8, u8, yb, pe, conv0[:, 1].reshape(seq_shape), conv0[:, 0].reshape(seq_shape), *weights)
    conv_new = jnp.stack([ga.reshape(nseq, D_FF), gb.reshape(nseq, D_FF)], axis=1)
    return out, conv_new


def _layer(h, pe, st, p, layer, s5_tables, seq_len, tm, final):
    ssm_re0, ssm_im0, rwkv0, shift0, conv0 = st
    u8, z, shift_new = _inproj(h, p, layer, shift0, seq_len, tm)
    y8, hre, him = _s5(u8, ssm_re0, ssm_im0, s5_tables, seq_len)
    yb, s_last = _rwkv(z, rwkv0, p, layer, seq_len)
    h, conv_new = _ffn(h, y8, u8, yb, pe, conv0, p, layer, seq_len, tm, final)
    return h, (hre, him, s_last, shift_new, conv_new)


def _stacked_params(w, state_rwkv, state_shift):
    depth = w['w_in'].shape[0]
    row = lambda t: t.reshape(depth, 1, -1).astype(F32)
    bf = lambda t: t.astype(BF16)
    return {
        'norm1_g': row(w['norm1_g']), 'w_in': bf(w['w_in']), 'shift_mu': row(w['shift_mu']),
        'd8': jnp.tile(row(w['ssm_d']), (1, 1, S5_STEPS)), 'w_glu': bf(w['ssm_w_glu']),
        'b_glu': row(w['ssm_b_glu']),
        'w0': row(w['rwkv_w0']), 'w2': bf(w['rwkv_w2']), 'a0': row(w['rwkv_a0']),
        'a2': bf(w['rwkv_a2']), 'g2': bf(w['rwkv_g2']), 'k_k': row(w['rwkv_k_k']),
        'k_a': row(w['rwkv_k_a']), 'r_k': row(w['rwkv_r_k']), 'lnx_g': row(w['rwkv_lnx_g']),
        'lnx_b': row(w['rwkv_lnx_b']), 'w_out': bf(w['w_out']), 'norm2_g': row(w['norm2_g']),
        'w_up': bf(w['w_up']), 'conv_w': w['conv_w'].astype(F32), 'conv_b': row(w['conv_b']),
        'w_down': bf(w['w_down']), 'w_ple': bf(w['w_ple']), 'w_pg': bf(w['w_pg']),
        'final_g': w['final_g'].reshape(1, -1).astype(F32),
        'state_rwkv': state_rwkv.astype(F32), 'state_shift': state_shift.astype(F32),
    }


def _forward(x_prompt, x_sample, p_prompt, p_sample, state_ssm_re, state_ssm_im, state_rwkv,
             state_shift, state_conv, w, tm_prompt, tm_sample):
    depth = w['w_in'].shape[0]
    bp, lp, _ = x_prompt.shape
    bs, ls, _ = x_sample.shape
    hp = x_prompt.reshape(bp * lp, D_MODEL).astype(F32)
    hs = x_sample.reshape(bs * ls, D_MODEL).astype(F32)
    pe_p = p_prompt.reshape(depth, bp * lp, D_PLE).astype(F32)
    pe_s = p_sample.reshape(depth, bs * ls, D_PLE).astype(F32)
    zero_st = (jnp.zeros((bp, N_SSM_GROUPS, SSM_STATE), F32),
               jnp.zeros((bp, N_SSM_GROUPS, SSM_STATE), F32),
               jnp.zeros((bp, N_RWKV_HEADS, RWKV_HEAD, RWKV_HEAD), F32),
               jnp.zeros((bp, N_SHIFT), F32),
               jnp.zeros((bp, 2, D_FF), F32))
    p = _stacked_params(w, state_rwkv, state_shift)
    new_p = [[] for _ in range(5)]
    new_s = [[] for _ in range(5)]
    for i in range(depth):
        tables = _s5_tables(w['ssm_lam_re'][i], w['ssm_lam_im'][i], w['ssm_log_dt'][i],
                            w['ssm_b_re'][i], w['ssm_b_im'][i], w['ssm_c_re'][i], w['ssm_c_im'][i])
        final = i == depth - 1
        hp, stp = _layer(hp, pe_p, zero_st, p, i, tables, lp, tm_prompt, final)
        st_in = (state_ssm_re[i].astype(F32), state_ssm_im[i].astype(F32), None, None,
                 state_conv[i].astype(F32))
        hs, sts = _layer(hs, pe_s, st_in, p, i, tables, ls, tm_sample, final)
        for j in range(5):
            new_p[j].append(stp[j])
            new_s[j].append(sts[j])
    y_prompt = hp.reshape(bp, lp, D_MODEL).astype(x_prompt.dtype)
    y_sample = hs.reshape(bs, ls, D_MODEL).astype(x_sample.dtype)
    dts = (state_ssm_re.dtype, state_ssm_im.dtype, state_rwkv.dtype, state_shift.dtype,
           state_conv.dtype)
    outs_p = tuple(jnp.stack(new_p[j]).astype(dts[j]) for j in range(5))
    outs_s = tuple(jnp.stack(new_s[j]).astype(dts[j]) for j in range(5))
    return (y_prompt, y_sample) + outs_p + outs_s


def kernel(x_prompt, x_sample, p_prompt, p_sample, state_ssm_re, state_ssm_im, state_rwkv, state_shift, state_conv, norm1_g, w_in, shift_mu, ssm_lam_re, ssm_lam_im, ssm_log_dt, ssm_b_re, ssm_b_im, ssm_c_re, ssm_c_im, ssm_d, ssm_w_glu, ssm_b_glu, rwkv_w0, rwkv_w2, rwkv_a0, rwkv_a2, rwkv_g2, rwkv_k_k, rwkv_k_a, rwkv_r_k, rwkv_lnx_g, rwkv_lnx_b, w_out, norm2_g, w_up, conv_w, conv_b, w_down, w_ple, w_pg, final_g):
    w = dict(norm1_g=norm1_g, w_in=w_in, shift_mu=shift_mu, ssm_lam_re=ssm_lam_re,
             ssm_lam_im=ssm_lam_im, ssm_log_dt=ssm_log_dt, ssm_b_re=ssm_b_re, ssm_b_im=ssm_b_im,
             ssm_c_re=ssm_c_re, ssm_c_im=ssm_c_im, ssm_d=ssm_d, ssm_w_glu=ssm_w_glu,
             ssm_b_glu=ssm_b_glu, rwkv_w0=rwkv_w0, rwkv_w2=rwkv_w2, rwkv_a0=rwkv_a0,
             rwkv_a2=rwkv_a2, rwkv_g2=rwkv_g2, rwkv_k_k=rwkv_k_k, rwkv_k_a=rwkv_k_a,
             rwkv_r_k=rwkv_r_k, rwkv_lnx_g=rwkv_lnx_g, rwkv_lnx_b=rwkv_lnx_b, w_out=w_out,
             norm2_g=norm2_g, w_up=w_up, conv_w=conv_w, conv_b=conv_b, w_down=w_down,
             w_ple=w_ple, w_pg=w_pg, final_g=final_g)
    lp = x_prompt.shape[1]
    ns = x_sample.shape[0] * x_sample.shape[1]
    return _forward(x_prompt, x_sample, p_prompt, p_sample, state_ssm_re, state_ssm_im,
                    state_rwkv, state_shift, state_conv, w,
                    tm_prompt=min(512, lp), tm_sample=min(256, ns))
```

```python
import functools
import math

import numpy as np
import jax
import jax.numpy as jnp
from jax import lax
from jax.experimental import pallas as pl
from jax.experimental.pallas import tpu as pltpu

F32 = jnp.float32
BF16 = jnp.bfloat16
HIGHEST = lax.Precision.HIGHEST

D_MODEL = 1024
D_SSM = 512
D_RWKV = 512
SSM_GROUP = 16
N_SSM_GROUPS = 32
SSM_STATE = 64
RWKV_HEAD = 64
N_RWKV_HEADS = 8
DECAY_LORA = 64
AAA_LORA = 64
GATE_LORA = 128
N_SHIFT = 3 * D_RWKV + DECAY_LORA + AAA_LORA + GATE_LORA
N_IN = D_SSM + N_SHIFT
D_FF = 2816
D_PLE = 256
RMS_EPS = 1e-6
GN_EPS = 64e-5
L2_EPS = 1e-12

LANES = 128
FOLD_LANE_BLOCKS = D_SSM // LANES
S5_STEPS = 8
S5_ROW = S5_STEPS * D_SSM
S5_LANE_BLOCKS = 4
S5_BLOCK_STATE = 2 * 8 * SSM_STATE
S5_STATE_ROW = S5_LANE_BLOCKS * S5_BLOCK_STATE
FF_CHUNK = 256
V7X_VMEM_LIMIT_BYTES = 56 * 1024 * 1024


def _mm(a, b):
    return jnp.dot(a.astype(BF16), b.astype(BF16), preferred_element_type=F32)


def _mm_nt(a, b):
    return lax.dot_general(a.astype(BF16), b.astype(BF16), (((1,), (1,)), ((), ())),
                           preferred_element_type=F32)


def _mm_tn(a, b):
    return lax.dot_general(a.astype(BF16), b.astype(BF16), (((0,), (0,)), ((), ())),
                           preferred_element_type=F32)


def _seg_sum(x, ones_bd):
    hi = x.astype(BF16)
    lo = (x - hi.astype(F32)).astype(BF16)
    gl = ones_bd.shape[0]
    return jnp.concatenate(
        [jnp.dot(hi[:, g:g + gl], ones_bd, preferred_element_type=F32)
         + jnp.dot(lo[:, g:g + gl], ones_bd, preferred_element_type=F32)
         for g in range(0, x.shape[1], gl)], axis=1)


def _rmsnorm(x, g):
    return x * lax.rsqrt(jnp.mean(x * x, axis=-1, keepdims=True) + RMS_EPS) * g


def _sigmoid(x):
    return 1.0 / (1.0 + jnp.exp(-x))


def _const_spec(shape):
    nd = len(shape)
    return pl.BlockSpec(shape, lambda *_: (0,) * nd, pipeline_mode=pl.Buffered(1))


def _layer_spec(t, layer):
    nd = t.ndim - 1
    return pl.BlockSpec((None,) + t.shape[1:], lambda *_: (layer,) + (0,) * nd,
                        pipeline_mode=pl.Buffered(1))


def _expand_rows(x, seq_len):
    nseq, w = x.shape
    return jnp.broadcast_to(x[:, None, :], (nseq, seq_len, w)).reshape(nseq * seq_len, w)


def _step_rows(x, seq_len, t):
    return x.reshape(x.shape[0] // seq_len, seq_len, x.shape[1])[:, t, :]


def _fold_rows(x, scr_ref, out_ref):
    tm = x.shape[0]
    for q in range(FOLD_LANE_BLOCKS):
        scr_ref[q] = x[:, q * LANES:(q + 1) * LANES]
    for s in range(S5_STEPS):
        for q in range(FOLD_LANE_BLOCKS):
            c0 = s * D_SSM + q * LANES
            out_ref[:, c0:c0 + LANES] = scr_ref[q, pl.ds(s, tm // S5_STEPS, stride=S5_STEPS), :]


def _unfold_rows(x8, scr_ref):
    tm = x8.shape[0] * S5_STEPS
    for s in range(S5_STEPS):
        for q in range(FOLD_LANE_BLOCKS):
            c0 = s * D_SSM + q * LANES
            scr_ref[q, pl.ds(s, tm // S5_STEPS, stride=S5_STEPS), :] = x8[:, c0:c0 + LANES]
    return jnp.concatenate([scr_ref[q] for q in range(FOLD_LANE_BLOCKS)], axis=1)


def _inproj_kernel(h_ref, g_ref, w_ref, mu_ref, init_ref, *rest, seq_len, tm):
    i = pl.program_id(0)
    long_seq = seq_len >= tm
    if long_seq:
        u8_ref, z_ref, last_ref, fold_ref, carry_ref = rest

        @pl.when(i == 0)
        def _():
            carry_ref[...] = jnp.zeros_like(carry_ref)
    else:
        u8_ref, z_ref, last_ref, fold_ref = rest
    xn = _rmsnorm(h_ref[...], g_ref[...])
    proj = jnp.dot(xn.astype(BF16), w_ref[...], preferred_element_type=F32)
    _fold_rows(proj[:, :D_SSM], fold_ref, u8_ref)
    zr = proj[:, D_SSM:]
    rolled = pltpu.roll(zr, 1, 0)
    row = lax.broadcasted_iota(jnp.int32, (tm, 1), 0)
    if long_seq:
        first = (i % (seq_len // tm)) == 0
        row0 = jnp.where(first, init_ref[0], carry_ref[...])
        prev = jnp.where(row == 0, row0, rolled)
        carry_ref[...] = zr[tm - 1:tm, :]
        last_ref[0] = zr[tm - 1:tm, :]
    else:
        prev = jnp.where(row % seq_len == 0, _expand_rows(init_ref[...], seq_len), rolled)
        last_ref[...] = _step_rows(zr, seq_len, seq_len - 1)
    z_ref[...] = (zr + (prev - zr) * mu_ref[...]).astype(z_ref.dtype)


def _inproj(h, p, layer, shift0, seq_len, tm):
    n = h.shape[0]
    nseq = n // seq_len
    fold_scratch = pltpu.VMEM((FOLD_LANE_BLOCKS, tm, LANES), F32)
    if seq_len >= tm:
        tps = seq_len // tm
        seq_spec = pl.BlockSpec((1, 1, N_SHIFT), lambda i: (i // tps, 0, 0))
        init_spec = seq_spec
        init = shift0.reshape(nseq, 1, N_SHIFT)
        last_shape = jax.ShapeDtypeStruct((nseq, 1, N_SHIFT), F32)
        scratch = [fold_scratch, pltpu.VMEM((1, N_SHIFT), F32)]
    else:
        spt = tm // seq_len
        seq_spec = pl.BlockSpec((spt, N_SHIFT), lambda i: (i, 0))
        init_spec = pl.BlockSpec((None, spt, N_SHIFT), lambda i: (layer, i, 0))
        init = p['state_shift']
        last_shape = jax.ShapeDtypeStruct((nseq, N_SHIFT), F32)
        scratch = [fold_scratch]
    g1, w_in, mu = p['norm1_g'], p['w_in'], p['shift_mu']
    u8, z, last = pl.pallas_call(
        functools.partial(_inproj_kernel, seq_len=seq_len, tm=tm),
        grid=(n // tm,),
        in_specs=[pl.BlockSpec((tm, D_MODEL), lambda i: (i, 0)),
                  _layer_spec(g1, layer), _layer_spec(w_in, layer), _layer_spec(mu, layer),
                  init_spec],
        out_specs=[pl.BlockSpec((tm // S5_STEPS, S5_ROW), lambda i: (i, 0)),
                   pl.BlockSpec((tm, N_SHIFT), lambda i: (i, 0)),
                   seq_spec],
        out_shape=[jax.ShapeDtypeStruct((n // S5_STEPS, S5_ROW), F32),
                   jax.ShapeDtypeStruct((n, N_SHIFT), BF16),
                   last_shape],
        scratch_shapes=scratch,
        compiler_params=pltpu.CompilerParams(dimension_semantics=("arbitrary",),
                                             vmem_limit_bytes=V7X_VMEM_LIMIT_BYTES),
        name="inproj",
    )(h, g1, w_in, mu, init)
    return u8, z, last.reshape(nseq, N_SHIFT)


def _s5_prep_kernel(lr_ref, li_ref, ldt_ref, br_ref, bi_ref, cr_ref, ci_ref, tile_ref, sel_ref,
                    wx_ref, wy_ref, apr_ref, api_ref):
    lr, li = lr_ref[...], li_ref[...]
    dt = jnp.exp(ldt_ref[...])
    mag = jnp.exp(lr * dt)
    ar = mag * jnp.cos(li * dt)
    ai = mag * jnp.sin(li * dt)
    den = lr * lr + li * li
    nr = ar - 1.0
    fr = (nr * lr + ai * li) / den
    fi = (ai * lr - nr * li) / den
    br, bi = br_ref[...], bi_ref[...]
    bbr = fr * br - fi * bi
    bbi = fr * bi + fi * br
    cr, ci = cr_ref[...], ci_ref[...]
    pr = [jnp.ones_like(ar)]
    pi = [jnp.zeros_like(ar)]
    for _ in range(S5_STEPS):
        pr.append(pr[-1] * ar - pi[-1] * ai)
        pi.append(pr[-2] * ai + pi[-1] * ar)
    nrow = lr.shape[0]
    blk = nrow // S5_LANE_BLOCKS
    half = S5_BLOCK_STATE // 2
    ri = lax.broadcasted_iota(jnp.int32, (nrow, nrow), 0)
    ci_ = lax.broadcasted_iota(jnp.int32, (nrow, nrow), 1)
    same_group = (ri // SSM_GROUP) == (ci_ // SSM_GROUP)
    keep_in = ((ri % blk) // SSM_GROUP) == (ci_ // SSM_STATE)
    keep_out = (ri // SSM_STATE) == ((ci_ % blk) // SSM_GROUP)
    tile, sel = tile_ref[...], sel_ref[...]
    nt = (((1,), (1,)), ((), ()))
    wy_ref[...] = jnp.zeros(wy_ref.shape, wy_ref.dtype)
    for s in range(S5_STEPS):
        qr, qi = pr[S5_STEPS - 1 - s], pi[S5_STEPS - 1 - s]
        wr = qr * bbr - qi * bbi
        wi = qr * bbi + qi * bbr
        for h, w_ in enumerate((wr, wi)):
            t = jnp.dot(w_.astype(BF16), tile, preferred_element_type=F32)
            t = jnp.where(keep_in, t, 0.0).astype(BF16)
            for q in range(S5_LANE_BLOCKS):
                wx_ref[q, s * blk:(s + 1) * blk, h * half:(h + 1) * half] = t[q * blk:(q + 1) * blk, :]
        tau = S5_STEPS - 1 - s
        kt = (lax.dot_general(wr, cr, nt, precision=HIGHEST, preferred_element_type=F32)
              - lax.dot_general(wi, ci, nt, precision=HIGHEST, preferred_element_type=F32))
        kt = jnp.where(same_group, kt, 0.0).astype(BF16)
        for q in range(S5_LANE_BLOCKS):
            kq = kt[q * blk:(q + 1) * blk, q * blk:(q + 1) * blk]
            for s_in in range(S5_STEPS - tau):
                s_out = s_in + tau
                r0 = S5_BLOCK_STATE + s_in * blk
                wy_ref[q, s_out // 2, r0:r0 + blk, (s_out % 2) * blk:(s_out % 2 + 1) * blk] = kq
        mr = cr * pr[s + 1] - ci * pi[s + 1]
        mi = cr * pi[s + 1] + ci * pr[s + 1]
        for h, m_ in enumerate((mr, -mi)):
            t = lax.dot_general(sel, m_.astype(BF16), nt, preferred_element_type=F32)
            t = jnp.where(keep_out, t, 0.0).astype(BF16)
            for q in range(S5_LANE_BLOCKS):
                wy_ref[q, s // 2, h * half:(h + 1) * half, (s % 2) * blk:(s % 2 + 1) * blk] = (
                    t[:, q * blk:(q + 1) * blk])
    a8r, a8i = pr[S5_STEPS], pi[S5_STEPS]
    er, ei = a8r, a8i
    for n in range(S5_STEPS):
        apr_ref[n] = er
        api_ref[n] = ei
        er, ei = er * a8r - ei * a8i, er * a8i + ei * a8r


def _s5_tables(lam_re, lam_im, log_dt, b_re, b_im, c_re, c_im):
    G, P, K = N_SSM_GROUPS, SSM_STATE, SSM_GROUP
    Q, GB = S5_LANE_BLOCKS, G // S5_LANE_BLOCKS
    rep = lambda t: jnp.repeat(t, K, axis=0)
    tile = np.tile(np.eye(P, dtype=np.float32), (1, GB))
    args = (rep(lam_re), rep(lam_im), rep(jnp.broadcast_to(log_dt[:, None], (G, P))),
            jnp.swapaxes(b_re, 1, 2).reshape(G * K, P), jnp.swapaxes(b_im, 1, 2).reshape(G * K, P),
            c_re.reshape(G * K, P), c_im.reshape(G * K, P),
            jnp.asarray(tile, BF16), jnp.asarray(tile.T, BF16))
    t3 = jax.ShapeDtypeStruct((S5_STEPS, G * K, P), F32)
    wx, wy, apr, api = pl.pallas_call(
        _s5_prep_kernel,
        out_shape=[jax.ShapeDtypeStruct((Q, S5_STEPS * GB * K, S5_BLOCK_STATE), BF16),
                   jax.ShapeDtypeStruct((Q, S5_STEPS // 2, S5_BLOCK_STATE + S5_STEPS * GB * K,
                                         2 * GB * K), BF16),
                   t3, t3],
        compiler_params=pltpu.CompilerParams(vmem_limit_bytes=V7X_VMEM_LIMIT_BYTES),
        name="s5_prep",
    )(*args)
    ap = jnp.concatenate([t[:, ::K, :].reshape(S5_STEPS, Q, GB * P) for t in (apr, api)], axis=-1)
    return wx, wy, ap.reshape(S5_STEPS, S5_STATE_ROW)


def _state_to_lanes(h_re, h_im):
    n = h_re.shape[0]
    parts = [t.reshape(n, S5_LANE_BLOCKS, -1) for t in (h_re, h_im)]
    return jnp.concatenate(parts, axis=-1).reshape(n, S5_STATE_ROW)


def _lanes_to_state(h):
    n = h.shape[0]
    h = h.reshape(n, S5_LANE_BLOCKS, 2, N_SSM_GROUPS // S5_LANE_BLOCKS, SSM_STATE)
    return (h[:, :, 0].reshape(n, N_SSM_GROUPS, SSM_STATE), h[:, :, 1].reshape(n, N_SSM_GROUPS, SSM_STATE))


def _s5_kernel(u_ref, h0_ref, wx_ref, wy_ref, ap_ref, y_ref, hl_ref, *scratch, rows, scan):
    half = S5_BLOCK_STATE // 2
    cw = D_SSM // S5_LANE_BLOCKS
    blocks = range(S5_LANE_BLOCKS)
    ub = u_ref[...].astype(BF16)
    ucat = [jnp.concatenate([ub[:, s * D_SSM + q * cw: s * D_SSM + (q + 1) * cw]
                             for s in range(S5_STEPS)], axis=1) for q in blocks]
    x = [jnp.dot(ucat[q], wx_ref[q], preferred_element_type=F32) for q in blocks]
    ap = [ap_ref[:, q * S5_BLOCK_STATE:(q + 1) * S5_BLOCK_STATE] for q in blocks]
    if scan:
        hs_ref, hp_ref = scratch
        sub3 = lax.broadcasted_iota(jnp.int32, (1, 8, 1), 1)
        sub2 = lax.broadcasted_iota(jnp.int32, (8, 1), 0)
        for q in blocks:
            xr = x[q][:, :half].reshape(rows // 8, 8, half)
            xi = x[q][:, half:].reshape(rows // 8, 8, half)
            for k in (1, 2, 4):
                er = jnp.where(sub3 >= k, ap[q][k - 1:k, :half].reshape(1, 1, half), 0.0)
                ei = jnp.where(sub3 >= k, ap[q][k - 1:k, half:].reshape(1, 1, half), 0.0)
                sr = pltpu.roll(xr, k, 1)
                si = pltpu.roll(xi, k, 1)
                xr, xi = xr + er * sr - ei * si, xi + er * si + ei * sr
            hs_ref[q, :, :half] = xr.reshape(rows, half)
            hs_ref[q, :, half:] = xi.reshape(rows, half)
        h0 = h0_ref[0]
        init = tuple(h0[:, q * S5_BLOCK_STATE + o:q * S5_BLOCK_STATE + o + half]
                     for q in blocks for o in (0, half))

        def body(j, carry):
            sl = pl.ds(pl.multiple_of(j * 8, 8), 8)
            out = []
            for q in blocks:
                cr, ci = carry[2 * q], carry[2 * q + 1]
                pwr, pwi = ap[q][:, :half], ap[q][:, half:]
                hr = hs_ref[q, sl, :half] + pwr * cr - pwi * ci
                hi = hs_ref[q, sl, half:] + pwr * ci + pwi * cr
                hp_ref[q, sl, :half] = jnp.where(sub2 == 0, cr, pltpu.roll(hr, 1, 0))
                hp_ref[q, sl, half:] = jnp.where(sub2 == 0, ci, pltpu.roll(hi, 1, 0))
                out += [hr[7:8, :], hi[7:8, :]]
            return tuple(out)

        last = lax.fori_loop(0, rows // 8, body, init)
        for q in blocks:
            lo = q * S5_BLOCK_STATE
            hl_ref[0, :, lo:lo + half] = last[2 * q]
            hl_ref[0, :, lo + half:lo + S5_BLOCK_STATE] = last[2 * q + 1]
        hprev = [hp_ref[q].astype(BF16) for q in blocks]
    else:
        hprev = []
        for q in blocks:
            lo = q * S5_BLOCK_STATE
            h0r = h0_ref[:, lo:lo + half]
            h0i = h0_ref[:, lo + half:lo + S5_BLOCK_STATE]
            er, ei = ap[q][0:1, :half], ap[q][0:1, half:]
            hl_ref[:, lo:lo + half] = er * h0r - ei * h0i + x[q][:, :half]
            hl_ref[:, lo + half:lo + S5_BLOCK_STATE] = er * h0i + ei * h0r + x[q][:, half:]
            hprev.append(jnp.concatenate([h0r, h0i], axis=1).astype(BF16))
    for q in blocks:
        lhs = jnp.concatenate([hprev[q], ucat[q]], axis=1)
        for j in range(S5_STEPS // 2):
            kk = S5_BLOCK_STATE + cw * (2 * j + 2)
            y2 = jnp.dot(lhs[:, :kk], wy_ref[q, j, :kk, :], preferred_element_type=F32)
            c0 = (2 * j) * D_SSM + q * cw
            c1 = (2 * j + 1) * D_SSM + q * cw
            y_ref[:, c0:c0 + cw] = y2[:, :cw]
            y_ref[:, c1:c1 + cw] = y2[:, cw:]


def _s5(u8, h_re, h_im, tables, seq_len):
    wx, wy, ap = tables
    n = u8.shape[0] * S5_STEPS
    nseq = n // seq_len
    h0 = _state_to_lanes(h_re, h_im)
    cps = seq_len // S5_STEPS
    scan = cps > 1
    if scan:
        rows = cps
        grid = (nseq,)
        h0 = h0.reshape(nseq, 1, S5_STATE_ROW)
        h_spec = pl.BlockSpec((1, 1, S5_STATE_ROW), lambda i: (i, 0, 0))
        h_shape = jax.ShapeDtypeStruct((nseq, 1, S5_STATE_ROW), F32)
    else:
        rows = min(nseq, 128)
        grid = (nseq // rows,)
        h_spec = pl.BlockSpec((rows, S5_STATE_ROW), lambda i: (i, 0))
        h_shape = jax.ShapeDtypeStruct((nseq, S5_STATE_ROW), F32)
    y8, hl = pl.pallas_call(
        functools.partial(_s5_kernel, rows=rows, scan=scan),
        grid=grid,
        in_specs=[pl.BlockSpec((rows, S5_ROW), lambda i: (i, 0)),
                  h_spec,
                  _const_spec(wx.shape), _const_spec(wy.shape), _const_spec(ap.shape)],
        out_specs=[pl.BlockSpec((rows, S5_ROW), lambda i: (i, 0)), h_spec],
        out_shape=[jax.ShapeDtypeStruct((n // S5_STEPS, S5_ROW), F32), h_shape],
        scratch_shapes=[pltpu.VMEM((S5_LANE_BLOCKS, rows, S5_BLOCK_STATE), F32)] * 2 if scan else [],
        compiler_params=pltpu.CompilerParams(dimension_semantics=("arbitrary",),
                                             vmem_limit_bytes=V7X_VMEM_LIMIT_BYTES),
        name="s5",
    )(u8, h0, wx, wy, ap)
    hre_new, him_new = _lanes_to_state(hl.reshape(nseq, S5_STATE_ROW))
    return y8, hre_new, him_new


RW_CHUNK = 64
RW_STEP_CHUNKS_LONG = 4
RW_STEP_CHUNKS_SHORT = 4
RW_GROUPS = 2
RW_GROUP_HEADS = N_RWKV_HEADS // RW_GROUPS
RW_GROUP_LANES = D_RWKV // RW_GROUPS


def _split3(x):
    hi = x.astype(BF16)
    r1 = x - hi.astype(F32)
    mid = r1.astype(BF16)
    lo = (r1 - mid.astype(F32)).astype(BF16)
    return hi, mid, lo


def _bd(x, hm_ref):
    xb = x.astype(BF16)
    return jnp.concatenate([xb * hm_ref[h] for h in range(RW_GROUP_HEADS)], axis=0)


def _unbd(f, hm_ref):
    n = RWKV_HEAD
    out = f[:n] * hm_ref[0].astype(F32)
    for h in range(1, RW_GROUP_HEADS):
        out = out + f[h * n:(h + 1) * n] * hm_ref[h].astype(F32)
    return out


def _rwkv_kernel(z_ref, s0_ref, w0_ref, w2_ref, a0_ref, a2_ref, g2_ref, kk_ref, ka_ref, rk_ref,
                 lg_ref, lb_ref, ones_ref, hm_ref, cm_ref, tri_ref, y_ref, sl_ref, ypre_ref,
                 *scratch, seg_len, step_chunks):
    C = RW_CHUNK
    N = RWKV_HEAD
    GL = RW_GROUP_LANES
    chained = seg_len == C
    if chained:
        (st_ref,) = scratch
        b = pl.program_id(1)

        @pl.when(b == 0)
        def _():
            for g in range(RW_GROUPS):
                st_ref[g] = jnp.concatenate(
                    [s0_ref[0, h] for h in range(g * RW_GROUP_HEADS, (g + 1) * RW_GROUP_HEADS)], axis=1)

    ones_bd = ones_ref[...]
    z = z_ref[...].astype(F32)
    r = z[:, :D_RWKV]
    k = z[:, D_RWKV:2 * D_RWKV]
    v = z[:, 2 * D_RWKV:3 * D_RWKV]
    o = 3 * D_RWKV
    xw = z[:, o:o + DECAY_LORA]
    xa = z[:, o + DECAY_LORA:o + DECAY_LORA + AAA_LORA]
    xg = z[:, o + DECAY_LORA + AAA_LORA:]
    wd = -(w0_ref[...] + _mm(jnp.tanh(xw), w2_ref[...]))
    w = -(jnp.maximum(wd, 0.0) + jnp.log1p(jnp.exp(-jnp.abs(wd)))) - 0.5
    lw = -jnp.exp(w)
    a = _sigmoid(a0_ref[...] + _mm(xa, a2_ref[...]))
    out_gate = _mm(_sigmoid(xg), g2_ref[...])
    kk = k * kk_ref[...]
    kk = kk / jnp.maximum(jnp.sqrt(_seg_sum(kk * kk, ones_bd)), L2_EPS)
    kmod = k * (1.0 + (a - 1.0) * ka_ref[...])

    tri = tri_ref[...]
    cw = sum(jnp.dot(tri, t, preferred_element_type=F32) for t in _split3(lw))
    tot = _expand_rows(_step_rows(cw, seg_len, seg_len - 1), seg_len)
    w_in = jnp.exp(cw)
    w_tail = jnp.exp(tot - cw)
    w_inv = jnp.exp(-cw)
    w_all = jnp.exp(tot)
    ah = -kk * jnp.exp(cw - lw)
    bh = kk * a * w_inv
    kh = kmod * w_inv
    rh = r * w_in
    bt = kk * a * w_tail
    kt = kmod * w_tail
    strict = cm_ref[0] > 0.0
    incl = cm_ref[1] > 0.0
    eye_cat = cm_ref[2]
    n_fac = seg_len.bit_length() - 1

    items = [(c, g) for c in range(step_chunks) for g in range(RW_GROUPS)]
    sub = lambda t, c, g: t[c * C:(c + 1) * C, g * GL:(g + 1) * GL]
    bd = lambda t: _bd(t, hm_ref)
    A = [sub(ah, c, g) for c, g in items]
    R = [sub(rh, c, g) for c, g in items]
    V = [sub(v, c, g) for c, g in items]
    G = [_mm_nt(jnp.concatenate([A[i], R[i]], axis=0),
                jnp.concatenate([bd(sub(bh, c, g)), bd(sub(kh, c, g))], axis=0))
         for i, (c, g) in enumerate(items)]
    AB = [jnp.where(strict, t[:C, :GL], 0.0) for t in G]
    AK = [jnp.where(strict, t[:C, GL:], 0.0) for t in G]
    RB = [jnp.where(incl, t[C:, :GL], 0.0) for t in G]
    RK = [jnp.where(incl, t[C:, GL:], 0.0) for t in G]
    KV = [_mm(jnp.concatenate([AK[i], RK[i]], axis=0), bd(V[i])) for i in range(len(items))]
    T = [eye_cat + t for t in AB]
    P = [_mm(t, bd(t)) for t in AB]
    for _ in range(n_fac - 2):
        PT = [_mm(jnp.concatenate([P[i], T[i]], axis=0), bd(P[i])) for i in range(len(items))]
        P = [t[:C] for t in PT]
        T = [T[i] + PT[i][C:] for i in range(len(items))]
    T = [T[i] + _mm(T[i], bd(P[i])) for i in range(len(items))]
    RBT = [_mm(RB[i], bd(T[i])) for i in range(len(items))]
    X = [_mm(jnp.concatenate([T[i], RBT[i]], axis=0),
             jnp.concatenate([bd(A[i]), bd(KV[i][:C])], axis=1)) for i in range(len(items))]
    TA = [t[:C, :GL] for t in X]
    U0 = [t[:C, GL:] for t in X]
    Rt = [R[i] + X[i][C:, :GL] for i in range(len(items))]
    Y0 = [X[i][C:, GL:] + KV[i][C:] for i in range(len(items))]
    nseg = C // seg_len
    heads = lambda g: range(g * RW_GROUP_HEADS, (g + 1) * RW_GROUP_HEADS)
    head_lanes = lambda h: slice((h % RW_GROUP_HEADS) * N, (h % RW_GROUP_HEADS + 1) * N)
    if chained:
        PhiT, PsiT = [], []
        for i, (c, g) in enumerate(items):
            Bt, Kt, Wa = sub(bt, c, g), sub(kt, c, g), sub(w_all, c, g)
            PhiT.append(bd(eye_cat * Wa[0:1, :] + _unbd(_mm_tn(TA[i], Bt), hm_ref)))
            PsiT.append(_unbd(_mm_tn(jnp.concatenate([U0[i], V[i]], axis=0),
                                     jnp.concatenate([Bt, Kt], axis=0)), hm_ref))
        states = [st_ref[g] for g in range(RW_GROUPS)]
        for i, (c, g) in enumerate(items):
            S = states[g]
            ypre_ref[c * C:(c + 1) * C, g * GL:(g + 1) * GL] = _mm_nt(Rt[i], bd(S)) + Y0[i]
            states[g] = _mm(S, PhiT[i]) + PsiT[i]
        for g in range(RW_GROUPS):
            st_ref[g] = states[g]
            for h in heads(g):
                sl_ref[0, h] = states[g][:, head_lanes(h)]
    else:
        segs = [(i, c, g, s) for i, (c, g) in enumerate(items) for s in range(nseg)]
        rows_of = lambda s: slice(s * seg_len, (s + 1) * seg_len)
        S0 = [jnp.concatenate([s0_ref[c * nseg + s, h] for h in heads(g)], axis=1)
              for i, c, g, s in segs]
        RU = [_mm_nt(jnp.concatenate([Rt[i][rows_of(s)], TA[i][rows_of(s)]], axis=0), bd(S0[n]))
              for n, (i, c, g, s) in enumerate(segs)]
        for n, (i, c, g, s) in enumerate(segs):
            ypre_ref[c * C + s * seg_len:c * C + (s + 1) * seg_len, g * GL:(g + 1) * GL] = (
                RU[n][:seg_len] + Y0[i][rows_of(s)])
        UB = [_mm_tn(jnp.concatenate([RU[n][seg_len:] + U0[i][rows_of(s)], V[i][rows_of(s)]], axis=0),
                     jnp.concatenate([sub(bt, c, g)[rows_of(s)], sub(kt, c, g)[rows_of(s)]], axis=0))
              for n, (i, c, g, s) in enumerate(segs)]
        for n, (i, c, g, s) in enumerate(segs):
            S = S0[n] * sub(w_all, c, g)[s * seg_len:s * seg_len + 1, :] + _unbd(UB[n], hm_ref)
            for h in heads(g):
                sl_ref[c * nseg + s, h] = S[:, head_lanes(h)]
    y = ypre_ref[...]

    mu = _seg_sum(y, ones_bd) * (1.0 / N)
    d = y - mu
    var = _seg_sum(d * d, ones_bd) * (1.0 / N)
    yn = d * lax.rsqrt(var + GN_EPS) * lg_ref[...] + lb_ref[...]
    bonus = _seg_sum(r * kmod * rk_ref[...], ones_bd) * v
    y_ref[...] = (yn + bonus) * out_gate


def _rwkv_constants(seg_len, rows):
    n, gh, gl = RWKV_HEAD, RW_GROUP_HEADS, RW_GROUP_LANES
    lane_head = np.arange(gl) // n
    hm = (lane_head[None, None, :] == np.arange(gh)[:, None, None]) * np.ones((gh, n, gl))
    i = np.arange(RW_CHUNK)[:, None]
    j = (np.arange(gl) % n)[None, :]
    same = (i // seg_len) == (j // seg_len)
    cm = np.stack([(j < i) & same, (j <= i) & same, j == i]).astype(np.float32)
    ri = np.arange(rows)[:, None]
    rj = np.arange(rows)[None, :]
    same_r = (ri // seg_len) == (rj // seg_len)
    tri = ((rj <= ri) & same_r).astype(np.float32)
    ones_bd = np.kron(np.eye(gh), np.ones((n, n)))
    return (jnp.asarray(ones_bd, BF16), jnp.asarray(hm, BF16), jnp.asarray(cm, F32),
            jnp.asarray(tri, BF16))


def _rwkv(z, s0, p, layer, seq_len):
    n = z.shape[0]
    if seq_len >= RW_CHUNK:
        step_chunks = min(RW_STEP_CHUNKS_LONG, seq_len // RW_CHUNK)
    else:
        step_chunks = min(RW_STEP_CHUNKS_SHORT, n // RW_CHUNK)
    nseq = n // seq_len
    nh, hd = N_RWKV_HEADS, RWKV_HEAD
    seg_len = min(seq_len, RW_CHUNK)
    rows = RW_CHUNK * step_chunks
    chained = seq_len >= RW_CHUNK
    if chained:
        steps = seq_len // rows
        grid = (nseq, steps)
        row_map = lambda s, b: (s * steps + b, 0)
        state_spec = pl.BlockSpec((1, nh, hd, hd), lambda s, b: (s, 0, 0, 0))
        scratch = [pltpu.VMEM((rows, D_RWKV), F32), pltpu.VMEM((RW_GROUPS, hd, RW_GROUP_LANES), F32)]
    else:
        grid = (n // rows, 1)
        row_map = lambda s, b: (s, 0)
        state_spec = pl.BlockSpec((rows // seq_len, nh, hd, hd), lambda s, b: (s, 0, 0, 0))
        scratch = [pltpu.VMEM((rows, D_RWKV), F32)]
    if s0 is None:
        s0 = p['state_rwkv']
        s0_spec = pl.BlockSpec((None,) + state_spec.block_shape,
                               lambda s, b: (layer, s, 0, 0, 0))
    else:
        s0_spec = state_spec
    stacked = [p[k] for k in ('w0', 'w2', 'a0', 'a2', 'g2', 'k_k', 'k_a', 'r_k', 'lnx_g', 'lnx_b')]
    consts = list(_rwkv_constants(seg_len, rows))
    y, s_new = pl.pallas_call(
        functools.partial(_rwkv_kernel, seg_len=seg_len, step_chunks=step_chunks),
        grid=grid,
        in_specs=[pl.BlockSpec((rows, N_SHIFT), row_map), s0_spec]
        + [_layer_spec(t, layer) for t in stacked] + [_const_spec(t.shape) for t in consts],
        out_specs=[pl.BlockSpec((rows, D_RWKV), row_map), state_spec],
        out_shape=[jax.ShapeDtypeStruct((n, D_RWKV), F32),
                   jax.ShapeDtypeStruct((nseq, nh, hd, hd), F32)],
        scratch_shapes=scratch,
        compiler_params=pltpu.CompilerParams(dimension_semantics=("arbitrary", "arbitrary"),
                                             vmem_limit_bytes=V7X_VMEM_LIMIT_BYTES),
        name="rwkv7",
    )(z, s0, *stacked, *consts)
    return y, s_new


def _ffn_kernel(h_ref, y8_ref, u8_ref, yb_ref, pe_ref, c1_ref, c2_ref, d8_ref, wglu_ref, bglu_ref,
                wout_ref, g2_ref, wup_ref, cw_ref, cb_ref, wdn_ref, wple_ref, wpg_ref, gf_ref,
                *rest, seq_len, tm, final):
    i = pl.program_id(0)
    long_seq = seq_len >= tm
    if long_seq:
        o_ref, ga_ref, gb_ref, fold_ref, act_ref, carry_ref = rest

        @pl.when(i == 0)
        def _():
            carry_ref[...] = jnp.zeros_like(carry_ref)
    else:
        o_ref, ga_ref, gb_ref, fold_ref, act_ref = rest
    ya = y8_ref[...] + d8_ref[...] * u8_ref[...]
    c_gelu = math.sqrt(2.0 / math.pi)
    ya = ya * (0.5 * (1.0 + jnp.tanh(c_gelu * (ya + 0.044715 * (ya * ya * ya)))))
    ya = _unfold_rows(ya, fold_ref)
    ya = ya * _sigmoid(_mm(ya, wglu_ref[...]) + bglu_ref[...])
    h1 = (h_ref[...] + _mm(ya, wout_ref[:D_SSM, :]) + _mm(yb_ref[...], wout_ref[D_SSM:, :]))
    x2 = _rmsnorm(h1, g2_ref[...]).astype(BF16)
    row = lax.broadcasted_iota(jnp.int32, (tm, 1), 0)
    if long_seq:
        first = (i % (seq_len // tm)) == 0
    else:
        t = row % seq_len
    def up(c):
        cs = slice(c * FF_CHUNK, (c + 1) * FF_CHUNK)
        gs = slice(D_FF + c * FF_CHUNK, D_FF + (c + 1) * FF_CHUNK)
        return (jnp.dot(x2, wup_ref[:, cs], preferred_element_type=F32),
                jnp.dot(x2, wup_ref[:, gs], preferred_element_type=F32))

    n_chunks = D_FF // FF_CHUNK
    nxt = up(0)
    for c in range(n_chunks):
        cs = slice(c * FF_CHUNK, (c + 1) * FF_CHUNK)
        val, gate = nxt
        if c + 1 < n_chunks:
            nxt = up(c + 1)
        r1 = pltpu.roll(gate, 1, 0)
        r2 = pltpu.roll(gate, 2, 0)
        if long_seq:
            m1 = jnp.where(first, c1_ref[0][:, cs], carry_ref[1:2, cs])
            m2 = jnp.where(first, c2_ref[0][:, cs], carry_ref[0:1, cs])
            p1 = jnp.where(row == 0, m1, r1)
            p2 = jnp.where(row == 0, m2, jnp.where(row == 1, m1, r2))
            carry_ref[:, cs] = gate[tm - 2:tm, :]
            ga_ref[0, :, cs] = gate[tm - 2:tm - 1, :]
            gb_ref[0, :, cs] = gate[tm - 1:tm, :]
        else:
            m1 = _expand_rows(c1_ref[:, cs], seq_len)
            m2 = _expand_rows(c2_ref[:, cs], seq_len)
            p1 = jnp.where(t == 0, m1, r1)
            p2 = jnp.where(t == 0, m2, jnp.where(t == 1, m1, r2))
            ga_ref[:, cs] = _step_rows(gate, seq_len, seq_len - 2)
            gb_ref[:, cs] = _step_rows(gate, seq_len, seq_len - 1)
        conv = (cb_ref[:, cs] + cw_ref[2:3, cs] * gate + cw_ref[1:2, cs] * p1 + cw_ref[0:1, cs] * p2)
        act_ref[:, cs] = (conv * _sigmoid(conv) * val).astype(BF16)
    h2 = h1 + jnp.dot(act_ref[...], wdn_ref[...], preferred_element_type=F32)
    h3 = h2 + _mm(pe_ref[...], wple_ref[...]) * _sigmoid(_mm(h2, wpg_ref[...]))
    if final:
        h3 = _rmsnorm(h3, gf_ref[...])
    o_ref[...] = h3


def _ffn(h, y8, u8, yb, pe, conv0, p, layer, seq_len, tm, final):
    n = h.shape[0]
    nseq = n // seq_len
    row_spec = lambda w: pl.BlockSpec((tm, w), lambda i: (i, 0))
    pe_spec = pl.BlockSpec((None, tm, D_PLE), lambda i: (layer, i, 0))
    fold_spec = pl.BlockSpec((tm // S5_STEPS, S5_ROW), lambda i: (i, 0))
    scratch = [pltpu.VMEM((FOLD_LANE_BLOCKS, tm, LANES), F32), pltpu.VMEM((tm, D_FF), BF16)]
    if seq_len >= tm:
        tps = seq_len // tm
        seq_spec = pl.BlockSpec((1, 1, D_FF), lambda i: (i // tps, 0, 0))
        seq_shape = (nseq, 1, D_FF)
        scratch.append(pltpu.VMEM((2, D_FF), F32))
    else:
        seq_spec = pl.BlockSpec((tm // seq_len, D_FF), lambda i: (i, 0))
        seq_shape = (nseq, D_FF)
    stacked = [p[k] for k in ('d8', 'w_glu', 'b_glu', 'w_out', 'norm2_g', 'w_up', 'conv_w',
                              'conv_b', 'w_down', 'w_ple', 'w_pg')]
    consts = [p['final_g']]
    weights = stacked + consts
    out, ga, gb = pl.pallas_call(
        functools.partial(_ffn_kernel, seq_len=seq_len, tm=tm, final=final),
        grid=(n // tm,),
        in_specs=[row_spec(D_MODEL), fold_spec, fold_spec, row_spec(D_RWKV), pe_spec,
                  seq_spec, seq_spec] + [_layer_spec(t, layer) for t in stacked]
        + [_const_spec(t.shape) for t in consts],
        out_specs=[row_spec(D_MODEL), seq_spec, seq_spec],
        out_shape=[jax.ShapeDtypeStruct((n, D_MODEL), F32)]
        + [jax.ShapeDtypeStruct(seq_shape, F32)] * 2,
        scratch_shapes=scratch,
        compiler_params=pltpu.CompilerParams(dimension_semantics=("arbitrary",),
                                             vmem_limit_bytes=V7X_VMEM_LIMIT_BYTES),
        name="ffn",
    )(h, y8, u8, yb, pe, conv0[:, 1].reshape(seq_shape), conv0[:, 0].reshape(seq_shape), *weights)
    conv_new = jnp.stack([ga.reshape(nseq, D_FF), gb.reshape(nseq, D_FF)], axis=1)
    return out, conv_new


def _layer(h, pe, st, p, layer, s5_tables, seq_len, tm, final):
    ssm_re0, ssm_im0, rwkv0, shift0, conv0 = st
    u8, z, shift_new = _inproj(h, p, layer, shift0, seq_len, tm)
    y8, hre, him = _s5(u8, ssm_re0, ssm_im0, s5_tables, seq_len)
    yb, s_last = _rwkv(z, rwkv0, p, layer, seq_len)
    h, conv_new = _ffn(h, y8, u8, yb, pe, conv0, p, layer, seq_len, tm, final)
    return h, (hre, him, s_last, shift_new, conv_new)


def _stacked_params(w, state_rwkv, state_shift):
    depth = w['w_in'].shape[0]
    row = lambda t: t.reshape(depth, 1, -1).astype(F32)
    bf = lambda t: t.astype(BF16)
    return {
        'norm1_g': row(w['norm1_g']), 'w_in': bf(w['w_in']), 'shift_mu': row(w['shift_mu']),
        'd8': jnp.tile(row(w['ssm_d']), (1, 1, S5_STEPS)), 'w_glu': bf(w['ssm_w_glu']),
        'b_glu': row(w['ssm_b_glu']),
        'w0': row(w['rwkv_w0']), 'w2': bf(w['rwkv_w2']), 'a0': row(w['rwkv_a0']),
        'a2': bf(w['rwkv_a2']), 'g2': bf(w['rwkv_g2']), 'k_k': row(w['rwkv_k_k']),
        'k_a': row(w['rwkv_k_a']), 'r_k': row(w['rwkv_r_k']), 'lnx_g': row(w['rwkv_lnx_g']),
        'lnx_b': row(w['rwkv_lnx_b']), 'w_out': bf(w['w_out']), 'norm2_g': row(w['norm2_g']),
        'w_up': bf(w['w_up']), 'conv_w': w['conv_w'].astype(F32), 'conv_b': row(w['conv_b']),
        'w_down': bf(w['w_down']), 'w_ple': bf(w['w_ple']), 'w_pg': bf(w['w_pg']),
        'final_g': w['final_g'].reshape(1, -1).astype(F32),
        'state_rwkv': state_rwkv.astype(F32), 'state_shift': state_shift.astype(F32),
    }


def _forward(x_prompt, x_sample, p_prompt, p_sample, state_ssm_re, state_ssm_im, state_rwkv,
             state_shift, state_conv, w, tm_prompt, tm_sample):
    depth = w['w_in'].shape[0]
    bp, lp, _ = x_prompt.shape
    bs, ls, _ = x_sample.shape
    hp = x_prompt.reshape(bp * lp, D_MODEL).astype(F32)
    hs = x_sample.reshape(bs * ls, D_MODEL).astype(F32)
    pe_p = p_prompt.reshape(depth, bp * lp, D_PLE).astype(F32)
    pe_s = p_sample.reshape(depth, bs * ls, D_PLE).astype(F32)
    zero_st = (jnp.zeros((bp, N_SSM_GROUPS, SSM_STATE), F32),
               jnp.zeros((bp, N_SSM_GROUPS, SSM_STATE), F32),
               jnp.zeros((bp, N_RWKV_HEADS, RWKV_HEAD, RWKV_HEAD), F32),
               jnp.zeros((bp, N_SHIFT), F32),
               jnp.zeros((bp, 2, D_FF), F32))
    p = _stacked_params(w, state_rwkv, state_shift)
    new_p = [[] for _ in range(5)]
    new_s = [[] for _ in range(5)]
    for i in range(depth):
        tables = _s5_tables(w['ssm_lam_re'][i], w['ssm_lam_im'][i], w['ssm_log_dt'][i],
                            w['ssm_b_re'][i], w['ssm_b_im'][i], w['ssm_c_re'][i], w['ssm_c_im'][i])
        final = i == depth - 1
        hp, stp = _layer(hp, pe_p, zero_st, p, i, tables, lp, tm_prompt, final)
        st_in = (state_ssm_re[i].astype(F32), state_ssm_im[i].astype(F32), None, None,
                 state_conv[i].astype(F32))
        hs, sts = _layer(hs, pe_s, st_in, p, i, tables, ls, tm_sample, final)
        for j in range(5):
            new_p[j].append(stp[j])
            new_s[j].append(sts[j])
    y_prompt = hp.reshape(bp, lp, D_MODEL).astype(x_prompt.dtype)
    y_sample = hs.reshape(bs, ls, D_MODEL).astype(x_sample.dtype)
    dts = (state_ssm_re.dtype, state_ssm_im.dtype, state_rwkv.dtype, state_shift.dtype,
           state_conv.dtype)
    outs_p = tuple(jnp.stack(new_p[j]).astype(dts[j]) for j in range(5))
    outs_s = tuple(jnp.stack(new_s[j]).astype(dts[j]) for j in range(5))
    return (y_prompt, y_sample) + outs_p + outs_s


def kernel(x_prompt, x_sample, p_prompt, p_sample, state_ssm_re, state_ssm_im, state_rwkv, state_shift, state_conv, norm1_g, w_in, shift_mu, ssm_lam_re, ssm_lam_im, ssm_log_dt, ssm_b_re, ssm_b_im, ssm_c_re, ssm_c_im, ssm_d, ssm_w_glu, ssm_b_glu, rwkv_w0, rwkv_w2, rwkv_a0, rwkv_a2, rwkv_g2, rwkv_k_k, rwkv_k_a, rwkv_r_k, rwkv_lnx_g, rwkv_lnx_b, w_out, norm2_g, w_up, conv_w, conv_b, w_down, w_ple, w_pg, final_g):
    w = dict(norm1_g=norm1_g, w_in=w_in, shift_mu=shift_mu, ssm_lam_re=ssm_lam_re,
             ssm_lam_im=ssm_lam_im, ssm_log_dt=ssm_log_dt, ssm_b_re=ssm_b_re, ssm_b_im=ssm_b_im,
             ssm_c_re=ssm_c_re, ssm_c_im=ssm_c_im, ssm_d=ssm_d, ssm_w_glu=ssm_w_glu,
             ssm_b_glu=ssm_b_glu, rwkv_w0=rwkv_w0, rwkv_w2=rwkv_w2, rwkv_a0=rwkv_a0,
             rwkv_a2=rwkv_a2, rwkv_g2=rwkv_g2, rwkv_k_k=rwkv_k_k, rwkv_k_a=rwkv_k_a,
             rwkv_r_k=rwkv_r_k, rwkv_lnx_g=rwkv_lnx_g, rwkv_lnx_b=rwkv_lnx_b, w_out=w_out,
             norm2_g=norm2_g, w_up=w_up, conv_w=conv_w, conv_b=conv_b, w_down=w_down,
             w_ple=w_ple, w_pg=w_pg, final_g=final_g)
    lp = x_prompt.shape[1]
    ns = x_sample.shape[0] * x_sample.shape[1]
    return _forward(x_prompt, x_sample, p_prompt, p_sample, state_ssm_re, state_ssm_im,
                    state_rwkv, state_shift, state_conv, w,
                    tm_prompt=min(512, lp), tm_sample=min(256, ns))
```

```python
import functools
import math

import numpy as np
import jax
import jax.numpy as jnp
from jax import lax
from jax.experimental import pallas as pl
from jax.experimental.pallas import tpu as pltpu

F32 = jnp.float32
BF16 = jnp.bfloat16
HIGHEST = lax.Precision.HIGHEST

D_MODEL = 1024
D_SSM = 512
D_RWKV = 512
SSM_GROUP = 16
N_SSM_GROUPS = 32
SSM_STATE = 64
RWKV_HEAD = 64
N_RWKV_HEADS = 8
DECAY_LORA = 64
AAA_LORA = 64
GATE_LORA = 128
N_SHIFT = 3 * D_RWKV + DECAY_LORA + AAA_LORA + GATE_LORA
N_IN = D_SSM + N_SHIFT
D_FF = 2816
D_PLE = 256
RMS_EPS = 1e-6
GN_EPS = 64e-5
L2_EPS = 1e-12

LANES = 128
FOLD_LANE_BLOCKS = D_SSM // LANES
S5_STEPS = 8
S5_ROW = S5_STEPS * D_SSM
S5_LANE_BLOCKS = 4
S5_BLOCK_STATE = 2 * 8 * SSM_STATE
S5_STATE_ROW = S5_LANE_BLOCKS * S5_BLOCK_STATE
FF_CHUNK = 256
V7X_VMEM_LIMIT_BYTES = 56 * 1024 * 1024


def _mm(a, b):
    return jnp.dot(a.astype(BF16), b.astype(BF16), preferred_element_type=F32)


def _mm_nt(a, b):
    return lax.dot_general(a.astype(BF16), b.astype(BF16), (((1,), (1,)), ((), ())),
                           preferred_element_type=F32)


def _mm_tn(a, b):
    return lax.dot_general(a.astype(BF16), b.astype(BF16), (((0,), (0,)), ((), ())),
                           preferred_element_type=F32)


def _seg_sum(x, ones_bd):
    hi = x.astype(BF16)
    lo = (x - hi.astype(F32)).astype(BF16)
    gl = ones_bd.shape[0]
    return jnp.concatenate(
        [jnp.dot(hi[:, g:g + gl], ones_bd, preferred_element_type=F32)
         + jnp.dot(lo[:, g:g + gl], ones_bd, preferred_element_type=F32)
         for g in range(0, x.shape[1], gl)], axis=1)


def _rmsnorm(x, g):
    return x * lax.rsqrt(jnp.mean(x * x, axis=-1, keepdims=True) + RMS_EPS) * g


def _sigmoid(x):
    return 1.0 / (1.0 + jnp.exp(-x))


def _const_spec(shape):
    nd = len(shape)
    return pl.BlockSpec(shape, lambda *_: (0,) * nd, pipeline_mode=pl.Buffered(1))


def _layer_spec(t, layer):
    if t.ndim == 2:
        return _const_spec(t.shape)
    nd = t.ndim - 1
    return pl.BlockSpec((None,) + t.shape[1:], lambda *_: (layer,) + (0,) * nd,
                        pipeline_mode=pl.Buffered(1))


def _expand_rows(x, seq_len):
    nseq, w = x.shape
    return jnp.broadcast_to(x[:, None, :], (nseq, seq_len, w)).reshape(nseq * seq_len, w)


def _step_rows(x, seq_len, t):
    return x.reshape(x.shape[0] // seq_len, seq_len, x.shape[1])[:, t, :]


def _fold_rows(x, scr_ref, out_ref):
    tm = x.shape[0]
    for q in range(FOLD_LANE_BLOCKS):
        scr_ref[q] = x[:, q * LANES:(q + 1) * LANES]
    for s in range(S5_STEPS):
        for q in range(FOLD_LANE_BLOCKS):
            c0 = s * D_SSM + q * LANES
            out_ref[:, c0:c0 + LANES] = scr_ref[q, pl.ds(s, tm // S5_STEPS, stride=S5_STEPS), :]


def _unfold_rows(x8, scr_ref):
    tm = x8.shape[0] * S5_STEPS
    for s in range(S5_STEPS):
        for q in range(FOLD_LANE_BLOCKS):
            c0 = s * D_SSM + q * LANES
            scr_ref[q, pl.ds(s, tm // S5_STEPS, stride=S5_STEPS), :] = x8[:, c0:c0 + LANES]
    return jnp.concatenate([scr_ref[q] for q in range(FOLD_LANE_BLOCKS)], axis=1)


def _inproj_kernel(h_ref, g_ref, w_ref, mu_ref, init_ref, *rest, seq_len, tm, layer):
    i = pl.program_id(0)
    row_of = lambda ref: ref[layer:layer + 1, :]
    long_seq = seq_len >= tm
    if long_seq:
        u8_ref, z_ref, last_ref, fold_ref, carry_ref = rest

        @pl.when(i == 0)
        def _():
            carry_ref[...] = jnp.zeros_like(carry_ref)
    else:
        u8_ref, z_ref, last_ref, fold_ref = rest
    xn = _rmsnorm(h_ref[...], row_of(g_ref))
    proj = jnp.dot(xn.astype(BF16), w_ref[...], preferred_element_type=F32)
    _fold_rows(proj[:, :D_SSM], fold_ref, u8_ref)
    zr = proj[:, D_SSM:]
    rolled = pltpu.roll(zr, 1, 0)
    row = lax.broadcasted_iota(jnp.int32, (tm, 1), 0)
    if long_seq:
        first = (i % (seq_len // tm)) == 0
        row0 = jnp.where(first, init_ref[0], carry_ref[...])
        prev = jnp.where(row == 0, row0, rolled)
        carry_ref[...] = zr[tm - 1:tm, :]
        last_ref[0] = zr[tm - 1:tm, :]
    else:
        prev = jnp.where(row % seq_len == 0, _expand_rows(init_ref[...], seq_len), rolled)
        last_ref[...] = _step_rows(zr, seq_len, seq_len - 1)
    z_ref[...] = (zr + (prev - zr) * row_of(mu_ref)).astype(z_ref.dtype)


def _inproj(h, p, layer, shift0, seq_len, tm):
    n = h.shape[0]
    nseq = n // seq_len
    fold_scratch = pltpu.VMEM((FOLD_LANE_BLOCKS, tm, LANES), F32)
    if seq_len >= tm:
        tps = seq_len // tm
        seq_spec = pl.BlockSpec((1, 1, N_SHIFT), lambda i: (i // tps, 0, 0))
        init_spec = seq_spec
        init = shift0.reshape(nseq, 1, N_SHIFT)
        last_shape = jax.ShapeDtypeStruct((nseq, 1, N_SHIFT), F32)
        scratch = [fold_scratch, pltpu.VMEM((1, N_SHIFT), F32)]
    else:
        spt = tm // seq_len
        seq_spec = pl.BlockSpec((spt, N_SHIFT), lambda i: (i, 0))
        init_spec = pl.BlockSpec((None, spt, N_SHIFT), lambda i: (layer, i, 0))
        init = p['state_shift']
        last_shape = jax.ShapeDtypeStruct((nseq, N_SHIFT), F32)
        scratch = [fold_scratch]
    g1, w_in, mu = p['norm1_g'], p['w_in'], p['shift_mu']
    u8, z, last = pl.pallas_call(
        functools.partial(_inproj_kernel, seq_len=seq_len, tm=tm, layer=layer),
        grid=(n // tm,),
        in_specs=[pl.BlockSpec((tm, D_MODEL), lambda i: (i, 0)),
                  _layer_spec(g1, layer), _layer_spec(w_in, layer), _layer_spec(mu, layer),
                  init_spec],
        out_specs=[pl.BlockSpec((tm // S5_STEPS, S5_ROW), lambda i: (i, 0)),
                   pl.BlockSpec((tm, N_SHIFT), lambda i: (i, 0)),
                   seq_spec],
        out_shape=[jax.ShapeDtypeStruct((n // S5_STEPS, S5_ROW), F32),
                   jax.ShapeDtypeStruct((n, N_SHIFT), BF16),
                   last_shape],
        scratch_shapes=scratch,
        compiler_params=pltpu.CompilerParams(dimension_semantics=("arbitrary",),
                                             vmem_limit_bytes=V7X_VMEM_LIMIT_BYTES),
        name="inproj",
    )(h, g1, w_in, mu, init)
    return u8, z, last.reshape(nseq, N_SHIFT)


def _s5_prep_kernel(lr_ref, li_ref, ldt_ref, br_ref, bi_ref, cr_ref, ci_ref, tile_ref, sel_ref,
                    wx_ref, wy_ref, apr_ref, api_ref):
    lr, li = lr_ref[...], li_ref[...]
    dt = jnp.exp(ldt_ref[...])
    mag = jnp.exp(lr * dt)
    ar = mag * jnp.cos(li * dt)
    ai = mag * jnp.sin(li * dt)
    den = lr * lr + li * li
    nr = ar - 1.0
    fr = (nr * lr + ai * li) / den
    fi = (ai * lr - nr * li) / den
    br, bi = br_ref[...], bi_ref[...]
    bbr = fr * br - fi * bi
    bbi = fr * bi + fi * br
    cr, ci = cr_ref[...], ci_ref[...]
    pr = [jnp.ones_like(ar)]
    pi = [jnp.zeros_like(ar)]
    for _ in range(S5_STEPS):
        pr.append(pr[-1] * ar - pi[-1] * ai)
        pi.append(pr[-2] * ai + pi[-1] * ar)
    nrow = lr.shape[0]
    blk = nrow // S5_LANE_BLOCKS
    half = S5_BLOCK_STATE // 2
    ri = lax.broadcasted_iota(jnp.int32, (nrow, nrow), 0)
    ci_ = lax.broadcasted_iota(jnp.int32, (nrow, nrow), 1)
    same_group = (ri // SSM_GROUP) == (ci_ // SSM_GROUP)
    keep_in = ((ri % blk) // SSM_GROUP) == (ci_ // SSM_STATE)
    keep_out = (ri // SSM_STATE) == ((ci_ % blk) // SSM_GROUP)
    tile, sel = tile_ref[...], sel_ref[...]
    nt = (((1,), (1,)), ((), ()))
    wy_ref[...] = jnp.zeros(wy_ref.shape, wy_ref.dtype)
    for s in range(S5_STEPS):
        qr, qi = pr[S5_STEPS - 1 - s], pi[S5_STEPS - 1 - s]
        wr = qr * bbr - qi * bbi
        wi = qr * bbi + qi * bbr
        for h, w_ in enumerate((wr, wi)):
            t = jnp.dot(w_.astype(BF16), tile, preferred_element_type=F32)
            t = jnp.where(keep_in, t, 0.0).astype(BF16)
            for q in range(S5_LANE_BLOCKS):
                wx_ref[q, s * blk:(s + 1) * blk, h * half:(h + 1) * half] = t[q * blk:(q + 1) * blk, :]
        tau = S5_STEPS - 1 - s
        kt = (lax.dot_general(wr, cr, nt, precision=HIGHEST, preferred_element_type=F32)
              - lax.dot_general(wi, ci, nt, precision=HIGHEST, preferred_element_type=F32))
        kt = jnp.where(same_group, kt, 0.0).astype(BF16)
        for q in range(S5_LANE_BLOCKS):
            kq = kt[q * blk:(q + 1) * blk, q * blk:(q + 1) * blk]
            for s_in in range(S5_STEPS - tau):
                s_out = s_in + tau
                r0 = S5_BLOCK_STATE + s_in * blk
                wy_ref[q, s_out // 2, r0:r0 + blk, (s_out % 2) * blk:(s_out % 2 + 1) * blk] = kq
        mr = cr * pr[s + 1] - ci * pi[s + 1]
        mi = cr * pi[s + 1] + ci * pr[s + 1]
        for h, m_ in enumerate((mr, -mi)):
            t = lax.dot_general(sel, m_.astype(BF16), nt, preferred_element_type=F32)
            t = jnp.where(keep_out, t, 0.0).astype(BF16)
            for q in range(S5_LANE_BLOCKS):
                wy_ref[q, s // 2, h * half:(h + 1) * half, (s % 2) * blk:(s % 2 + 1) * blk] = (
                    t[:, q * blk:(q + 1) * blk])
    a8r, a8i = pr[S5_STEPS], pi[S5_STEPS]
    er, ei = a8r, a8i
    for n in range(S5_STEPS):
        apr_ref[n] = er
        api_ref[n] = ei
        er, ei = er * a8r - ei * a8i, er * a8i + ei * a8r


def _s5_tables(lam_re, lam_im, log_dt, b_re, b_im, c_re, c_im):
    G, P, K = N_SSM_GROUPS, SSM_STATE, SSM_GROUP
    Q, GB = S5_LANE_BLOCKS, G // S5_LANE_BLOCKS
    rep = lambda t: jnp.repeat(t, K, axis=0)
    tile = np.tile(np.eye(P, dtype=np.float32), (1, GB))
    args = (rep(lam_re), rep(lam_im), rep(jnp.broadcast_to(log_dt[:, None], (G, P))),
            jnp.swapaxes(b_re, 1, 2).reshape(G * K, P), jnp.swapaxes(b_im, 1, 2).reshape(G * K, P),
            c_re.reshape(G * K, P), c_im.reshape(G * K, P),
            jnp.asarray(tile, BF16), jnp.asarray(tile.T, BF16))
    t3 = jax.ShapeDtypeStruct((S5_STEPS, G * K, P), F32)
    wx, wy, apr, api = pl.pallas_call(
        _s5_prep_kernel,
        out_shape=[jax.ShapeDtypeStruct((Q, S5_STEPS * GB * K, S5_BLOCK_STATE), BF16),
                   jax.ShapeDtypeStruct((Q, S5_STEPS // 2, S5_BLOCK_STATE + S5_STEPS * GB * K,
                                         2 * GB * K), BF16),
                   t3, t3],
        compiler_params=pltpu.CompilerParams(vmem_limit_bytes=V7X_VMEM_LIMIT_BYTES),
        name="s5_prep",
    )(*args)
    ap = jnp.concatenate([t[:, ::K, :].reshape(S5_STEPS, Q, GB * P) for t in (apr, api)], axis=-1)
    return wx, wy, ap.reshape(S5_STEPS, S5_STATE_ROW)


def _state_to_lanes(h_re, h_im):
    n = h_re.shape[0]
    parts = [t.reshape(n, S5_LANE_BLOCKS, -1) for t in (h_re, h_im)]
    return jnp.concatenate(parts, axis=-1).reshape(n, S5_STATE_ROW)


def _lanes_to_state(h):
    n = h.shape[0]
    h = h.reshape(n, S5_LANE_BLOCKS, 2, N_SSM_GROUPS // S5_LANE_BLOCKS, SSM_STATE)
    return (h[:, :, 0].reshape(n, N_SSM_GROUPS, SSM_STATE), h[:, :, 1].reshape(n, N_SSM_GROUPS, SSM_STATE))


def _s5_kernel(u_ref, h0_ref, wx_ref, wy_ref, ap_ref, y_ref, hl_ref, *scratch, rows, scan):
    half = S5_BLOCK_STATE // 2
    cw = D_SSM // S5_LANE_BLOCKS
    blocks = range(S5_LANE_BLOCKS)
    ub = u_ref[...].astype(BF16)
    ucat = [jnp.concatenate([ub[:, s * D_SSM + q * cw: s * D_SSM + (q + 1) * cw]
                             for s in range(S5_STEPS)], axis=1) for q in blocks]
    x = [jnp.dot(ucat[q], wx_ref[q], preferred_element_type=F32) for q in blocks]
    ap = [ap_ref[:, q * S5_BLOCK_STATE:(q + 1) * S5_BLOCK_STATE] for q in blocks]

    def emit_outputs(q, hprev):
        lhs = jnp.concatenate([hprev, ucat[q]], axis=1)
        for j in range(S5_STEPS // 2):
            kk = S5_BLOCK_STATE + cw * (2 * j + 2)
            y2 = jnp.dot(lhs[:, :kk], wy_ref[q, j, :kk, :], preferred_element_type=F32)
            c0 = (2 * j) * D_SSM + q * cw
            c1 = (2 * j + 1) * D_SSM + q * cw
            y_ref[:, c0:c0 + cw] = y2[:, :cw]
            y_ref[:, c1:c1 + cw] = y2[:, cw:]

    if scan:
        hs_ref, hp_ref = scratch
        sub3 = lax.broadcasted_iota(jnp.int32, (1, 8, 1), 1)
        sub2 = lax.broadcasted_iota(jnp.int32, (8, 1), 0)
        for q in blocks:
            xr = x[q][:, :half].reshape(rows // 8, 8, half)
            xi = x[q][:, half:].reshape(rows // 8, 8, half)
            for k in (1, 2, 4):
                er = jnp.where(sub3 >= k, ap[q][k - 1:k, :half].reshape(1, 1, half), 0.0)
                ei = jnp.where(sub3 >= k, ap[q][k - 1:k, half:].reshape(1, 1, half), 0.0)
                sr = pltpu.roll(xr, k, 1)
                si = pltpu.roll(xi, k, 1)
                xr, xi = xr + er * sr - ei * si, xi + er * si + ei * sr
            hs_ref[q, :, :half] = xr.reshape(rows, half)
            hs_ref[q, :, half:] = xi.reshape(rows, half)
        h0 = h0_ref[0]
        for q in blocks:
            lo = q * S5_BLOCK_STATE
            cr, ci = h0[:, lo:lo + half], h0[:, lo + half:lo + S5_BLOCK_STATE]
            pwr, pwi = ap[q][:, :half], ap[q][:, half:]
            for j in range(rows // 8):
                sl = slice(j * 8, (j + 1) * 8)
                hr = hs_ref[q, sl, :half] + pwr * cr - pwi * ci
                hi = hs_ref[q, sl, half:] + pwr * ci + pwi * cr
                hp_ref[q, sl, :half] = jnp.where(sub2 == 0, cr, pltpu.roll(hr, 1, 0))
                hp_ref[q, sl, half:] = jnp.where(sub2 == 0, ci, pltpu.roll(hi, 1, 0))
                cr, ci = hr[7:8, :], hi[7:8, :]
            hl_ref[0, :, lo:lo + half] = cr
            hl_ref[0, :, lo + half:lo + S5_BLOCK_STATE] = ci
            emit_outputs(q, hp_ref[q].astype(BF16))
    else:
        for q in blocks:
            lo = q * S5_BLOCK_STATE
            h0r = h0_ref[:, lo:lo + half]
            h0i = h0_ref[:, lo + half:lo + S5_BLOCK_STATE]
            er, ei = ap[q][0:1, :half], ap[q][0:1, half:]
            hl_ref[:, lo:lo + half] = er * h0r - ei * h0i + x[q][:, :half]
            hl_ref[:, lo + half:lo + S5_BLOCK_STATE] = er * h0i + ei * h0r + x[q][:, half:]
            emit_outputs(q, jnp.concatenate([h0r, h0i], axis=1).astype(BF16))


def _s5(u8, h_re, h_im, tables, seq_len):
    wx, wy, ap = tables
    n = u8.shape[0] * S5_STEPS
    nseq = n // seq_len
    h0 = _state_to_lanes(h_re, h_im)
    cps = seq_len // S5_STEPS
    scan = cps > 1
    if scan:
        rows = cps
        grid = (nseq,)
        h0 = h0.reshape(nseq, 1, S5_STATE_ROW)
        h_spec = pl.BlockSpec((1, 1, S5_STATE_ROW), lambda i: (i, 0, 0))
        h_shape = jax.ShapeDtypeStruct((nseq, 1, S5_STATE_ROW), F32)
    else:
        rows = min(nseq, 128)
        grid = (nseq // rows,)
        h_spec = pl.BlockSpec((rows, S5_STATE_ROW), lambda i: (i, 0))
        h_shape = jax.ShapeDtypeStruct((nseq, S5_STATE_ROW), F32)
    y8, hl = pl.pallas_call(
        functools.partial(_s5_kernel, rows=rows, scan=scan),
        grid=grid,
        in_specs=[pl.BlockSpec((rows, S5_ROW), lambda i: (i, 0)),
                  h_spec,
                  _const_spec(wx.shape), _const_spec(wy.shape), _const_spec(ap.shape)],
        out_specs=[pl.BlockSpec((rows, S5_ROW), lambda i: (i, 0)), h_spec],
        out_shape=[jax.ShapeDtypeStruct((n // S5_STEPS, S5_ROW), F32), h_shape],
        scratch_shapes=[pltpu.VMEM((S5_LANE_BLOCKS, rows, S5_BLOCK_STATE), F32)] * 2 if scan else [],
        compiler_params=pltpu.CompilerParams(dimension_semantics=("arbitrary",),
                                             vmem_limit_bytes=V7X_VMEM_LIMIT_BYTES),
        name="s5",
    )(u8, h0, wx, wy, ap)
    hre_new, him_new = _lanes_to_state(hl.reshape(nseq, S5_STATE_ROW))
    return y8, hre_new, him_new


RW_CHUNK = 64
RW_STEP_CHUNKS_LONG = 4
RW_STEP_CHUNKS_SHORT = 4
RW_GROUPS = 2
RW_GROUP_HEADS = N_RWKV_HEADS // RW_GROUPS
RW_GROUP_LANES = D_RWKV // RW_GROUPS


def _split3(x):
    hi = x.astype(BF16)
    r1 = x - hi.astype(F32)
    mid = r1.astype(BF16)
    lo = (r1 - mid.astype(F32)).astype(BF16)
    return hi, mid, lo


def _bd(x, hm_ref):
    xb = x.astype(BF16)
    return jnp.concatenate([xb * hm_ref[h] for h in range(RW_GROUP_HEADS)], axis=0)


def _unbd(f, hm_ref):
    n = RWKV_HEAD
    out = f[:n] * hm_ref[0].astype(F32)
    for h in range(1, RW_GROUP_HEADS):
        out = out + f[h * n:(h + 1) * n] * hm_ref[h].astype(F32)
    return out


def _rwkv_kernel(z_ref, s0_ref, w0_ref, w2_ref, a0_ref, a2_ref, g2_ref, kk_ref, ka_ref, rk_ref,
                 lg_ref, lb_ref, ones_ref, hm_ref, cm_ref, tri_ref, y_ref, sl_ref, ypre_ref,
                 *scratch, seg_len, step_chunks, layer):
    row_of = lambda ref: ref[layer:layer + 1, :]
    C = RW_CHUNK
    N = RWKV_HEAD
    GL = RW_GROUP_LANES
    chained = seg_len == C
    if chained:
        (st_ref,) = scratch
        b = pl.program_id(1)

        @pl.when(b == 0)
        def _():
            for g in range(RW_GROUPS):
                st_ref[g] = jnp.concatenate(
                    [s0_ref[0, h] for h in range(g * RW_GROUP_HEADS, (g + 1) * RW_GROUP_HEADS)], axis=1)

    ones_bd = ones_ref[...]
    z = z_ref[...].astype(F32)
    r = z[:, :D_RWKV]
    k = z[:, D_RWKV:2 * D_RWKV]
    v = z[:, 2 * D_RWKV:3 * D_RWKV]
    o = 3 * D_RWKV
    xw = z[:, o:o + DECAY_LORA]
    xa = z[:, o + DECAY_LORA:o + DECAY_LORA + AAA_LORA]
    xg = z[:, o + DECAY_LORA + AAA_LORA:]
    wd = -(row_of(w0_ref) + _mm(jnp.tanh(xw), w2_ref[...]))
    w = -(jnp.maximum(wd, 0.0) + jnp.log1p(jnp.exp(-jnp.abs(wd)))) - 0.5
    lw = -jnp.exp(w)
    a = _sigmoid(row_of(a0_ref) + _mm(xa, a2_ref[...]))
    out_gate = _mm(_sigmoid(xg), g2_ref[...])
    kk = k * row_of(kk_ref)
    kk = kk / jnp.maximum(jnp.sqrt(_seg_sum(kk * kk, ones_bd)), L2_EPS)
    kmod = k * (1.0 + (a - 1.0) * row_of(ka_ref))

    tri = tri_ref[...]
    cw = sum(jnp.dot(tri, t, preferred_element_type=F32) for t in _split3(lw))
    tot = _expand_rows(_step_rows(cw, seg_len, seg_len - 1), seg_len)
    w_in = jnp.exp(cw)
    w_tail = jnp.exp(tot - cw)
    w_inv = jnp.exp(-cw)
    w_all = jnp.exp(tot)
    ah = -kk * jnp.exp(cw - lw)
    bh = kk * a * w_inv
    kh = kmod * w_inv
    rh = r * w_in
    bt = kk * a * w_tail
    kt = kmod * w_tail
    strict = cm_ref[0] > 0.0
    incl = cm_ref[1] > 0.0
    eye_cat = cm_ref[2]
    n_fac = seg_len.bit_length() - 1

    items = [(c, g) for c in range(step_chunks) for g in range(RW_GROUPS)]
    sub = lambda t, c, g: t[c * C:(c + 1) * C, g * GL:(g + 1) * GL]
    bd = lambda t: _bd(t, hm_ref)
    A = [sub(ah, c, g) for c, g in items]
    R = [sub(rh, c, g) for c, g in items]
    V = [sub(v, c, g) for c, g in items]
    G = [_mm_nt(jnp.concatenate([A[i], R[i]], axis=0),
                jnp.concatenate([bd(sub(bh, c, g)), bd(sub(kh, c, g))], axis=0))
         for i, (c, g) in enumerate(items)]
    AB = [jnp.where(strict, t[:C, :GL], 0.0) for t in G]
    AK = [jnp.where(strict, t[:C, GL:], 0.0) for t in G]
    RB = [jnp.where(incl, t[C:, :GL], 0.0) for t in G]
    RK = [jnp.where(incl, t[C:, GL:], 0.0) for t in G]
    KV = [_mm(jnp.concatenate([AK[i], RK[i]], axis=0), bd(V[i])) for i in range(len(items))]
    T = [eye_cat + t for t in AB]
    P = [_mm(t, bd(t)) for t in AB]
    for _ in range(n_fac - 2):
        PT = [_mm(jnp.concatenate([P[i], T[i]], axis=0), bd(P[i])) for i in range(len(items))]
        P = [t[:C] for t in PT]
        T = [T[i] + PT[i][C:] for i in range(len(items))]
    T = [T[i] + _mm(T[i], bd(P[i])) for i in range(len(items))]
    RBT = [_mm(RB[i], bd(T[i])) for i in range(len(items))]
    X = [_mm(jnp.concatenate([T[i], RBT[i]], axis=0),
             jnp.concatenate([bd(A[i]), bd(KV[i][:C])], axis=1)) for i in range(len(items))]
    TA = [t[:C, :GL] for t in X]
    U0 = [t[:C, GL:] for t in X]
    Rt = [R[i] + X[i][C:, :GL] for i in range(len(items))]
    Y0 = [X[i][C:, GL:] + KV[i][C:] for i in range(len(items))]
    nseg = C // seg_len
    heads = lambda g: range(g * RW_GROUP_HEADS, (g + 1) * RW_GROUP_HEADS)
    head_lanes = lambda h: slice((h % RW_GROUP_HEADS) * N, (h % RW_GROUP_HEADS + 1) * N)
    if chained:
        PhiT, PsiT = [], []
        for i, (c, g) in enumerate(items):
            Bt, Kt, Wa = sub(bt, c, g), sub(kt, c, g), sub(w_all, c, g)
            PhiT.append(bd(eye_cat * Wa[0:1, :] + _unbd(_mm_tn(TA[i], Bt), hm_ref)))
            PsiT.append(_unbd(_mm_tn(jnp.concatenate([U0[i], V[i]], axis=0),
                                     jnp.concatenate([Bt, Kt], axis=0)), hm_ref))
        states = [st_ref[g] for g in range(RW_GROUPS)]
        for i, (c, g) in enumerate(items):
            S = states[g]
            ypre_ref[c * C:(c + 1) * C, g * GL:(g + 1) * GL] = _mm_nt(Rt[i], bd(S)) + Y0[i]
            states[g] = _mm(S, PhiT[i]) + PsiT[i]
        for g in range(RW_GROUPS):
            st_ref[g] = states[g]
            for h in heads(g):
                sl_ref[0, h] = states[g][:, head_lanes(h)]
    else:
        segs = [(i, c, g, s) for i, (c, g) in enumerate(items) for s in range(nseg)]
        rows_of = lambda s: slice(s * seg_len, (s + 1) * seg_len)
        S0 = [jnp.concatenate([s0_ref[c * nseg + s, h] for h in heads(g)], axis=1)
              for i, c, g, s in segs]
        RU = [_mm_nt(jnp.concatenate([Rt[i][rows_of(s)], TA[i][rows_of(s)]], axis=0), bd(S0[n]))
              for n, (i, c, g, s) in enumerate(segs)]
        for n, (i, c, g, s) in enumerate(segs):
            ypre_ref[c * C + s * seg_len:c * C + (s + 1) * seg_len, g * GL:(g + 1) * GL] = (
                RU[n][:seg_len] + Y0[i][rows_of(s)])
        UB = [_mm_tn(jnp.concatenate([RU[n][seg_len:] + U0[i][rows_of(s)], V[i][rows_of(s)]], axis=0),
                     jnp.concatenate([sub(bt, c, g)[rows_of(s)], sub(kt, c, g)[rows_of(s)]], axis=0))
              for n, (i, c, g, s) in enumerate(segs)]
        for n, (i, c, g, s) in enumerate(segs):
            S = S0[n] * sub(w_all, c, g)[s * seg_len:s * seg_len + 1, :] + _unbd(UB[n], hm_ref)
            for h in heads(g):
                sl_ref[c * nseg + s, h] = S[:, head_lanes(h)]
    y = ypre_ref[...]

    mu = _seg_sum(y, ones_bd) * (1.0 / N)
    d = y - mu
    var = _seg_sum(d * d, ones_bd) * (1.0 / N)
    yn = d * lax.rsqrt(var + GN_EPS) * row_of(lg_ref) + row_of(lb_ref)
    bonus = _seg_sum(r * kmod * row_of(rk_ref), ones_bd) * v
    y_ref[...] = (yn + bonus) * out_gate


def _rwkv_constants(seg_len, rows):
    n, gh, gl = RWKV_HEAD, RW_GROUP_HEADS, RW_GROUP_LANES
    lane_head = np.arange(gl) // n
    hm = (lane_head[None, None, :] == np.arange(gh)[:, None, None]) * np.ones((gh, n, gl))
    i = np.arange(RW_CHUNK)[:, None]
    j = (np.arange(gl) % n)[None, :]
    same = (i // seg_len) == (j // seg_len)
    cm = np.stack([(j < i) & same, (j <= i) & same, j == i]).astype(np.float32)
    ri = np.arange(rows)[:, None]
    rj = np.arange(rows)[None, :]
    same_r = (ri // seg_len) == (rj // seg_len)
    tri = ((rj <= ri) & same_r).astype(np.float32)
    ones_bd = np.kron(np.eye(gh), np.ones((n, n)))
    return (jnp.asarray(ones_bd, BF16), jnp.asarray(hm, BF16), jnp.asarray(cm, F32),
            jnp.asarray(tri, BF16))


def _rwkv(z, s0, p, layer, seq_len):
    n = z.shape[0]
    if seq_len >= RW_CHUNK:
        step_chunks = min(RW_STEP_CHUNKS_LONG, seq_len // RW_CHUNK)
    else:
        step_chunks = min(RW_STEP_CHUNKS_SHORT, n // RW_CHUNK)
    nseq = n // seq_len
    nh, hd = N_RWKV_HEADS, RWKV_HEAD
    seg_len = min(seq_len, RW_CHUNK)
    rows = RW_CHUNK * step_chunks
    chained = seq_len >= RW_CHUNK
    if chained:
        steps = seq_len // rows
        grid = (nseq, steps)
        row_map = lambda s, b: (s * steps + b, 0)
        state_spec = pl.BlockSpec((1, nh, hd, hd), lambda s, b: (s, 0, 0, 0))
        scratch = [pltpu.VMEM((rows, D_RWKV), F32), pltpu.VMEM((RW_GROUPS, hd, RW_GROUP_LANES), F32)]
    else:
        grid = (n // rows, 1)
        row_map = lambda s, b: (s, 0)
        state_spec = pl.BlockSpec((rows // seq_len, nh, hd, hd), lambda s, b: (s, 0, 0, 0))
        scratch = [pltpu.VMEM((rows, D_RWKV), F32)]
    if s0 is None:
        s0 = p['state_rwkv']
        s0_spec = pl.BlockSpec((None,) + state_spec.block_shape,
                               lambda s, b: (layer, s, 0, 0, 0))
    else:
        s0_spec = state_spec
    stacked = [p[k] for k in ('w0', 'w2', 'a0', 'a2', 'g2', 'k_k', 'k_a', 'r_k', 'lnx_g', 'lnx_b')]
    consts = list(_rwkv_constants(seg_len, rows))
    y, s_new = pl.pallas_call(
        functools.partial(_rwkv_kernel, seg_len=seg_len, step_chunks=step_chunks, layer=layer),
        grid=grid,
        in_specs=[pl.BlockSpec((rows, N_SHIFT), row_map), s0_spec]
        + [_layer_spec(t, layer) for t in stacked] + [_const_spec(t.shape) for t in consts],
        out_specs=[pl.BlockSpec((rows, D_RWKV), row_map), state_spec],
        out_shape=[jax.ShapeDtypeStruct((n, D_RWKV), F32),
                   jax.ShapeDtypeStruct((nseq, nh, hd, hd), F32)],
        scratch_shapes=scratch,
        compiler_params=pltpu.CompilerParams(dimension_semantics=("arbitrary", "arbitrary"),
                                             vmem_limit_bytes=V7X_VMEM_LIMIT_BYTES),
        name="rwkv7",
    )(z, s0, *stacked, *consts)
    return y, s_new


def _ffn_kernel(h_ref, y8_ref, u8_ref, yb_ref, pe_ref, c1_ref, c2_ref, d8_ref, wglu_ref, bglu_ref,
                wout_ref, g2_ref, wup_ref, cw_ref, cb_ref, wdn_ref, wple_ref, wpg_ref, gf_ref,
                *rest, seq_len, tm, final, layer):
    row_of = lambda ref: ref[layer:layer + 1, :]
    i = pl.program_id(0)
    long_seq = seq_len >= tm
    if long_seq:
        o_ref, ga_ref, gb_ref, fold_ref, act_ref, carry_ref = rest

        @pl.when(i == 0)
        def _():
            carry_ref[...] = jnp.zeros_like(carry_ref)
    else:
        o_ref, ga_ref, gb_ref, fold_ref, act_ref = rest
    ya = y8_ref[...] + row_of(d8_ref) * u8_ref[...]
    c_gelu = math.sqrt(2.0 / math.pi)
    ya = ya * (0.5 * (1.0 + jnp.tanh(c_gelu * (ya + 0.044715 * (ya * ya * ya)))))
    ya = _unfold_rows(ya, fold_ref)
    ya = ya * _sigmoid(_mm(ya, wglu_ref[...]) + row_of(bglu_ref))
    h1 = (h_ref[...] + _mm(ya, wout_ref[:D_SSM, :]) + _mm(yb_ref[...], wout_ref[D_SSM:, :]))
    x2 = _rmsnorm(h1, row_of(g2_ref)).astype(BF16)
    row = lax.broadcasted_iota(jnp.int32, (tm, 1), 0)
    if long_seq:
        first = (i % (seq_len // tm)) == 0
    else:
        t = row % seq_len
    def up(c):
        cs = slice(c * FF_CHUNK, (c + 1) * FF_CHUNK)
        gs = slice(D_FF + c * FF_CHUNK, D_FF + (c + 1) * FF_CHUNK)
        return (jnp.dot(x2, wup_ref[:, cs], preferred_element_type=F32),
                jnp.dot(x2, wup_ref[:, gs], preferred_element_type=F32))

    n_chunks = D_FF // FF_CHUNK
    nxt = up(0)
    for c in range(n_chunks):
        cs = slice(c * FF_CHUNK, (c + 1) * FF_CHUNK)
        val, gate = nxt
        if c + 1 < n_chunks:
            nxt = up(c + 1)
        r1 = pltpu.roll(gate, 1, 0)
        r2 = pltpu.roll(gate, 2, 0)
        if long_seq:
            m1 = jnp.where(first, c1_ref[0][:, cs], carry_ref[1:2, cs])
            m2 = jnp.where(first, c2_ref[0][:, cs], carry_ref[0:1, cs])
            p1 = jnp.where(row == 0, m1, r1)
            p2 = jnp.where(row == 0, m2, jnp.where(row == 1, m1, r2))
            carry_ref[:, cs] = gate[tm - 2:tm, :]
            ga_ref[0, :, cs] = gate[tm - 2:tm - 1, :]
            gb_ref[0, :, cs] = gate[tm - 1:tm, :]
        else:
            m1 = _expand_rows(c1_ref[:, cs], seq_len)
            m2 = _expand_rows(c2_ref[:, cs], seq_len)
            p1 = jnp.where(t == 0, m1, r1)
            p2 = jnp.where(t == 0, m2, jnp.where(t == 1, m1, r2))
            ga_ref[:, cs] = _step_rows(gate, seq_len, seq_len - 2)
            gb_ref[:, cs] = _step_rows(gate, seq_len, seq_len - 1)
        conv = (cb_ref[layer:layer + 1, cs] + cw_ref[2:3, cs] * gate + cw_ref[1:2, cs] * p1 + cw_ref[0:1, cs] * p2)
        act_ref[:, cs] = (conv * _sigmoid(conv) * val).astype(BF16)
    h2 = h1 + jnp.dot(act_ref[...], wdn_ref[...], preferred_element_type=F32)
    h3 = h2 + _mm(pe_ref[...], wple_ref[...]) * _sigmoid(_mm(h2, wpg_ref[...]))
    if final:
        h3 = _rmsnorm(h3, gf_ref[...])
    o_ref[...] = h3


def _ffn(h, y8, u8, yb, pe, conv0, p, layer, seq_len, tm, final):
    n = h.shape[0]
    nseq = n // seq_len
    row_spec = lambda w: pl.BlockSpec((tm, w), lambda i: (i, 0))
    pe_spec = pl.BlockSpec((None, tm, D_PLE), lambda i: (layer, i, 0))
    fold_spec = pl.BlockSpec((tm // S5_STEPS, S5_ROW), lambda i: (i, 0))
    scratch = [pltpu.VMEM((FOLD_LANE_BLOCKS, tm, LANES), F32), pltpu.VMEM((tm, D_FF), BF16)]
    if seq_len >= tm:
        tps = seq_len // tm
        seq_spec = pl.BlockSpec((1, 1, D_FF), lambda i: (i // tps, 0, 0))
        seq_shape = (nseq, 1, D_FF)
        scratch.append(pltpu.VMEM((2, D_FF), F32))
    else:
        seq_spec = pl.BlockSpec((tm // seq_len, D_FF), lambda i: (i, 0))
        seq_shape = (nseq, D_FF)
    stacked = [p[k] for k in ('d8', 'w_glu', 'b_glu', 'w_out', 'norm2_g', 'w_up', 'conv_w',
                              'conv_b', 'w_down', 'w_ple', 'w_pg')]
    consts = [p['final_g']]
    weights = stacked + consts
    out, ga, gb = pl.pallas_call(
        functools.partial(_ffn_kernel, seq_len=seq_len, tm=tm, final=final, layer=layer),
        grid=(n // tm,),
        in_specs=[row_spec(D_MODEL), fold_spec, fold_spec, row_spec(D_RWKV), pe_spec,
                  seq_spec, seq_spec] + [_layer_spec(t, layer) for t in stacked]
        + [_const_spec(t.shape) for t in consts],
        out_specs=[row_spec(D_MODEL), seq_spec, seq_spec],
        out_shape=[jax.ShapeDtypeStruct((n, D_MODEL), F32)]
        + [jax.ShapeDtypeStruct(seq_shape, F32)] * 2,
        scratch_shapes=scratch,
        compiler_params=pltpu.CompilerParams(dimension_semantics=("arbitrary",),
                                             vmem_limit_bytes=V7X_VMEM_LIMIT_BYTES),
        name="ffn",
    )(h, y8, u8, yb, pe, conv0[:, 1].reshape(seq_shape), conv0[:, 0].reshape(seq_shape), *weights)
    conv_new = jnp.stack([ga.reshape(nseq, D_FF), gb.reshape(nseq, D_FF)], axis=1)
    return out, conv_new


def _layer(h, pe, st, p, layer, s5_tables, seq_len, tm, final):
    ssm_re0, ssm_im0, rwkv0, shift0, conv0 = st
    tm_in, tm_ffn = tm
    u8, z, shift_new = _inproj(h, p, layer, shift0, seq_len, tm_in)
    y8, hre, him = _s5(u8, ssm_re0, ssm_im0, s5_tables, seq_len)
    yb, s_last = _rwkv(z, rwkv0, p, layer, seq_len)
    h, conv_new = _ffn(h, y8, u8, yb, pe, conv0, p, layer, seq_len, tm_ffn, final)
    return h, (hre, him, s_last, shift_new, conv_new)


def _stacked_params(w, state_rwkv, state_shift):
    depth = w['w_in'].shape[0]
    row = lambda t: t.reshape(depth, -1).astype(F32)
    bf = lambda t: t.astype(BF16)
    return {
        'norm1_g': row(w['norm1_g']), 'w_in': bf(w['w_in']), 'shift_mu': row(w['shift_mu']),
        'd8': jnp.tile(row(w['ssm_d']), (1, S5_STEPS)), 'w_glu': bf(w['ssm_w_glu']),
        'b_glu': row(w['ssm_b_glu']),
        'w0': row(w['rwkv_w0']), 'w2': bf(w['rwkv_w2']), 'a0': row(w['rwkv_a0']),
        'a2': bf(w['rwkv_a2']), 'g2': bf(w['rwkv_g2']), 'k_k': row(w['rwkv_k_k']),
        'k_a': row(w['rwkv_k_a']), 'r_k': row(w['rwkv_r_k']), 'lnx_g': row(w['rwkv_lnx_g']),
        'lnx_b': row(w['rwkv_lnx_b']), 'w_out': bf(w['w_out']), 'norm2_g': row(w['norm2_g']),
        'w_up': bf(w['w_up']), 'conv_w': w['conv_w'].astype(F32), 'conv_b': row(w['conv_b']),
        'w_down': bf(w['w_down']), 'w_ple': bf(w['w_ple']), 'w_pg': bf(w['w_pg']),
        'final_g': w['final_g'].reshape(1, -1).astype(F32),
        'state_rwkv': state_rwkv.astype(F32), 'state_shift': state_shift.astype(F32),
    }


def _forward(x_prompt, x_sample, p_prompt, p_sample, state_ssm_re, state_ssm_im, state_rwkv,
             state_shift, state_conv, w, tm_prompt, tm_sample):
    depth = w['w_in'].shape[0]
    bp, lp, _ = x_prompt.shape
    bs, ls, _ = x_sample.shape
    hp = x_prompt.reshape(bp * lp, D_MODEL).astype(F32)
    hs = x_sample.reshape(bs * ls, D_MODEL).astype(F32)
    pe_p = p_prompt.reshape(depth, bp * lp, D_PLE).astype(F32)
    pe_s = p_sample.reshape(depth, bs * ls, D_PLE).astype(F32)
    zero_st = (jnp.zeros((bp, N_SSM_GROUPS, SSM_STATE), F32),
               jnp.zeros((bp, N_SSM_GROUPS, SSM_STATE), F32),
               jnp.zeros((bp, N_RWKV_HEADS, RWKV_HEAD, RWKV_HEAD), F32),
               jnp.zeros((bp, N_SHIFT), F32),
               jnp.zeros((bp, 2, D_FF), F32))
    p = _stacked_params(w, state_rwkv, state_shift)
    new_p = [[] for _ in range(5)]
    new_s = [[] for _ in range(5)]
    for i in range(depth):
        tables = _s5_tables(w['ssm_lam_re'][i], w['ssm_lam_im'][i], w['ssm_log_dt'][i],
                            w['ssm_b_re'][i], w['ssm_b_im'][i], w['ssm_c_re'][i], w['ssm_c_im'][i])
        final = i == depth - 1
        hp, stp = _layer(hp, pe_p, zero_st, p, i, tables, lp, tm_prompt, final)
        st_in = (state_ssm_re[i].astype(F32), state_ssm_im[i].astype(F32), None, None,
                 state_conv[i].astype(F32))
        hs, sts = _layer(hs, pe_s, st_in, p, i, tables, ls, tm_sample, final)
        for j in range(5):
            new_p[j].append(stp[j])
            new_s[j].append(sts[j])
    y_prompt = hp.reshape(bp, lp, D_MODEL).astype(x_prompt.dtype)
    y_sample = hs.reshape(bs, ls, D_MODEL).astype(x_sample.dtype)
    dts = (state_ssm_re.dtype, state_ssm_im.dtype, state_rwkv.dtype, state_shift.dtype,
           state_conv.dtype)
    outs_p = tuple(jnp.stack(new_p[j]).astype(dts[j]) for j in range(5))
    outs_s = tuple(jnp.stack(new_s[j]).astype(dts[j]) for j in range(5))
    return (y_prompt, y_sample) + outs_p + outs_s


def kernel(x_prompt, x_sample, p_prompt, p_sample, state_ssm_re, state_ssm_im, state_rwkv, state_shift, state_conv, norm1_g, w_in, shift_mu, ssm_lam_re, ssm_lam_im, ssm_log_dt, ssm_b_re, ssm_b_im, ssm_c_re, ssm_c_im, ssm_d, ssm_w_glu, ssm_b_glu, rwkv_w0, rwkv_w2, rwkv_a0, rwkv_a2, rwkv_g2, rwkv_k_k, rwkv_k_a, rwkv_r_k, rwkv_lnx_g, rwkv_lnx_b, w_out, norm2_g, w_up, conv_w, conv_b, w_down, w_ple, w_pg, final_g):
    w = dict(norm1_g=norm1_g, w_in=w_in, shift_mu=shift_mu, ssm_lam_re=ssm_lam_re,
             ssm_lam_im=ssm_lam_im, ssm_log_dt=ssm_log_dt, ssm_b_re=ssm_b_re, ssm_b_im=ssm_b_im,
             ssm_c_re=ssm_c_re, ssm_c_im=ssm_c_im, ssm_d=ssm_d, ssm_w_glu=ssm_w_glu,
             ssm_b_glu=ssm_b_glu, rwkv_w0=rwkv_w0, rwkv_w2=rwkv_w2, rwkv_a0=rwkv_a0,
             rwkv_a2=rwkv_a2, rwkv_g2=rwkv_g2, rwkv_k_k=rwkv_k_k, rwkv_k_a=rwkv_k_a,
             rwkv_r_k=rwkv_r_k, rwkv_lnx_g=rwkv_lnx_g, rwkv_lnx_b=rwkv_lnx_b, w_out=w_out,
             norm2_g=norm2_g, w_up=w_up, conv_w=conv_w, conv_b=conv_b, w_down=w_down,
             w_ple=w_ple, w_pg=w_pg, final_g=final_g)
    lp = x_prompt.shape[1]
    ns = x_sample.shape[0] * x_sample.shape[1]
    return _forward(x_prompt, x_sample, p_prompt, p_sample, state_ssm_re, state_ssm_im,
                    state_rwkv, state_shift, state_conv, w,
                    tm_prompt=(min(1024, lp), min(512, lp)),
                    tm_sample=(min(512, ns), min(256, ns)))
```

```python
import functools
import math

import numpy as np
import jax
import jax.numpy as jnp
from jax import lax
from jax.experimental import pallas as pl
from jax.experimental.pallas import tpu as pltpu

F32 = jnp.float32
BF16 = jnp.bfloat16

D_MODEL = 1024
D_SSM = 512
D_RWKV = 512
SSM_GROUP = 16
N_SSM_GROUPS = 32
SSM_STATE = 64
RWKV_HEAD = 64
N_RWKV_HEADS = 8
DECAY_LORA = 64
AAA_LORA = 64
GATE_LORA = 128
N_SHIFT = 3 * D_RWKV + DECAY_LORA + AAA_LORA + GATE_LORA
N_IN = D_SSM + N_SHIFT
D_FF = 2816
D_PLE = 256
RMS_EPS = 1e-6
GN_EPS = 64e-5
L2_EPS = 1e-12

LANES = 128
FOLD_LANE_BLOCKS = D_SSM // LANES
S5_STEPS = 8
S5_ROW = S5_STEPS * D_SSM
S5_LANE_BLOCKS = 4
S5_BLOCK_STATE = 2 * 8 * SSM_STATE
S5_STATE_ROW = S5_LANE_BLOCKS * S5_BLOCK_STATE
FF_CHUNK = 256
V7X_VMEM_LIMIT_BYTES = 56 * 1024 * 1024


def _mm(a, b):
    return jnp.dot(a.astype(BF16), b.astype(BF16), preferred_element_type=F32)


def _mm_nt(a, b):
    return lax.dot_general(a.astype(BF16), b.astype(BF16), (((1,), (1,)), ((), ())),
                           preferred_element_type=F32)


def _mm_tn(a, b):
    return lax.dot_general(a.astype(BF16), b.astype(BF16), (((0,), (0,)), ((), ())),
                           preferred_element_type=F32)


def _split2(x):
    hi = x.astype(BF16)
    return hi, (x - hi.astype(F32)).astype(BF16)


def _seg_sum(x, ones_bd):
    hi, lo = _split2(x)
    rows, gl = x.shape[0], ones_bd.shape[0]
    parts = []
    for g in range(0, x.shape[1], gl):
        both = jnp.dot(jnp.concatenate([hi[:, g:g + gl], lo[:, g:g + gl]], axis=0), ones_bd,
                       preferred_element_type=F32)
        parts.append(both[:rows] + both[rows:])
    return jnp.concatenate(parts, axis=1)


def _rmsnorm(x, g):
    return x * lax.rsqrt(jnp.mean(x * x, axis=-1, keepdims=True) + RMS_EPS) * g


def _sigmoid(x):
    return 1.0 / (1.0 + jnp.exp(-x))


def _const_spec(shape):
    nd = len(shape)
    return pl.BlockSpec(shape, lambda *_: (0,) * nd, pipeline_mode=pl.Buffered(1))


def _layer_spec(t, layer):
    if t.ndim == 2:
        return _const_spec(t.shape)
    nd = t.ndim - 1
    return pl.BlockSpec((None,) + t.shape[1:], lambda *_: (layer,) + (0,) * nd,
                        pipeline_mode=pl.Buffered(1))


def _expand_rows(x, seq_len):
    nseq, w = x.shape
    return jnp.broadcast_to(x[:, None, :], (nseq, seq_len, w)).reshape(nseq * seq_len, w)


def _step_rows(x, seq_len, t):
    return x.reshape(x.shape[0] // seq_len, seq_len, x.shape[1])[:, t, :]


def _fold_rows(x, scr_ref, out_ref):
    tm = x.shape[0]
    for q in range(FOLD_LANE_BLOCKS):
        scr_ref[q] = x[:, q * LANES:(q + 1) * LANES]
    for s in range(S5_STEPS):
        for q in range(FOLD_LANE_BLOCKS):
            c0 = s * D_SSM + q * LANES
            out_ref[:, c0:c0 + LANES] = scr_ref[q, pl.ds(s, tm // S5_STEPS, stride=S5_STEPS), :]


def _unfold_rows(x8, scr_ref):
    tm = x8.shape[0] * S5_STEPS
    for s in range(S5_STEPS):
        for q in range(FOLD_LANE_BLOCKS):
            c0 = s * D_SSM + q * LANES
            scr_ref[q, pl.ds(s, tm // S5_STEPS, stride=S5_STEPS), :] = x8[:, c0:c0 + LANES]
    return jnp.concatenate([scr_ref[q] for q in range(FOLD_LANE_BLOCKS)], axis=1)


def _inproj_kernel(h_ref, g_ref, w_ref, mu_ref, init_ref, *rest, seq_len, tm, layer):
    i = pl.program_id(0)
    row_of = lambda ref: ref[layer:layer + 1, :]
    long_seq = seq_len >= tm
    if long_seq:
        u8_ref, z_ref, last_ref, fold_ref, carry_ref = rest

        @pl.when(i == 0)
        def _():
            carry_ref[...] = jnp.zeros_like(carry_ref)
    else:
        u8_ref, z_ref, last_ref, fold_ref = rest
    xn = _rmsnorm(h_ref[...], row_of(g_ref))
    proj = jnp.dot(xn.astype(BF16), w_ref[...], preferred_element_type=F32)
    _fold_rows(proj[:, :D_SSM], fold_ref, u8_ref)
    zr = proj[:, D_SSM:]
    rolled = pltpu.roll(zr, 1, 0)
    row = lax.broadcasted_iota(jnp.int32, (tm, 1), 0)
    if long_seq:
        first = (i % (seq_len // tm)) == 0
        row0 = jnp.where(first, init_ref[0], carry_ref[...])
        prev = jnp.where(row == 0, row0, rolled)
        carry_ref[...] = zr[tm - 1:tm, :]
        last_ref[0] = zr[tm - 1:tm, :]
    else:
        prev = jnp.where(row % seq_len == 0, _expand_rows(init_ref[...], seq_len), rolled)
        last_ref[...] = _step_rows(zr, seq_len, seq_len - 1)
    z_ref[...] = (zr + (prev - zr) * row_of(mu_ref)).astype(z_ref.dtype)


def _inproj(h, p, layer, shift0, seq_len, tm):
    n = h.shape[0]
    nseq = n // seq_len
    fold_scratch = pltpu.VMEM((FOLD_LANE_BLOCKS, tm, LANES), F32)
    if seq_len >= tm:
        tps = seq_len // tm
        seq_spec = pl.BlockSpec((1, 1, N_SHIFT), lambda i: (i // tps, 0, 0))
        init_spec = seq_spec
        init = shift0.reshape(nseq, 1, N_SHIFT)
        last_shape = jax.ShapeDtypeStruct((nseq, 1, N_SHIFT), F32)
        scratch = [fold_scratch, pltpu.VMEM((1, N_SHIFT), F32)]
    else:
        spt = tm // seq_len
        seq_spec = pl.BlockSpec((spt, N_SHIFT), lambda i: (i, 0))
        init_spec = pl.BlockSpec((None, spt, N_SHIFT), lambda i: (layer, i, 0))
        init = p['state_shift']
        last_shape = jax.ShapeDtypeStruct((nseq, N_SHIFT), F32)
        scratch = [fold_scratch]
    g1, w_in, mu = p['norm1_g'], p['w_in'], p['shift_mu']
    u8, z, last = pl.pallas_call(
        functools.partial(_inproj_kernel, seq_len=seq_len, tm=tm, layer=layer),
        grid=(n // tm,),
        in_specs=[pl.BlockSpec((tm, D_MODEL), lambda i: (i, 0)),
                  _layer_spec(g1, layer), _layer_spec(w_in, layer), _layer_spec(mu, layer),
                  init_spec],
        out_specs=[pl.BlockSpec((tm // S5_STEPS, S5_ROW), lambda i: (i, 0)),
                   pl.BlockSpec((tm, N_SHIFT), lambda i: (i, 0)),
                   seq_spec],
        out_shape=[jax.ShapeDtypeStruct((n // S5_STEPS, S5_ROW), F32),
                   jax.ShapeDtypeStruct((n, N_SHIFT), BF16),
                   last_shape],
        scratch_shapes=scratch,
        compiler_params=pltpu.CompilerParams(dimension_semantics=("arbitrary",),
                                             vmem_limit_bytes=V7X_VMEM_LIMIT_BYTES),
        name="inproj",
    )(h, g1, w_in, mu, init)
    return u8, z, last.reshape(nseq, N_SHIFT)


def _s5_prep_kernel(lr_ref, li_ref, ldt_ref, br_ref, bi_ref, cr_ref, ci_ref, tile_ref, sel_ref,
                    wx_ref, wy_ref, apr_ref, api_ref):
    lr, li = lr_ref[...], li_ref[...]
    dt = jnp.exp(ldt_ref[...])
    mag = jnp.exp(lr * dt)
    ar = mag * jnp.cos(li * dt)
    ai = mag * jnp.sin(li * dt)
    den = lr * lr + li * li
    nr = ar - 1.0
    fr = (nr * lr + ai * li) / den
    fi = (ai * lr - nr * li) / den
    br, bi = br_ref[...], bi_ref[...]
    bbr = fr * br - fi * bi
    bbi = fr * bi + fi * br
    cr, ci = cr_ref[...], ci_ref[...]
    pr = [jnp.ones_like(ar)]
    pi = [jnp.zeros_like(ar)]
    for _ in range(S5_STEPS):
        pr.append(pr[-1] * ar - pi[-1] * ai)
        pi.append(pr[-2] * ai + pi[-1] * ar)
    nrow = lr.shape[0]
    blk = nrow // S5_LANE_BLOCKS
    half = S5_BLOCK_STATE // 2
    ri = lax.broadcasted_iota(jnp.int32, (nrow, nrow), 0)
    ci_ = lax.broadcasted_iota(jnp.int32, (nrow, nrow), 1)
    same_group = (ri // SSM_GROUP) == (ci_ // SSM_GROUP)
    keep_in = ((ri % blk) // SSM_GROUP) == (ci_ // SSM_STATE)
    keep_out = (ri // SSM_STATE) == ((ci_ % blk) // SSM_GROUP)
    tile, sel = tile_ref[...], sel_ref[...]
    nt = (((1,), (1,)), ((), ()))
    c_hi, c_lo = _split2(jnp.concatenate([cr, -ci], axis=1))
    wy_ref[...] = jnp.zeros(wy_ref.shape, wy_ref.dtype)
    for s in range(S5_STEPS):
        qr, qi = pr[S5_STEPS - 1 - s], pi[S5_STEPS - 1 - s]
        wr = qr * bbr - qi * bbi
        wi = qr * bbi + qi * bbr
        for h, w_ in enumerate((wr, wi)):
            t = jnp.dot(w_.astype(BF16), tile, preferred_element_type=F32)
            t = jnp.where(keep_in, t, 0.0).astype(BF16)
            for q in range(S5_LANE_BLOCKS):
                wx_ref[q, s * blk:(s + 1) * blk, h * half:(h + 1) * half] = t[q * blk:(q + 1) * blk, :]
        tau = S5_STEPS - 1 - s
        lhs_hi, lhs_lo = _split2(jnp.concatenate([wr, wi], axis=1))
        kt = (lax.dot_general(lhs_hi, c_hi, nt, preferred_element_type=F32)
              + lax.dot_general(lhs_hi, c_lo, nt, preferred_element_type=F32)
              + lax.dot_general(lhs_lo, c_hi, nt, preferred_element_type=F32))
        kt = jnp.where(same_group, kt, 0.0).astype(BF16)
        for q in range(S5_LANE_BLOCKS):
            kq = kt[q * blk:(q + 1) * blk, q * blk:(q + 1) * blk]
            for s_in in range(S5_STEPS - tau):
                s_out = s_in + tau
                r0 = S5_BLOCK_STATE + s_in * blk
                wy_ref[q, s_out // 2, r0:r0 + blk, (s_out % 2) * blk:(s_out % 2 + 1) * blk] = kq
        mr = cr * pr[s + 1] - ci * pi[s + 1]
        mi = cr * pi[s + 1] + ci * pr[s + 1]
        for h, m_ in enumerate((mr, -mi)):
            t = lax.dot_general(sel, m_.astype(BF16), nt, preferred_element_type=F32)
            t = jnp.where(keep_out, t, 0.0).astype(BF16)
            for q in range(S5_LANE_BLOCKS):
                wy_ref[q, s // 2, h * half:(h + 1) * half, (s % 2) * blk:(s % 2 + 1) * blk] = (
                    t[:, q * blk:(q + 1) * blk])
    a8r, a8i = pr[S5_STEPS], pi[S5_STEPS]
    er, ei = a8r, a8i
    for n in range(S5_STEPS):
        apr_ref[n] = er
        api_ref[n] = ei
        er, ei = er * a8r - ei * a8i, er * a8i + ei * a8r


def _s5_tables(lam_re, lam_im, log_dt, b_re, b_im, c_re, c_im):
    G, P, K = N_SSM_GROUPS, SSM_STATE, SSM_GROUP
    Q, GB = S5_LANE_BLOCKS, G // S5_LANE_BLOCKS
    rep = lambda t: jnp.repeat(t, K, axis=0)
    tile = np.tile(np.eye(P, dtype=np.float32), (1, GB))
    args = (rep(lam_re), rep(lam_im), rep(jnp.broadcast_to(log_dt[:, None], (G, P))),
            jnp.swapaxes(b_re, 1, 2).reshape(G * K, P), jnp.swapaxes(b_im, 1, 2).reshape(G * K, P),
            c_re.reshape(G * K, P), c_im.reshape(G * K, P),
            jnp.asarray(tile, BF16), jnp.asarray(tile.T, BF16))
    t3 = jax.ShapeDtypeStruct((S5_STEPS, G * K, P), F32)
    wx, wy, apr, api = pl.pallas_call(
        _s5_prep_kernel,
        out_shape=[jax.ShapeDtypeStruct((Q, S5_STEPS * GB * K, S5_BLOCK_STATE), BF16),
                   jax.ShapeDtypeStruct((Q, S5_STEPS // 2, S5_BLOCK_STATE + S5_STEPS * GB * K,
                                         2 * GB * K), BF16),
                   t3, t3],
        compiler_params=pltpu.CompilerParams(vmem_limit_bytes=V7X_VMEM_LIMIT_BYTES),
        name="s5_prep",
    )(*args)
    ap = jnp.concatenate([t[:, ::K, :].reshape(S5_STEPS, Q, GB * P) for t in (apr, api)], axis=-1)
    return wx, wy, ap.reshape(S5_STEPS, S5_STATE_ROW)


def _state_to_lanes(h_re, h_im):
    n = h_re.shape[0]
    parts = [t.reshape(n, S5_LANE_BLOCKS, -1) for t in (h_re, h_im)]
    return jnp.concatenate(parts, axis=-1).reshape(n, S5_STATE_ROW)


def _lanes_to_state(h):
    n = h.shape[0]
    h = h.reshape(n, S5_LANE_BLOCKS, 2, N_SSM_GROUPS // S5_LANE_BLOCKS, SSM_STATE)
    return (h[:, :, 0].reshape(n, N_SSM_GROUPS, SSM_STATE), h[:, :, 1].reshape(n, N_SSM_GROUPS, SSM_STATE))


def _s5_kernel(u_ref, h0_ref, wx_ref, wy_ref, ap_ref, y_ref, hl_ref, *scratch, rows, scan):
    half = S5_BLOCK_STATE // 2
    cw = D_SSM // S5_LANE_BLOCKS
    blocks = range(S5_LANE_BLOCKS)
    ub = u_ref[...].astype(BF16)
    ucat = [jnp.concatenate([ub[:, s * D_SSM + q * cw: s * D_SSM + (q + 1) * cw]
                             for s in range(S5_STEPS)], axis=1) for q in blocks]
    x = [jnp.dot(ucat[q], wx_ref[q], preferred_element_type=F32) for q in blocks]
    ap = [ap_ref[:, q * S5_BLOCK_STATE:(q + 1) * S5_BLOCK_STATE] for q in blocks]

    def emit_outputs(q, hprev):
        lhs = jnp.concatenate([hprev, ucat[q]], axis=1)
        for j in range(S5_STEPS // 2):
            kk = S5_BLOCK_STATE + cw * (2 * j + 2)
            y2 = jnp.dot(lhs[:, :kk], wy_ref[q, j, :kk, :], preferred_element_type=F32)
            c0 = (2 * j) * D_SSM + q * cw
            c1 = (2 * j + 1) * D_SSM + q * cw
            y_ref[:, c0:c0 + cw] = y2[:, :cw]
            y_ref[:, c1:c1 + cw] = y2[:, cw:]

    if scan:
        hs_ref, hp_ref = scratch
        sub3 = lax.broadcasted_iota(jnp.int32, (1, 8, 1), 1)
        sub2 = lax.broadcasted_iota(jnp.int32, (8, 1), 0)
        for q in blocks:
            xr = x[q][:, :half].reshape(rows // 8, 8, half)
            xi = x[q][:, half:].reshape(rows // 8, 8, half)
            for k in (1, 2, 4):
                er = jnp.where(sub3 >= k, ap[q][k - 1:k, :half].reshape(1, 1, half), 0.0)
                ei = jnp.where(sub3 >= k, ap[q][k - 1:k, half:].reshape(1, 1, half), 0.0)
                sr = pltpu.roll(xr, k, 1)
                si = pltpu.roll(xi, k, 1)
                xr, xi = xr + er * sr - ei * si, xi + er * si + ei * sr
            hs_ref[q, :, :half] = xr.reshape(rows, half)
            hs_ref[q, :, half:] = xi.reshape(rows, half)
        h0 = h0_ref[0]
        for q in blocks:
            lo = q * S5_BLOCK_STATE
            cr, ci = h0[:, lo:lo + half], h0[:, lo + half:lo + S5_BLOCK_STATE]
            pwr, pwi = ap[q][:, :half], ap[q][:, half:]
            for j in range(rows // 8):
                sl = slice(j * 8, (j + 1) * 8)
                hr = hs_ref[q, sl, :half] + pwr * cr - pwi * ci
                hi = hs_ref[q, sl, half:] + pwr * ci + pwi * cr
                hp_ref[q, sl, :half] = jnp.where(sub2 == 0, cr, pltpu.roll(hr, 1, 0))
                hp_ref[q, sl, half:] = jnp.where(sub2 == 0, ci, pltpu.roll(hi, 1, 0))
                cr, ci = hr[7:8, :], hi[7:8, :]
            hl_ref[0, :, lo:lo + half] = cr
            hl_ref[0, :, lo + half:lo + S5_BLOCK_STATE] = ci
            emit_outputs(q, hp_ref[q].astype(BF16))
    else:
        for q in blocks:
            lo = q * S5_BLOCK_STATE
            h0r = h0_ref[:, lo:lo + half]
            h0i = h0_ref[:, lo + half:lo + S5_BLOCK_STATE]
            er, ei = ap[q][0:1, :half], ap[q][0:1, half:]
            hl_ref[:, lo:lo + half] = er * h0r - ei * h0i + x[q][:, :half]
            hl_ref[:, lo + half:lo + S5_BLOCK_STATE] = er * h0i + ei * h0r + x[q][:, half:]
            emit_outputs(q, jnp.concatenate([h0r, h0i], axis=1).astype(BF16))


def _s5(u8, h_re, h_im, tables, seq_len):
    wx, wy, ap = tables
    n = u8.shape[0] * S5_STEPS
    nseq = n // seq_len
    h0 = _state_to_lanes(h_re, h_im)
    cps = seq_len // S5_STEPS
    scan = cps > 1
    if scan:
        rows = cps
        grid = (nseq,)
        h0 = h0.reshape(nseq, 1, S5_STATE_ROW)
        h_spec = pl.BlockSpec((1, 1, S5_STATE_ROW), lambda i: (i, 0, 0))
        h_shape = jax.ShapeDtypeStruct((nseq, 1, S5_STATE_ROW), F32)
    else:
        rows = min(nseq, 128)
        grid = (nseq // rows,)
        h_spec = pl.BlockSpec((rows, S5_STATE_ROW), lambda i: (i, 0))
        h_shape = jax.ShapeDtypeStruct((nseq, S5_STATE_ROW), F32)
    y8, hl = pl.pallas_call(
        functools.partial(_s5_kernel, rows=rows, scan=scan),
        grid=grid,
        in_specs=[pl.BlockSpec((rows, S5_ROW), lambda i: (i, 0)),
                  h_spec,
                  _const_spec(wx.shape), _const_spec(wy.shape), _const_spec(ap.shape)],
        out_specs=[pl.BlockSpec((rows, S5_ROW), lambda i: (i, 0)), h_spec],
        out_shape=[jax.ShapeDtypeStruct((n // S5_STEPS, S5_ROW), F32), h_shape],
        scratch_shapes=[pltpu.VMEM((S5_LANE_BLOCKS, rows, S5_BLOCK_STATE), F32)] * 2 if scan else [],
        compiler_params=pltpu.CompilerParams(dimension_semantics=("arbitrary",),
                                             vmem_limit_bytes=V7X_VMEM_LIMIT_BYTES),
        name="s5",
    )(u8, h0, wx, wy, ap)
    hre_new, him_new = _lanes_to_state(hl.reshape(nseq, S5_STATE_ROW))
    return y8, hre_new, him_new


RW_CHUNK = 64
RW_STEP_CHUNKS_LONG = 4
RW_STEP_CHUNKS_SHORT = 4
RW_GROUPS = 2
RW_GROUP_HEADS = N_RWKV_HEADS // RW_GROUPS
RW_GROUP_LANES = D_RWKV // RW_GROUPS


def _split3(x):
    hi = x.astype(BF16)
    r1 = x - hi.astype(F32)
    mid = r1.astype(BF16)
    lo = (r1 - mid.astype(F32)).astype(BF16)
    return hi, mid, lo


def _bd(x, hm_ref):
    xb = x.astype(BF16)
    return jnp.concatenate([xb * hm_ref[h] for h in range(RW_GROUP_HEADS)], axis=0)


def _unbd(f, hm_ref):
    n = RWKV_HEAD
    out = f[:n] * hm_ref[0].astype(F32)
    for h in range(1, RW_GROUP_HEADS):
        out = out + f[h * n:(h + 1) * n] * hm_ref[h].astype(F32)
    return out


def _rwkv_kernel(z_ref, s0_ref, w0_ref, w2_ref, a0_ref, a2_ref, g2_ref, kk_ref, ka_ref, rk_ref,
                 lg_ref, lb_ref, ones_ref, hm_ref, cm_ref, tri_ref, y_ref, sl_ref, ypre_ref,
                 *scratch, seg_len, step_chunks, layer):
    row_of = lambda ref: ref[layer:layer + 1, :]
    C = RW_CHUNK
    N = RWKV_HEAD
    GL = RW_GROUP_LANES
    chained = seg_len == C
    if chained:
        (st_ref,) = scratch
        b = pl.program_id(1)

        @pl.when(b == 0)
        def _():
            for g in range(RW_GROUPS):
                st_ref[g] = jnp.concatenate(
                    [s0_ref[0, h] for h in range(g * RW_GROUP_HEADS, (g + 1) * RW_GROUP_HEADS)], axis=1)

    ones_bd = ones_ref[...]
    z = z_ref[...].astype(F32)
    r = z[:, :D_RWKV]
    k = z[:, D_RWKV:2 * D_RWKV]
    v = z[:, 2 * D_RWKV:3 * D_RWKV]
    o = 3 * D_RWKV
    xw = z[:, o:o + DECAY_LORA]
    xa = z[:, o + DECAY_LORA:o + DECAY_LORA + AAA_LORA]
    xg = z[:, o + DECAY_LORA + AAA_LORA:]
    wd = -(row_of(w0_ref) + _mm(jnp.tanh(xw), w2_ref[...]))
    w = -(jnp.maximum(wd, 0.0) + jnp.log1p(jnp.exp(-jnp.abs(wd)))) - 0.5
    lw = -jnp.exp(w)
    a = _sigmoid(row_of(a0_ref) + _mm(xa, a2_ref[...]))
    out_gate = _mm(_sigmoid(xg), g2_ref[...])
    kk = k * row_of(kk_ref)
    kk = kk / jnp.maximum(jnp.sqrt(_seg_sum(kk * kk, ones_bd)), L2_EPS)
    kmod = k * (1.0 + (a - 1.0) * row_of(ka_ref))

    tri = tri_ref[...]
    cw = sum(jnp.dot(tri, t, preferred_element_type=F32) for t in _split3(lw))
    tot = _expand_rows(_step_rows(cw, seg_len, seg_len - 1), seg_len)
    w_in = jnp.exp(cw)
    w_tail = jnp.exp(tot - cw)
    w_inv = jnp.exp(-cw)
    w_all = jnp.exp(tot)
    ah = -kk * jnp.exp(cw - lw)
    bh = kk * a * w_inv
    kh = kmod * w_inv
    rh = r * w_in
    bt = kk * a * w_tail
    kt = kmod * w_tail
    strict = cm_ref[0] > 0.0
    incl = cm_ref[1] > 0.0
    eye_cat = cm_ref[2]
    n_fac = seg_len.bit_length() - 1

    items = [(c, g) for c in range(step_chunks) for g in range(RW_GROUPS)]
    sub = lambda t, c, g: t[c * C:(c + 1) * C, g * GL:(g + 1) * GL]
    bd = lambda t: _bd(t, hm_ref)
    A = [sub(ah, c, g) for c, g in items]
    R = [sub(rh, c, g) for c, g in items]
    V = [sub(v, c, g) for c, g in items]
    G = [_mm_nt(jnp.concatenate([A[i], R[i]], axis=0),
                jnp.concatenate([bd(sub(bh, c, g)), bd(sub(kh, c, g))], axis=0))
         for i, (c, g) in enumerate(items)]
    AB = [jnp.where(strict, t[:C, :GL], 0.0) for t in G]
    AK = [jnp.where(strict, t[:C, GL:], 0.0) for t in G]
    RB = [jnp.where(incl, t[C:, :GL], 0.0) for t in G]
    RK = [jnp.where(incl, t[C:, GL:], 0.0) for t in G]
    KV = [_mm(jnp.concatenate([AK[i], RK[i]], axis=0), bd(V[i])) for i in range(len(items))]
    T = [eye_cat + t for t in AB]
    P = [_mm(t, bd(t)) for t in AB]
    for _ in range(n_fac - 2):
        PT = [_mm(jnp.concatenate([P[i], T[i]], axis=0), bd(P[i])) for i in range(len(items))]
        P = [t[:C] for t in PT]
        T = [T[i] + PT[i][C:] for i in range(len(items))]
    T = [T[i] + _mm(T[i], bd(P[i])) for i in range(len(items))]
    RBT = [_mm(RB[i], bd(T[i])) for i in range(len(items))]
    X = [_mm(jnp.concatenate([T[i], RBT[i]], axis=0),
             jnp.concatenate([bd(A[i]), bd(KV[i][:C])], axis=1)) for i in range(len(items))]
    TA = [t[:C, :GL] for t in X]
    U0 = [t[:C, GL:] for t in X]
    Rt = [R[i] + X[i][C:, :GL] for i in range(len(items))]
    Y0 = [X[i][C:, GL:] + KV[i][C:] for i in range(len(items))]
    nseg = C // seg_len
    heads = lambda g: range(g * RW_GROUP_HEADS, (g + 1) * RW_GROUP_HEADS)
    head_lanes = lambda h: slice((h % RW_GROUP_HEADS) * N, (h % RW_GROUP_HEADS + 1) * N)
    if chained:
        PhiT, PsiT = [], []
        for i, (c, g) in enumerate(items):
            Bt, Kt, Wa = sub(bt, c, g), sub(kt, c, g), sub(w_all, c, g)
            PhiT.append(bd(eye_cat * Wa[0:1, :] + _unbd(_mm_tn(TA[i], Bt), hm_ref)))
            PsiT.append(_unbd(_mm_tn(jnp.concatenate([U0[i], V[i]], axis=0),
                                     jnp.concatenate([Bt, Kt], axis=0)), hm_ref))
        states = [st_ref[g] for g in range(RW_GROUPS)]
        for i, (c, g) in enumerate(items):
            S = states[g]
            ypre_ref[c * C:(c + 1) * C, g * GL:(g + 1) * GL] = _mm_nt(Rt[i], bd(S)) + Y0[i]
            states[g] = _mm(S, PhiT[i]) + PsiT[i]
        for g in range(RW_GROUPS):
            st_ref[g] = states[g]
            for h in heads(g):
                sl_ref[0, h] = states[g][:, head_lanes(h)]
    else:
        segs = [(i, c, g, s) for i, (c, g) in enumerate(items) for s in range(nseg)]
        rows_of = lambda s: slice(s * seg_len, (s + 1) * seg_len)
        S0 = [jnp.concatenate([s0_ref[c * nseg + s, h] for h in heads(g)], axis=1)
              for i, c, g, s in segs]
        RU = [_mm_nt(jnp.concatenate([Rt[i][rows_of(s)], TA[i][rows_of(s)]], axis=0), bd(S0[n]))
              for n, (i, c, g, s) in enumerate(segs)]
        for n, (i, c, g, s) in enumerate(segs):
            ypre_ref[c * C + s * seg_len:c * C + (s + 1) * seg_len, g * GL:(g + 1) * GL] = (
                RU[n][:seg_len] + Y0[i][rows_of(s)])
        UB = [_mm_tn(jnp.concatenate([RU[n][seg_len:] + U0[i][rows_of(s)], V[i][rows_of(s)]], axis=0),
                     jnp.concatenate([sub(bt, c, g)[rows_of(s)], sub(kt, c, g)[rows_of(s)]], axis=0))
              for n, (i, c, g, s) in enumerate(segs)]
        for n, (i, c, g, s) in enumerate(segs):
            S = S0[n] * sub(w_all, c, g)[s * seg_len:s * seg_len + 1, :] + _unbd(UB[n], hm_ref)
            for h in heads(g):
                sl_ref[c * nseg + s, h] = S[:, head_lanes(h)]
    y = ypre_ref[...]

    mu = _seg_sum(y, ones_bd) * (1.0 / N)
    d = y - mu
    var = _seg_sum(d * d, ones_bd) * (1.0 / N)
    yn = d * lax.rsqrt(var + GN_EPS) * row_of(lg_ref) + row_of(lb_ref)
    bonus = _seg_sum(r * kmod * row_of(rk_ref), ones_bd) * v
    y_ref[...] = (yn + bonus) * out_gate


def _rwkv_constants(seg_len, rows):
    n, gh, gl = RWKV_HEAD, RW_GROUP_HEADS, RW_GROUP_LANES
    lane_head = np.arange(gl) // n
    hm = (lane_head[None, None, :] == np.arange(gh)[:, None, None]) * np.ones((gh, n, gl))
    i = np.arange(RW_CHUNK)[:, None]
    j = (np.arange(gl) % n)[None, :]
    same = (i // seg_len) == (j // seg_len)
    cm = np.stack([(j < i) & same, (j <= i) & same, j == i]).astype(np.float32)
    ri = np.arange(rows)[:, None]
    rj = np.arange(rows)[None, :]
    same_r = (ri // seg_len) == (rj // seg_len)
    tri = ((rj <= ri) & same_r).astype(np.float32)
    ones_bd = np.kron(np.eye(gh), np.ones((n, n)))
    return (jnp.asarray(ones_bd, BF16), jnp.asarray(hm, BF16), jnp.asarray(cm, F32),
            jnp.asarray(tri, BF16))


def _rwkv(z, s0, p, layer, seq_len):
    n = z.shape[0]
    if seq_len >= RW_CHUNK:
        step_chunks = min(RW_STEP_CHUNKS_LONG, seq_len // RW_CHUNK)
    else:
        step_chunks = min(RW_STEP_CHUNKS_SHORT, n // RW_CHUNK)
    nseq = n // seq_len
    nh, hd = N_RWKV_HEADS, RWKV_HEAD
    seg_len = min(seq_len, RW_CHUNK)
    rows = RW_CHUNK * step_chunks
    chained = seq_len >= RW_CHUNK
    if chained:
        steps = seq_len // rows
        grid = (nseq, steps)
        row_map = lambda s, b: (s * steps + b, 0)
        state_spec = pl.BlockSpec((1, nh, hd, hd), lambda s, b: (s, 0, 0, 0))
        scratch = [pltpu.VMEM((rows, D_RWKV), F32), pltpu.VMEM((RW_GROUPS, hd, RW_GROUP_LANES), F32)]
    else:
        grid = (n // rows, 1)
        row_map = lambda s, b: (s, 0)
        state_spec = pl.BlockSpec((rows // seq_len, nh, hd, hd), lambda s, b: (s, 0, 0, 0))
        scratch = [pltpu.VMEM((rows, D_RWKV), F32)]
    if s0 is None:
        s0 = p['state_rwkv']
        s0_spec = pl.BlockSpec((None,) + state_spec.block_shape,
                               lambda s, b: (layer, s, 0, 0, 0))
    else:
        s0_spec = state_spec
    stacked = [p[k] for k in ('w0', 'w2', 'a0', 'a2', 'g2', 'k_k', 'k_a', 'r_k', 'lnx_g', 'lnx_b')]
    consts = list(_rwkv_constants(seg_len, rows))
    y, s_new = pl.pallas_call(
        functools.partial(_rwkv_kernel, seg_len=seg_len, step_chunks=step_chunks, layer=layer),
        grid=grid,
        in_specs=[pl.BlockSpec((rows, N_SHIFT), row_map), s0_spec]
        + [_layer_spec(t, layer) for t in stacked] + [_const_spec(t.shape) for t in consts],
        out_specs=[pl.BlockSpec((rows, D_RWKV), row_map), state_spec],
        out_shape=[jax.ShapeDtypeStruct((n, D_RWKV), F32),
                   jax.ShapeDtypeStruct((nseq, nh, hd, hd), F32)],
        scratch_shapes=scratch,
        compiler_params=pltpu.CompilerParams(dimension_semantics=("arbitrary", "arbitrary"),
                                             vmem_limit_bytes=V7X_VMEM_LIMIT_BYTES),
        name="rwkv7",
    )(z, s0, *stacked, *consts)
    return y, s_new


def _ffn_kernel(h_ref, y8_ref, u8_ref, yb_ref, pe_ref, c1_ref, c2_ref, d8_ref, wglu_ref, bglu_ref,
                wout_ref, g2_ref, wup_ref, cw_ref, cb_ref, wdn_ref, wple_ref, wpg_ref, gf_ref,
                *rest, seq_len, tm, final, layer):
    row_of = lambda ref: ref[layer:layer + 1, :]
    i = pl.program_id(0)
    long_seq = seq_len >= tm
    if long_seq:
        o_ref, ga_ref, gb_ref, fold_ref, act_ref, carry_ref = rest

        @pl.when(i == 0)
        def _():
            carry_ref[...] = jnp.zeros_like(carry_ref)
    else:
        o_ref, ga_ref, gb_ref, fold_ref, act_ref = rest
    ya = y8_ref[...] + row_of(d8_ref) * u8_ref[...]
    c_gelu = math.sqrt(2.0 / math.pi)
    ya = ya * (0.5 * (1.0 + jnp.tanh(c_gelu * (ya + 0.044715 * (ya * ya * ya)))))
    ya = _unfold_rows(ya, fold_ref)
    ya = ya * _sigmoid(_mm(ya, wglu_ref[...]) + row_of(bglu_ref))
    h1 = (h_ref[...] + _mm(ya, wout_ref[:D_SSM, :]) + _mm(yb_ref[...], wout_ref[D_SSM:, :]))
    x2 = _rmsnorm(h1, row_of(g2_ref)).astype(BF16)
    row = lax.broadcasted_iota(jnp.int32, (tm, 1), 0)
    if long_seq:
        first = (i % (seq_len // tm)) == 0
    else:
        t = row % seq_len
    def up(c):
        cs = slice(c * FF_CHUNK, (c + 1) * FF_CHUNK)
        gs = slice(D_FF + c * FF_CHUNK, D_FF + (c + 1) * FF_CHUNK)
        return (jnp.dot(x2, wup_ref[:, cs], preferred_element_type=F32),
                jnp.dot(x2, wup_ref[:, gs], preferred_element_type=F32))

    n_chunks = D_FF // FF_CHUNK
    nxt = up(0)
    for c in range(n_chunks):
        cs = slice(c * FF_CHUNK, (c + 1) * FF_CHUNK)
        val, gate = nxt
        if c + 1 < n_chunks:
            nxt = up(c + 1)
        r1 = pltpu.roll(gate, 1, 0)
        r2 = pltpu.roll(gate, 2, 0)
        if long_seq:
            m1 = jnp.where(first, c1_ref[0][:, cs], carry_ref[1:2, cs])
            m2 = jnp.where(first, c2_ref[0][:, cs], carry_ref[0:1, cs])
            p1 = jnp.where(row == 0, m1, r1)
            p2 = jnp.where(row == 0, m2, jnp.where(row == 1, m1, r2))
            carry_ref[:, cs] = gate[tm - 2:tm, :]
            ga_ref[0, :, cs] = gate[tm - 2:tm - 1, :]
            gb_ref[0, :, cs] = gate[tm - 1:tm, :]
        else:
            m1 = _expand_rows(c1_ref[:, cs], seq_len)
            m2 = _expand_rows(c2_ref[:, cs], seq_len)
            p1 = jnp.where(t == 0, m1, r1)
            p2 = jnp.where(t == 0, m2, jnp.where(t == 1, m1, r2))
            ga_ref[:, cs] = _step_rows(gate, seq_len, seq_len - 2)
            gb_ref[:, cs] = _step_rows(gate, seq_len, seq_len - 1)
        conv = (cb_ref[layer:layer + 1, cs] + cw_ref[2:3, cs] * gate + cw_ref[1:2, cs] * p1 + cw_ref[0:1, cs] * p2)
        act_ref[:, cs] = (conv * _sigmoid(conv) * val).astype(BF16)
    h2 = h1 + jnp.dot(act_ref[...], wdn_ref[...], preferred_element_type=F32)
    h3 = h2 + _mm(pe_ref[...], wple_ref[...]) * _sigmoid(_mm(h2, wpg_ref[...]))
    if final:
        h3 = _rmsnorm(h3, gf_ref[...])
    o_ref[...] = h3


def _ffn(h, y8, u8, yb, pe, conv0, p, layer, seq_len, tm, final):
    n = h.shape[0]
    nseq = n // seq_len
    row_spec = lambda w: pl.BlockSpec((tm, w), lambda i: (i, 0))
    pe_spec = pl.BlockSpec((None, tm, D_PLE), lambda i: (layer, i, 0))
    fold_spec = pl.BlockSpec((tm // S5_STEPS, S5_ROW), lambda i: (i, 0))
    scratch = [pltpu.VMEM((FOLD_LANE_BLOCKS, tm, LANES), F32), pltpu.VMEM((tm, D_FF), BF16)]
    if seq_len >= tm:
        tps = seq_len // tm
        seq_spec = pl.BlockSpec((1, 1, D_FF), lambda i: (i // tps, 0, 0))
        seq_shape = (nseq, 1, D_FF)
        scratch.append(pltpu.VMEM((2, D_FF), F32))
    else:
        seq_spec = pl.BlockSpec((tm // seq_len, D_FF), lambda i: (i, 0))
        seq_shape = (nseq, D_FF)
    stacked = [p[k] for k in ('d8', 'w_glu', 'b_glu', 'w_out', 'norm2_g', 'w_up', 'conv_w',
                              'conv_b', 'w_down', 'w_ple', 'w_pg')]
    consts = [p['final_g']]
    weights = stacked + consts
    out, ga, gb = pl.pallas_call(
        functools.partial(_ffn_kernel, seq_len=seq_len, tm=tm, final=final, layer=layer),
        grid=(n // tm,),
        in_specs=[row_spec(D_MODEL), fold_spec, fold_spec, row_spec(D_RWKV), pe_spec,
                  seq_spec, seq_spec] + [_layer_spec(t, layer) for t in stacked]
        + [_const_spec(t.shape) for t in consts],
        out_specs=[row_spec(D_MODEL), seq_spec, seq_spec],
        out_shape=[jax.ShapeDtypeStruct((n, D_MODEL), F32)]
        + [jax.ShapeDtypeStruct(seq_shape, F32)] * 2,
        scratch_shapes=scratch,
        compiler_params=pltpu.CompilerParams(dimension_semantics=("arbitrary",),
                                             vmem_limit_bytes=V7X_VMEM_LIMIT_BYTES),
        name="ffn",
    )(h, y8, u8, yb, pe, conv0[:, 1].reshape(seq_shape), conv0[:, 0].reshape(seq_shape), *weights)
    conv_new = jnp.stack([ga.reshape(nseq, D_FF), gb.reshape(nseq, D_FF)], axis=1)
    return out, conv_new


def _layer(h, pe, st, p, layer, s5_tables, seq_len, tm, final):
    ssm_re0, ssm_im0, rwkv0, shift0, conv0 = st
    tm_in, tm_ffn = tm
    u8, z, shift_new = _inproj(h, p, layer, shift0, seq_len, tm_in)
    y8, hre, him = _s5(u8, ssm_re0, ssm_im0, s5_tables, seq_len)
    yb, s_last = _rwkv(z, rwkv0, p, layer, seq_len)
    h, conv_new = _ffn(h, y8, u8, yb, pe, conv0, p, layer, seq_len, tm_ffn, final)
    return h, (hre, him, s_last, shift_new, conv_new)


def _stacked_params(w, state_rwkv, state_shift):
    depth = w['w_in'].shape[0]
    row = lambda t: t.reshape(depth, -1).astype(F32)
    bf = lambda t: t.astype(BF16)
    return {
        'norm1_g': row(w['norm1_g']), 'w_in': bf(w['w_in']), 'shift_mu': row(w['shift_mu']),
        'd8': jnp.tile(row(w['ssm_d']), (1, S5_STEPS)), 'w_glu': bf(w['ssm_w_glu']),
        'b_glu': row(w['ssm_b_glu']),
        'w0': row(w['rwkv_w0']), 'w2': bf(w['rwkv_w2']), 'a0': row(w['rwkv_a0']),
        'a2': bf(w['rwkv_a2']), 'g2': bf(w['rwkv_g2']), 'k_k': row(w['rwkv_k_k']),
        'k_a': row(w['rwkv_k_a']), 'r_k': row(w['rwkv_r_k']), 'lnx_g': row(w['rwkv_lnx_g']),
        'lnx_b': row(w['rwkv_lnx_b']), 'w_out': bf(w['w_out']), 'norm2_g': row(w['norm2_g']),
        'w_up': bf(w['w_up']), 'conv_w': w['conv_w'].astype(F32), 'conv_b': row(w['conv_b']),
        'w_down': bf(w['w_down']), 'w_ple': bf(w['w_ple']), 'w_pg': bf(w['w_pg']),
        'final_g': w['final_g'].reshape(1, -1).astype(F32),
        'state_rwkv': state_rwkv.astype(F32), 'state_shift': state_shift.astype(F32),
    }


def _forward(x_prompt, x_sample, p_prompt, p_sample, state_ssm_re, state_ssm_im, state_rwkv,
             state_shift, state_conv, w, tm_prompt, tm_sample):
    depth = w['w_in'].shape[0]
    bp, lp, _ = x_prompt.shape
    bs, ls, _ = x_sample.shape
    hp = x_prompt.reshape(bp * lp, D_MODEL).astype(F32)
    hs = x_sample.reshape(bs * ls, D_MODEL).astype(F32)
    pe_p = p_prompt.reshape(depth, bp * lp, D_PLE).astype(F32)
    pe_s = p_sample.reshape(depth, bs * ls, D_PLE).astype(F32)
    zero_st = (jnp.zeros((bp, N_SSM_GROUPS, SSM_STATE), F32),
               jnp.zeros((bp, N_SSM_GROUPS, SSM_STATE), F32),
               jnp.zeros((bp, N_RWKV_HEADS, RWKV_HEAD, RWKV_HEAD), F32),
               jnp.zeros((bp, N_SHIFT), F32),
               jnp.zeros((bp, 2, D_FF), F32))
    p = _stacked_params(w, state_rwkv, state_shift)
    new_p = [[] for _ in range(5)]
    new_s = [[] for _ in range(5)]
    for i in range(depth):
        tables = _s5_tables(w['ssm_lam_re'][i], w['ssm_lam_im'][i], w['ssm_log_dt'][i],
                            w['ssm_b_re'][i], w['ssm_b_im'][i], w['ssm_c_re'][i], w['ssm_c_im'][i])
        final = i == depth - 1
        hp, stp = _layer(hp, pe_p, zero_st, p, i, tables, lp, tm_prompt, final)
        st_in = (state_ssm_re[i].astype(F32), state_ssm_im[i].astype(F32), None, None,
                 state_conv[i].astype(F32))
        hs, sts = _layer(hs, pe_s, st_in, p, i, tables, ls, tm_sample, final)
        for j in range(5):
            new_p[j].append(stp[j])
            new_s[j].append(sts[j])
    y_prompt = hp.reshape(bp, lp, D_MODEL).astype(x_prompt.dtype)
    y_sample = hs.reshape(bs, ls, D_MODEL).astype(x_sample.dtype)
    dts = (state_ssm_re.dtype, state_ssm_im.dtype, state_rwkv.dtype, state_shift.dtype,
           state_conv.dtype)
    outs_p = tuple(jnp.stack(new_p[j]).astype(dts[j]) for j in range(5))
    outs_s = tuple(jnp.stack(new_s[j]).astype(dts[j]) for j in range(5))
    return (y_prompt, y_sample) + outs_p + outs_s


def kernel(x_prompt, x_sample, p_prompt, p_sample, state_ssm_re, state_ssm_im, state_rwkv, state_shift, state_conv, norm1_g, w_in, shift_mu, ssm_lam_re, ssm_lam_im, ssm_log_dt, ssm_b_re, ssm_b_im, ssm_c_re, ssm_c_im, ssm_d, ssm_w_glu, ssm_b_glu, rwkv_w0, rwkv_w2, rwkv_a0, rwkv_a2, rwkv_g2, rwkv_k_k, rwkv_k_a, rwkv_r_k, rwkv_lnx_g, rwkv_lnx_b, w_out, norm2_g, w_up, conv_w, conv_b, w_down, w_ple, w_pg, final_g):
    w = dict(norm1_g=norm1_g, w_in=w_in, shift_mu=shift_mu, ssm_lam_re=ssm_lam_re,
             ssm_lam_im=ssm_lam_im, ssm_log_dt=ssm_log_dt, ssm_b_re=ssm_b_re, ssm_b_im=ssm_b_im,
             ssm_c_re=ssm_c_re, ssm_c_im=ssm_c_im, ssm_d=ssm_d, ssm_w_glu=ssm_w_glu,
             ssm_b_glu=ssm_b_glu, rwkv_w0=rwkv_w0, rwkv_w2=rwkv_w2, rwkv_a0=rwkv_a0,
             rwkv_a2=rwkv_a2, rwkv_g2=rwkv_g2, rwkv_k_k=rwkv_k_k, rwkv_k_a=rwkv_k_a,
             rwkv_r_k=rwkv_r_k, rwkv_lnx_g=rwkv_lnx_g, rwkv_lnx_b=rwkv_lnx_b, w_out=w_out,
             norm2_g=norm2_g, w_up=w_up, conv_w=conv_w, conv_b=conv_b, w_down=w_down,
             w_ple=w_ple, w_pg=w_pg, final_g=final_g)
    lp = x_prompt.shape[1]
    ns = x_sample.shape[0] * x_sample.shape[1]
    return _forward(x_prompt, x_sample, p_prompt, p_sample, state_ssm_re, state_ssm_im,
                    state_rwkv, state_shift, state_conv, w,
                    tm_prompt=(min(1024, lp), min(512, lp)),
                    tm_sample=(min(512, ns), min(256, ns)))
```

```python
import functools
import math

import numpy as np
import jax
import jax.numpy as jnp
from jax import lax
from jax.experimental import pallas as pl
from jax.experimental.pallas import tpu as pltpu

F32 = jnp.float32
BF16 = jnp.bfloat16

D_MODEL = 1024
D_SSM = 512
D_RWKV = 512
SSM_GROUP = 16
N_SSM_GROUPS = 32
SSM_STATE = 64
RWKV_HEAD = 64
N_RWKV_HEADS = 8
DECAY_LORA = 64
AAA_LORA = 64
GATE_LORA = 128
N_SHIFT = 3 * D_RWKV + DECAY_LORA + AAA_LORA + GATE_LORA
N_IN = D_SSM + N_SHIFT
D_FF = 2816
D_PLE = 256
RMS_EPS = 1e-6
GN_EPS = 64e-5
L2_EPS = 1e-12

LANES = 128
FOLD_LANE_BLOCKS = D_SSM // LANES
S5_STEPS = 8
S5_ROW = S5_STEPS * D_SSM
S5_LANE_BLOCKS = D_SSM // LANES
S5_BLOCK_GROUPS = N_SSM_GROUPS // S5_LANE_BLOCKS
S5_BLOCK_STATE = 2 * S5_BLOCK_GROUPS * SSM_STATE
S5_STATE_ROW = S5_LANE_BLOCKS * S5_BLOCK_STATE
FF_CHUNK = 2 * LANES
V7X_VMEM_BYTES = 64 * 1024 * 1024
V7X_VMEM_LIMIT_BYTES = V7X_VMEM_BYTES * 7 // 8


def _mm(a, b):
    return jnp.dot(a.astype(BF16), b.astype(BF16), preferred_element_type=F32)


def _mm_nt(a, b):
    return lax.dot_general(a.astype(BF16), b.astype(BF16), (((1,), (1,)), ((), ())),
                           preferred_element_type=F32)


def _mm_tn(a, b):
    return lax.dot_general(a.astype(BF16), b.astype(BF16), (((0,), (0,)), ((), ())),
                           preferred_element_type=F32)


def _split2(x):
    hi = x.astype(BF16)
    return hi, (x - hi.astype(F32)).astype(BF16)


def _seg_sum(x, ones_bd):
    hi, lo = _split2(x)
    rows, gl = x.shape[0], ones_bd.shape[0]
    parts = []
    for g in range(0, x.shape[1], gl):
        both = jnp.dot(jnp.concatenate([hi[:, g:g + gl], lo[:, g:g + gl]], axis=0), ones_bd,
                       preferred_element_type=F32)
        parts.append(both[:rows] + both[rows:])
    return jnp.concatenate(parts, axis=1)


def _rmsnorm(x, g):
    return x * lax.rsqrt(jnp.mean(x * x, axis=-1, keepdims=True) + RMS_EPS) * g


def _sigmoid(x):
    return 1.0 / (1.0 + jnp.exp(-x))


def _const_spec(shape):
    nd = len(shape)
    return pl.BlockSpec(shape, lambda *_: (0,) * nd, pipeline_mode=pl.Buffered(1))


def _layer_spec(t, layer):
    if t.ndim == 2:
        return _const_spec(t.shape)
    nd = t.ndim - 1
    return pl.BlockSpec((None,) + t.shape[1:], lambda *_: (layer,) + (0,) * nd,
                        pipeline_mode=pl.Buffered(1))


def _expand_rows(x, seq_len):
    nseq, w = x.shape
    return jnp.broadcast_to(x[:, None, :], (nseq, seq_len, w)).reshape(nseq * seq_len, w)


def _step_rows(x, seq_len, t):
    return x.reshape(x.shape[0] // seq_len, seq_len, x.shape[1])[:, t, :]


def _fold_rows(x, scr_ref, out_ref):
    tm = x.shape[0]
    for q in range(FOLD_LANE_BLOCKS):
        scr_ref[q] = x[:, q * LANES:(q + 1) * LANES]
    for s in range(S5_STEPS):
        for q in range(FOLD_LANE_BLOCKS):
            c0 = s * D_SSM + q * LANES
            out_ref[:, c0:c0 + LANES] = scr_ref[q, pl.ds(s, tm // S5_STEPS, stride=S5_STEPS), :]


def _unfold_rows(x8, scr_ref):
    tm = x8.shape[0] * S5_STEPS
    for s in range(S5_STEPS):
        for q in range(FOLD_LANE_BLOCKS):
            c0 = s * D_SSM + q * LANES
            scr_ref[q, pl.ds(s, tm // S5_STEPS, stride=S5_STEPS), :] = x8[:, c0:c0 + LANES]
    return jnp.concatenate([scr_ref[q] for q in range(FOLD_LANE_BLOCKS)], axis=1)


def _inproj_kernel(h_ref, g_ref, w_ref, mu_ref, init_ref, *rest, seq_len, tm, layer):
    i = pl.program_id(0)
    row_of = lambda ref: ref[layer:layer + 1, :]
    long_seq = seq_len >= tm
    if long_seq:
        u8_ref, z_ref, last_ref, fold_ref, carry_ref = rest

        @pl.when(i == 0)
        def _():
            carry_ref[...] = jnp.zeros_like(carry_ref)
    else:
        u8_ref, z_ref, last_ref, fold_ref = rest
    xn = _rmsnorm(h_ref[...], row_of(g_ref))
    proj = jnp.dot(xn.astype(BF16), w_ref[...], preferred_element_type=F32)
    _fold_rows(proj[:, :D_SSM], fold_ref, u8_ref)
    zr = proj[:, D_SSM:]
    rolled = pltpu.roll(zr, 1, 0)
    row = lax.broadcasted_iota(jnp.int32, (tm, 1), 0)
    if long_seq:
        first = (i % (seq_len // tm)) == 0
        row0 = jnp.where(first, init_ref[0], carry_ref[...])
        prev = jnp.where(row == 0, row0, rolled)
        carry_ref[...] = zr[tm - 1:tm, :]
        last_ref[0] = zr[tm - 1:tm, :]
    else:
        prev = jnp.where(row % seq_len == 0, _expand_rows(init_ref[...], seq_len), rolled)
        last_ref[...] = _step_rows(zr, seq_len, seq_len - 1)
    z_ref[...] = (zr + (prev - zr) * row_of(mu_ref)).astype(z_ref.dtype)


def _inproj(h, p, layer, shift0, seq_len, tm):
    n = h.shape[0]
    nseq = n // seq_len
    fold_scratch = pltpu.VMEM((FOLD_LANE_BLOCKS, tm, LANES), F32)
    if seq_len >= tm:
        tps = seq_len // tm
        seq_spec = pl.BlockSpec((1, 1, N_SHIFT), lambda i: (i // tps, 0, 0))
        init_spec = seq_spec
        init = shift0.reshape(nseq, 1, N_SHIFT)
        last_shape = jax.ShapeDtypeStruct((nseq, 1, N_SHIFT), F32)
        scratch = [fold_scratch, pltpu.VMEM((1, N_SHIFT), F32)]
    else:
        spt = tm // seq_len
        seq_spec = pl.BlockSpec((spt, N_SHIFT), lambda i: (i, 0))
        init_spec = pl.BlockSpec((None, spt, N_SHIFT), lambda i: (layer, i, 0))
        init = p['state_shift']
        last_shape = jax.ShapeDtypeStruct((nseq, N_SHIFT), F32)
        scratch = [fold_scratch]
    g1, w_in, mu = p['norm1_g'], p['w_in'], p['shift_mu']
    u8, z, last = pl.pallas_call(
        functools.partial(_inproj_kernel, seq_len=seq_len, tm=tm, layer=layer),
        grid=(n // tm,),
        in_specs=[pl.BlockSpec((tm, D_MODEL), lambda i: (i, 0)),
                  _layer_spec(g1, layer), _layer_spec(w_in, layer), _layer_spec(mu, layer),
                  init_spec],
        out_specs=[pl.BlockSpec((tm // S5_STEPS, S5_ROW), lambda i: (i, 0)),
                   pl.BlockSpec((tm, N_SHIFT), lambda i: (i, 0)),
                   seq_spec],
        out_shape=[jax.ShapeDtypeStruct((n // S5_STEPS, S5_ROW), F32),
                   jax.ShapeDtypeStruct((n, N_SHIFT), BF16),
                   last_shape],
        scratch_shapes=scratch,
        compiler_params=pltpu.CompilerParams(dimension_semantics=("arbitrary",),
                                             vmem_limit_bytes=V7X_VMEM_LIMIT_BYTES),
        name="inproj",
    )(h, g1, w_in, mu, init)
    return u8, z, last.reshape(nseq, N_SHIFT)


def _s5_prep_kernel(lr_ref, li_ref, ldt_ref, br_ref, bi_ref, cr_ref, ci_ref, tile_ref, sel_ref,
                    wx_ref, wy_ref, apr_ref, api_ref):
    lr, li = lr_ref[...], li_ref[...]
    dt = jnp.exp(ldt_ref[...])
    mag = jnp.exp(lr * dt)
    ar = mag * jnp.cos(li * dt)
    ai = mag * jnp.sin(li * dt)
    den = lr * lr + li * li
    nr = ar - 1.0
    fr = (nr * lr + ai * li) / den
    fi = (ai * lr - nr * li) / den
    br, bi = br_ref[...], bi_ref[...]
    bbr = fr * br - fi * bi
    bbi = fr * bi + fi * br
    cr, ci = cr_ref[...], ci_ref[...]
    pr = [jnp.ones_like(ar)]
    pi = [jnp.zeros_like(ar)]
    for _ in range(S5_STEPS):
        pr.append(pr[-1] * ar - pi[-1] * ai)
        pi.append(pr[-2] * ai + pi[-1] * ar)
    nrow = lr.shape[0]
    blk = nrow // S5_LANE_BLOCKS
    half = S5_BLOCK_STATE // 2
    ri = lax.broadcasted_iota(jnp.int32, (nrow, nrow), 0)
    ci_ = lax.broadcasted_iota(jnp.int32, (nrow, nrow), 1)
    same_group = (ri // SSM_GROUP) == (ci_ // SSM_GROUP)
    keep_in = ((ri % blk) // SSM_GROUP) == (ci_ // SSM_STATE)
    keep_out = (ri // SSM_STATE) == ((ci_ % blk) // SSM_GROUP)
    tile, sel = tile_ref[...], sel_ref[...]
    nt = (((1,), (1,)), ((), ()))
    c_hi, c_lo = _split2(jnp.concatenate([cr, -ci], axis=1))
    wy_ref[...] = jnp.zeros(wy_ref.shape, wy_ref.dtype)
    for s in range(S5_STEPS):
        qr, qi = pr[S5_STEPS - 1 - s], pi[S5_STEPS - 1 - s]
        wr = qr * bbr - qi * bbi
        wi = qr * bbi + qi * bbr
        for h, w_ in enumerate((wr, wi)):
            t = jnp.dot(w_.astype(BF16), tile, preferred_element_type=F32)
            t = jnp.where(keep_in, t, 0.0).astype(BF16)
            for q in range(S5_LANE_BLOCKS):
                wx_ref[q, s * blk:(s + 1) * blk, h * half:(h + 1) * half] = t[q * blk:(q + 1) * blk, :]
        tau = S5_STEPS - 1 - s
        lhs_hi, lhs_lo = _split2(jnp.concatenate([wr, wi], axis=1))
        kt = (lax.dot_general(lhs_hi, c_hi, nt, preferred_element_type=F32)
              + lax.dot_general(lhs_hi, c_lo, nt, preferred_element_type=F32)
              + lax.dot_general(lhs_lo, c_hi, nt, preferred_element_type=F32))
        kt = jnp.where(same_group, kt, 0.0).astype(BF16)
        for q in range(S5_LANE_BLOCKS):
            kq = kt[q * blk:(q + 1) * blk, q * blk:(q + 1) * blk]
            for s_in in range(S5_STEPS - tau):
                s_out = s_in + tau
                r0 = S5_BLOCK_STATE + s_in * blk
                wy_ref[q, s_out // 2, r0:r0 + blk, (s_out % 2) * blk:(s_out % 2 + 1) * blk] = kq
        mr = cr * pr[s + 1] - ci * pi[s + 1]
        mi = cr * pi[s + 1] + ci * pr[s + 1]
        for h, m_ in enumerate((mr, -mi)):
            t = lax.dot_general(sel, m_.astype(BF16), nt, preferred_element_type=F32)
            t = jnp.where(keep_out, t, 0.0).astype(BF16)
            for q in range(S5_LANE_BLOCKS):
                wy_ref[q, s // 2, h * half:(h + 1) * half, (s % 2) * blk:(s % 2 + 1) * blk] = (
                    t[:, q * blk:(q + 1) * blk])
    a8r, a8i = pr[S5_STEPS], pi[S5_STEPS]
    er, ei = a8r, a8i
    for n in range(S5_STEPS):
        apr_ref[n] = er
        api_ref[n] = ei
        er, ei = er * a8r - ei * a8i, er * a8i + ei * a8r


def _s5_tables(lam_re, lam_im, log_dt, b_re, b_im, c_re, c_im):
    G, P, K = N_SSM_GROUPS, SSM_STATE, SSM_GROUP
    Q, GB = S5_LANE_BLOCKS, G // S5_LANE_BLOCKS
    rep = lambda t: jnp.repeat(t, K, axis=0)
    tile = np.tile(np.eye(P, dtype=np.float32), (1, GB))
    args = (rep(lam_re), rep(lam_im), rep(jnp.broadcast_to(log_dt[:, None], (G, P))),
            jnp.swapaxes(b_re, 1, 2).reshape(G * K, P), jnp.swapaxes(b_im, 1, 2).reshape(G * K, P),
            c_re.reshape(G * K, P), c_im.reshape(G * K, P),
            jnp.asarray(tile, BF16), jnp.asarray(tile.T, BF16))
    t3 = jax.ShapeDtypeStruct((S5_STEPS, G * K, P), F32)
    wx, wy, apr, api = pl.pallas_call(
        _s5_prep_kernel,
        out_shape=[jax.ShapeDtypeStruct((Q, S5_STEPS * GB * K, S5_BLOCK_STATE), BF16),
                   jax.ShapeDtypeStruct((Q, S5_STEPS // 2, S5_BLOCK_STATE + S5_STEPS * GB * K,
                                         2 * GB * K), BF16),
                   t3, t3],
        compiler_params=pltpu.CompilerParams(vmem_limit_bytes=V7X_VMEM_LIMIT_BYTES),
        name="s5_prep",
    )(*args)
    ap = jnp.concatenate([t[:, ::K, :].reshape(S5_STEPS, Q, GB * P) for t in (apr, api)], axis=-1)
    return wx, wy, ap.reshape(S5_STEPS, S5_STATE_ROW)


def _state_to_lanes(h_re, h_im):
    n = h_re.shape[0]
    parts = [t.reshape(n, S5_LANE_BLOCKS, -1) for t in (h_re, h_im)]
    return jnp.concatenate(parts, axis=-1).reshape(n, S5_STATE_ROW)


def _lanes_to_state(h):
    n = h.shape[0]
    h = h.reshape(n, S5_LANE_BLOCKS, 2, N_SSM_GROUPS // S5_LANE_BLOCKS, SSM_STATE)
    return (h[:, :, 0].reshape(n, N_SSM_GROUPS, SSM_STATE), h[:, :, 1].reshape(n, N_SSM_GROUPS, SSM_STATE))


def _s5_kernel(u_ref, h0_ref, wx_ref, wy_ref, ap_ref, y_ref, hl_ref, *scratch, rows, scan):
    half = S5_BLOCK_STATE // 2
    cw = D_SSM // S5_LANE_BLOCKS
    blocks = range(S5_LANE_BLOCKS)
    ub = u_ref[...].astype(BF16)
    ucat = [jnp.concatenate([ub[:, s * D_SSM + q * cw: s * D_SSM + (q + 1) * cw]
                             for s in range(S5_STEPS)], axis=1) for q in blocks]
    x = [jnp.dot(ucat[q], wx_ref[q], preferred_element_type=F32) for q in blocks]
    ap = [ap_ref[:, q * S5_BLOCK_STATE:(q + 1) * S5_BLOCK_STATE] for q in blocks]

    def emit_outputs(q, hprev):
        lhs = jnp.concatenate([hprev, ucat[q]], axis=1)
        for j in range(S5_STEPS // 2):
            kk = S5_BLOCK_STATE + cw * (2 * j + 2)
            y2 = jnp.dot(lhs[:, :kk], wy_ref[q, j, :kk, :], preferred_element_type=F32)
            c0 = (2 * j) * D_SSM + q * cw
            c1 = (2 * j + 1) * D_SSM + q * cw
            y_ref[:, c0:c0 + cw] = y2[:, :cw]
            y_ref[:, c1:c1 + cw] = y2[:, cw:]

    if scan:
        hs_ref, hp_ref = scratch
        sub3 = lax.broadcasted_iota(jnp.int32, (1, 8, 1), 1)
        sub2 = lax.broadcasted_iota(jnp.int32, (8, 1), 0)
        for q in blocks:
            xr = x[q][:, :half].reshape(rows // 8, 8, half)
            xi = x[q][:, half:].reshape(rows // 8, 8, half)
            for k in (1, 2, 4):
                er = jnp.where(sub3 >= k, ap[q][k - 1:k, :half].reshape(1, 1, half), 0.0)
                ei = jnp.where(sub3 >= k, ap[q][k - 1:k, half:].reshape(1, 1, half), 0.0)
                sr = pltpu.roll(xr, k, 1)
                si = pltpu.roll(xi, k, 1)
                xr, xi = xr + er * sr - ei * si, xi + er * si + ei * sr
            hs_ref[q, :, :half] = xr.reshape(rows, half)
            hs_ref[q, :, half:] = xi.reshape(rows, half)
        h0 = h0_ref[0]
        for q in blocks:
            lo = q * S5_BLOCK_STATE
            cr, ci = h0[:, lo:lo + half], h0[:, lo + half:lo + S5_BLOCK_STATE]
            pwr, pwi = ap[q][:, :half], ap[q][:, half:]
            for j in range(rows // 8):
                sl = slice(j * 8, (j + 1) * 8)
                hr = hs_ref[q, sl, :half] + pwr * cr - pwi * ci
                hi = hs_ref[q, sl, half:] + pwr * ci + pwi * cr
                hp_ref[q, sl, :half] = jnp.where(sub2 == 0, cr, pltpu.roll(hr, 1, 0))
                hp_ref[q, sl, half:] = jnp.where(sub2 == 0, ci, pltpu.roll(hi, 1, 0))
                cr, ci = hr[7:8, :], hi[7:8, :]
            hl_ref[0, :, lo:lo + half] = cr
            hl_ref[0, :, lo + half:lo + S5_BLOCK_STATE] = ci
            emit_outputs(q, hp_ref[q].astype(BF16))
    else:
        for q in blocks:
            lo = q * S5_BLOCK_STATE
            h0r = h0_ref[:, lo:lo + half]
            h0i = h0_ref[:, lo + half:lo + S5_BLOCK_STATE]
            er, ei = ap[q][0:1, :half], ap[q][0:1, half:]
            hl_ref[:, lo:lo + half] = er * h0r - ei * h0i + x[q][:, :half]
            hl_ref[:, lo + half:lo + S5_BLOCK_STATE] = er * h0i + ei * h0r + x[q][:, half:]
            emit_outputs(q, jnp.concatenate([h0r, h0i], axis=1).astype(BF16))


def _s5(u8, h_re, h_im, tables, seq_len):
    wx, wy, ap = tables
    n = u8.shape[0] * S5_STEPS
    nseq = n // seq_len
    h0 = _state_to_lanes(h_re, h_im)
    cps = seq_len // S5_STEPS
    scan = cps > 1
    if scan:
        rows = cps
        grid = (nseq,)
        h0 = h0.reshape(nseq, 1, S5_STATE_ROW)
        h_spec = pl.BlockSpec((1, 1, S5_STATE_ROW), lambda i: (i, 0, 0))
        h_shape = jax.ShapeDtypeStruct((nseq, 1, S5_STATE_ROW), F32)
    else:
        rows = min(nseq, 128)
        grid = (nseq // rows,)
        h_spec = pl.BlockSpec((rows, S5_STATE_ROW), lambda i: (i, 0))
        h_shape = jax.ShapeDtypeStruct((nseq, S5_STATE_ROW), F32)
    y8, hl = pl.pallas_call(
        functools.partial(_s5_kernel, rows=rows, scan=scan),
        grid=grid,
        in_specs=[pl.BlockSpec((rows, S5_ROW), lambda i: (i, 0)),
                  h_spec,
                  _const_spec(wx.shape), _const_spec(wy.shape), _const_spec(ap.shape)],
        out_specs=[pl.BlockSpec((rows, S5_ROW), lambda i: (i, 0)), h_spec],
        out_shape=[jax.ShapeDtypeStruct((n // S5_STEPS, S5_ROW), F32), h_shape],
        scratch_shapes=[pltpu.VMEM((S5_LANE_BLOCKS, rows, S5_BLOCK_STATE), F32)] * 2 if scan else [],
        compiler_params=pltpu.CompilerParams(dimension_semantics=("arbitrary",),
                                             vmem_limit_bytes=V7X_VMEM_LIMIT_BYTES),
        name="s5",
    )(u8, h0, wx, wy, ap)
    hre_new, him_new = _lanes_to_state(hl.reshape(nseq, S5_STATE_ROW))
    return y8, hre_new, him_new


RW_CHUNK = 64
RW_STEP_CHUNKS_LONG = 8
RW_STEP_CHUNKS_SHORT = 4
RW_GROUPS = 2
RW_GROUP_HEADS = N_RWKV_HEADS // RW_GROUPS
RW_GROUP_LANES = D_RWKV // RW_GROUPS


def _split3(x):
    hi = x.astype(BF16)
    r1 = x - hi.astype(F32)
    mid = r1.astype(BF16)
    lo = (r1 - mid.astype(F32)).astype(BF16)
    return hi, mid, lo


def _bd(x, hm_ref):
    xb = x.astype(BF16)
    return jnp.concatenate([xb * hm_ref[h] for h in range(RW_GROUP_HEADS)], axis=0)


def _unbd(f, hm_ref):
    n = RWKV_HEAD
    out = f[:n] * hm_ref[0].astype(F32)
    for h in range(1, RW_GROUP_HEADS):
        out = out + f[h * n:(h + 1) * n] * hm_ref[h].astype(F32)
    return out


def _rwkv_kernel(z_ref, s0_ref, w0_ref, w2_ref, a0_ref, a2_ref, g2_ref, kk_ref, ka_ref, rk_ref,
                 lg_ref, lb_ref, ones_ref, hm_ref, cm_ref, tri_ref, y_ref, sl_ref, ypre_ref,
                 *scratch, seg_len, step_chunks, layer):
    row_of = lambda ref: ref[layer:layer + 1, :]
    C = RW_CHUNK
    N = RWKV_HEAD
    GL = RW_GROUP_LANES
    chained = seg_len == C
    if chained:
        (st_ref,) = scratch
        b = pl.program_id(1)

        @pl.when(b == 0)
        def _():
            for g in range(RW_GROUPS):
                st_ref[g] = jnp.concatenate(
                    [s0_ref[0, h] for h in range(g * RW_GROUP_HEADS, (g + 1) * RW_GROUP_HEADS)], axis=1)

    ones_bd = ones_ref[...]
    z = z_ref[...].astype(F32)
    r = z[:, :D_RWKV]
    k = z[:, D_RWKV:2 * D_RWKV]
    v = z[:, 2 * D_RWKV:3 * D_RWKV]
    o = 3 * D_RWKV
    xw = z[:, o:o + DECAY_LORA]
    xa = z[:, o + DECAY_LORA:o + DECAY_LORA + AAA_LORA]
    xg = z[:, o + DECAY_LORA + AAA_LORA:]
    wd = -(row_of(w0_ref) + _mm(jnp.tanh(xw), w2_ref[...]))
    w = -(jnp.maximum(wd, 0.0) + jnp.log1p(jnp.exp(-jnp.abs(wd)))) - 0.5
    lw = -jnp.exp(w)
    a = _sigmoid(row_of(a0_ref) + _mm(xa, a2_ref[...]))
    out_gate = _mm(_sigmoid(xg), g2_ref[...])
    kk = k * row_of(kk_ref)
    kk = kk / jnp.maximum(jnp.sqrt(_seg_sum(kk * kk, ones_bd)), L2_EPS)
    kmod = k * (1.0 + (a - 1.0) * row_of(ka_ref))

    tri = tri_ref[...]
    cw = sum(jnp.dot(tri, t, preferred_element_type=F32) for t in _split3(lw))
    tot = _expand_rows(_step_rows(cw, seg_len, seg_len - 1), seg_len)
    w_in = jnp.exp(cw)
    w_tail = jnp.exp(tot - cw)
    w_inv = jnp.exp(-cw)
    w_all = jnp.exp(tot)
    ah = -kk * jnp.exp(cw - lw)
    bh = kk * a * w_inv
    kh = kmod * w_inv
    rh = r * w_in
    bt = kk * a * w_tail
    kt = kmod * w_tail
    strict = cm_ref[0] > 0.0
    incl = cm_ref[1] > 0.0
    eye_cat = cm_ref[2]
    n_fac = seg_len.bit_length() - 1

    items = [(c, g) for c in range(step_chunks) for g in range(RW_GROUPS)]
    sub = lambda t, c, g: t[c * C:(c + 1) * C, g * GL:(g + 1) * GL]
    bd = lambda t: _bd(t, hm_ref)
    A = [sub(ah, c, g) for c, g in items]
    R = [sub(rh, c, g) for c, g in items]
    V = [sub(v, c, g) for c, g in items]
    G = [_mm_nt(jnp.concatenate([A[i], R[i]], axis=0),
                jnp.concatenate([bd(sub(bh, c, g)), bd(sub(kh, c, g))], axis=0))
         for i, (c, g) in enumerate(items)]
    AB = [jnp.where(strict, t[:C, :GL], 0.0) for t in G]
    AK = [jnp.where(strict, t[:C, GL:], 0.0) for t in G]
    RB = [jnp.where(incl, t[C:, :GL], 0.0) for t in G]
    RK = [jnp.where(incl, t[C:, GL:], 0.0) for t in G]
    KV = [_mm(jnp.concatenate([AK[i], RK[i]], axis=0), bd(V[i])) for i in range(len(items))]
    T = [eye_cat + t for t in AB]
    P = [_mm(t, bd(t)) for t in AB]
    for _ in range(n_fac - 2):
        PT = [_mm(jnp.concatenate([P[i], T[i]], axis=0), bd(P[i])) for i in range(len(items))]
        P = [t[:C] for t in PT]
        T = [T[i] + PT[i][C:] for i in range(len(items))]
    T = [T[i] + _mm(T[i], bd(P[i])) for i in range(len(items))]
    RBT = [_mm(RB[i], bd(T[i])) for i in range(len(items))]
    X = [_mm(jnp.concatenate([T[i], RBT[i]], axis=0),
             jnp.concatenate([bd(A[i]), bd(KV[i][:C])], axis=1)) for i in range(len(items))]
    TA = [t[:C, :GL] for t in X]
    U0 = [t[:C, GL:] for t in X]
    Rt = [R[i] + X[i][C:, :GL] for i in range(len(items))]
    Y0 = [X[i][C:, GL:] + KV[i][C:] for i in range(len(items))]
    nseg = C // seg_len
    heads = lambda g: range(g * RW_GROUP_HEADS, (g + 1) * RW_GROUP_HEADS)
    head_lanes = lambda h: slice((h % RW_GROUP_HEADS) * N, (h % RW_GROUP_HEADS + 1) * N)
    if chained:
        PhiT, PsiT = [], []
        for i, (c, g) in enumerate(items):
            Bt, Kt, Wa = sub(bt, c, g), sub(kt, c, g), sub(w_all, c, g)
            PhiT.append(bd(eye_cat * Wa[0:1, :] + _unbd(_mm_tn(TA[i], Bt), hm_ref)))
            PsiT.append(_unbd(_mm_tn(jnp.concatenate([U0[i], V[i]], axis=0),
                                     jnp.concatenate([Bt, Kt], axis=0)), hm_ref))
        states = [st_ref[g] for g in range(RW_GROUPS)]
        for i, (c, g) in enumerate(items):
            S = states[g]
            ypre_ref[c * C:(c + 1) * C, g * GL:(g + 1) * GL] = _mm_nt(Rt[i], bd(S)) + Y0[i]
            states[g] = _mm(S, PhiT[i]) + PsiT[i]
        for g in range(RW_GROUPS):
            st_ref[g] = states[g]
            for h in heads(g):
                sl_ref[0, h] = states[g][:, head_lanes(h)]
    else:
        segs = [(i, c, g, s) for i, (c, g) in enumerate(items) for s in range(nseg)]
        rows_of = lambda s: slice(s * seg_len, (s + 1) * seg_len)
        S0 = [jnp.concatenate([s0_ref[c * nseg + s, h] for h in heads(g)], axis=1)
              for i, c, g, s in segs]
        RU = [_mm_nt(jnp.concatenate([Rt[i][rows_of(s)], TA[i][rows_of(s)]], axis=0), bd(S0[n]))
              for n, (i, c, g, s) in enumerate(segs)]
        for n, (i, c, g, s) in enumerate(segs):
            ypre_ref[c * C + s * seg_len:c * C + (s + 1) * seg_len, g * GL:(g + 1) * GL] = (
                RU[n][:seg_len] + Y0[i][rows_of(s)])
        UB = [_mm_tn(jnp.concatenate([RU[n][seg_len:] + U0[i][rows_of(s)], V[i][rows_of(s)]], axis=0),
                     jnp.concatenate([sub(bt, c, g)[rows_of(s)], sub(kt, c, g)[rows_of(s)]], axis=0))
              for n, (i, c, g, s) in enumerate(segs)]
        for n, (i, c, g, s) in enumerate(segs):
            S = S0[n] * sub(w_all, c, g)[s * seg_len:s * seg_len + 1, :] + _unbd(UB[n], hm_ref)
            for h in heads(g):
                sl_ref[c * nseg + s, h] = S[:, head_lanes(h)]
    y = ypre_ref[...]

    mu = _seg_sum(y, ones_bd) * (1.0 / N)
    d = y - mu
    var = _seg_sum(d * d, ones_bd) * (1.0 / N)
    yn = d * lax.rsqrt(var + GN_EPS) * row_of(lg_ref) + row_of(lb_ref)
    bonus = _seg_sum(r * kmod * row_of(rk_ref), ones_bd) * v
    y_ref[...] = (yn + bonus) * out_gate


def _rwkv_constants(seg_len, rows):
    n, gh, gl = RWKV_HEAD, RW_GROUP_HEADS, RW_GROUP_LANES
    lane_head = np.arange(gl) // n
    hm = (lane_head[None, None, :] == np.arange(gh)[:, None, None]) * np.ones((gh, n, gl))
    i = np.arange(RW_CHUNK)[:, None]
    j = (np.arange(gl) % n)[None, :]
    same = (i // seg_len) == (j // seg_len)
    cm = np.stack([(j < i) & same, (j <= i) & same, j == i]).astype(np.float32)
    ri = np.arange(rows)[:, None]
    rj = np.arange(rows)[None, :]
    same_r = (ri // seg_len) == (rj // seg_len)
    tri = ((rj <= ri) & same_r).astype(np.float32)
    ones_bd = np.kron(np.eye(gh), np.ones((n, n)))
    return (jnp.asarray(ones_bd, BF16), jnp.asarray(hm, BF16), jnp.asarray(cm, F32),
            jnp.asarray(tri, BF16))


def _rwkv(z, s0, p, layer, seq_len):
    n = z.shape[0]
    if seq_len >= RW_CHUNK:
        step_chunks = min(RW_STEP_CHUNKS_LONG, seq_len // RW_CHUNK)
    else:
        step_chunks = min(RW_STEP_CHUNKS_SHORT, n // RW_CHUNK)
    nseq = n // seq_len
    nh, hd = N_RWKV_HEADS, RWKV_HEAD
    seg_len = min(seq_len, RW_CHUNK)
    rows = RW_CHUNK * step_chunks
    chained = seq_len >= RW_CHUNK
    if chained:
        steps = seq_len // rows
        grid = (nseq, steps)
        row_map = lambda s, b: (s * steps + b, 0)
        state_spec = pl.BlockSpec((1, nh, hd, hd), lambda s, b: (s, 0, 0, 0))
        scratch = [pltpu.VMEM((rows, D_RWKV), F32), pltpu.VMEM((RW_GROUPS, hd, RW_GROUP_LANES), F32)]
    else:
        grid = (n // rows, 1)
        row_map = lambda s, b: (s, 0)
        state_spec = pl.BlockSpec((rows // seq_len, nh, hd, hd), lambda s, b: (s, 0, 0, 0))
        scratch = [pltpu.VMEM((rows, D_RWKV), F32)]
    if s0 is None:
        s0 = p['state_rwkv']
        s0_spec = pl.BlockSpec((None,) + state_spec.block_shape,
                               lambda s, b: (layer, s, 0, 0, 0))
    else:
        s0_spec = state_spec
    stacked = [p[k] for k in ('w0', 'w2', 'a0', 'a2', 'g2', 'k_k', 'k_a', 'r_k', 'lnx_g', 'lnx_b')]
    consts = list(_rwkv_constants(seg_len, rows))
    y, s_new = pl.pallas_call(
        functools.partial(_rwkv_kernel, seg_len=seg_len, step_chunks=step_chunks, layer=layer),
        grid=grid,
        in_specs=[pl.BlockSpec((rows, N_SHIFT), row_map), s0_spec]
        + [_layer_spec(t, layer) for t in stacked] + [_const_spec(t.shape) for t in consts],
        out_specs=[pl.BlockSpec((rows, D_RWKV), row_map), state_spec],
        out_shape=[jax.ShapeDtypeStruct((n, D_RWKV), F32),
                   jax.ShapeDtypeStruct((nseq, nh, hd, hd), F32)],
        scratch_shapes=scratch,
        compiler_params=pltpu.CompilerParams(dimension_semantics=("arbitrary", "arbitrary"),
                                             vmem_limit_bytes=V7X_VMEM_LIMIT_BYTES),
        name="rwkv7",
    )(z, s0, *stacked, *consts)
    return y, s_new


def _ffn_kernel(h_ref, y8_ref, u8_ref, yb_ref, pe_ref, c1_ref, c2_ref, d8_ref, wglu_ref, bglu_ref,
                wout_ref, g2_ref, wup_ref, cw_ref, cb_ref, wdn_ref, wple_ref, wpg_ref, gf_ref,
                *rest, seq_len, tm, final, layer):
    row_of = lambda ref: ref[layer:layer + 1, :]
    i = pl.program_id(0)
    long_seq = seq_len >= tm
    if long_seq:
        o_ref, ga_ref, gb_ref, fold_ref, act_ref, carry_ref = rest

        @pl.when(i == 0)
        def _():
            carry_ref[...] = jnp.zeros_like(carry_ref)
    else:
        o_ref, ga_ref, gb_ref, fold_ref, act_ref = rest
    ya = y8_ref[...] + row_of(d8_ref) * u8_ref[...]
    c_gelu = math.sqrt(2.0 / math.pi)
    ya = ya * (0.5 * (1.0 + jnp.tanh(c_gelu * (ya + 0.044715 * (ya * ya * ya)))))
    ya = _unfold_rows(ya, fold_ref)
    ya = ya * _sigmoid(_mm(ya, wglu_ref[...]) + row_of(bglu_ref))
    h1 = (h_ref[...] + _mm(ya, wout_ref[:D_SSM, :]) + _mm(yb_ref[...], wout_ref[D_SSM:, :]))
    x2 = _rmsnorm(h1, row_of(g2_ref)).astype(BF16)
    row = lax.broadcasted_iota(jnp.int32, (tm, 1), 0)
    if long_seq:
        first = (i % (seq_len // tm)) == 0
    else:
        t = row % seq_len
    def up(c):
        cs = slice(c * FF_CHUNK, (c + 1) * FF_CHUNK)
        gs = slice(D_FF + c * FF_CHUNK, D_FF + (c + 1) * FF_CHUNK)
        return (jnp.dot(x2, wup_ref[:, cs], preferred_element_type=F32),
                jnp.dot(x2, wup_ref[:, gs], preferred_element_type=F32))

    n_chunks = D_FF // FF_CHUNK
    nxt = up(0)
    for c in range(n_chunks):
        cs = slice(c * FF_CHUNK, (c + 1) * FF_CHUNK)
        val, gate = nxt
        if c + 1 < n_chunks:
            nxt = up(c + 1)
        r1 = pltpu.roll(gate, 1, 0)
        r2 = pltpu.roll(gate, 2, 0)
        if long_seq:
            m1 = jnp.where(first, c1_ref[0][:, cs], carry_ref[1:2, cs])
            m2 = jnp.where(first, c2_ref[0][:, cs], carry_ref[0:1, cs])
            p1 = jnp.where(row == 0, m1, r1)
            p2 = jnp.where(row == 0, m2, jnp.where(row == 1, m1, r2))
            carry_ref[:, cs] = gate[tm - 2:tm, :]
            ga_ref[0, :, cs] = gate[tm - 2:tm - 1, :]
            gb_ref[0, :, cs] = gate[tm - 1:tm, :]
        else:
            m1 = _expand_rows(c1_ref[:, cs], seq_len)
            m2 = _expand_rows(c2_ref[:, cs], seq_len)
            p1 = jnp.where(t == 0, m1, r1)
            p2 = jnp.where(t == 0, m2, jnp.where(t == 1, m1, r2))
            ga_ref[:, cs] = _step_rows(gate, seq_len, seq_len - 2)
            gb_ref[:, cs] = _step_rows(gate, seq_len, seq_len - 1)
        conv = (cb_ref[layer:layer + 1, cs] + cw_ref[2:3, cs] * gate + cw_ref[1:2, cs] * p1 + cw_ref[0:1, cs] * p2)
        act_ref[:, cs] = (conv * _sigmoid(conv) * val).astype(BF16)
    h2 = h1 + jnp.dot(act_ref[...], wdn_ref[...], preferred_element_type=F32)
    h3 = h2 + _mm(pe_ref[...], wple_ref[...]) * _sigmoid(_mm(h2, wpg_ref[...]))
    if final:
        h3 = _rmsnorm(h3, gf_ref[...])
    o_ref[...] = h3


def _ffn(h, y8, u8, yb, pe, conv0, p, layer, seq_len, tm, final):
    n = h.shape[0]
    nseq = n // seq_len
    row_spec = lambda w: pl.BlockSpec((tm, w), lambda i: (i, 0))
    pe_spec = pl.BlockSpec((None, tm, D_PLE), lambda i: (layer, i, 0))
    fold_spec = pl.BlockSpec((tm // S5_STEPS, S5_ROW), lambda i: (i, 0))
    scratch = [pltpu.VMEM((FOLD_LANE_BLOCKS, tm, LANES), F32), pltpu.VMEM((tm, D_FF), BF16)]
    if seq_len >= tm:
        tps = seq_len // tm
        seq_spec = pl.BlockSpec((1, 1, D_FF), lambda i: (i // tps, 0, 0))
        seq_shape = (nseq, 1, D_FF)
        scratch.append(pltpu.VMEM((2, D_FF), F32))
    else:
        seq_spec = pl.BlockSpec((tm // seq_len, D_FF), lambda i: (i, 0))
        seq_shape = (nseq, D_FF)
    stacked = [p[k] for k in ('d8', 'w_glu', 'b_glu', 'w_out', 'norm2_g', 'w_up', 'conv_w',
                              'conv_b', 'w_down', 'w_ple', 'w_pg')]
    consts = [p['final_g']]
    weights = stacked + consts
    out, ga, gb = pl.pallas_call(
        functools.partial(_ffn_kernel, seq_len=seq_len, tm=tm, final=final, layer=layer),
        grid=(n // tm,),
        in_specs=[row_spec(D_MODEL), fold_spec, fold_spec, row_spec(D_RWKV), pe_spec,
                  seq_spec, seq_spec] + [_layer_spec(t, layer) for t in stacked]
        + [_const_spec(t.shape) for t in consts],
        out_specs=[row_spec(D_MODEL), seq_spec, seq_spec],
        out_shape=[jax.ShapeDtypeStruct((n, D_MODEL), F32)]
        + [jax.ShapeDtypeStruct(seq_shape, F32)] * 2,
        scratch_shapes=scratch,
        compiler_params=pltpu.CompilerParams(dimension_semantics=("arbitrary",),
                                             vmem_limit_bytes=V7X_VMEM_LIMIT_BYTES),
        name="ffn",
    )(h, y8, u8, yb, pe, conv0[:, 1].reshape(seq_shape), conv0[:, 0].reshape(seq_shape), *weights)
    conv_new = jnp.stack([ga.reshape(nseq, D_FF), gb.reshape(nseq, D_FF)], axis=1)
    return out, conv_new


def _layer(h, pe, st, p, layer, s5_tables, seq_len, tm, final):
    ssm_re0, ssm_im0, rwkv0, shift0, conv0 = st
    tm_in, tm_ffn = tm
    u8, z, shift_new = _inproj(h, p, layer, shift0, seq_len, tm_in)
    y8, hre, him = _s5(u8, ssm_re0, ssm_im0, s5_tables, seq_len)
    yb, s_last = _rwkv(z, rwkv0, p, layer, seq_len)
    h, conv_new = _ffn(h, y8, u8, yb, pe, conv0, p, layer, seq_len, tm_ffn, final)
    return h, (hre, him, s_last, shift_new, conv_new)


def _stacked_params(w, state_rwkv, state_shift):
    depth = w['w_in'].shape[0]
    row = lambda t: t.reshape(depth, -1).astype(F32)
    bf = lambda t: t.astype(BF16)
    return {
        'norm1_g': row(w['norm1_g']), 'w_in': bf(w['w_in']), 'shift_mu': row(w['shift_mu']),
        'd8': jnp.tile(row(w['ssm_d']), (1, S5_STEPS)), 'w_glu': bf(w['ssm_w_glu']),
        'b_glu': row(w['ssm_b_glu']),
        'w0': row(w['rwkv_w0']), 'w2': bf(w['rwkv_w2']), 'a0': row(w['rwkv_a0']),
        'a2': bf(w['rwkv_a2']), 'g2': bf(w['rwkv_g2']), 'k_k': row(w['rwkv_k_k']),
        'k_a': row(w['rwkv_k_a']), 'r_k': row(w['rwkv_r_k']), 'lnx_g': row(w['rwkv_lnx_g']),
        'lnx_b': row(w['rwkv_lnx_b']), 'w_out': bf(w['w_out']), 'norm2_g': row(w['norm2_g']),
        'w_up': bf(w['w_up']), 'conv_w': w['conv_w'].astype(F32), 'conv_b': row(w['conv_b']),
        'w_down': bf(w['w_down']), 'w_ple': bf(w['w_ple']), 'w_pg': bf(w['w_pg']),
        'final_g': w['final_g'].reshape(1, -1).astype(F32),
        'state_rwkv': state_rwkv.astype(F32), 'state_shift': state_shift.astype(F32),
    }


def _forward(x_prompt, x_sample, p_prompt, p_sample, state_ssm_re, state_ssm_im, state_rwkv,
             state_shift, state_conv, w, tm_prompt, tm_sample):
    depth = w['w_in'].shape[0]
    bp, lp, _ = x_prompt.shape
    bs, ls, _ = x_sample.shape
    hp = x_prompt.reshape(bp * lp, D_MODEL).astype(F32)
    hs = x_sample.reshape(bs * ls, D_MODEL).astype(F32)
    pe_p = p_prompt.reshape(depth, bp * lp, D_PLE).astype(F32)
    pe_s = p_sample.reshape(depth, bs * ls, D_PLE).astype(F32)
    zero_st = (jnp.zeros((bp, N_SSM_GROUPS, SSM_STATE), F32),
               jnp.zeros((bp, N_SSM_GROUPS, SSM_STATE), F32),
               jnp.zeros((bp, N_RWKV_HEADS, RWKV_HEAD, RWKV_HEAD), F32),
               jnp.zeros((bp, N_SHIFT), F32),
               jnp.zeros((bp, 2, D_FF), F32))
    p = _stacked_params(w, state_rwkv, state_shift)
    new_p = [[] for _ in range(5)]
    new_s = [[] for _ in range(5)]
    for i in range(depth):
        tables = _s5_tables(w['ssm_lam_re'][i], w['ssm_lam_im'][i], w['ssm_log_dt'][i],
                            w['ssm_b_re'][i], w['ssm_b_im'][i], w['ssm_c_re'][i], w['ssm_c_im'][i])
        final = i == depth - 1
        hp, stp = _layer(hp, pe_p, zero_st, p, i, tables, lp, tm_prompt, final)
        st_in = (state_ssm_re[i].astype(F32), state_ssm_im[i].astype(F32), None, None,
                 state_conv[i].astype(F32))
        hs, sts = _layer(hs, pe_s, st_in, p, i, tables, ls, tm_sample, final)
        for j in range(5):
            new_p[j].append(stp[j])
            new_s[j].append(sts[j])
    y_prompt = hp.reshape(bp, lp, D_MODEL).astype(x_prompt.dtype)
    y_sample = hs.reshape(bs, ls, D_MODEL).astype(x_sample.dtype)
    dts = (state_ssm_re.dtype, state_ssm_im.dtype, state_rwkv.dtype, state_shift.dtype,
           state_conv.dtype)
    outs_p = tuple(jnp.stack(new_p[j]).astype(dts[j]) for j in range(5))
    outs_s = tuple(jnp.stack(new_s[j]).astype(dts[j]) for j in range(5))
    return (y_prompt, y_sample) + outs_p + outs_s


def kernel(x_prompt, x_sample, p_prompt, p_sample, state_ssm_re, state_ssm_im, state_rwkv, state_shift, state_conv, norm1_g, w_in, shift_mu, ssm_lam_re, ssm_lam_im, ssm_log_dt, ssm_b_re, ssm_b_im, ssm_c_re, ssm_c_im, ssm_d, ssm_w_glu, ssm_b_glu, rwkv_w0, rwkv_w2, rwkv_a0, rwkv_a2, rwkv_g2, rwkv_k_k, rwkv_k_a, rwkv_r_k, rwkv_lnx_g, rwkv_lnx_b, w_out, norm2_g, w_up, conv_w, conv_b, w_down, w_ple, w_pg, final_g):
    w = dict(norm1_g=norm1_g, w_in=w_in, shift_mu=shift_mu, ssm_lam_re=ssm_lam_re,
             ssm_lam_im=ssm_lam_im, ssm_log_dt=ssm_log_dt, ssm_b_re=ssm_b_re, ssm_b_im=ssm_b_im,
             ssm_c_re=ssm_c_re, ssm_c_im=ssm_c_im, ssm_d=ssm_d, ssm_w_glu=ssm_w_glu,
             ssm_b_glu=ssm_b_glu, rwkv_w0=rwkv_w0, rwkv_w2=rwkv_w2, rwkv_a0=rwkv_a0,
             rwkv_a2=rwkv_a2, rwkv_g2=rwkv_g2, rwkv_k_k=rwkv_k_k, rwkv_k_a=rwkv_k_a,
             rwkv_r_k=rwkv_r_k, rwkv_lnx_g=rwkv_lnx_g, rwkv_lnx_b=rwkv_lnx_b, w_out=w_out,
             norm2_g=norm2_g, w_up=w_up, conv_w=conv_w, conv_b=conv_b, w_down=w_down,
             w_ple=w_ple, w_pg=w_pg, final_g=final_g)
    lp = x_prompt.shape[1]
    ns = x_sample.shape[0] * x_sample.shape[1]
    return _forward(x_prompt, x_sample, p_prompt, p_sample, state_ssm_re, state_ssm_im,
                    state_rwkv, state_shift, state_conv, w,
                    tm_prompt=(min(1024, lp), min(512, lp)),
                    tm_sample=(min(512, ns), min(512, ns)))
```

```python
import functools
import math

import numpy as np
import jax
import jax.numpy as jnp
from jax import lax
from jax.experimental import pallas as pl
from jax.experimental.pallas import tpu as pltpu

F32 = jnp.float32
BF16 = jnp.bfloat16

D_MODEL = 1024
D_SSM = 512
D_RWKV = 512
SSM_GROUP = 16
N_SSM_GROUPS = 32
SSM_STATE = 64
RWKV_HEAD = 64
N_RWKV_HEADS = 8
DECAY_LORA = 64
AAA_LORA = 64
GATE_LORA = 128
N_SHIFT = 3 * D_RWKV + DECAY_LORA + AAA_LORA + GATE_LORA
N_IN = D_SSM + N_SHIFT
D_FF = 2816
D_PLE = 256
RMS_EPS = 1e-6
GN_EPS = 64e-5
L2_EPS = 1e-12

LANES = 128
FOLD_LANE_BLOCKS = D_SSM // LANES
S5_STEPS = 8
S5_ROW = S5_STEPS * D_SSM
S5_LANE_BLOCKS = D_SSM // LANES
S5_BLOCK_GROUPS = N_SSM_GROUPS // S5_LANE_BLOCKS
S5_BLOCK_STATE = 2 * S5_BLOCK_GROUPS * SSM_STATE
S5_STATE_ROW = S5_LANE_BLOCKS * S5_BLOCK_STATE
FF_CHUNK = 2 * LANES
V7X_VMEM_BYTES = 64 * 1024 * 1024
V7X_VMEM_LIMIT_BYTES = V7X_VMEM_BYTES * 7 // 8


def _mm(a, b):
    return jnp.dot(a.astype(BF16), b.astype(BF16), preferred_element_type=F32)


def _mm_nt(a, b):
    return lax.dot_general(a.astype(BF16), b.astype(BF16), (((1,), (1,)), ((), ())),
                           preferred_element_type=F32)


def _mm_tn(a, b):
    return lax.dot_general(a.astype(BF16), b.astype(BF16), (((0,), (0,)), ((), ())),
                           preferred_element_type=F32)


def _split2(x):
    hi = x.astype(BF16)
    return hi, (x - hi.astype(F32)).astype(BF16)


def _seg_sum(x, ones_bd):
    hi, lo = _split2(x)
    rows, gl = x.shape[0], ones_bd.shape[0]
    parts = []
    for g in range(0, x.shape[1], gl):
        both = jnp.dot(jnp.concatenate([hi[:, g:g + gl], lo[:, g:g + gl]], axis=0), ones_bd,
                       preferred_element_type=F32)
        parts.append(both[:rows] + both[rows:])
    return jnp.concatenate(parts, axis=1)


def _rmsnorm(x, g):
    return x * lax.rsqrt(jnp.mean(x * x, axis=-1, keepdims=True) + RMS_EPS) * g


def _sigmoid(x):
    return 1.0 / (1.0 + jnp.exp(-x))


def _const_spec(shape):
    nd = len(shape)
    return pl.BlockSpec(shape, lambda *_: (0,) * nd, pipeline_mode=pl.Buffered(1))


def _layer_spec(t, layer):
    if t.ndim == 2:
        return _const_spec(t.shape)
    nd = t.ndim - 1
    return pl.BlockSpec((None,) + t.shape[1:], lambda *_: (layer,) + (0,) * nd,
                        pipeline_mode=pl.Buffered(1))


def _expand_rows(x, seq_len):
    nseq, w = x.shape
    return jnp.broadcast_to(x[:, None, :], (nseq, seq_len, w)).reshape(nseq * seq_len, w)


def _step_rows(x, seq_len, t):
    return x.reshape(x.shape[0] // seq_len, seq_len, x.shape[1])[:, t, :]


def _fold_rows(x, scr_ref, out_ref):
    tm = x.shape[0]
    for q in range(FOLD_LANE_BLOCKS):
        scr_ref[q] = x[:, q * LANES:(q + 1) * LANES]
    for s in range(S5_STEPS):
        for q in range(FOLD_LANE_BLOCKS):
            c0 = s * D_SSM + q * LANES
            out_ref[:, c0:c0 + LANES] = scr_ref[q, pl.ds(s, tm // S5_STEPS, stride=S5_STEPS), :]


def _unfold_rows(x8, scr_ref):
    tm = x8.shape[0] * S5_STEPS
    for s in range(S5_STEPS):
        for q in range(FOLD_LANE_BLOCKS):
            c0 = s * D_SSM + q * LANES
            scr_ref[q, pl.ds(s, tm // S5_STEPS, stride=S5_STEPS), :] = x8[:, c0:c0 + LANES]
    return jnp.concatenate([scr_ref[q] for q in range(FOLD_LANE_BLOCKS)], axis=1)


def _inproj_kernel(h_ref, g_ref, w_ref, mu_ref, init_ref, *rest, seq_len, tm, layer):
    i = pl.program_id(0)
    row_of = lambda ref: ref[layer:layer + 1, :]
    long_seq = seq_len >= tm
    if long_seq:
        u8_ref, z_ref, last_ref, fold_ref, carry_ref = rest

        @pl.when(i == 0)
        def _():
            carry_ref[...] = jnp.zeros_like(carry_ref)
    else:
        u8_ref, z_ref, last_ref, fold_ref = rest
    xn = _rmsnorm(h_ref[...], row_of(g_ref))
    proj = jnp.dot(xn.astype(BF16), w_ref[...], preferred_element_type=F32)
    _fold_rows(proj[:, :D_SSM], fold_ref, u8_ref)
    zr = proj[:, D_SSM:]
    rolled = pltpu.roll(zr, 1, 0)
    row = lax.broadcasted_iota(jnp.int32, (tm, 1), 0)
    if long_seq:
        first = (i % (seq_len // tm)) == 0
        row0 = jnp.where(first, init_ref[0], carry_ref[...])
        prev = jnp.where(row == 0, row0, rolled)
        carry_ref[...] = zr[tm - 1:tm, :]
        last_ref[0] = zr[tm - 1:tm, :]
    else:
        prev = jnp.where(row % seq_len == 0, _expand_rows(init_ref[...], seq_len), rolled)
        last_ref[...] = _step_rows(zr, seq_len, seq_len - 1)
    z_ref[...] = (zr + (prev - zr) * row_of(mu_ref)).astype(z_ref.dtype)


def _inproj(h, p, layer, shift0, seq_len, tm):
    n = h.shape[0]
    nseq = n // seq_len
    fold_scratch = pltpu.VMEM((FOLD_LANE_BLOCKS, tm, LANES), F32)
    if seq_len >= tm:
        tps = seq_len // tm
        seq_spec = pl.BlockSpec((1, 1, N_SHIFT), lambda i: (i // tps, 0, 0))
        init_spec = seq_spec
        init = shift0.reshape(nseq, 1, N_SHIFT)
        last_shape = jax.ShapeDtypeStruct((nseq, 1, N_SHIFT), F32)
        scratch = [fold_scratch, pltpu.VMEM((1, N_SHIFT), F32)]
    else:
        spt = tm // seq_len
        seq_spec = pl.BlockSpec((spt, N_SHIFT), lambda i: (i, 0))
        init_spec = pl.BlockSpec((None, spt, N_SHIFT), lambda i: (layer, i, 0))
        init = p['state_shift']
        last_shape = jax.ShapeDtypeStruct((nseq, N_SHIFT), F32)
        scratch = [fold_scratch]
    g1, w_in, mu = p['norm1_g'], p['w_in'], p['shift_mu']
    u8, z, last = pl.pallas_call(
        functools.partial(_inproj_kernel, seq_len=seq_len, tm=tm, layer=layer),
        grid=(n // tm,),
        in_specs=[pl.BlockSpec((tm, D_MODEL), lambda i: (i, 0)),
                  _layer_spec(g1, layer), _layer_spec(w_in, layer), _layer_spec(mu, layer),
                  init_spec],
        out_specs=[pl.BlockSpec((tm // S5_STEPS, S5_ROW), lambda i: (i, 0)),
                   pl.BlockSpec((tm, N_SHIFT), lambda i: (i, 0)),
                   seq_spec],
        out_shape=[jax.ShapeDtypeStruct((n // S5_STEPS, S5_ROW), F32),
                   jax.ShapeDtypeStruct((n, N_SHIFT), BF16),
                   last_shape],
        scratch_shapes=scratch,
        compiler_params=pltpu.CompilerParams(dimension_semantics=("arbitrary",),
                                             vmem_limit_bytes=V7X_VMEM_LIMIT_BYTES),
        name="inproj",
    )(h, g1, w_in, mu, init)
    return u8, z, last.reshape(nseq, N_SHIFT)


def _s5_prep_kernel(lr_ref, li_ref, ldt_ref, br_ref, bi_ref, cr_ref, ci_ref, tile_ref, sel_ref,
                    wx_ref, wy_ref, apr_ref, api_ref):
    lr, li = lr_ref[...], li_ref[...]
    dt = jnp.exp(ldt_ref[...])
    mag = jnp.exp(lr * dt)
    ar = mag * jnp.cos(li * dt)
    ai = mag * jnp.sin(li * dt)
    den = lr * lr + li * li
    nr = ar - 1.0
    fr = (nr * lr + ai * li) / den
    fi = (ai * lr - nr * li) / den
    br, bi = br_ref[...], bi_ref[...]
    bbr = fr * br - fi * bi
    bbi = fr * bi + fi * br
    cr, ci = cr_ref[...], ci_ref[...]
    pr = [jnp.ones_like(ar)]
    pi = [jnp.zeros_like(ar)]
    for _ in range(S5_STEPS):
        pr.append(pr[-1] * ar - pi[-1] * ai)
        pi.append(pr[-2] * ai + pi[-1] * ar)
    nrow = lr.shape[0]
    blk = nrow // S5_LANE_BLOCKS
    half = S5_BLOCK_STATE // 2
    ri = lax.broadcasted_iota(jnp.int32, (nrow, nrow), 0)
    ci_ = lax.broadcasted_iota(jnp.int32, (nrow, nrow), 1)
    same_group = (ri // SSM_GROUP) == (ci_ // SSM_GROUP)
    keep_in = ((ri % blk) // SSM_GROUP) == (ci_ // SSM_STATE)
    keep_out = (ri // SSM_STATE) == ((ci_ % blk) // SSM_GROUP)
    tile, sel = tile_ref[...], sel_ref[...]
    nt = (((1,), (1,)), ((), ()))
    c_hi, c_lo = _split2(jnp.concatenate([cr, -ci], axis=1))
    wy_ref[...] = jnp.zeros(wy_ref.shape, wy_ref.dtype)
    for s in range(S5_STEPS):
        qr, qi = pr[S5_STEPS - 1 - s], pi[S5_STEPS - 1 - s]
        wr = qr * bbr - qi * bbi
        wi = qr * bbi + qi * bbr
        for h, w_ in enumerate((wr, wi)):
            t = jnp.dot(w_.astype(BF16), tile, preferred_element_type=F32)
            t = jnp.where(keep_in, t, 0.0).astype(BF16)
            for q in range(S5_LANE_BLOCKS):
                wx_ref[q, s * blk:(s + 1) * blk, h * half:(h + 1) * half] = t[q * blk:(q + 1) * blk, :]
        tau = S5_STEPS - 1 - s
        lhs_hi, lhs_lo = _split2(jnp.concatenate([wr, wi], axis=1))
        kt = (lax.dot_general(lhs_hi, c_hi, nt, preferred_element_type=F32)
              + lax.dot_general(lhs_hi, c_lo, nt, preferred_element_type=F32)
              + lax.dot_general(lhs_lo, c_hi, nt, preferred_element_type=F32))
        kt = jnp.where(same_group, kt, 0.0).astype(BF16)
        for q in range(S5_LANE_BLOCKS):
            kq = kt[q * blk:(q + 1) * blk, q * blk:(q + 1) * blk]
            for s_in in range(S5_STEPS - tau):
                s_out = s_in + tau
                r0 = S5_BLOCK_STATE + s_in * blk
                wy_ref[q, s_out // 2, r0:r0 + blk, (s_out % 2) * blk:(s_out % 2 + 1) * blk] = kq
        mr = cr * pr[s + 1] - ci * pi[s + 1]
        mi = cr * pi[s + 1] + ci * pr[s + 1]
        for h, m_ in enumerate((mr, -mi)):
            t = lax.dot_general(sel, m_.astype(BF16), nt, preferred_element_type=F32)
            t = jnp.where(keep_out, t, 0.0).astype(BF16)
            for q in range(S5_LANE_BLOCKS):
                wy_ref[q, s // 2, h * half:(h + 1) * half, (s % 2) * blk:(s % 2 + 1) * blk] = (
                    t[:, q * blk:(q + 1) * blk])
    a8r, a8i = pr[S5_STEPS], pi[S5_STEPS]
    er, ei = a8r, a8i
    for n in range(S5_STEPS):
        apr_ref[n] = er
        api_ref[n] = ei
        er, ei = er * a8r - ei * a8i, er * a8i + ei * a8r


def _s5_table_inputs(lam_re, lam_im, log_dt, b_re, b_im, c_re, c_im):
    depth = lam_re.shape[0]
    G, P, K = N_SSM_GROUPS, SSM_STATE, SSM_GROUP
    rep = lambda t: jnp.repeat(t, K, axis=1)
    rows = lambda t: t.reshape(depth, G * K, P)
    return (rep(lam_re), rep(lam_im), rep(jnp.broadcast_to(log_dt[:, :, None], (depth, G, P))),
            rows(jnp.swapaxes(b_re, 2, 3)), rows(jnp.swapaxes(b_im, 2, 3)), rows(c_re), rows(c_im))


def _s5_tables(stacked_inputs, layer):
    G, P, K = N_SSM_GROUPS, SSM_STATE, SSM_GROUP
    Q, GB = S5_LANE_BLOCKS, G // S5_LANE_BLOCKS
    tile = np.tile(np.eye(P, dtype=np.float32), (1, GB))
    consts = (jnp.asarray(tile, BF16), jnp.asarray(tile.T, BF16))
    t3 = jax.ShapeDtypeStruct((S5_STEPS, G * K, P), F32)
    out_shape = [jax.ShapeDtypeStruct((Q, S5_STEPS * GB * K, S5_BLOCK_STATE), BF16),
                 jax.ShapeDtypeStruct((Q, S5_STEPS // 2, S5_BLOCK_STATE + S5_STEPS * GB * K,
                                       2 * GB * K), BF16),
                 t3, t3]
    whole = lambda sds: pl.BlockSpec(sds.shape, lambda i: (0,) * len(sds.shape),
                                     pipeline_mode=pl.Buffered(1))
    wx, wy, apr, api = pl.pallas_call(
        _s5_prep_kernel,
        grid=(1,),
        in_specs=[_layer_spec(t, layer) for t in stacked_inputs]
        + [_const_spec(t.shape) for t in consts],
        out_specs=[whole(sds) for sds in out_shape],
        out_shape=out_shape,
        compiler_params=pltpu.CompilerParams(dimension_semantics=("arbitrary",),
                                             vmem_limit_bytes=V7X_VMEM_LIMIT_BYTES),
        name="s5_prep",
    )(*stacked_inputs, *consts)
    ap = jnp.concatenate([t[:, ::K, :].reshape(S5_STEPS, Q, GB * P) for t in (apr, api)], axis=-1)
    return wx, wy, ap.reshape(S5_STEPS, S5_STATE_ROW)


def _state_to_lanes(h_re, h_im):
    n = h_re.shape[0]
    parts = [t.reshape(n, S5_LANE_BLOCKS, -1) for t in (h_re, h_im)]
    return jnp.concatenate(parts, axis=-1).reshape(n, S5_STATE_ROW)


def _lanes_to_state(h):
    n = h.shape[0]
    h = h.reshape(n, S5_LANE_BLOCKS, 2, N_SSM_GROUPS // S5_LANE_BLOCKS, SSM_STATE)
    return (h[:, :, 0].reshape(n, N_SSM_GROUPS, SSM_STATE), h[:, :, 1].reshape(n, N_SSM_GROUPS, SSM_STATE))


def _s5_kernel(u_ref, h0_ref, wx_ref, wy_ref, ap_ref, y_ref, hl_ref, *scratch, rows, scan):
    half = S5_BLOCK_STATE // 2
    cw = D_SSM // S5_LANE_BLOCKS
    blocks = range(S5_LANE_BLOCKS)
    ub = u_ref[...].astype(BF16)
    ucat = [jnp.concatenate([ub[:, s * D_SSM + q * cw: s * D_SSM + (q + 1) * cw]
                             for s in range(S5_STEPS)], axis=1) for q in blocks]
    x = [jnp.dot(ucat[q], wx_ref[q], preferred_element_type=F32) for q in blocks]
    ap = [ap_ref[:, q * S5_BLOCK_STATE:(q + 1) * S5_BLOCK_STATE] for q in blocks]

    def emit_outputs(q, hprev):
        lhs = jnp.concatenate([hprev, ucat[q]], axis=1)
        for j in range(S5_STEPS // 2):
            kk = S5_BLOCK_STATE + cw * (2 * j + 2)
            y2 = jnp.dot(lhs[:, :kk], wy_ref[q, j, :kk, :], preferred_element_type=F32)
            c0 = (2 * j) * D_SSM + q * cw
            c1 = (2 * j + 1) * D_SSM + q * cw
            y_ref[:, c0:c0 + cw] = y2[:, :cw]
            y_ref[:, c1:c1 + cw] = y2[:, cw:]

    if scan:
        hs_ref, hp_ref = scratch
        sub3 = lax.broadcasted_iota(jnp.int32, (1, 8, 1), 1)
        sub2 = lax.broadcasted_iota(jnp.int32, (8, 1), 0)
        for q in blocks:
            xr = x[q][:, :half].reshape(rows // 8, 8, half)
            xi = x[q][:, half:].reshape(rows // 8, 8, half)
            for k in (1, 2, 4):
                er = jnp.where(sub3 >= k, ap[q][k - 1:k, :half].reshape(1, 1, half), 0.0)
                ei = jnp.where(sub3 >= k, ap[q][k - 1:k, half:].reshape(1, 1, half), 0.0)
                sr = pltpu.roll(xr, k, 1)
                si = pltpu.roll(xi, k, 1)
                xr, xi = xr + er * sr - ei * si, xi + er * si + ei * sr
            hs_ref[q, :, :half] = xr.reshape(rows, half)
            hs_ref[q, :, half:] = xi.reshape(rows, half)
        h0 = h0_ref[0]
        for q in blocks:
            lo = q * S5_BLOCK_STATE
            cr, ci = h0[:, lo:lo + half], h0[:, lo + half:lo + S5_BLOCK_STATE]
            pwr, pwi = ap[q][:, :half], ap[q][:, half:]
            for j in range(rows // 8):
                sl = slice(j * 8, (j + 1) * 8)
                hr = hs_ref[q, sl, :half] + pwr * cr - pwi * ci
                hi = hs_ref[q, sl, half:] + pwr * ci + pwi * cr
                hp_ref[q, sl, :half] = jnp.where(sub2 == 0, cr, pltpu.roll(hr, 1, 0))
                hp_ref[q, sl, half:] = jnp.where(sub2 == 0, ci, pltpu.roll(hi, 1, 0))
                cr, ci = hr[7:8, :], hi[7:8, :]
            hl_ref[0, :, lo:lo + half] = cr
            hl_ref[0, :, lo + half:lo + S5_BLOCK_STATE] = ci
            emit_outputs(q, hp_ref[q].astype(BF16))
    else:
        for q in blocks:
            lo = q * S5_BLOCK_STATE
            h0r = h0_ref[:, lo:lo + half]
            h0i = h0_ref[:, lo + half:lo + S5_BLOCK_STATE]
            er, ei = ap[q][0:1, :half], ap[q][0:1, half:]
            hl_ref[:, lo:lo + half] = er * h0r - ei * h0i + x[q][:, :half]
            hl_ref[:, lo + half:lo + S5_BLOCK_STATE] = er * h0i + ei * h0r + x[q][:, half:]
            emit_outputs(q, jnp.concatenate([h0r, h0i], axis=1).astype(BF16))


def _s5(u8, h_re, h_im, tables, seq_len):
    wx, wy, ap = tables
    n = u8.shape[0] * S5_STEPS
    nseq = n // seq_len
    h0 = _state_to_lanes(h_re, h_im)
    cps = seq_len // S5_STEPS
    scan = cps > 1
    if scan:
        rows = cps
        grid = (nseq,)
        h0 = h0.reshape(nseq, 1, S5_STATE_ROW)
        h_spec = pl.BlockSpec((1, 1, S5_STATE_ROW), lambda i: (i, 0, 0))
        h_shape = jax.ShapeDtypeStruct((nseq, 1, S5_STATE_ROW), F32)
    else:
        rows = min(nseq, 128)
        grid = (nseq // rows,)
        h_spec = pl.BlockSpec((rows, S5_STATE_ROW), lambda i: (i, 0))
        h_shape = jax.ShapeDtypeStruct((nseq, S5_STATE_ROW), F32)
    y8, hl = pl.pallas_call(
        functools.partial(_s5_kernel, rows=rows, scan=scan),
        grid=grid,
        in_specs=[pl.BlockSpec((rows, S5_ROW), lambda i: (i, 0)),
                  h_spec,
                  _const_spec(wx.shape), _const_spec(wy.shape), _const_spec(ap.shape)],
        out_specs=[pl.BlockSpec((rows, S5_ROW), lambda i: (i, 0)), h_spec],
        out_shape=[jax.ShapeDtypeStruct((n // S5_STEPS, S5_ROW), F32), h_shape],
        scratch_shapes=[pltpu.VMEM((S5_LANE_BLOCKS, rows, S5_BLOCK_STATE), F32)] * 2 if scan else [],
        compiler_params=pltpu.CompilerParams(dimension_semantics=("arbitrary",),
                                             vmem_limit_bytes=V7X_VMEM_LIMIT_BYTES),
        name="s5",
    )(u8, h0, wx, wy, ap)
    hre_new, him_new = _lanes_to_state(hl.reshape(nseq, S5_STATE_ROW))
    return y8, hre_new, him_new


RW_CHUNK = 64
RW_STEP_CHUNKS_LONG = 8
RW_STEP_CHUNKS_SHORT = 4
RW_GROUPS = 2
RW_GROUP_HEADS = N_RWKV_HEADS // RW_GROUPS
RW_GROUP_LANES = D_RWKV // RW_GROUPS


def _split3(x):
    hi = x.astype(BF16)
    r1 = x - hi.astype(F32)
    mid = r1.astype(BF16)
    lo = (r1 - mid.astype(F32)).astype(BF16)
    return hi, mid, lo


def _bd(x, hm_ref):
    xb = x.astype(BF16)
    return jnp.concatenate([xb * hm_ref[h] for h in range(RW_GROUP_HEADS)], axis=0)


def _unbd(f, hm_ref):
    n = RWKV_HEAD
    out = f[:n] * hm_ref[0].astype(F32)
    for h in range(1, RW_GROUP_HEADS):
        out = out + f[h * n:(h + 1) * n] * hm_ref[h].astype(F32)
    return out


def _rwkv_kernel(z_ref, s0_ref, w0_ref, w2_ref, a0_ref, a2_ref, g2_ref, kk_ref, ka_ref, rk_ref,
                 lg_ref, lb_ref, ones_ref, hm_ref, cm_ref, tri_ref, y_ref, sl_ref, ypre_ref,
                 *scratch, seg_len, step_chunks, layer):
    row_of = lambda ref: ref[layer:layer + 1, :]
    C = RW_CHUNK
    N = RWKV_HEAD
    GL = RW_GROUP_LANES
    chained = seg_len == C
    if chained:
        (st_ref,) = scratch
        b = pl.program_id(1)

        @pl.when(b == 0)
        def _():
            for g in range(RW_GROUPS):
                st_ref[g] = jnp.concatenate(
                    [s0_ref[0, h] for h in range(g * RW_GROUP_HEADS, (g + 1) * RW_GROUP_HEADS)], axis=1)

    ones_bd = ones_ref[...]
    z = z_ref[...].astype(F32)
    r = z[:, :D_RWKV]
    k = z[:, D_RWKV:2 * D_RWKV]
    v = z[:, 2 * D_RWKV:3 * D_RWKV]
    o = 3 * D_RWKV
    xw = z[:, o:o + DECAY_LORA]
    xa = z[:, o + DECAY_LORA:o + DECAY_LORA + AAA_LORA]
    xg = z[:, o + DECAY_LORA + AAA_LORA:]
    wd = -(row_of(w0_ref) + _mm(jnp.tanh(xw), w2_ref[...]))
    w = -(jnp.maximum(wd, 0.0) + jnp.log1p(jnp.exp(-jnp.abs(wd)))) - 0.5
    lw = -jnp.exp(w)
    a = _sigmoid(row_of(a0_ref) + _mm(xa, a2_ref[...]))
    out_gate = _mm(_sigmoid(xg), g2_ref[...])
    kk = k * row_of(kk_ref)
    kk = kk / jnp.maximum(jnp.sqrt(_seg_sum(kk * kk, ones_bd)), L2_EPS)
    kmod = k * (1.0 + (a - 1.0) * row_of(ka_ref))

    tri = tri_ref[...]
    cw = sum(jnp.dot(tri, t, preferred_element_type=F32) for t in _split3(lw))
    tot = _expand_rows(_step_rows(cw, seg_len, seg_len - 1), seg_len)
    w_in = jnp.exp(cw)
    w_tail = jnp.exp(tot - cw)
    w_inv = jnp.exp(-cw)
    w_all = jnp.exp(tot)
    ah = -kk * jnp.exp(cw - lw)
    bh = kk * a * w_inv
    kh = kmod * w_inv
    rh = r * w_in
    bt = kk * a * w_tail
    kt = kmod * w_tail
    strict = cm_ref[0] > 0.0
    incl = cm_ref[1] > 0.0
    eye_cat = cm_ref[2]
    n_fac = seg_len.bit_length() - 1

    items = [(c, g) for c in range(step_chunks) for g in range(RW_GROUPS)]
    sub = lambda t, c, g: t[c * C:(c + 1) * C, g * GL:(g + 1) * GL]
    bd = lambda t: _bd(t, hm_ref)
    A = [sub(ah, c, g) for c, g in items]
    R = [sub(rh, c, g) for c, g in items]
    V = [sub(v, c, g) for c, g in items]
    G = [_mm_nt(jnp.concatenate([A[i], R[i]], axis=0),
                jnp.concatenate([bd(sub(bh, c, g)), bd(sub(kh, c, g))], axis=0))
         for i, (c, g) in enumerate(items)]
    AB = [jnp.where(strict, t[:C, :GL], 0.0) for t in G]
    AK = [jnp.where(strict, t[:C, GL:], 0.0) for t in G]
    RB = [jnp.where(incl, t[C:, :GL], 0.0) for t in G]
    RK = [jnp.where(incl, t[C:, GL:], 0.0) for t in G]
    KV = [_mm(jnp.concatenate([AK[i], RK[i]], axis=0), bd(V[i])) for i in range(len(items))]
    T = [eye_cat + t for t in AB]
    P = [_mm(t, bd(t)) for t in AB]
    for _ in range(n_fac - 2):
        PT = [_mm(jnp.concatenate([P[i], T[i]], axis=0), bd(P[i])) for i in range(len(items))]
        P = [t[:C] for t in PT]
        T = [T[i] + PT[i][C:] for i in range(len(items))]
    T = [T[i] + _mm(T[i], bd(P[i])) for i in range(len(items))]
    RBT = [_mm(RB[i], bd(T[i])) for i in range(len(items))]
    X = [_mm(jnp.concatenate([T[i], RBT[i]], axis=0),
             jnp.concatenate([bd(A[i]), bd(KV[i][:C])], axis=1)) for i in range(len(items))]
    TA = [t[:C, :GL] for t in X]
    U0 = [t[:C, GL:] for t in X]
    Rt = [R[i] + X[i][C:, :GL] for i in range(len(items))]
    Y0 = [X[i][C:, GL:] + KV[i][C:] for i in range(len(items))]
    nseg = C // seg_len
    heads = lambda g: range(g * RW_GROUP_HEADS, (g + 1) * RW_GROUP_HEADS)
    head_lanes = lambda h: slice((h % RW_GROUP_HEADS) * N, (h % RW_GROUP_HEADS + 1) * N)
    if chained:
        PhiT, PsiT = [], []
        for i, (c, g) in enumerate(items):
            Bt, Kt, Wa = sub(bt, c, g), sub(kt, c, g), sub(w_all, c, g)
            PhiT.append(bd(eye_cat * Wa[0:1, :] + _unbd(_mm_tn(TA[i], Bt), hm_ref)))
            PsiT.append(_unbd(_mm_tn(jnp.concatenate([U0[i], V[i]], axis=0),
                                     jnp.concatenate([Bt, Kt], axis=0)), hm_ref))
        states = [st_ref[g] for g in range(RW_GROUPS)]
        for i, (c, g) in enumerate(items):
            S = states[g]
            ypre_ref[c * C:(c + 1) * C, g * GL:(g + 1) * GL] = _mm_nt(Rt[i], bd(S)) + Y0[i]
            states[g] = _mm(S, PhiT[i]) + PsiT[i]
        for g in range(RW_GROUPS):
            st_ref[g] = states[g]
            for h in heads(g):
                sl_ref[0, h] = states[g][:, head_lanes(h)]
    else:
        segs = [(i, c, g, s) for i, (c, g) in enumerate(items) for s in range(nseg)]
        rows_of = lambda s: slice(s * seg_len, (s + 1) * seg_len)
        S0 = [jnp.concatenate([s0_ref[c * nseg + s, h] for h in heads(g)], axis=1)
              for i, c, g, s in segs]
        RU = [_mm_nt(jnp.concatenate([Rt[i][rows_of(s)], TA[i][rows_of(s)]], axis=0), bd(S0[n]))
              for n, (i, c, g, s) in enumerate(segs)]
        for n, (i, c, g, s) in enumerate(segs):
            ypre_ref[c * C + s * seg_len:c * C + (s + 1) * seg_len, g * GL:(g + 1) * GL] = (
                RU[n][:seg_len] + Y0[i][rows_of(s)])
        UB = [_mm_tn(jnp.concatenate([RU[n][seg_len:] + U0[i][rows_of(s)], V[i][rows_of(s)]], axis=0),
                     jnp.concatenate([sub(bt, c, g)[rows_of(s)], sub(kt, c, g)[rows_of(s)]], axis=0))
              for n, (i, c, g, s) in enumerate(segs)]
        for n, (i, c, g, s) in enumerate(segs):
            S = S0[n] * sub(w_all, c, g)[s * seg_len:s * seg_len + 1, :] + _unbd(UB[n], hm_ref)
            for h in heads(g):
                sl_ref[c * nseg + s, h] = S[:, head_lanes(h)]
    y = ypre_ref[...]

    mu = _seg_sum(y, ones_bd) * (1.0 / N)
    d = y - mu
    var = _seg_sum(d * d, ones_bd) * (1.0 / N)
    yn = d * lax.rsqrt(var + GN_EPS) * row_of(lg_ref) + row_of(lb_ref)
    bonus = _seg_sum(r * kmod * row_of(rk_ref), ones_bd) * v
    y_ref[...] = (yn + bonus) * out_gate


def _rwkv_constants(seg_len, rows):
    n, gh, gl = RWKV_HEAD, RW_GROUP_HEADS, RW_GROUP_LANES
    lane_head = np.arange(gl) // n
    hm = (lane_head[None, None, :] == np.arange(gh)[:, None, None]) * np.ones((gh, n, gl))
    i = np.arange(RW_CHUNK)[:, None]
    j = (np.arange(gl) % n)[None, :]
    same = (i // seg_len) == (j // seg_len)
    cm = np.stack([(j < i) & same, (j <= i) & same, j == i]).astype(np.float32)
    ri = np.arange(rows)[:, None]
    rj = np.arange(rows)[None, :]
    same_r = (ri // seg_len) == (rj // seg_len)
    tri = ((rj <= ri) & same_r).astype(np.float32)
    ones_bd = np.kron(np.eye(gh), np.ones((n, n)))
    return (jnp.asarray(ones_bd, BF16), jnp.asarray(hm, BF16), jnp.asarray(cm, F32),
            jnp.asarray(tri, BF16))


def _rwkv(z, s0, p, layer, seq_len):
    n = z.shape[0]
    if seq_len >= RW_CHUNK:
        step_chunks = min(RW_STEP_CHUNKS_LONG, seq_len // RW_CHUNK)
    else:
        step_chunks = min(RW_STEP_CHUNKS_SHORT, n // RW_CHUNK)
    nseq = n // seq_len
    nh, hd = N_RWKV_HEADS, RWKV_HEAD
    seg_len = min(seq_len, RW_CHUNK)
    rows = RW_CHUNK * step_chunks
    chained = seq_len >= RW_CHUNK
    if chained:
        steps = seq_len // rows
        grid = (nseq, steps)
        row_map = lambda s, b: (s * steps + b, 0)
        state_spec = pl.BlockSpec((1, nh, hd, hd), lambda s, b: (s, 0, 0, 0))
        scratch = [pltpu.VMEM((rows, D_RWKV), F32), pltpu.VMEM((RW_GROUPS, hd, RW_GROUP_LANES), F32)]
    else:
        grid = (n // rows, 1)
        row_map = lambda s, b: (s, 0)
        state_spec = pl.BlockSpec((rows // seq_len, nh, hd, hd), lambda s, b: (s, 0, 0, 0))
        scratch = [pltpu.VMEM((rows, D_RWKV), F32)]
    if s0 is None:
        s0 = p['state_rwkv']
        s0_spec = pl.BlockSpec((None,) + state_spec.block_shape,
                               lambda s, b: (layer, s, 0, 0, 0))
    else:
        s0_spec = state_spec
    stacked = [p[k] for k in ('w0', 'w2', 'a0', 'a2', 'g2', 'k_k', 'k_a', 'r_k', 'lnx_g', 'lnx_b')]
    consts = list(_rwkv_constants(seg_len, rows))
    y, s_new = pl.pallas_call(
        functools.partial(_rwkv_kernel, seg_len=seg_len, step_chunks=step_chunks, layer=layer),
        grid=grid,
        in_specs=[pl.BlockSpec((rows, N_SHIFT), row_map), s0_spec]
        + [_layer_spec(t, layer) for t in stacked] + [_const_spec(t.shape) for t in consts],
        out_specs=[pl.BlockSpec((rows, D_RWKV), row_map), state_spec],
        out_shape=[jax.ShapeDtypeStruct((n, D_RWKV), F32),
                   jax.ShapeDtypeStruct((nseq, nh, hd, hd), F32)],
        scratch_shapes=scratch,
        compiler_params=pltpu.CompilerParams(dimension_semantics=("arbitrary", "arbitrary"),
                                             vmem_limit_bytes=V7X_VMEM_LIMIT_BYTES),
        name="rwkv7",
    )(z, s0, *stacked, *consts)
    return y, s_new


def _ffn_kernel(h_ref, y8_ref, u8_ref, yb_ref, pe_ref, c1_ref, c2_ref, d8_ref, wglu_ref, bglu_ref,
                wout_ref, g2_ref, wup_ref, cw_ref, cb_ref, wdn_ref, wple_ref, wpg_ref, gf_ref,
                *rest, seq_len, tm, final, layer):
    row_of = lambda ref: ref[layer:layer + 1, :]
    i = pl.program_id(0)
    long_seq = seq_len >= tm
    if long_seq:
        o_ref, ga_ref, gb_ref, fold_ref, act_ref, carry_ref = rest

        @pl.when(i == 0)
        def _():
            carry_ref[...] = jnp.zeros_like(carry_ref)
    else:
        o_ref, ga_ref, gb_ref, fold_ref, act_ref = rest
    ya = y8_ref[...] + row_of(d8_ref) * u8_ref[...]
    c_gelu = math.sqrt(2.0 / math.pi)
    ya = ya * (0.5 * (1.0 + jnp.tanh(c_gelu * (ya + 0.044715 * (ya * ya * ya)))))
    ya = _unfold_rows(ya, fold_ref)
    ya = ya * _sigmoid(_mm(ya, wglu_ref[...]) + row_of(bglu_ref))
    h1 = (h_ref[...] + _mm(ya, wout_ref[:D_SSM, :]) + _mm(yb_ref[...], wout_ref[D_SSM:, :]))
    x2 = _rmsnorm(h1, row_of(g2_ref)).astype(BF16)
    row = lax.broadcasted_iota(jnp.int32, (tm, 1), 0)
    if long_seq:
        first = (i % (seq_len // tm)) == 0
    else:
        t = row % seq_len
    def up(c):
        cs = slice(c * FF_CHUNK, (c + 1) * FF_CHUNK)
        gs = slice(D_FF + c * FF_CHUNK, D_FF + (c + 1) * FF_CHUNK)
        return (jnp.dot(x2, wup_ref[:, cs], preferred_element_type=F32),
                jnp.dot(x2, wup_ref[:, gs], preferred_element_type=F32))

    n_chunks = D_FF // FF_CHUNK
    nxt = up(0)
    for c in range(n_chunks):
        cs = slice(c * FF_CHUNK, (c + 1) * FF_CHUNK)
        val, gate = nxt
        if c + 1 < n_chunks:
            nxt = up(c + 1)
        r1 = pltpu.roll(gate, 1, 0)
        r2 = pltpu.roll(gate, 2, 0)
        if long_seq:
            m1 = jnp.where(first, c1_ref[0][:, cs], carry_ref[1:2, cs])
            m2 = jnp.where(first, c2_ref[0][:, cs], carry_ref[0:1, cs])
            p1 = jnp.where(row == 0, m1, r1)
            p2 = jnp.where(row == 0, m2, jnp.where(row == 1, m1, r2))
            carry_ref[:, cs] = gate[tm - 2:tm, :]
            ga_ref[0, :, cs] = gate[tm - 2:tm - 1, :]
            gb_ref[0, :, cs] = gate[tm - 1:tm, :]
        else:
            m1 = _expand_rows(c1_ref[:, cs], seq_len)
            m2 = _expand_rows(c2_ref[:, cs], seq_len)
            p1 = jnp.where(t == 0, m1, r1)
            p2 = jnp.where(t == 0, m2, jnp.where(t == 1, m1, r2))
            ga_ref[:, cs] = _step_rows(gate, seq_len, seq_len - 2)
            gb_ref[:, cs] = _step_rows(gate, seq_len, seq_len - 1)
        conv = (cb_ref[layer:layer + 1, cs] + cw_ref[2:3, cs] * gate + cw_ref[1:2, cs] * p1 + cw_ref[0:1, cs] * p2)
        act_ref[:, cs] = (conv * _sigmoid(conv) * val).astype(BF16)
    h2 = h1 + jnp.dot(act_ref[...], wdn_ref[...], preferred_element_type=F32)
    h3 = h2 + _mm(pe_ref[...], wple_ref[...]) * _sigmoid(_mm(h2, wpg_ref[...]))
    if final:
        h3 = _rmsnorm(h3, gf_ref[...])
    o_ref[...] = h3


def _ffn(h, y8, u8, yb, pe, conv0, p, layer, seq_len, tm, final):
    n = h.shape[0]
    nseq = n // seq_len
    row_spec = lambda w: pl.BlockSpec((tm, w), lambda i: (i, 0))
    pe_spec = pl.BlockSpec((None, tm, D_PLE), lambda i: (layer, i, 0))
    fold_spec = pl.BlockSpec((tm // S5_STEPS, S5_ROW), lambda i: (i, 0))
    scratch = [pltpu.VMEM((FOLD_LANE_BLOCKS, tm, LANES), F32), pltpu.VMEM((tm, D_FF), BF16)]
    if seq_len >= tm:
        tps = seq_len // tm
        seq_spec = pl.BlockSpec((1, 1, D_FF), lambda i: (i // tps, 0, 0))
        seq_shape = (nseq, 1, D_FF)
        scratch.append(pltpu.VMEM((2, D_FF), F32))
    else:
        seq_spec = pl.BlockSpec((tm // seq_len, D_FF), lambda i: (i, 0))
        seq_shape = (nseq, D_FF)
    stacked = [p[k] for k in ('d8', 'w_glu', 'b_glu', 'w_out', 'norm2_g', 'w_up', 'conv_w',
                              'conv_b', 'w_down', 'w_ple', 'w_pg')]
    consts = [p['final_g']]
    weights = stacked + consts
    out, ga, gb = pl.pallas_call(
        functools.partial(_ffn_kernel, seq_len=seq_len, tm=tm, final=final, layer=layer),
        grid=(n // tm,),
        in_specs=[row_spec(D_MODEL), fold_spec, fold_spec, row_spec(D_RWKV), pe_spec,
                  seq_spec, seq_spec] + [_layer_spec(t, layer) for t in stacked]
        + [_const_spec(t.shape) for t in consts],
        out_specs=[row_spec(D_MODEL), seq_spec, seq_spec],
        out_shape=[jax.ShapeDtypeStruct((n, D_MODEL), F32)]
        + [jax.ShapeDtypeStruct(seq_shape, F32)] * 2,
        scratch_shapes=scratch,
        compiler_params=pltpu.CompilerParams(dimension_semantics=("arbitrary",),
                                             vmem_limit_bytes=V7X_VMEM_LIMIT_BYTES),
        name="ffn",
    )(h, y8, u8, yb, pe, conv0[:, 1].reshape(seq_shape), conv0[:, 0].reshape(seq_shape), *weights)
    conv_new = jnp.stack([ga.reshape(nseq, D_FF), gb.reshape(nseq, D_FF)], axis=1)
    return out, conv_new


def _layer(h, pe, st, p, layer, s5_tables, seq_len, tm, final):
    ssm_re0, ssm_im0, rwkv0, shift0, conv0 = st
    tm_in, tm_ffn = tm
    u8, z, shift_new = _inproj(h, p, layer, shift0, seq_len, tm_in)
    y8, hre, him = _s5(u8, ssm_re0, ssm_im0, s5_tables, seq_len)
    yb, s_last = _rwkv(z, rwkv0, p, layer, seq_len)
    h, conv_new = _ffn(h, y8, u8, yb, pe, conv0, p, layer, seq_len, tm_ffn, final)
    return h, (hre, him, s_last, shift_new, conv_new)


def _stacked_params(w, state_rwkv, state_shift):
    depth = w['w_in'].shape[0]
    row = lambda t: t.reshape(depth, -1).astype(F32)
    bf = lambda t: t.astype(BF16)
    return {
        'norm1_g': row(w['norm1_g']), 'w_in': bf(w['w_in']), 'shift_mu': row(w['shift_mu']),
        'd8': jnp.tile(row(w['ssm_d']), (1, S5_STEPS)), 'w_glu': bf(w['ssm_w_glu']),
        'b_glu': row(w['ssm_b_glu']),
        'w0': row(w['rwkv_w0']), 'w2': bf(w['rwkv_w2']), 'a0': row(w['rwkv_a0']),
        'a2': bf(w['rwkv_a2']), 'g2': bf(w['rwkv_g2']), 'k_k': row(w['rwkv_k_k']),
        'k_a': row(w['rwkv_k_a']), 'r_k': row(w['rwkv_r_k']), 'lnx_g': row(w['rwkv_lnx_g']),
        'lnx_b': row(w['rwkv_lnx_b']), 'w_out': bf(w['w_out']), 'norm2_g': row(w['norm2_g']),
        'w_up': bf(w['w_up']), 'conv_w': w['conv_w'].astype(F32), 'conv_b': row(w['conv_b']),
        'w_down': bf(w['w_down']), 'w_ple': bf(w['w_ple']), 'w_pg': bf(w['w_pg']),
        'final_g': w['final_g'].reshape(1, -1).astype(F32),
        'state_rwkv': state_rwkv.astype(F32), 'state_shift': state_shift.astype(F32),
    }


def _forward(x_prompt, x_sample, p_prompt, p_sample, state_ssm_re, state_ssm_im, state_rwkv,
             state_shift, state_conv, w, tm_prompt, tm_sample):
    depth = w['w_in'].shape[0]
    bp, lp, _ = x_prompt.shape
    bs, ls, _ = x_sample.shape
    hp = x_prompt.reshape(bp * lp, D_MODEL).astype(F32)
    hs = x_sample.reshape(bs * ls, D_MODEL).astype(F32)
    pe_p = p_prompt.reshape(depth, bp * lp, D_PLE).astype(F32)
    pe_s = p_sample.reshape(depth, bs * ls, D_PLE).astype(F32)
    zero_st = (jnp.zeros((bp, N_SSM_GROUPS, SSM_STATE), F32),
               jnp.zeros((bp, N_SSM_GROUPS, SSM_STATE), F32),
               jnp.zeros((bp, N_RWKV_HEADS, RWKV_HEAD, RWKV_HEAD), F32),
               jnp.zeros((bp, N_SHIFT), F32),
               jnp.zeros((bp, 2, D_FF), F32))
    p = _stacked_params(w, state_rwkv, state_shift)
    s5_inputs = _s5_table_inputs(w['ssm_lam_re'], w['ssm_lam_im'], w['ssm_log_dt'], w['ssm_b_re'],
                                 w['ssm_b_im'], w['ssm_c_re'], w['ssm_c_im'])
    new_p = [[] for _ in range(5)]
    new_s = [[] for _ in range(5)]
    for i in range(depth):
        tables = _s5_tables(s5_inputs, i)
        final = i == depth - 1
        hp, stp = _layer(hp, pe_p, zero_st, p, i, tables, lp, tm_prompt, final)
        st_in = (state_ssm_re[i].astype(F32), state_ssm_im[i].astype(F32), None, None,
                 state_conv[i].astype(F32))
        hs, sts = _layer(hs, pe_s, st_in, p, i, tables, ls, tm_sample, final)
        for j in range(5):
            new_p[j].append(stp[j])
            new_s[j].append(sts[j])
    y_prompt = hp.reshape(bp, lp, D_MODEL).astype(x_prompt.dtype)
    y_sample = hs.reshape(bs, ls, D_MODEL).astype(x_sample.dtype)
    dts = (state_ssm_re.dtype, state_ssm_im.dtype, state_rwkv.dtype, state_shift.dtype,
           state_conv.dtype)
    outs_p = tuple(jnp.stack(new_p[j]).astype(dts[j]) for j in range(5))
    outs_s = tuple(jnp.stack(new_s[j]).astype(dts[j]) for j in range(5))
    return (y_prompt, y_sample) + outs_p + outs_s


def kernel(x_prompt, x_sample, p_prompt, p_sample, state_ssm_re, state_ssm_im, state_rwkv, state_shift, state_conv, norm1_g, w_in, shift_mu, ssm_lam_re, ssm_lam_im, ssm_log_dt, ssm_b_re, ssm_b_im, ssm_c_re, ssm_c_im, ssm_d, ssm_w_glu, ssm_b_glu, rwkv_w0, rwkv_w2, rwkv_a0, rwkv_a2, rwkv_g2, rwkv_k_k, rwkv_k_a, rwkv_r_k, rwkv_lnx_g, rwkv_lnx_b, w_out, norm2_g, w_up, conv_w, conv_b, w_down, w_ple, w_pg, final_g):
    w = dict(norm1_g=norm1_g, w_in=w_in, shift_mu=shift_mu, ssm_lam_re=ssm_lam_re,
             ssm_lam_im=ssm_lam_im, ssm_log_dt=ssm_log_dt, ssm_b_re=ssm_b_re, ssm_b_im=ssm_b_im,
             ssm_c_re=ssm_c_re, ssm_c_im=ssm_c_im, ssm_d=ssm_d, ssm_w_glu=ssm_w_glu,
             ssm_b_glu=ssm_b_glu, rwkv_w0=rwkv_w0, rwkv_w2=rwkv_w2, rwkv_a0=rwkv_a0,
             rwkv_a2=rwkv_a2, rwkv_g2=rwkv_g2, rwkv_k_k=rwkv_k_k, rwkv_k_a=rwkv_k_a,
             rwkv_r_k=rwkv_r_k, rwkv_lnx_g=rwkv_lnx_g, rwkv_lnx_b=rwkv_lnx_b, w_out=w_out,
             norm2_g=norm2_g, w_up=w_up, conv_w=conv_w, conv_b=conv_b, w_down=w_down,
             w_ple=w_ple, w_pg=w_pg, final_g=final_g)
    lp = x_prompt.shape[1]
    ns = x_sample.shape[0] * x_sample.shape[1]
    return _forward(x_prompt, x_sample, p_prompt, p_sample, state_ssm_re, state_ssm_im,
                    state_rwkv, state_shift, state_conv, w,
                    tm_prompt=(min(1024, lp), min(512, lp)),
                    tm_sample=(min(512, ns), min(512, ns)))
```

```python
import functools
import math

import numpy as np
import jax
import jax.numpy as jnp
from jax import lax
from jax.experimental import pallas as pl
from jax.experimental.pallas import tpu as pltpu

F32 = jnp.float32
BF16 = jnp.bfloat16

D_MODEL = 1024
D_SSM = 512
D_RWKV = 512
SSM_GROUP = 16
N_SSM_GROUPS = 32
SSM_STATE = 64
RWKV_HEAD = 64
N_RWKV_HEADS = 8
DECAY_LORA = 64
AAA_LORA = 64
GATE_LORA = 128
N_SHIFT = 3 * D_RWKV + DECAY_LORA + AAA_LORA + GATE_LORA
N_IN = D_SSM + N_SHIFT
D_FF = 2816
D_PLE = 256
RMS_EPS = 1e-6
GN_EPS = 64e-5
L2_EPS = 1e-12

LANES = 128
FOLD_LANE_BLOCKS = D_SSM // LANES
S5_STEPS = 8
S5_ROW = S5_STEPS * D_SSM
S5_LANE_BLOCKS = D_SSM // LANES
S5_BLOCK_GROUPS = N_SSM_GROUPS // S5_LANE_BLOCKS
S5_BLOCK_STATE = 2 * S5_BLOCK_GROUPS * SSM_STATE
S5_STATE_ROW = S5_LANE_BLOCKS * S5_BLOCK_STATE
FF_CHUNK = 2 * LANES
V7X_VMEM_BYTES = 64 * 1024 * 1024
V7X_VMEM_LIMIT_BYTES = V7X_VMEM_BYTES * 7 // 8


def _mm(a, b):
    return jnp.dot(a.astype(BF16), b.astype(BF16), preferred_element_type=F32)


def _mm_nt(a, b):
    return lax.dot_general(a.astype(BF16), b.astype(BF16), (((1,), (1,)), ((), ())),
                           preferred_element_type=F32)


def _mm_tn(a, b):
    return lax.dot_general(a.astype(BF16), b.astype(BF16), (((0,), (0,)), ((), ())),
                           preferred_element_type=F32)


def _split2(x):
    hi = x.astype(BF16)
    return hi, (x - hi.astype(F32)).astype(BF16)


def _seg_sum(x, ones_bd):
    hi, lo = _split2(x)
    rows, gl = x.shape[0], ones_bd.shape[0]
    parts = []
    for g in range(0, x.shape[1], gl):
        both = jnp.dot(jnp.concatenate([hi[:, g:g + gl], lo[:, g:g + gl]], axis=0), ones_bd,
                       preferred_element_type=F32)
        parts.append(both[:rows] + both[rows:])
    return jnp.concatenate(parts, axis=1)


def _rmsnorm(x, g):
    return x * lax.rsqrt(jnp.mean(x * x, axis=-1, keepdims=True) + RMS_EPS) * g


def _sigmoid(x):
    return 1.0 / (1.0 + jnp.exp(-x))


def _const_spec(shape):
    nd = len(shape)
    return pl.BlockSpec(shape, lambda *_: (0,) * nd, pipeline_mode=pl.Buffered(1))


def _layer_spec(t, layer):
    if t.ndim == 2:
        return _const_spec(t.shape)
    nd = t.ndim - 1
    return pl.BlockSpec((None,) + t.shape[1:], lambda *_: (layer,) + (0,) * nd,
                        pipeline_mode=pl.Buffered(1))


def _expand_rows(x, seq_len):
    nseq, w = x.shape
    return jnp.broadcast_to(x[:, None, :], (nseq, seq_len, w)).reshape(nseq * seq_len, w)


def _step_rows(x, seq_len, t):
    return x.reshape(x.shape[0] // seq_len, seq_len, x.shape[1])[:, t, :]


def _fold_rows(x, scr_ref, out_ref):
    tm = x.shape[0]
    for q in range(FOLD_LANE_BLOCKS):
        scr_ref[q] = x[:, q * LANES:(q + 1) * LANES]
    for s in range(S5_STEPS):
        for q in range(FOLD_LANE_BLOCKS):
            c0 = s * D_SSM + q * LANES
            out_ref[:, c0:c0 + LANES] = scr_ref[q, pl.ds(s, tm // S5_STEPS, stride=S5_STEPS), :]


def _unfold_rows(x8, scr_ref):
    tm = x8.shape[0] * S5_STEPS
    for s in range(S5_STEPS):
        for q in range(FOLD_LANE_BLOCKS):
            c0 = s * D_SSM + q * LANES
            scr_ref[q, pl.ds(s, tm // S5_STEPS, stride=S5_STEPS), :] = x8[:, c0:c0 + LANES]
    return jnp.concatenate([scr_ref[q] for q in range(FOLD_LANE_BLOCKS)], axis=1)


def _inproj_kernel(h_ref, g_ref, w_ref, mu_ref, init_ref, *rest, seq_len, tm, layer):
    i = pl.program_id(0)
    row_of = lambda ref: ref[layer:layer + 1, :]
    long_seq = seq_len >= tm
    if long_seq:
        u8_ref, z_ref, last_ref, fold_ref, carry_ref = rest

        @pl.when(i == 0)
        def _():
            carry_ref[...] = jnp.zeros_like(carry_ref)
    else:
        u8_ref, z_ref, last_ref, fold_ref = rest
    xn = _rmsnorm(h_ref[...], row_of(g_ref))
    proj = jnp.dot(xn.astype(BF16), w_ref[...], preferred_element_type=F32)
    _fold_rows(proj[:, :D_SSM], fold_ref, u8_ref)
    zr = proj[:, D_SSM:]
    rolled = pltpu.roll(zr, 1, 0)
    row = lax.broadcasted_iota(jnp.int32, (tm, 1), 0)
    if long_seq:
        first = (i % (seq_len // tm)) == 0
        row0 = jnp.where(first, init_ref[0], carry_ref[...])
        prev = jnp.where(row == 0, row0, rolled)
        carry_ref[...] = zr[tm - 1:tm, :]
        last_ref[0] = zr[tm - 1:tm, :]
    else:
        prev = jnp.where(row % seq_len == 0, _expand_rows(init_ref[...], seq_len), rolled)
        last_ref[...] = _step_rows(zr, seq_len, seq_len - 1)
    z_ref[...] = (zr + (prev - zr) * row_of(mu_ref)).astype(z_ref.dtype)


def _inproj(h, p, layer, shift0, seq_len, tm):
    n = h.shape[0]
    nseq = n // seq_len
    fold_scratch = pltpu.VMEM((FOLD_LANE_BLOCKS, tm, LANES), F32)
    if seq_len >= tm:
        tps = seq_len // tm
        seq_spec = pl.BlockSpec((1, 1, N_SHIFT), lambda i: (i // tps, 0, 0))
        init_spec = seq_spec
        init = shift0.reshape(nseq, 1, N_SHIFT)
        last_shape = jax.ShapeDtypeStruct((nseq, 1, N_SHIFT), F32)
        scratch = [fold_scratch, pltpu.VMEM((1, N_SHIFT), F32)]
    else:
        spt = tm // seq_len
        seq_spec = pl.BlockSpec((spt, N_SHIFT), lambda i: (i, 0))
        init_spec = pl.BlockSpec((None, spt, N_SHIFT), lambda i: (layer, i, 0))
        init = p['state_shift']
        last_shape = jax.ShapeDtypeStruct((nseq, N_SHIFT), F32)
        scratch = [fold_scratch]
    g1, w_in, mu = p['norm1_g'], p['w_in'], p['shift_mu']
    u8, z, last = pl.pallas_call(
        functools.partial(_inproj_kernel, seq_len=seq_len, tm=tm, layer=layer),
        grid=(n // tm,),
        in_specs=[pl.BlockSpec((tm, D_MODEL), lambda i: (i, 0)),
                  _layer_spec(g1, layer), _layer_spec(w_in, layer), _layer_spec(mu, layer),
                  init_spec],
        out_specs=[pl.BlockSpec((tm // S5_STEPS, S5_ROW), lambda i: (i, 0)),
                   pl.BlockSpec((tm, N_SHIFT), lambda i: (i, 0)),
                   seq_spec],
        out_shape=[jax.ShapeDtypeStruct((n // S5_STEPS, S5_ROW), F32),
                   jax.ShapeDtypeStruct((n, N_SHIFT), BF16),
                   last_shape],
        scratch_shapes=scratch,
        compiler_params=pltpu.CompilerParams(dimension_semantics=("arbitrary",),
                                             vmem_limit_bytes=V7X_VMEM_LIMIT_BYTES),
        name="inproj",
    )(h, g1, w_in, mu, init)
    return u8, z, last.reshape(nseq, N_SHIFT)


def _s5_prep_kernel(lr_ref, li_ref, ldt_ref, br_ref, bi_ref, cr_ref, ci_ref, tile_ref, sel_ref,
                    wx_ref, wy_ref, apr_ref, api_ref):
    lr, li = lr_ref[...], li_ref[...]
    dt = jnp.exp(ldt_ref[...])
    mag = jnp.exp(lr * dt)
    ar = mag * jnp.cos(li * dt)
    ai = mag * jnp.sin(li * dt)
    den = lr * lr + li * li
    nr = ar - 1.0
    fr = (nr * lr + ai * li) / den
    fi = (ai * lr - nr * li) / den
    br, bi = br_ref[...], bi_ref[...]
    bbr = fr * br - fi * bi
    bbi = fr * bi + fi * br
    cr, ci = cr_ref[...], ci_ref[...]
    pr = [jnp.ones_like(ar)]
    pi = [jnp.zeros_like(ar)]
    for _ in range(S5_STEPS):
        pr.append(pr[-1] * ar - pi[-1] * ai)
        pi.append(pr[-2] * ai + pi[-1] * ar)
    nrow = lr.shape[0]
    blk = nrow // S5_LANE_BLOCKS
    half = S5_BLOCK_STATE // 2
    ri = lax.broadcasted_iota(jnp.int32, (nrow, nrow), 0)
    ci_ = lax.broadcasted_iota(jnp.int32, (nrow, nrow), 1)
    same_group = (ri // SSM_GROUP) == (ci_ // SSM_GROUP)
    keep_in = ((ri % blk) // SSM_GROUP) == (ci_ // SSM_STATE)
    keep_out = (ri // SSM_STATE) == ((ci_ % blk) // SSM_GROUP)
    tile, sel = tile_ref[...], sel_ref[...]
    nt = (((1,), (1,)), ((), ()))
    c_hi, c_lo = _split2(jnp.concatenate([cr, -ci], axis=1))
    wy_ref[...] = jnp.zeros(wy_ref.shape, wy_ref.dtype)
    for s in range(S5_STEPS):
        qr, qi = pr[S5_STEPS - 1 - s], pi[S5_STEPS - 1 - s]
        wr = qr * bbr - qi * bbi
        wi = qr * bbi + qi * bbr
        for h, w_ in enumerate((wr, wi)):
            t = jnp.dot(w_.astype(BF16), tile, preferred_element_type=F32)
            t = jnp.where(keep_in, t, 0.0).astype(BF16)
            for q in range(S5_LANE_BLOCKS):
                wx_ref[q, s * blk:(s + 1) * blk, h * half:(h + 1) * half] = t[q * blk:(q + 1) * blk, :]
        tau = S5_STEPS - 1 - s
        lhs_hi, lhs_lo = _split2(jnp.concatenate([wr, wi], axis=1))
        kt = (lax.dot_general(lhs_hi, c_hi, nt, preferred_element_type=F32)
              + lax.dot_general(lhs_hi, c_lo, nt, preferred_element_type=F32)
              + lax.dot_general(lhs_lo, c_hi, nt, preferred_element_type=F32))
        kt = jnp.where(same_group, kt, 0.0).astype(BF16)
        for q in range(S5_LANE_BLOCKS):
            kq = kt[q * blk:(q + 1) * blk, q * blk:(q + 1) * blk]
            for s_in in range(S5_STEPS - tau):
                s_out = s_in + tau
                r0 = S5_BLOCK_STATE + s_in * blk
                wy_ref[q, s_out // 2, r0:r0 + blk, (s_out % 2) * blk:(s_out % 2 + 1) * blk] = kq
        mr = cr * pr[s + 1] - ci * pi[s + 1]
        mi = cr * pi[s + 1] + ci * pr[s + 1]
        for h, m_ in enumerate((mr, -mi)):
            t = lax.dot_general(sel, m_.astype(BF16), nt, preferred_element_type=F32)
            t = jnp.where(keep_out, t, 0.0).astype(BF16)
            for q in range(S5_LANE_BLOCKS):
                wy_ref[q, s // 2, h * half:(h + 1) * half, (s % 2) * blk:(s % 2 + 1) * blk] = (
                    t[:, q * blk:(q + 1) * blk])
    a8r, a8i = pr[S5_STEPS], pi[S5_STEPS]
    er, ei = a8r, a8i
    for n in range(S5_STEPS):
        apr_ref[n] = er
        api_ref[n] = ei
        er, ei = er * a8r - ei * a8i, er * a8i + ei * a8r


def _s5_table_inputs(lam_re, lam_im, log_dt, b_re, b_im, c_re, c_im):
    depth = lam_re.shape[0]
    G, P, K = N_SSM_GROUPS, SSM_STATE, SSM_GROUP
    rep = lambda t: jnp.repeat(t, K, axis=1)
    rows = lambda t: t.reshape(depth, G * K, P)
    return (rep(lam_re), rep(lam_im), rep(jnp.broadcast_to(log_dt[:, :, None], (depth, G, P))),
            rows(jnp.swapaxes(b_re, 2, 3)), rows(jnp.swapaxes(b_im, 2, 3)), rows(c_re), rows(c_im))


def _s5_tables(stacked_inputs, layer):
    G, P, K = N_SSM_GROUPS, SSM_STATE, SSM_GROUP
    Q, GB = S5_LANE_BLOCKS, G // S5_LANE_BLOCKS
    tile = np.tile(np.eye(P, dtype=np.float32), (1, GB))
    consts = (jnp.asarray(tile, BF16), jnp.asarray(tile.T, BF16))
    t3 = jax.ShapeDtypeStruct((S5_STEPS, G * K, P), F32)
    out_shape = [jax.ShapeDtypeStruct((Q, S5_STEPS * GB * K, S5_BLOCK_STATE), BF16),
                 jax.ShapeDtypeStruct((Q, S5_STEPS // 2, S5_BLOCK_STATE + S5_STEPS * GB * K,
                                       2 * GB * K), BF16),
                 t3, t3]
    whole = lambda sds: pl.BlockSpec(sds.shape, lambda i: (0,) * len(sds.shape),
                                     pipeline_mode=pl.Buffered(1))
    wx, wy, apr, api = pl.pallas_call(
        _s5_prep_kernel,
        grid=(1,),
        in_specs=[_layer_spec(t, layer) for t in stacked_inputs]
        + [_const_spec(t.shape) for t in consts],
        out_specs=[whole(sds) for sds in out_shape],
        out_shape=out_shape,
        compiler_params=pltpu.CompilerParams(dimension_semantics=("arbitrary",),
                                             vmem_limit_bytes=V7X_VMEM_LIMIT_BYTES),
        name="s5_prep",
    )(*stacked_inputs, *consts)
    ap = jnp.concatenate([t[:, ::K, :].reshape(S5_STEPS, Q, GB * P) for t in (apr, api)], axis=-1)
    return wx, wy, ap.reshape(S5_STEPS, S5_STATE_ROW)


def _state_to_lanes(h_re, h_im):
    n = h_re.shape[0]
    parts = [t.reshape(n, S5_LANE_BLOCKS, -1) for t in (h_re, h_im)]
    return jnp.concatenate(parts, axis=-1).reshape(n, S5_STATE_ROW)


def _lanes_to_state(h):
    n = h.shape[0]
    h = h.reshape(n, S5_LANE_BLOCKS, 2, N_SSM_GROUPS // S5_LANE_BLOCKS, SSM_STATE)
    return (h[:, :, 0].reshape(n, N_SSM_GROUPS, SSM_STATE), h[:, :, 1].reshape(n, N_SSM_GROUPS, SSM_STATE))


def _s5_kernel(u_ref, h0_ref, wx_ref, wy_ref, ap_ref, y_ref, hl_ref, *scratch, rows, scan):
    half = S5_BLOCK_STATE // 2
    cw = D_SSM // S5_LANE_BLOCKS
    blocks = range(S5_LANE_BLOCKS)
    ub = u_ref[...].astype(BF16)
    ucat = [jnp.concatenate([ub[:, s * D_SSM + q * cw: s * D_SSM + (q + 1) * cw]
                             for s in range(S5_STEPS)], axis=1) for q in blocks]
    x = [jnp.dot(ucat[q], wx_ref[q], preferred_element_type=F32) for q in blocks]
    ap = [ap_ref[:, q * S5_BLOCK_STATE:(q + 1) * S5_BLOCK_STATE] for q in blocks]

    def emit_outputs(q, hprev):
        lhs = jnp.concatenate([hprev, ucat[q]], axis=1)
        for j in range(S5_STEPS // 2):
            kk = S5_BLOCK_STATE + cw * (2 * j + 2)
            y2 = jnp.dot(lhs[:, :kk], wy_ref[q, j, :kk, :], preferred_element_type=F32)
            c0 = (2 * j) * D_SSM + q * cw
            c1 = (2 * j + 1) * D_SSM + q * cw
            y_ref[:, c0:c0 + cw] = y2[:, :cw]
            y_ref[:, c1:c1 + cw] = y2[:, cw:]

    if scan:
        hs_ref, hp_ref = scratch
        sub3 = lax.broadcasted_iota(jnp.int32, (1, 8, 1), 1)
        sub2 = lax.broadcasted_iota(jnp.int32, (8, 1), 0)
        for q in blocks:
            xr = x[q][:, :half].reshape(rows // 8, 8, half)
            xi = x[q][:, half:].reshape(rows // 8, 8, half)
            for k in (1, 2, 4):
                er = jnp.where(sub3 >= k, ap[q][k - 1:k, :half].reshape(1, 1, half), 0.0)
                ei = jnp.where(sub3 >= k, ap[q][k - 1:k, half:].reshape(1, 1, half), 0.0)
                sr = pltpu.roll(xr, k, 1)
                si = pltpu.roll(xi, k, 1)
                xr, xi = xr + er * sr - ei * si, xi + er * si + ei * sr
            hs_ref[q, :, :half] = xr.reshape(rows, half)
            hs_ref[q, :, half:] = xi.reshape(rows, half)
        h0 = h0_ref[0]
        for q in blocks:
            lo = q * S5_BLOCK_STATE
            cr, ci = h0[:, lo:lo + half], h0[:, lo + half:lo + S5_BLOCK_STATE]
            pwr, pwi = ap[q][:, :half], ap[q][:, half:]
            for j in range(rows // 8):
                sl = slice(j * 8, (j + 1) * 8)
                hr = hs_ref[q, sl, :half] + pwr * cr - pwi * ci
                hi = hs_ref[q, sl, half:] + pwr * ci + pwi * cr
                hp_ref[q, sl, :half] = jnp.where(sub2 == 0, cr, pltpu.roll(hr, 1, 0))
                hp_ref[q, sl, half:] = jnp.where(sub2 == 0, ci, pltpu.roll(hi, 1, 0))
                cr, ci = hr[7:8, :], hi[7:8, :]
            hl_ref[0, :, lo:lo + half] = cr
            hl_ref[0, :, lo + half:lo + S5_BLOCK_STATE] = ci
            emit_outputs(q, hp_ref[q].astype(BF16))
    else:
        for q in blocks:
            lo = q * S5_BLOCK_STATE
            h0r = h0_ref[:, lo:lo + half]
            h0i = h0_ref[:, lo + half:lo + S5_BLOCK_STATE]
            er, ei = ap[q][0:1, :half], ap[q][0:1, half:]
            hl_ref[:, lo:lo + half] = er * h0r - ei * h0i + x[q][:, :half]
            hl_ref[:, lo + half:lo + S5_BLOCK_STATE] = er * h0i + ei * h0r + x[q][:, half:]
            emit_outputs(q, jnp.concatenate([h0r, h0i], axis=1).astype(BF16))


def _s5(u8, h_re, h_im, tables, seq_len):
    wx, wy, ap = tables
    n = u8.shape[0] * S5_STEPS
    nseq = n // seq_len
    h0 = _state_to_lanes(h_re, h_im)
    cps = seq_len // S5_STEPS
    scan = cps > 1
    if scan:
        rows = cps
        grid = (nseq,)
        h0 = h0.reshape(nseq, 1, S5_STATE_ROW)
        h_spec = pl.BlockSpec((1, 1, S5_STATE_ROW), lambda i: (i, 0, 0))
        h_shape = jax.ShapeDtypeStruct((nseq, 1, S5_STATE_ROW), F32)
    else:
        rows = min(nseq, 128)
        grid = (nseq // rows,)
        h_spec = pl.BlockSpec((rows, S5_STATE_ROW), lambda i: (i, 0))
        h_shape = jax.ShapeDtypeStruct((nseq, S5_STATE_ROW), F32)
    y8, hl = pl.pallas_call(
        functools.partial(_s5_kernel, rows=rows, scan=scan),
        grid=grid,
        in_specs=[pl.BlockSpec((rows, S5_ROW), lambda i: (i, 0)),
                  h_spec,
                  _const_spec(wx.shape), _const_spec(wy.shape), _const_spec(ap.shape)],
        out_specs=[pl.BlockSpec((rows, S5_ROW), lambda i: (i, 0)), h_spec],
        out_shape=[jax.ShapeDtypeStruct((n // S5_STEPS, S5_ROW), F32), h_shape],
        scratch_shapes=[pltpu.VMEM((S5_LANE_BLOCKS, rows, S5_BLOCK_STATE), F32)] * 2 if scan else [],
        compiler_params=pltpu.CompilerParams(dimension_semantics=("arbitrary",),
                                             vmem_limit_bytes=V7X_VMEM_LIMIT_BYTES),
        name="s5",
    )(u8, h0, wx, wy, ap)
    hre_new, him_new = _lanes_to_state(hl.reshape(nseq, S5_STATE_ROW))
    return y8, hre_new, him_new


RW_CHUNK = 64
RW_STEP_CHUNKS_LONG = 8
RW_STEP_CHUNKS_SHORT = 4
RW_GROUPS = 2
RW_GROUP_HEADS = N_RWKV_HEADS // RW_GROUPS
RW_GROUP_LANES = D_RWKV // RW_GROUPS


def _split3(x):
    hi = x.astype(BF16)
    r1 = x - hi.astype(F32)
    mid = r1.astype(BF16)
    lo = (r1 - mid.astype(F32)).astype(BF16)
    return hi, mid, lo


def _bd(x, hm_ref):
    xb = x.astype(BF16)
    return jnp.concatenate([xb * hm_ref[h] for h in range(RW_GROUP_HEADS)], axis=0)


def _unbd(f, hm_ref):
    n = RWKV_HEAD
    out = f[:n] * hm_ref[0].astype(F32)
    for h in range(1, RW_GROUP_HEADS):
        out = out + f[h * n:(h + 1) * n] * hm_ref[h].astype(F32)
    return out


def _rwkv_kernel(z_ref, s0_ref, w0_ref, w2_ref, a0_ref, a2_ref, g2_ref, kk_ref, ka_ref, rk_ref,
                 lg_ref, lb_ref, ones_ref, hm_ref, cm_ref, tri_ref, y_ref, sl_ref, ypre_ref,
                 *scratch, seg_len, step_chunks, layer):
    row_of = lambda ref: ref[layer:layer + 1, :]
    C = RW_CHUNK
    N = RWKV_HEAD
    GL = RW_GROUP_LANES
    chained = seg_len == C
    if chained:
        (st_ref,) = scratch
        b = pl.program_id(1)

        @pl.when(b == 0)
        def _():
            for g in range(RW_GROUPS):
                st_ref[g] = jnp.concatenate(
                    [s0_ref[0, h] for h in range(g * RW_GROUP_HEADS, (g + 1) * RW_GROUP_HEADS)], axis=1)

    ones_bd = ones_ref[...]
    z = z_ref[...].astype(F32)
    r = z[:, :D_RWKV]
    k = z[:, D_RWKV:2 * D_RWKV]
    v = z[:, 2 * D_RWKV:3 * D_RWKV]
    o = 3 * D_RWKV
    xw = z[:, o:o + DECAY_LORA]
    xa = z[:, o + DECAY_LORA:o + DECAY_LORA + AAA_LORA]
    xg = z[:, o + DECAY_LORA + AAA_LORA:]
    wd = -(row_of(w0_ref) + _mm(jnp.tanh(xw), w2_ref[...]))
    w = -(jnp.maximum(wd, 0.0) + jnp.log1p(jnp.exp(-jnp.abs(wd)))) - 0.5
    lw = -jnp.exp(w)
    a = _sigmoid(row_of(a0_ref) + _mm(xa, a2_ref[...]))
    out_gate = _mm(_sigmoid(xg), g2_ref[...])
    kk = k * row_of(kk_ref)
    kk = kk / jnp.maximum(jnp.sqrt(_seg_sum(kk * kk, ones_bd)), L2_EPS)
    kmod = k * (1.0 + (a - 1.0) * row_of(ka_ref))

    tri = tri_ref[...]
    tb = tri.shape[0]
    terms = _split3(lw)
    cw = jnp.concatenate([sum(jnp.dot(tri, t[i:i + tb], preferred_element_type=F32) for t in terms)
                          for i in range(0, lw.shape[0], tb)], axis=0)
    tot = _expand_rows(_step_rows(cw, seg_len, seg_len - 1), seg_len)
    w_in = jnp.exp(cw)
    w_tail = jnp.exp(tot - cw)
    w_inv = jnp.exp(-cw)
    w_all = jnp.exp(tot)
    ah = -kk * jnp.exp(cw - lw)
    bh = kk * a * w_inv
    kh = kmod * w_inv
    rh = r * w_in
    bt = kk * a * w_tail
    kt = kmod * w_tail
    strict = cm_ref[0] > 0.0
    incl = cm_ref[1] > 0.0
    eye_cat = cm_ref[2]
    n_fac = seg_len.bit_length() - 1

    items = [(c, g) for c in range(step_chunks) for g in range(RW_GROUPS)]
    sub = lambda t, c, g: t[c * C:(c + 1) * C, g * GL:(g + 1) * GL]
    bd = lambda t: _bd(t, hm_ref)
    A = [sub(ah, c, g) for c, g in items]
    R = [sub(rh, c, g) for c, g in items]
    V = [sub(v, c, g) for c, g in items]
    G = [_mm_nt(jnp.concatenate([A[i], R[i]], axis=0),
                jnp.concatenate([bd(sub(bh, c, g)), bd(sub(kh, c, g))], axis=0))
         for i, (c, g) in enumerate(items)]
    AB = [jnp.where(strict, t[:C, :GL], 0.0) for t in G]
    AK = [jnp.where(strict, t[:C, GL:], 0.0) for t in G]
    RB = [jnp.where(incl, t[C:, :GL], 0.0) for t in G]
    RK = [jnp.where(incl, t[C:, GL:], 0.0) for t in G]
    KV = [_mm(jnp.concatenate([AK[i], RK[i]], axis=0), bd(V[i])) for i in range(len(items))]
    T = [eye_cat + t for t in AB]
    P = [_mm(t, bd(t)) for t in AB]
    for _ in range(n_fac - 2):
        PT = [_mm(jnp.concatenate([P[i], T[i]], axis=0), bd(P[i])) for i in range(len(items))]
        P = [t[:C] for t in PT]
        T = [T[i] + PT[i][C:] for i in range(len(items))]
    T = [T[i] + _mm(T[i], bd(P[i])) for i in range(len(items))]
    RBT = [_mm(RB[i], bd(T[i])) for i in range(len(items))]
    X = [_mm(jnp.concatenate([T[i], RBT[i]], axis=0),
             jnp.concatenate([bd(A[i]), bd(KV[i][:C])], axis=1)) for i in range(len(items))]
    TA = [t[:C, :GL] for t in X]
    U0 = [t[:C, GL:] for t in X]
    Rt = [R[i] + X[i][C:, :GL] for i in range(len(items))]
    Y0 = [X[i][C:, GL:] + KV[i][C:] for i in range(len(items))]
    nseg = C // seg_len
    heads = lambda g: range(g * RW_GROUP_HEADS, (g + 1) * RW_GROUP_HEADS)
    head_lanes = lambda h: slice((h % RW_GROUP_HEADS) * N, (h % RW_GROUP_HEADS + 1) * N)
    if chained:
        PhiT, PsiT = [], []
        for i, (c, g) in enumerate(items):
            Bt, Kt, Wa = sub(bt, c, g), sub(kt, c, g), sub(w_all, c, g)
            PhiT.append(bd(eye_cat * Wa[0:1, :] + _unbd(_mm_tn(TA[i], Bt), hm_ref)))
            PsiT.append(_unbd(_mm_tn(jnp.concatenate([U0[i], V[i]], axis=0),
                                     jnp.concatenate([Bt, Kt], axis=0)), hm_ref))
        states = [st_ref[g] for g in range(RW_GROUPS)]
        for i, (c, g) in enumerate(items):
            S = states[g]
            ypre_ref[c * C:(c + 1) * C, g * GL:(g + 1) * GL] = _mm_nt(Rt[i], bd(S)) + Y0[i]
            states[g] = _mm(S, PhiT[i]) + PsiT[i]
        for g in range(RW_GROUPS):
            st_ref[g] = states[g]
            for h in heads(g):
                sl_ref[0, h] = states[g][:, head_lanes(h)]
    else:
        segs = [(i, c, g, s) for i, (c, g) in enumerate(items) for s in range(nseg)]
        rows_of = lambda s: slice(s * seg_len, (s + 1) * seg_len)
        S0 = [jnp.concatenate([s0_ref[c * nseg + s, h] for h in heads(g)], axis=1)
              for i, c, g, s in segs]
        RU = [_mm_nt(jnp.concatenate([Rt[i][rows_of(s)], TA[i][rows_of(s)]], axis=0), bd(S0[n]))
              for n, (i, c, g, s) in enumerate(segs)]
        for n, (i, c, g, s) in enumerate(segs):
            ypre_ref[c * C + s * seg_len:c * C + (s + 1) * seg_len, g * GL:(g + 1) * GL] = (
                RU[n][:seg_len] + Y0[i][rows_of(s)])
        UB = [_mm_tn(jnp.concatenate([RU[n][seg_len:] + U0[i][rows_of(s)], V[i][rows_of(s)]], axis=0),
                     jnp.concatenate([sub(bt, c, g)[rows_of(s)], sub(kt, c, g)[rows_of(s)]], axis=0))
              for n, (i, c, g, s) in enumerate(segs)]
        for n, (i, c, g, s) in enumerate(segs):
            S = S0[n] * sub(w_all, c, g)[s * seg_len:s * seg_len + 1, :] + _unbd(UB[n], hm_ref)
            for h in heads(g):
                sl_ref[c * nseg + s, h] = S[:, head_lanes(h)]
    y = ypre_ref[...]

    mu = _seg_sum(y, ones_bd) * (1.0 / N)
    d = y - mu
    var = _seg_sum(d * d, ones_bd) * (1.0 / N)
    yn = d * lax.rsqrt(var + GN_EPS) * row_of(lg_ref) + row_of(lb_ref)
    bonus = _seg_sum(r * kmod * row_of(rk_ref), ones_bd) * v
    y_ref[...] = (yn + bonus) * out_gate


def _rwkv_constants(seg_len, rows):
    n, gh, gl = RWKV_HEAD, RW_GROUP_HEADS, RW_GROUP_LANES
    lane_head = np.arange(gl) // n
    hm = (lane_head[None, None, :] == np.arange(gh)[:, None, None]) * np.ones((gh, n, gl))
    i = np.arange(RW_CHUNK)[:, None]
    j = (np.arange(gl) % n)[None, :]
    same = (i // seg_len) == (j // seg_len)
    cm = np.stack([(j < i) & same, (j <= i) & same, j == i]).astype(np.float32)
    tb = min(rows, 2 * LANES)
    ri = np.arange(tb)[:, None]
    rj = np.arange(tb)[None, :]
    same_r = (ri // seg_len) == (rj // seg_len)
    tri = ((rj <= ri) & same_r).astype(np.float32)
    ones_bd = np.kron(np.eye(gh), np.ones((n, n)))
    return (jnp.asarray(ones_bd, BF16), jnp.asarray(hm, BF16), jnp.asarray(cm, F32),
            jnp.asarray(tri, BF16))


def _rwkv(z, s0, p, layer, seq_len):
    n = z.shape[0]
    if seq_len >= RW_CHUNK:
        step_chunks = min(RW_STEP_CHUNKS_LONG, seq_len // RW_CHUNK)
    else:
        step_chunks = min(RW_STEP_CHUNKS_SHORT, n // RW_CHUNK)
    nseq = n // seq_len
    nh, hd = N_RWKV_HEADS, RWKV_HEAD
    seg_len = min(seq_len, RW_CHUNK)
    rows = RW_CHUNK * step_chunks
    chained = seq_len >= RW_CHUNK
    if chained:
        steps = seq_len // rows
        grid = (nseq, steps)
        row_map = lambda s, b: (s * steps + b, 0)
        state_spec = pl.BlockSpec((1, nh, hd, hd), lambda s, b: (s, 0, 0, 0))
        scratch = [pltpu.VMEM((rows, D_RWKV), F32), pltpu.VMEM((RW_GROUPS, hd, RW_GROUP_LANES), F32)]
    else:
        grid = (n // rows, 1)
        row_map = lambda s, b: (s, 0)
        state_spec = pl.BlockSpec((rows // seq_len, nh, hd, hd), lambda s, b: (s, 0, 0, 0))
        scratch = [pltpu.VMEM((rows, D_RWKV), F32)]
    if s0 is None:
        s0 = p['state_rwkv']
        s0_spec = pl.BlockSpec((None,) + state_spec.block_shape,
                               lambda s, b: (layer, s, 0, 0, 0))
    else:
        s0_spec = state_spec
    stacked = [p[k] for k in ('w0', 'w2', 'a0', 'a2', 'g2', 'k_k', 'k_a', 'r_k', 'lnx_g', 'lnx_b')]
    consts = list(_rwkv_constants(seg_len, rows))
    y, s_new = pl.pallas_call(
        functools.partial(_rwkv_kernel, seg_len=seg_len, step_chunks=step_chunks, layer=layer),
        grid=grid,
        in_specs=[pl.BlockSpec((rows, N_SHIFT), row_map), s0_spec]
        + [_layer_spec(t, layer) for t in stacked] + [_const_spec(t.shape) for t in consts],
        out_specs=[pl.BlockSpec((rows, D_RWKV), row_map), state_spec],
        out_shape=[jax.ShapeDtypeStruct((n, D_RWKV), F32),
                   jax.ShapeDtypeStruct((nseq, nh, hd, hd), F32)],
        scratch_shapes=scratch,
        compiler_params=pltpu.CompilerParams(dimension_semantics=("arbitrary", "arbitrary"),
                                             vmem_limit_bytes=V7X_VMEM_LIMIT_BYTES),
        name="rwkv7",
    )(z, s0, *stacked, *consts)
    return y, s_new


def _ffn_kernel(h_ref, y8_ref, u8_ref, yb_ref, pe_ref, c1_ref, c2_ref, d8_ref, wglu_ref, bglu_ref,
                wout_ref, g2_ref, wup_ref, cw_ref, cb_ref, wdn_ref, wple_ref, wpg_ref, gf_ref,
                *rest, seq_len, tm, final, layer):
    row_of = lambda ref: ref[layer:layer + 1, :]
    i = pl.program_id(0)
    long_seq = seq_len >= tm
    if long_seq:
        o_ref, ga_ref, gb_ref, fold_ref, act_ref, carry_ref = rest

        @pl.when(i == 0)
        def _():
            carry_ref[...] = jnp.zeros_like(carry_ref)
    else:
        o_ref, ga_ref, gb_ref, fold_ref, act_ref = rest
    ya = y8_ref[...] + row_of(d8_ref) * u8_ref[...]
    c_gelu = math.sqrt(2.0 / math.pi)
    ya = ya * (0.5 * (1.0 + jnp.tanh(c_gelu * (ya + 0.044715 * (ya * ya * ya)))))
    ya = _unfold_rows(ya, fold_ref)
    ya = ya * _sigmoid(_mm(ya, wglu_ref[...]) + row_of(bglu_ref))
    h1 = (h_ref[...] + _mm(ya, wout_ref[:D_SSM, :]) + _mm(yb_ref[...], wout_ref[D_SSM:, :]))
    x2 = _rmsnorm(h1, row_of(g2_ref)).astype(BF16)
    row = lax.broadcasted_iota(jnp.int32, (tm, 1), 0)
    if long_seq:
        first = (i % (seq_len // tm)) == 0
    else:
        t = row % seq_len
    def up(c):
        cs = slice(c * FF_CHUNK, (c + 1) * FF_CHUNK)
        gs = slice(D_FF + c * FF_CHUNK, D_FF + (c + 1) * FF_CHUNK)
        return (jnp.dot(x2, wup_ref[:, cs], preferred_element_type=F32),
                jnp.dot(x2, wup_ref[:, gs], preferred_element_type=F32))

    n_chunks = D_FF // FF_CHUNK
    nxt = up(0)
    for c in range(n_chunks):
        cs = slice(c * FF_CHUNK, (c + 1) * FF_CHUNK)
        val, gate = nxt
        if c + 1 < n_chunks:
            nxt = up(c + 1)
        r1 = pltpu.roll(gate, 1, 0)
        r2 = pltpu.roll(gate, 2, 0)
        if long_seq:
            m1 = jnp.where(first, c1_ref[0][:, cs], carry_ref[1:2, cs])
            m2 = jnp.where(first, c2_ref[0][:, cs], carry_ref[0:1, cs])
            p1 = jnp.where(row == 0, m1, r1)
            p2 = jnp.where(row == 0, m2, jnp.where(row == 1, m1, r2))
            carry_ref[:, cs] = gate[tm - 2:tm, :]
            ga_ref[0, :, cs] = gate[tm - 2:tm - 1, :]
            gb_ref[0, :, cs] = gate[tm - 1:tm, :]
        else:
            m1 = _expand_rows(c1_ref[:, cs], seq_len)
            m2 = _expand_rows(c2_ref[:, cs], seq_len)
            p1 = jnp.where(t == 0, m1, r1)
            p2 = jnp.where(t == 0, m2, jnp.where(t == 1, m1, r2))
            ga_ref[:, cs] = _step_rows(gate, seq_len, seq_len - 2)
            gb_ref[:, cs] = _step_rows(gate, seq_len, seq_len - 1)
        conv = (cb_ref[layer:layer + 1, cs] + cw_ref[2:3, cs] * gate + cw_ref[1:2, cs] * p1 + cw_ref[0:1, cs] * p2)
        act_ref[:, cs] = (conv * _sigmoid(conv) * val).astype(BF16)
    h2 = h1 + jnp.dot(act_ref[...], wdn_ref[...], preferred_element_type=F32)
    h3 = h2 + _mm(pe_ref[...], wple_ref[...]) * _sigmoid(_mm(h2, wpg_ref[...]))
    if final:
        h3 = _rmsnorm(h3, gf_ref[...])
    o_ref[...] = h3


def _ffn(h, y8, u8, yb, pe, conv0, p, layer, seq_len, tm, final):
    n = h.shape[0]
    nseq = n // seq_len
    row_spec = lambda w: pl.BlockSpec((tm, w), lambda i: (i, 0))
    pe_spec = pl.BlockSpec((None, tm, D_PLE), lambda i: (layer, i, 0))
    fold_spec = pl.BlockSpec((tm // S5_STEPS, S5_ROW), lambda i: (i, 0))
    scratch = [pltpu.VMEM((FOLD_LANE_BLOCKS, tm, LANES), F32), pltpu.VMEM((tm, D_FF), BF16)]
    if seq_len >= tm:
        tps = seq_len // tm
        seq_spec = pl.BlockSpec((1, 1, D_FF), lambda i: (i // tps, 0, 0))
        seq_shape = (nseq, 1, D_FF)
        scratch.append(pltpu.VMEM((2, D_FF), F32))
    else:
        seq_spec = pl.BlockSpec((tm // seq_len, D_FF), lambda i: (i, 0))
        seq_shape = (nseq, D_FF)
    stacked = [p[k] for k in ('d8', 'w_glu', 'b_glu', 'w_out', 'norm2_g', 'w_up', 'conv_w',
                              'conv_b', 'w_down', 'w_ple', 'w_pg')]
    consts = [p['final_g']]
    weights = stacked + consts
    out, ga, gb = pl.pallas_call(
        functools.partial(_ffn_kernel, seq_len=seq_len, tm=tm, final=final, layer=layer),
        grid=(n // tm,),
        in_specs=[row_spec(D_MODEL), fold_spec, fold_spec, row_spec(D_RWKV), pe_spec,
                  seq_spec, seq_spec] + [_layer_spec(t, layer) for t in stacked]
        + [_const_spec(t.shape) for t in consts],
        out_specs=[row_spec(D_MODEL), seq_spec, seq_spec],
        out_shape=[jax.ShapeDtypeStruct((n, D_MODEL), F32)]
        + [jax.ShapeDtypeStruct(seq_shape, F32)] * 2,
        scratch_shapes=scratch,
        compiler_params=pltpu.CompilerParams(dimension_semantics=("arbitrary",),
                                             vmem_limit_bytes=V7X_VMEM_LIMIT_BYTES),
        name="ffn",
    )(h, y8, u8, yb, pe, conv0[:, 1].reshape(seq_shape), conv0[:, 0].reshape(seq_shape), *weights)
    conv_new = jnp.stack([ga.reshape(nseq, D_FF), gb.reshape(nseq, D_FF)], axis=1)
    return out, conv_new


def _layer(h, pe, st, p, layer, s5_tables, seq_len, tm, final):
    ssm_re0, ssm_im0, rwkv0, shift0, conv0 = st
    tm_in, tm_ffn = tm
    u8, z, shift_new = _inproj(h, p, layer, shift0, seq_len, tm_in)
    y8, hre, him = _s5(u8, ssm_re0, ssm_im0, s5_tables, seq_len)
    yb, s_last = _rwkv(z, rwkv0, p, layer, seq_len)
    h, conv_new = _ffn(h, y8, u8, yb, pe, conv0, p, layer, seq_len, tm_ffn, final)
    return h, (hre, him, s_last, shift_new, conv_new)


def _stacked_params(w, state_rwkv, state_shift):
    depth = w['w_in'].shape[0]
    row = lambda t: t.reshape(depth, -1).astype(F32)
    bf = lambda t: t.astype(BF16)
    return {
        'norm1_g': row(w['norm1_g']), 'w_in': bf(w['w_in']), 'shift_mu': row(w['shift_mu']),
        'd8': jnp.tile(row(w['ssm_d']), (1, S5_STEPS)), 'w_glu': bf(w['ssm_w_glu']),
        'b_glu': row(w['ssm_b_glu']),
        'w0': row(w['rwkv_w0']), 'w2': bf(w['rwkv_w2']), 'a0': row(w['rwkv_a0']),
        'a2': bf(w['rwkv_a2']), 'g2': bf(w['rwkv_g2']), 'k_k': row(w['rwkv_k_k']),
        'k_a': row(w['rwkv_k_a']), 'r_k': row(w['rwkv_r_k']), 'lnx_g': row(w['rwkv_lnx_g']),
        'lnx_b': row(w['rwkv_lnx_b']), 'w_out': bf(w['w_out']), 'norm2_g': row(w['norm2_g']),
        'w_up': bf(w['w_up']), 'conv_w': w['conv_w'].astype(F32), 'conv_b': row(w['conv_b']),
        'w_down': bf(w['w_down']), 'w_ple': bf(w['w_ple']), 'w_pg': bf(w['w_pg']),
        'final_g': w['final_g'].reshape(1, -1).astype(F32),
        'state_rwkv': state_rwkv.astype(F32), 'state_shift': state_shift.astype(F32),
    }


def _forward(x_prompt, x_sample, p_prompt, p_sample, state_ssm_re, state_ssm_im, state_rwkv,
             state_shift, state_conv, w, tm_prompt, tm_sample):
    depth = w['w_in'].shape[0]
    bp, lp, _ = x_prompt.shape
    bs, ls, _ = x_sample.shape
    hp = x_prompt.reshape(bp * lp, D_MODEL).astype(F32)
    hs = x_sample.reshape(bs * ls, D_MODEL).astype(F32)
    pe_p = p_prompt.reshape(depth, bp * lp, D_PLE).astype(F32)
    pe_s = p_sample.reshape(depth, bs * ls, D_PLE).astype(F32)
    zero_st = (jnp.zeros((bp, N_SSM_GROUPS, SSM_STATE), F32),
               jnp.zeros((bp, N_SSM_GROUPS, SSM_STATE), F32),
               jnp.zeros((bp, N_RWKV_HEADS, RWKV_HEAD, RWKV_HEAD), F32),
               jnp.zeros((bp, N_SHIFT), F32),
               jnp.zeros((bp, 2, D_FF), F32))
    p = _stacked_params(w, state_rwkv, state_shift)
    s5_inputs = _s5_table_inputs(w['ssm_lam_re'], w['ssm_lam_im'], w['ssm_log_dt'], w['ssm_b_re'],
                                 w['ssm_b_im'], w['ssm_c_re'], w['ssm_c_im'])
    new_p = [[] for _ in range(5)]
    new_s = [[] for _ in range(5)]
    for i in range(depth):
        tables = _s5_tables(s5_inputs, i)
        final = i == depth - 1
        hp, stp = _layer(hp, pe_p, zero_st, p, i, tables, lp, tm_prompt, final)
        st_in = (state_ssm_re[i].astype(F32), state_ssm_im[i].astype(F32), None, None,
                 state_conv[i].astype(F32))
        hs, sts = _layer(hs, pe_s, st_in, p, i, tables, ls, tm_sample, final)
        for j in range(5):
            new_p[j].append(stp[j])
            new_s[j].append(sts[j])
    y_prompt = hp.reshape(bp, lp, D_MODEL).astype(x_prompt.dtype)
    y_sample = hs.reshape(bs, ls, D_MODEL).astype(x_sample.dtype)
    dts = (state_ssm_re.dtype, state_ssm_im.dtype, state_rwkv.dtype, state_shift.dtype,
           state_conv.dtype)
    outs_p = tuple(jnp.stack(new_p[j]).astype(dts[j]) for j in range(5))
    outs_s = tuple(jnp.stack(new_s[j]).astype(dts[j]) for j in range(5))
    return (y_prompt, y_sample) + outs_p + outs_s


def kernel(x_prompt, x_sample, p_prompt, p_sample, state_ssm_re, state_ssm_im, state_rwkv, state_shift, state_conv, norm1_g, w_in, shift_mu, ssm_lam_re, ssm_lam_im, ssm_log_dt, ssm_b_re, ssm_b_im, ssm_c_re, ssm_c_im, ssm_d, ssm_w_glu, ssm_b_glu, rwkv_w0, rwkv_w2, rwkv_a0, rwkv_a2, rwkv_g2, rwkv_k_k, rwkv_k_a, rwkv_r_k, rwkv_lnx_g, rwkv_lnx_b, w_out, norm2_g, w_up, conv_w, conv_b, w_down, w_ple, w_pg, final_g):
    w = dict(norm1_g=norm1_g, w_in=w_in, shift_mu=shift_mu, ssm_lam_re=ssm_lam_re,
             ssm_lam_im=ssm_lam_im, ssm_log_dt=ssm_log_dt, ssm_b_re=ssm_b_re, ssm_b_im=ssm_b_im,
             ssm_c_re=ssm_c_re, ssm_c_im=ssm_c_im, ssm_d=ssm_d, ssm_w_glu=ssm_w_glu,
             ssm_b_glu=ssm_b_glu, rwkv_w0=rwkv_w0, rwkv_w2=rwkv_w2, rwkv_a0=rwkv_a0,
             rwkv_a2=rwkv_a2, rwkv_g2=rwkv_g2, rwkv_k_k=rwkv_k_k, rwkv_k_a=rwkv_k_a,
             rwkv_r_k=rwkv_r_k, rwkv_lnx_g=rwkv_lnx_g, rwkv_lnx_b=rwkv_lnx_b, w_out=w_out,
             norm2_g=norm2_g, w_up=w_up, conv_w=conv_w, conv_b=conv_b, w_down=w_down,
             w_ple=w_ple, w_pg=w_pg, final_g=final_g)
    lp = x_prompt.shape[1]
    ns = x_sample.shape[0] * x_sample.shape[1]
    return _forward(x_prompt, x_sample, p_prompt, p_sample, state_ssm_re, state_ssm_im,
                    state_rwkv, state_shift, state_conv, w,
                    tm_prompt=(min(1024, lp), min(512, lp)),
                    tm_sample=(min(512, ns), min(512, ns)))
```

```python
import functools
import math

import numpy as np
import jax
import jax.numpy as jnp
from jax import lax
from jax.experimental import pallas as pl
from jax.experimental.pallas import tpu as pltpu

F32 = jnp.float32
BF16 = jnp.bfloat16

D_MODEL = 1024
D_SSM = 512
D_RWKV = 512
SSM_GROUP = 16
N_SSM_GROUPS = 32
SSM_STATE = 64
RWKV_HEAD = 64
N_RWKV_HEADS = 8
DECAY_LORA = 64
AAA_LORA = 64
GATE_LORA = 128
N_SHIFT = 3 * D_RWKV + DECAY_LORA + AAA_LORA + GATE_LORA
N_IN = D_SSM + N_SHIFT
D_FF = 2816
D_PLE = 256
RMS_EPS = 1e-6
GN_EPS = 64e-5
L2_EPS = 1e-12

LANES = 128
FOLD_LANE_BLOCKS = D_SSM // LANES
S5_STEPS = 8
S5_ROW = S5_STEPS * D_SSM
S5_LANE_BLOCKS = D_SSM // LANES
S5_BLOCK_GROUPS = N_SSM_GROUPS // S5_LANE_BLOCKS
S5_BLOCK_STATE = 2 * S5_BLOCK_GROUPS * SSM_STATE
S5_STATE_ROW = S5_LANE_BLOCKS * S5_BLOCK_STATE
FF_CHUNK = 2 * LANES
V7X_VMEM_BYTES = 64 * 1024 * 1024
V7X_VMEM_LIMIT_BYTES = V7X_VMEM_BYTES * 7 // 8


def _mm(a, b):
    return jnp.dot(a.astype(BF16), b.astype(BF16), preferred_element_type=F32)


def _mm_nt(a, b):
    return lax.dot_general(a.astype(BF16), b.astype(BF16), (((1,), (1,)), ((), ())),
                           preferred_element_type=F32)


def _mm_tn(a, b):
    return lax.dot_general(a.astype(BF16), b.astype(BF16), (((0,), (0,)), ((), ())),
                           preferred_element_type=F32)


def _split2(x):
    hi = x.astype(BF16)
    return hi, (x - hi.astype(F32)).astype(BF16)


def _seg_sum(x, ones_bd):
    xb = x.astype(BF16)
    gl = ones_bd.shape[0]
    return jnp.concatenate([jnp.dot(xb[:, g:g + gl], ones_bd, preferred_element_type=F32)
                            for g in range(0, x.shape[1], gl)], axis=1)


def _rmsnorm(x, g):
    return x * lax.rsqrt(jnp.mean(x * x, axis=-1, keepdims=True) + RMS_EPS) * g


def _sigmoid(x):
    return 1.0 / (1.0 + jnp.exp(-x))


def _const_spec(shape):
    nd = len(shape)
    return pl.BlockSpec(shape, lambda *_: (0,) * nd, pipeline_mode=pl.Buffered(1))


def _layer_spec(t, layer):
    if t.ndim == 2:
        return _const_spec(t.shape)
    nd = t.ndim - 1
    return pl.BlockSpec((None,) + t.shape[1:], lambda *_: (layer,) + (0,) * nd,
                        pipeline_mode=pl.Buffered(1))


def _expand_rows(x, seq_len):
    nseq, w = x.shape
    return jnp.broadcast_to(x[:, None, :], (nseq, seq_len, w)).reshape(nseq * seq_len, w)


def _step_rows(x, seq_len, t):
    return x.reshape(x.shape[0] // seq_len, seq_len, x.shape[1])[:, t, :]


def _fold_rows(x, scr_ref, out_ref):
    tm = x.shape[0]
    for q in range(FOLD_LANE_BLOCKS):
        scr_ref[q] = x[:, q * LANES:(q + 1) * LANES]
    for s in range(S5_STEPS):
        for q in range(FOLD_LANE_BLOCKS):
            c0 = s * D_SSM + q * LANES
            out_ref[:, c0:c0 + LANES] = scr_ref[q, pl.ds(s, tm // S5_STEPS, stride=S5_STEPS), :]


def _unfold_rows(x8, scr_ref):
    tm = x8.shape[0] * S5_STEPS
    for s in range(S5_STEPS):
        for q in range(FOLD_LANE_BLOCKS):
            c0 = s * D_SSM + q * LANES
            scr_ref[q, pl.ds(s, tm // S5_STEPS, stride=S5_STEPS), :] = x8[:, c0:c0 + LANES]
    return jnp.concatenate([scr_ref[q] for q in range(FOLD_LANE_BLOCKS)], axis=1)


def _inproj_kernel(h_ref, g_ref, w_ref, mu_ref, init_ref, *rest, seq_len, tm, layer):
    i = pl.program_id(0)
    row_of = lambda ref: ref[layer:layer + 1, :]
    long_seq = seq_len >= tm
    if long_seq:
        u8_ref, z_ref, last_ref, fold_ref, carry_ref = rest

        @pl.when(i == 0)
        def _():
            carry_ref[...] = jnp.zeros_like(carry_ref)
    else:
        u8_ref, z_ref, last_ref, fold_ref = rest
    xn = _rmsnorm(h_ref[...], row_of(g_ref))
    proj = jnp.dot(xn.astype(BF16), w_ref[...], preferred_element_type=F32)
    _fold_rows(proj[:, :D_SSM], fold_ref, u8_ref)
    zr = proj[:, D_SSM:]
    rolled = pltpu.roll(zr, 1, 0)
    row = lax.broadcasted_iota(jnp.int32, (tm, 1), 0)
    if long_seq:
        first = (i % (seq_len // tm)) == 0
        row0 = jnp.where(first, init_ref[0], carry_ref[...])
        prev = jnp.where(row == 0, row0, rolled)
        carry_ref[...] = zr[tm - 1:tm, :]
        last_ref[0] = zr[tm - 1:tm, :]
    else:
        prev = jnp.where(row % seq_len == 0, _expand_rows(init_ref[...], seq_len), rolled)
        last_ref[...] = _step_rows(zr, seq_len, seq_len - 1)
    z_ref[...] = (zr + (prev - zr) * row_of(mu_ref)).astype(z_ref.dtype)


def _inproj(h, p, layer, shift0, seq_len, tm):
    n = h.shape[0]
    nseq = n // seq_len
    fold_scratch = pltpu.VMEM((FOLD_LANE_BLOCKS, tm, LANES), F32)
    if seq_len >= tm:
        tps = seq_len // tm
        seq_spec = pl.BlockSpec((1, 1, N_SHIFT), lambda i: (i // tps, 0, 0))
        init_spec = seq_spec
        init = shift0.reshape(nseq, 1, N_SHIFT)
        last_shape = jax.ShapeDtypeStruct((nseq, 1, N_SHIFT), F32)
        scratch = [fold_scratch, pltpu.VMEM((1, N_SHIFT), F32)]
    else:
        spt = tm // seq_len
        seq_spec = pl.BlockSpec((spt, N_SHIFT), lambda i: (i, 0))
        init_spec = pl.BlockSpec((None, spt, N_SHIFT), lambda i: (layer, i, 0))
        init = p['state_shift']
        last_shape = jax.ShapeDtypeStruct((nseq, N_SHIFT), F32)
        scratch = [fold_scratch]
    g1, w_in, mu = p['norm1_g'], p['w_in'], p['shift_mu']
    u8, z, last = pl.pallas_call(
        functools.partial(_inproj_kernel, seq_len=seq_len, tm=tm, layer=layer),
        grid=(n // tm,),
        in_specs=[pl.BlockSpec((tm, D_MODEL), lambda i: (i, 0)),
                  _layer_spec(g1, layer), _layer_spec(w_in, layer), _layer_spec(mu, layer),
                  init_spec],
        out_specs=[pl.BlockSpec((tm // S5_STEPS, S5_ROW), lambda i: (i, 0)),
                   pl.BlockSpec((tm, N_SHIFT), lambda i: (i, 0)),
                   seq_spec],
        out_shape=[jax.ShapeDtypeStruct((n // S5_STEPS, S5_ROW), F32),
                   jax.ShapeDtypeStruct((n, N_SHIFT), BF16),
                   last_shape],
        scratch_shapes=scratch,
        compiler_params=pltpu.CompilerParams(dimension_semantics=("arbitrary",),
                                             vmem_limit_bytes=V7X_VMEM_LIMIT_BYTES),
        name="inproj",
    )(h, g1, w_in, mu, init)
    return u8, z, last.reshape(nseq, N_SHIFT)


def _s5_prep_kernel(lr_ref, li_ref, ldt_ref, br_ref, bi_ref, cr_ref, ci_ref, tile_ref, sel_ref,
                    wx_ref, wy_ref, apr_ref, api_ref):
    lr, li = lr_ref[...], li_ref[...]
    dt = jnp.exp(ldt_ref[...])
    mag = jnp.exp(lr * dt)
    ar = mag * jnp.cos(li * dt)
    ai = mag * jnp.sin(li * dt)
    den = lr * lr + li * li
    nr = ar - 1.0
    fr = (nr * lr + ai * li) / den
    fi = (ai * lr - nr * li) / den
    br, bi = br_ref[...], bi_ref[...]
    bbr = fr * br - fi * bi
    bbi = fr * bi + fi * br
    cr, ci = cr_ref[...], ci_ref[...]
    pr = [jnp.ones_like(ar)]
    pi = [jnp.zeros_like(ar)]
    for _ in range(S5_STEPS):
        pr.append(pr[-1] * ar - pi[-1] * ai)
        pi.append(pr[-2] * ai + pi[-1] * ar)
    nrow = lr.shape[0]
    blk = nrow // S5_LANE_BLOCKS
    half = S5_BLOCK_STATE // 2
    ri = lax.broadcasted_iota(jnp.int32, (nrow, nrow), 0)
    ci_ = lax.broadcasted_iota(jnp.int32, (nrow, nrow), 1)
    same_group = (ri // SSM_GROUP) == (ci_ // SSM_GROUP)
    keep_in = ((ri % blk) // SSM_GROUP) == (ci_ // SSM_STATE)
    keep_out = (ri // SSM_STATE) == ((ci_ % blk) // SSM_GROUP)
    tile, sel = tile_ref[...], sel_ref[...]
    nt = (((1,), (1,)), ((), ()))
    c_hi, c_lo = _split2(jnp.concatenate([cr, -ci], axis=1))
    wy_ref[...] = jnp.zeros(wy_ref.shape, wy_ref.dtype)
    for s in range(S5_STEPS):
        qr, qi = pr[S5_STEPS - 1 - s], pi[S5_STEPS - 1 - s]
        wr = qr * bbr - qi * bbi
        wi = qr * bbi + qi * bbr
        for h, w_ in enumerate((wr, wi)):
            t = jnp.dot(w_.astype(BF16), tile, preferred_element_type=F32)
            t = jnp.where(keep_in, t, 0.0).astype(BF16)
            for q in range(S5_LANE_BLOCKS):
                wx_ref[q, s * blk:(s + 1) * blk, h * half:(h + 1) * half] = t[q * blk:(q + 1) * blk, :]
        tau = S5_STEPS - 1 - s
        lhs_hi, lhs_lo = _split2(jnp.concatenate([wr, wi], axis=1))
        kt = (lax.dot_general(lhs_hi, c_hi, nt, preferred_element_type=F32)
              + lax.dot_general(lhs_hi, c_lo, nt, preferred_element_type=F32)
              + lax.dot_general(lhs_lo, c_hi, nt, preferred_element_type=F32))
        kt = jnp.where(same_group, kt, 0.0).astype(BF16)
        for q in range(S5_LANE_BLOCKS):
            kq = kt[q * blk:(q + 1) * blk, q * blk:(q + 1) * blk]
            for s_in in range(S5_STEPS - tau):
                s_out = s_in + tau
                r0 = S5_BLOCK_STATE + s_in * blk
                wy_ref[q, s_out // 2, r0:r0 + blk, (s_out % 2) * blk:(s_out % 2 + 1) * blk] = kq
        mr = cr * pr[s + 1] - ci * pi[s + 1]
        mi = cr * pi[s + 1] + ci * pr[s + 1]
        for h, m_ in enumerate((mr, -mi)):
            t = lax.dot_general(sel, m_.astype(BF16), nt, preferred_element_type=F32)
            t = jnp.where(keep_out, t, 0.0).astype(BF16)
            for q in range(S5_LANE_BLOCKS):
                wy_ref[q, s // 2, h * half:(h + 1) * half, (s % 2) * blk:(s % 2 + 1) * blk] = (
                    t[:, q * blk:(q + 1) * blk])
    a8r, a8i = pr[S5_STEPS], pi[S5_STEPS]
    er, ei = a8r, a8i
    for n in range(S5_STEPS):
        apr_ref[n] = er
        api_ref[n] = ei
        er, ei = er * a8r - ei * a8i, er * a8i + ei * a8r


def _s5_table_inputs(lam_re, lam_im, log_dt, b_re, b_im, c_re, c_im):
    depth = lam_re.shape[0]
    G, P, K = N_SSM_GROUPS, SSM_STATE, SSM_GROUP
    rep = lambda t: jnp.repeat(t, K, axis=1)
    rows = lambda t: t.reshape(depth, G * K, P)
    return (rep(lam_re), rep(lam_im), rep(jnp.broadcast_to(log_dt[:, :, None], (depth, G, P))),
            rows(jnp.swapaxes(b_re, 2, 3)), rows(jnp.swapaxes(b_im, 2, 3)), rows(c_re), rows(c_im))


def _s5_tables(stacked_inputs, layer):
    G, P, K = N_SSM_GROUPS, SSM_STATE, SSM_GROUP
    Q, GB = S5_LANE_BLOCKS, G // S5_LANE_BLOCKS
    tile = np.tile(np.eye(P, dtype=np.float32), (1, GB))
    consts = (jnp.asarray(tile, BF16), jnp.asarray(tile.T, BF16))
    t3 = jax.ShapeDtypeStruct((S5_STEPS, G * K, P), F32)
    out_shape = [jax.ShapeDtypeStruct((Q, S5_STEPS * GB * K, S5_BLOCK_STATE), BF16),
                 jax.ShapeDtypeStruct((Q, S5_STEPS // 2, S5_BLOCK_STATE + S5_STEPS * GB * K,
                                       2 * GB * K), BF16),
                 t3, t3]
    whole = lambda sds: pl.BlockSpec(sds.shape, lambda i: (0,) * len(sds.shape),
                                     pipeline_mode=pl.Buffered(1))
    wx, wy, apr, api = pl.pallas_call(
        _s5_prep_kernel,
        grid=(1,),
        in_specs=[_layer_spec(t, layer) for t in stacked_inputs]
        + [_const_spec(t.shape) for t in consts],
        out_specs=[whole(sds) for sds in out_shape],
        out_shape=out_shape,
        compiler_params=pltpu.CompilerParams(dimension_semantics=("arbitrary",),
                                             vmem_limit_bytes=V7X_VMEM_LIMIT_BYTES),
        name="s5_prep",
    )(*stacked_inputs, *consts)
    ap = jnp.concatenate([t[:, ::K, :].reshape(S5_STEPS, Q, GB * P) for t in (apr, api)], axis=-1)
    return wx, wy, ap.reshape(S5_STEPS, S5_STATE_ROW)


def _state_to_lanes(h_re, h_im):
    n = h_re.shape[0]
    parts = [t.reshape(n, S5_LANE_BLOCKS, -1) for t in (h_re, h_im)]
    return jnp.concatenate(parts, axis=-1).reshape(n, S5_STATE_ROW)


def _lanes_to_state(h):
    n = h.shape[0]
    h = h.reshape(n, S5_LANE_BLOCKS, 2, N_SSM_GROUPS // S5_LANE_BLOCKS, SSM_STATE)
    return (h[:, :, 0].reshape(n, N_SSM_GROUPS, SSM_STATE), h[:, :, 1].reshape(n, N_SSM_GROUPS, SSM_STATE))


def _s5_kernel(u_ref, h0_ref, wx_ref, wy_ref, ap_ref, y_ref, hl_ref, *scratch, rows, scan):
    half = S5_BLOCK_STATE // 2
    cw = D_SSM // S5_LANE_BLOCKS
    blocks = range(S5_LANE_BLOCKS)
    ub = u_ref[...].astype(BF16)
    ucat = [jnp.concatenate([ub[:, s * D_SSM + q * cw: s * D_SSM + (q + 1) * cw]
                             for s in range(S5_STEPS)], axis=1) for q in blocks]
    x = [jnp.dot(ucat[q], wx_ref[q], preferred_element_type=F32) for q in blocks]
    ap = [ap_ref[:, q * S5_BLOCK_STATE:(q + 1) * S5_BLOCK_STATE] for q in blocks]

    def emit_outputs(q, hprev):
        lhs = jnp.concatenate([hprev, ucat[q]], axis=1)
        for j in range(S5_STEPS // 2):
            kk = S5_BLOCK_STATE + cw * (2 * j + 2)
            y2 = jnp.dot(lhs[:, :kk], wy_ref[q, j, :kk, :], preferred_element_type=F32)
            c0 = (2 * j) * D_SSM + q * cw
            c1 = (2 * j + 1) * D_SSM + q * cw
            y_ref[:, c0:c0 + cw] = y2[:, :cw]
            y_ref[:, c1:c1 + cw] = y2[:, cw:]

    if scan:
        hs_ref, hp_ref = scratch
        sub3 = lax.broadcasted_iota(jnp.int32, (1, 8, 1), 1)
        sub2 = lax.broadcasted_iota(jnp.int32, (8, 1), 0)
        for q in blocks:
            xr = x[q][:, :half].reshape(rows // 8, 8, half)
            xi = x[q][:, half:].reshape(rows // 8, 8, half)
            for k in (1, 2, 4):
                er = jnp.where(sub3 >= k, ap[q][k - 1:k, :half].reshape(1, 1, half), 0.0)
                ei = jnp.where(sub3 >= k, ap[q][k - 1:k, half:].reshape(1, 1, half), 0.0)
                sr = pltpu.roll(xr, k, 1)
                si = pltpu.roll(xi, k, 1)
                xr, xi = xr + er * sr - ei * si, xi + er * si + ei * sr
            hs_ref[q, :, :half] = xr.reshape(rows, half)
            hs_ref[q, :, half:] = xi.reshape(rows, half)
        h0 = h0_ref[0]
        for q in blocks:
            lo = q * S5_BLOCK_STATE
            cr, ci = h0[:, lo:lo + half], h0[:, lo + half:lo + S5_BLOCK_STATE]
            pwr, pwi = ap[q][:, :half], ap[q][:, half:]
            for j in range(rows // 8):
                sl = slice(j * 8, (j + 1) * 8)
                hr = hs_ref[q, sl, :half] + pwr * cr - pwi * ci
                hi = hs_ref[q, sl, half:] + pwr * ci + pwi * cr
                hp_ref[q, sl, :half] = jnp.where(sub2 == 0, cr, pltpu.roll(hr, 1, 0))
                hp_ref[q, sl, half:] = jnp.where(sub2 == 0, ci, pltpu.roll(hi, 1, 0))
                cr, ci = hr[7:8, :], hi[7:8, :]
            hl_ref[0, :, lo:lo + half] = cr
            hl_ref[0, :, lo + half:lo + S5_BLOCK_STATE] = ci
            emit_outputs(q, hp_ref[q].astype(BF16))
    else:
        for q in blocks:
            lo = q * S5_BLOCK_STATE
            h0r = h0_ref[:, lo:lo + half]
            h0i = h0_ref[:, lo + half:lo + S5_BLOCK_STATE]
            er, ei = ap[q][0:1, :half], ap[q][0:1, half:]
            hl_ref[:, lo:lo + half] = er * h0r - ei * h0i + x[q][:, :half]
            hl_ref[:, lo + half:lo + S5_BLOCK_STATE] = er * h0i + ei * h0r + x[q][:, half:]
            emit_outputs(q, jnp.concatenate([h0r, h0i], axis=1).astype(BF16))


def _s5(u8, h_re, h_im, tables, seq_len):
    wx, wy, ap = tables
    n = u8.shape[0] * S5_STEPS
    nseq = n // seq_len
    h0 = _state_to_lanes(h_re, h_im)
    cps = seq_len // S5_STEPS
    scan = cps > 1
    if scan:
        rows = cps
        grid = (nseq,)
        h0 = h0.reshape(nseq, 1, S5_STATE_ROW)
        h_spec = pl.BlockSpec((1, 1, S5_STATE_ROW), lambda i: (i, 0, 0))
        h_shape = jax.ShapeDtypeStruct((nseq, 1, S5_STATE_ROW), F32)
    else:
        rows = min(nseq, 128)
        grid = (nseq // rows,)
        h_spec = pl.BlockSpec((rows, S5_STATE_ROW), lambda i: (i, 0))
        h_shape = jax.ShapeDtypeStruct((nseq, S5_STATE_ROW), F32)
    y8, hl = pl.pallas_call(
        functools.partial(_s5_kernel, rows=rows, scan=scan),
        grid=grid,
        in_specs=[pl.BlockSpec((rows, S5_ROW), lambda i: (i, 0)),
                  h_spec,
                  _const_spec(wx.shape), _const_spec(wy.shape), _const_spec(ap.shape)],
        out_specs=[pl.BlockSpec((rows, S5_ROW), lambda i: (i, 0)), h_spec],
        out_shape=[jax.ShapeDtypeStruct((n // S5_STEPS, S5_ROW), F32), h_shape],
        scratch_shapes=[pltpu.VMEM((S5_LANE_BLOCKS, rows, S5_BLOCK_STATE), F32)] * 2 if scan else [],
        compiler_params=pltpu.CompilerParams(dimension_semantics=("arbitrary",),
                                             vmem_limit_bytes=V7X_VMEM_LIMIT_BYTES),
        name="s5",
    )(u8, h0, wx, wy, ap)
    hre_new, him_new = _lanes_to_state(hl.reshape(nseq, S5_STATE_ROW))
    return y8, hre_new, him_new


RW_CHUNK = 64
RW_STEP_CHUNKS_LONG = 8
RW_STEP_CHUNKS_SHORT = 4
RW_GROUPS = 2
RW_GROUP_HEADS = N_RWKV_HEADS // RW_GROUPS
RW_GROUP_LANES = D_RWKV // RW_GROUPS


def _split3(x):
    hi = x.astype(BF16)
    r1 = x - hi.astype(F32)
    mid = r1.astype(BF16)
    lo = (r1 - mid.astype(F32)).astype(BF16)
    return hi, mid, lo


def _bd(x, hm_ref):
    xb = x.astype(BF16)
    return jnp.concatenate([xb * hm_ref[h] for h in range(RW_GROUP_HEADS)], axis=0)


def _unbd(f, hm_ref):
    n = RWKV_HEAD
    out = f[:n] * hm_ref[0].astype(F32)
    for h in range(1, RW_GROUP_HEADS):
        out = out + f[h * n:(h + 1) * n] * hm_ref[h].astype(F32)
    return out


def _rwkv_kernel(z_ref, s0_ref, w0_ref, w2_ref, a0_ref, a2_ref, g2_ref, kk_ref, ka_ref, rk_ref,
                 lg_ref, lb_ref, ones_ref, hm_ref, cm_ref, tri_ref, y_ref, sl_ref, ypre_ref,
                 *scratch, seg_len, step_chunks, layer):
    row_of = lambda ref: ref[layer:layer + 1, :]
    C = RW_CHUNK
    N = RWKV_HEAD
    GL = RW_GROUP_LANES
    chained = seg_len == C
    if chained:
        (st_ref,) = scratch
        b = pl.program_id(1)

        @pl.when(b == 0)
        def _():
            for g in range(RW_GROUPS):
                st_ref[g] = jnp.concatenate(
                    [s0_ref[0, h] for h in range(g * RW_GROUP_HEADS, (g + 1) * RW_GROUP_HEADS)], axis=1)

    ones_bd = ones_ref[...]
    z = z_ref[...].astype(F32)
    r = z[:, :D_RWKV]
    k = z[:, D_RWKV:2 * D_RWKV]
    v = z[:, 2 * D_RWKV:3 * D_RWKV]
    o = 3 * D_RWKV
    xw = z[:, o:o + DECAY_LORA]
    xa = z[:, o + DECAY_LORA:o + DECAY_LORA + AAA_LORA]
    xg = z[:, o + DECAY_LORA + AAA_LORA:]
    wd = -(row_of(w0_ref) + _mm(jnp.tanh(xw), w2_ref[...]))
    w = -(jnp.maximum(wd, 0.0) + jnp.log1p(jnp.exp(-jnp.abs(wd)))) - 0.5
    lw = -jnp.exp(w)
    a = _sigmoid(row_of(a0_ref) + _mm(xa, a2_ref[...]))
    out_gate = _mm(_sigmoid(xg), g2_ref[...])
    kk = k * row_of(kk_ref)
    kk = kk / jnp.maximum(jnp.sqrt(_seg_sum(kk * kk, ones_bd)), L2_EPS)
    kmod = k * (1.0 + (a - 1.0) * row_of(ka_ref))

    tri = tri_ref[...]
    tb = tri.shape[0]
    terms = _split3(lw)
    cw = jnp.concatenate([sum(jnp.dot(tri, t[i:i + tb], preferred_element_type=F32) for t in terms)
                          for i in range(0, lw.shape[0], tb)], axis=0)
    tot = _expand_rows(_step_rows(cw, seg_len, seg_len - 1), seg_len)
    w_in = jnp.exp(cw)
    w_tail = jnp.exp(tot - cw)
    w_inv = jnp.exp(-cw)
    w_all = jnp.exp(tot)
    ah = -kk * jnp.exp(cw - lw)
    bh = kk * a * w_inv
    kh = kmod * w_inv
    rh = r * w_in
    bt = kk * a * w_tail
    kt = kmod * w_tail
    strict = cm_ref[0] > 0.0
    incl = cm_ref[1] > 0.0
    eye_cat = cm_ref[2]
    n_fac = seg_len.bit_length() - 1

    items = [(c, g) for c in range(step_chunks) for g in range(RW_GROUPS)]
    sub = lambda t, c, g: t[c * C:(c + 1) * C, g * GL:(g + 1) * GL]
    bd = lambda t: _bd(t, hm_ref)
    A = [sub(ah, c, g) for c, g in items]
    R = [sub(rh, c, g) for c, g in items]
    V = [sub(v, c, g) for c, g in items]
    G = [_mm_nt(jnp.concatenate([A[i], R[i]], axis=0),
                jnp.concatenate([bd(sub(bh, c, g)), bd(sub(kh, c, g))], axis=0))
         for i, (c, g) in enumerate(items)]
    AB = [jnp.where(strict, t[:C, :GL], 0.0) for t in G]
    AK = [jnp.where(strict, t[:C, GL:], 0.0) for t in G]
    RB = [jnp.where(incl, t[C:, :GL], 0.0) for t in G]
    RK = [jnp.where(incl, t[C:, GL:], 0.0) for t in G]
    KV = [_mm(jnp.concatenate([AK[i], RK[i]], axis=0), bd(V[i])) for i in range(len(items))]
    T = [eye_cat + t for t in AB]
    P = [_mm(t, bd(t)) for t in AB]
    for _ in range(n_fac - 2):
        PT = [_mm(jnp.concatenate([P[i], T[i]], axis=0), bd(P[i])) for i in range(len(items))]
        P = [t[:C] for t in PT]
        T = [T[i] + PT[i][C:] for i in range(len(items))]
    T = [T[i] + _mm(T[i], bd(P[i])) for i in range(len(items))]
    RBT = [_mm(RB[i], bd(T[i])) for i in range(len(items))]
    X = [_mm(jnp.concatenate([T[i], RBT[i]], axis=0),
             jnp.concatenate([bd(A[i]), bd(KV[i][:C])], axis=1)) for i in range(len(items))]
    TA = [t[:C, :GL] for t in X]
    U0 = [t[:C, GL:] for t in X]
    Rt = [R[i] + X[i][C:, :GL] for i in range(len(items))]
    Y0 = [X[i][C:, GL:] + KV[i][C:] for i in range(len(items))]
    nseg = C // seg_len
    heads = lambda g: range(g * RW_GROUP_HEADS, (g + 1) * RW_GROUP_HEADS)
    head_lanes = lambda h: slice((h % RW_GROUP_HEADS) * N, (h % RW_GROUP_HEADS + 1) * N)
    if chained:
        PhiT, PsiT = [], []
        for i, (c, g) in enumerate(items):
            Bt, Kt, Wa = sub(bt, c, g), sub(kt, c, g), sub(w_all, c, g)
            PhiT.append(bd(eye_cat * Wa[0:1, :] + _unbd(_mm_tn(TA[i], Bt), hm_ref)))
            PsiT.append(_unbd(_mm_tn(jnp.concatenate([U0[i], V[i]], axis=0),
                                     jnp.concatenate([Bt, Kt], axis=0)), hm_ref))
        states = [st_ref[g] for g in range(RW_GROUPS)]
        for i, (c, g) in enumerate(items):
            S = states[g]
            ypre_ref[c * C:(c + 1) * C, g * GL:(g + 1) * GL] = _mm_nt(Rt[i], bd(S)) + Y0[i]
            states[g] = _mm(S, PhiT[i]) + PsiT[i]
        for g in range(RW_GROUPS):
            st_ref[g] = states[g]
            for h in heads(g):
                sl_ref[0, h] = states[g][:, head_lanes(h)]
    else:
        segs = [(i, c, g, s) for i, (c, g) in enumerate(items) for s in range(nseg)]
        rows_of = lambda s: slice(s * seg_len, (s + 1) * seg_len)
        S0 = [jnp.concatenate([s0_ref[c * nseg + s, h] for h in heads(g)], axis=1)
              for i, c, g, s in segs]
        RU = [_mm_nt(jnp.concatenate([Rt[i][rows_of(s)], TA[i][rows_of(s)]], axis=0), bd(S0[n]))
              for n, (i, c, g, s) in enumerate(segs)]
        for n, (i, c, g, s) in enumerate(segs):
            ypre_ref[c * C + s * seg_len:c * C + (s + 1) * seg_len, g * GL:(g + 1) * GL] = (
                RU[n][:seg_len] + Y0[i][rows_of(s)])
        UB = [_mm_tn(jnp.concatenate([RU[n][seg_len:] + U0[i][rows_of(s)], V[i][rows_of(s)]], axis=0),
                     jnp.concatenate([sub(bt, c, g)[rows_of(s)], sub(kt, c, g)[rows_of(s)]], axis=0))
              for n, (i, c, g, s) in enumerate(segs)]
        for n, (i, c, g, s) in enumerate(segs):
            S = S0[n] * sub(w_all, c, g)[s * seg_len:s * seg_len + 1, :] + _unbd(UB[n], hm_ref)
            for h in heads(g):
                sl_ref[c * nseg + s, h] = S[:, head_lanes(h)]
    y = ypre_ref[...]

    mu = _seg_sum(y, ones_bd) * (1.0 / N)
    d = y - mu
    var = _seg_sum(d * d, ones_bd) * (1.0 / N)
    yn = d * lax.rsqrt(var + GN_EPS) * row_of(lg_ref) + row_of(lb_ref)
    bonus = _seg_sum(r * kmod * row_of(rk_ref), ones_bd) * v
    y_ref[...] = (yn + bonus) * out_gate


def _rwkv_constants(seg_len, rows):
    n, gh, gl = RWKV_HEAD, RW_GROUP_HEADS, RW_GROUP_LANES
    lane_head = np.arange(gl) // n
    hm = (lane_head[None, None, :] == np.arange(gh)[:, None, None]) * np.ones((gh, n, gl))
    i = np.arange(RW_CHUNK)[:, None]
    j = (np.arange(gl) % n)[None, :]
    same = (i // seg_len) == (j // seg_len)
    cm = np.stack([(j < i) & same, (j <= i) & same, j == i]).astype(np.float32)
    tb = min(rows, 2 * LANES)
    ri = np.arange(tb)[:, None]
    rj = np.arange(tb)[None, :]
    same_r = (ri // seg_len) == (rj // seg_len)
    tri = ((rj <= ri) & same_r).astype(np.float32)
    ones_bd = np.kron(np.eye(gh), np.ones((n, n)))
    return (jnp.asarray(ones_bd, BF16), jnp.asarray(hm, BF16), jnp.asarray(cm, F32),
            jnp.asarray(tri, BF16))


def _rwkv(z, s0, p, layer, seq_len):
    n = z.shape[0]
    if seq_len >= RW_CHUNK:
        step_chunks = min(RW_STEP_CHUNKS_LONG, seq_len // RW_CHUNK)
    else:
        step_chunks = min(RW_STEP_CHUNKS_SHORT, n // RW_CHUNK)
    nseq = n // seq_len
    nh, hd = N_RWKV_HEADS, RWKV_HEAD
    seg_len = min(seq_len, RW_CHUNK)
    rows = RW_CHUNK * step_chunks
    chained = seq_len >= RW_CHUNK
    if chained:
        steps = seq_len // rows
        grid = (nseq, steps)
        row_map = lambda s, b: (s * steps + b, 0)
        state_spec = pl.BlockSpec((1, nh, hd, hd), lambda s, b: (s, 0, 0, 0))
        scratch = [pltpu.VMEM((rows, D_RWKV), F32), pltpu.VMEM((RW_GROUPS, hd, RW_GROUP_LANES), F32)]
    else:
        grid = (n // rows, 1)
        row_map = lambda s, b: (s, 0)
        state_spec = pl.BlockSpec((rows // seq_len, nh, hd, hd), lambda s, b: (s, 0, 0, 0))
        scratch = [pltpu.VMEM((rows, D_RWKV), F32)]
    if s0 is None:
        s0 = p['state_rwkv']
        s0_spec = pl.BlockSpec((None,) + state_spec.block_shape,
                               lambda s, b: (layer, s, 0, 0, 0))
    else:
        s0_spec = state_spec
    stacked = [p[k] for k in ('w0', 'w2', 'a0', 'a2', 'g2', 'k_k', 'k_a', 'r_k', 'lnx_g', 'lnx_b')]
    consts = list(_rwkv_constants(seg_len, rows))
    y, s_new = pl.pallas_call(
        functools.partial(_rwkv_kernel, seg_len=seg_len, step_chunks=step_chunks, layer=layer),
        grid=grid,
        in_specs=[pl.BlockSpec((rows, N_SHIFT), row_map), s0_spec]
        + [_layer_spec(t, layer) for t in stacked] + [_const_spec(t.shape) for t in consts],
        out_specs=[pl.BlockSpec((rows, D_RWKV), row_map), state_spec],
        out_shape=[jax.ShapeDtypeStruct((n, D_RWKV), F32),
                   jax.ShapeDtypeStruct((nseq, nh, hd, hd), F32)],
        scratch_shapes=scratch,
        compiler_params=pltpu.CompilerParams(dimension_semantics=("arbitrary", "arbitrary"),
                                             vmem_limit_bytes=V7X_VMEM_LIMIT_BYTES),
        name="rwkv7",
    )(z, s0, *stacked, *consts)
    return y, s_new


def _ffn_kernel(h_ref, y8_ref, u8_ref, yb_ref, pe_ref, c1_ref, c2_ref, d8_ref, wglu_ref, bglu_ref,
                wout_ref, g2_ref, wup_ref, cw_ref, cb_ref, wdn_ref, wple_ref, wpg_ref, gf_ref,
                *rest, seq_len, tm, final, layer):
    row_of = lambda ref: ref[layer:layer + 1, :]
    i = pl.program_id(0)
    long_seq = seq_len >= tm
    if long_seq:
        o_ref, ga_ref, gb_ref, fold_ref, act_ref, carry_ref = rest

        @pl.when(i == 0)
        def _():
            carry_ref[...] = jnp.zeros_like(carry_ref)
    else:
        o_ref, ga_ref, gb_ref, fold_ref, act_ref = rest
    ya = y8_ref[...] + row_of(d8_ref) * u8_ref[...]
    c_gelu = math.sqrt(2.0 / math.pi)
    ya = ya * (0.5 * (1.0 + jnp.tanh(c_gelu * (ya + 0.044715 * (ya * ya * ya)))))
    ya = _unfold_rows(ya, fold_ref)
    ya = ya * _sigmoid(_mm(ya, wglu_ref[...]) + row_of(bglu_ref))
    h1 = (h_ref[...] + _mm(ya, wout_ref[:D_SSM, :]) + _mm(yb_ref[...], wout_ref[D_SSM:, :]))
    x2 = _rmsnorm(h1, row_of(g2_ref)).astype(BF16)
    row = lax.broadcasted_iota(jnp.int32, (tm, 1), 0)
    if long_seq:
        first = (i % (seq_len // tm)) == 0
    else:
        t = row % seq_len
    def up(c):
        cs = slice(c * FF_CHUNK, (c + 1) * FF_CHUNK)
        gs = slice(D_FF + c * FF_CHUNK, D_FF + (c + 1) * FF_CHUNK)
        return (jnp.dot(x2, wup_ref[:, cs], preferred_element_type=F32),
                jnp.dot(x2, wup_ref[:, gs], preferred_element_type=F32))

    n_chunks = D_FF // FF_CHUNK
    nxt = up(0)
    for c in range(n_chunks):
        cs = slice(c * FF_CHUNK, (c + 1) * FF_CHUNK)
        val, gate = nxt
        if c + 1 < n_chunks:
            nxt = up(c + 1)
        r1 = pltpu.roll(gate, 1, 0)
        r2 = pltpu.roll(gate, 2, 0)
        if long_seq:
            m1 = jnp.where(first, c1_ref[0][:, cs], carry_ref[1:2, cs])
            m2 = jnp.where(first, c2_ref[0][:, cs], carry_ref[0:1, cs])
            p1 = jnp.where(row == 0, m1, r1)
            p2 = jnp.where(row == 0, m2, jnp.where(row == 1, m1, r2))
            carry_ref[:, cs] = gate[tm - 2:tm, :]
            ga_ref[0, :, cs] = gate[tm - 2:tm - 1, :]
            gb_ref[0, :, cs] = gate[tm - 1:tm, :]
        else:
            m1 = _expand_rows(c1_ref[:, cs], seq_len)
            m2 = _expand_rows(c2_ref[:, cs], seq_len)
            p1 = jnp.where(t == 0, m1, r1)
            p2 = jnp.where(t == 0, m2, jnp.where(t == 1, m1, r2))
            ga_ref[:, cs] = _step_rows(gate, seq_len, seq_len - 2)
            gb_ref[:, cs] = _step_rows(gate, seq_len, seq_len - 1)
        conv = (cb_ref[layer:layer + 1, cs] + cw_ref[2:3, cs] * gate + cw_ref[1:2, cs] * p1 + cw_ref[0:1, cs] * p2)
        act_ref[:, cs] = (conv * _sigmoid(conv) * val).astype(BF16)
    h2 = h1 + jnp.dot(act_ref[...], wdn_ref[...], preferred_element_type=F32)
    h3 = h2 + _mm(pe_ref[...], wple_ref[...]) * _sigmoid(_mm(h2, wpg_ref[...]))
    if final:
        h3 = _rmsnorm(h3, gf_ref[...])
    o_ref[...] = h3


def _ffn(h, y8, u8, yb, pe, conv0, p, layer, seq_len, tm, final):
    n = h.shape[0]
    nseq = n // seq_len
    row_spec = lambda w: pl.BlockSpec((tm, w), lambda i: (i, 0))
    pe_spec = pl.BlockSpec((None, tm, D_PLE), lambda i: (layer, i, 0))
    fold_spec = pl.BlockSpec((tm // S5_STEPS, S5_ROW), lambda i: (i, 0))
    scratch = [pltpu.VMEM((FOLD_LANE_BLOCKS, tm, LANES), F32), pltpu.VMEM((tm, D_FF), BF16)]
    if seq_len >= tm:
        tps = seq_len // tm
        seq_spec = pl.BlockSpec((1, 1, D_FF), lambda i: (i // tps, 0, 0))
        seq_shape = (nseq, 1, D_FF)
        scratch.append(pltpu.VMEM((2, D_FF), F32))
    else:
        seq_spec = pl.BlockSpec((tm // seq_len, D_FF), lambda i: (i, 0))
        seq_shape = (nseq, D_FF)
    stacked = [p[k] for k in ('d8', 'w_glu', 'b_glu', 'w_out', 'norm2_g', 'w_up', 'conv_w',
                              'conv_b', 'w_down', 'w_ple', 'w_pg')]
    consts = [p['final_g']]
    weights = stacked + consts
    out, ga, gb = pl.pallas_call(
        functools.partial(_ffn_kernel, seq_len=seq_len, tm=tm, final=final, layer=layer),
        grid=(n // tm,),
        in_specs=[row_spec(D_MODEL), fold_spec, fold_spec, row_spec(D_RWKV), pe_spec,
                  seq_spec, seq_spec] + [_layer_spec(t, layer) for t in stacked]
        + [_const_spec(t.shape) for t in consts],
        out_specs=[row_spec(D_MODEL), seq_spec, seq_spec],
        out_shape=[jax.ShapeDtypeStruct((n, D_MODEL), F32)]
        + [jax.ShapeDtypeStruct(seq_shape, F32)] * 2,
        scratch_shapes=scratch,
        compiler_params=pltpu.CompilerParams(dimension_semantics=("arbitrary",),
                                             vmem_limit_bytes=V7X_VMEM_LIMIT_BYTES),
        name="ffn",
    )(h, y8, u8, yb, pe, conv0[:, 1].reshape(seq_shape), conv0[:, 0].reshape(seq_shape), *weights)
    conv_new = jnp.stack([ga.reshape(nseq, D_FF), gb.reshape(nseq, D_FF)], axis=1)
    return out, conv_new


def _layer(h, pe, st, p, layer, s5_tables, seq_len, tm, final):
    ssm_re0, ssm_im0, rwkv0, shift0, conv0 = st
    tm_in, tm_ffn = tm
    u8, z, shift_new = _inproj(h, p, layer, shift0, seq_len, tm_in)
    y8, hre, him = _s5(u8, ssm_re0, ssm_im0, s5_tables, seq_len)
    yb, s_last = _rwkv(z, rwkv0, p, layer, seq_len)
    h, conv_new = _ffn(h, y8, u8, yb, pe, conv0, p, layer, seq_len, tm_ffn, final)
    return h, (hre, him, s_last, shift_new, conv_new)


def _stacked_params(w, state_rwkv, state_shift):
    depth = w['w_in'].shape[0]
    row = lambda t: t.reshape(depth, -1).astype(F32)
    bf = lambda t: t.astype(BF16)
    return {
        'norm1_g': row(w['norm1_g']), 'w_in': bf(w['w_in']), 'shift_mu': row(w['shift_mu']),
        'd8': jnp.tile(row(w['ssm_d']), (1, S5_STEPS)), 'w_glu': bf(w['ssm_w_glu']),
        'b_glu': row(w['ssm_b_glu']),
        'w0': row(w['rwkv_w0']), 'w2': bf(w['rwkv_w2']), 'a0': row(w['rwkv_a0']),
        'a2': bf(w['rwkv_a2']), 'g2': bf(w['rwkv_g2']), 'k_k': row(w['rwkv_k_k']),
        'k_a': row(w['rwkv_k_a']), 'r_k': row(w['rwkv_r_k']), 'lnx_g': row(w['rwkv_lnx_g']),
        'lnx_b': row(w['rwkv_lnx_b']), 'w_out': bf(w['w_out']), 'norm2_g': row(w['norm2_g']),
        'w_up': bf(w['w_up']), 'conv_w': w['conv_w'].astype(F32), 'conv_b': row(w['conv_b']),
        'w_down': bf(w['w_down']), 'w_ple': bf(w['w_ple']), 'w_pg': bf(w['w_pg']),
        'final_g': w['final_g'].reshape(1, -1).astype(F32),
        'state_rwkv': state_rwkv.astype(F32), 'state_shift': state_shift.astype(F32),
    }


def _forward(x_prompt, x_sample, p_prompt, p_sample, state_ssm_re, state_ssm_im, state_rwkv,
             state_shift, state_conv, w, tm_prompt, tm_sample):
    depth = w['w_in'].shape[0]
    bp, lp, _ = x_prompt.shape
    bs, ls, _ = x_sample.shape
    hp = x_prompt.reshape(bp * lp, D_MODEL).astype(F32)
    hs = x_sample.reshape(bs * ls, D_MODEL).astype(F32)
    pe_p = p_prompt.reshape(depth, bp * lp, D_PLE).astype(F32)
    pe_s = p_sample.reshape(depth, bs * ls, D_PLE).astype(F32)
    zero_st = (jnp.zeros((bp, N_SSM_GROUPS, SSM_STATE), F32),
               jnp.zeros((bp, N_SSM_GROUPS, SSM_STATE), F32),
               jnp.zeros((bp, N_RWKV_HEADS, RWKV_HEAD, RWKV_HEAD), F32),
               jnp.zeros((bp, N_SHIFT), F32),
               jnp.zeros((bp, 2, D_FF), F32))
    p = _stacked_params(w, state_rwkv, state_shift)
    s5_inputs = _s5_table_inputs(w['ssm_lam_re'], w['ssm_lam_im'], w['ssm_log_dt'], w['ssm_b_re'],
                                 w['ssm_b_im'], w['ssm_c_re'], w['ssm_c_im'])
    new_p = [[] for _ in range(5)]
    new_s = [[] for _ in range(5)]
    for i in range(depth):
        tables = _s5_tables(s5_inputs, i)
        final = i == depth - 1
        hp, stp = _layer(hp, pe_p, zero_st, p, i, tables, lp, tm_prompt, final)
        st_in = (state_ssm_re[i].astype(F32), state_ssm_im[i].astype(F32), None, None,
                 state_conv[i].astype(F32))
        hs, sts = _layer(hs, pe_s, st_in, p, i, tables, ls, tm_sample, final)
        for j in range(5):
            new_p[j].append(stp[j])
            new_s[j].append(sts[j])
    y_prompt = hp.reshape(bp, lp, D_MODEL).astype(x_prompt.dtype)
    y_sample = hs.reshape(bs, ls, D_MODEL).astype(x_sample.dtype)
    dts = (state_ssm_re.dtype, state_ssm_im.dtype, state_rwkv.dtype, state_shift.dtype,
           state_conv.dtype)
    outs_p = tuple(jnp.stack(new_p[j]).astype(dts[j]) for j in range(5))
    outs_s = tuple(jnp.stack(new_s[j]).astype(dts[j]) for j in range(5))
    return (y_prompt, y_sample) + outs_p + outs_s


def kernel(x_prompt, x_sample, p_prompt, p_sample, state_ssm_re, state_ssm_im, state_rwkv, state_shift, state_conv, norm1_g, w_in, shift_mu, ssm_lam_re, ssm_lam_im, ssm_log_dt, ssm_b_re, ssm_b_im, ssm_c_re, ssm_c_im, ssm_d, ssm_w_glu, ssm_b_glu, rwkv_w0, rwkv_w2, rwkv_a0, rwkv_a2, rwkv_g2, rwkv_k_k, rwkv_k_a, rwkv_r_k, rwkv_lnx_g, rwkv_lnx_b, w_out, norm2_g, w_up, conv_w, conv_b, w_down, w_ple, w_pg, final_g):
    w = dict(norm1_g=norm1_g, w_in=w_in, shift_mu=shift_mu, ssm_lam_re=ssm_lam_re,
             ssm_lam_im=ssm_lam_im, ssm_log_dt=ssm_log_dt, ssm_b_re=ssm_b_re, ssm_b_im=ssm_b_im,
             ssm_c_re=ssm_c_re, ssm_c_im=ssm_c_im, ssm_d=ssm_d, ssm_w_glu=ssm_w_glu,
             ssm_b_glu=ssm_b_glu, rwkv_w0=rwkv_w0, rwkv_w2=rwkv_w2, rwkv_a0=rwkv_a0,
             rwkv_a2=rwkv_a2, rwkv_g2=rwkv_g2, rwkv_k_k=rwkv_k_k, rwkv_k_a=rwkv_k_a,
             rwkv_r_k=rwkv_r_k, rwkv_lnx_g=rwkv_lnx_g, rwkv_lnx_b=rwkv_lnx_b, w_out=w_out,
             norm2_g=norm2_g, w_up=w_up, conv_w=conv_w, conv_b=conv_b, w_down=w_down,
             w_ple=w_ple, w_pg=w_pg, final_g=final_g)
    lp = x_prompt.shape[1]
    ns = x_sample.shape[0] * x_sample.shape[1]
    return _forward(x_prompt, x_sample, p_prompt, p_sample, state_ssm_re, state_ssm_im,
                    state_rwkv, state_shift, state_conv, w,
                    tm_prompt=(min(1024, lp), min(512, lp)),
                    tm_sample=(min(512, ns), min(512, ns)))
```

```python
import functools
import math

import numpy as np
import jax
import jax.numpy as jnp
from jax import lax
from jax.experimental import pallas as pl
from jax.experimental.pallas import tpu as pltpu

F32 = jnp.float32
BF16 = jnp.bfloat16

D_MODEL = 1024
D_SSM = 512
D_RWKV = 512
SSM_GROUP = 16
N_SSM_GROUPS = 32
SSM_STATE = 64
RWKV_HEAD = 64
N_RWKV_HEADS = 8
DECAY_LORA = 64
AAA_LORA = 64
GATE_LORA = 128
N_SHIFT = 3 * D_RWKV + DECAY_LORA + AAA_LORA + GATE_LORA
N_IN = D_SSM + N_SHIFT
D_FF = 2816
D_PLE = 256
RMS_EPS = 1e-6
GN_EPS = 64e-5
L2_EPS = 1e-12

LANES = 128
FOLD_LANE_BLOCKS = D_SSM // LANES
S5_STEPS = 8
S5_ROW = S5_STEPS * D_SSM
S5_LANE_BLOCKS = D_SSM // LANES
S5_BLOCK_GROUPS = N_SSM_GROUPS // S5_LANE_BLOCKS
S5_BLOCK_STATE = 2 * S5_BLOCK_GROUPS * SSM_STATE
S5_STATE_ROW = S5_LANE_BLOCKS * S5_BLOCK_STATE
FF_CHUNK = 2 * LANES
V7X_VMEM_BYTES = 64 * 1024 * 1024
V7X_VMEM_LIMIT_BYTES = V7X_VMEM_BYTES * 7 // 8


def _mm(a, b):
    return jnp.dot(a.astype(BF16), b.astype(BF16), preferred_element_type=F32)


def _mm_nt(a, b):
    return lax.dot_general(a.astype(BF16), b.astype(BF16), (((1,), (1,)), ((), ())),
                           preferred_element_type=F32)


def _mm_tn(a, b):
    return lax.dot_general(a.astype(BF16), b.astype(BF16), (((0,), (0,)), ((), ())),
                           preferred_element_type=F32)


def _split2(x):
    hi = x.astype(BF16)
    return hi, (x - hi.astype(F32)).astype(BF16)


def _seg_sum(x, ones_bd):
    xb = x.astype(BF16)
    gl = ones_bd.shape[0]
    return jnp.concatenate([jnp.dot(xb[:, g:g + gl], ones_bd, preferred_element_type=F32)
                            for g in range(0, x.shape[1], gl)], axis=1)


def _rmsnorm(x, g):
    return x * lax.rsqrt(jnp.mean(x * x, axis=-1, keepdims=True) + RMS_EPS) * g


def _sigmoid(x):
    return 1.0 / (1.0 + jnp.exp(-x))


def _const_spec(shape):
    nd = len(shape)
    return pl.BlockSpec(shape, lambda *_: (0,) * nd, pipeline_mode=pl.Buffered(1))


def _layer_spec(t, layer):
    if t.ndim == 2:
        return _const_spec(t.shape)
    nd = t.ndim - 1
    return pl.BlockSpec((None,) + t.shape[1:], lambda *_: (layer,) + (0,) * nd,
                        pipeline_mode=pl.Buffered(1))


def _expand_rows(x, seq_len):
    nseq, w = x.shape
    return jnp.broadcast_to(x[:, None, :], (nseq, seq_len, w)).reshape(nseq * seq_len, w)


def _step_rows(x, seq_len, t):
    return x.reshape(x.shape[0] // seq_len, seq_len, x.shape[1])[:, t, :]


def _fold_rows(x, scr_ref, out_ref):
    tm = x.shape[0]
    for q in range(FOLD_LANE_BLOCKS):
        scr_ref[q] = x[:, q * LANES:(q + 1) * LANES]
    for s in range(S5_STEPS):
        for q in range(FOLD_LANE_BLOCKS):
            c0 = s * D_SSM + q * LANES
            out_ref[:, c0:c0 + LANES] = scr_ref[q, pl.ds(s, tm // S5_STEPS, stride=S5_STEPS), :]


def _unfold_rows(x8, scr_ref):
    tm = x8.shape[0] * S5_STEPS
    for s in range(S5_STEPS):
        for q in range(FOLD_LANE_BLOCKS):
            c0 = s * D_SSM + q * LANES
            scr_ref[q, pl.ds(s, tm // S5_STEPS, stride=S5_STEPS), :] = x8[:, c0:c0 + LANES]
    return jnp.concatenate([scr_ref[q] for q in range(FOLD_LANE_BLOCKS)], axis=1)


def _inproj_kernel(h_ref, g_ref, w_ref, mu_ref, init_ref, *rest, seq_len, tm, layer):
    i = pl.program_id(0)
    row_of = lambda ref: ref[layer:layer + 1, :]
    long_seq = seq_len >= tm
    if long_seq:
        u8_ref, z_ref, last_ref, fold_ref, carry_ref = rest

        @pl.when(i == 0)
        def _():
            carry_ref[...] = jnp.zeros_like(carry_ref)
    else:
        u8_ref, z_ref, last_ref, fold_ref = rest
    xn = _rmsnorm(h_ref[...], row_of(g_ref))
    proj = jnp.dot(xn.astype(BF16), w_ref[...], preferred_element_type=F32)
    _fold_rows(proj[:, :D_SSM], fold_ref, u8_ref)
    zr = proj[:, D_SSM:]
    rolled = pltpu.roll(zr, 1, 0)
    row = lax.broadcasted_iota(jnp.int32, (tm, 1), 0)
    if long_seq:
        first = (i % (seq_len // tm)) == 0
        row0 = jnp.where(first, init_ref[0], carry_ref[...])
        prev = jnp.where(row == 0, row0, rolled)
        carry_ref[...] = zr[tm - 1:tm, :]
        last_ref[0] = zr[tm - 1:tm, :]
    else:
        prev = jnp.where(row % seq_len == 0, _expand_rows(init_ref[...], seq_len), rolled)
        last_ref[...] = _step_rows(zr, seq_len, seq_len - 1)
    z_ref[...] = (zr + (prev - zr) * row_of(mu_ref)).astype(z_ref.dtype)


def _inproj(h, p, layer, shift0, seq_len, tm):
    n = h.shape[0]
    nseq = n // seq_len
    fold_scratch = pltpu.VMEM((FOLD_LANE_BLOCKS, tm, LANES), F32)
    if seq_len >= tm:
        tps = seq_len // tm
        seq_spec = pl.BlockSpec((1, 1, N_SHIFT), lambda i: (i // tps, 0, 0))
        init_spec = seq_spec
        init = shift0.reshape(nseq, 1, N_SHIFT)
        last_shape = jax.ShapeDtypeStruct((nseq, 1, N_SHIFT), F32)
        scratch = [fold_scratch, pltpu.VMEM((1, N_SHIFT), F32)]
    else:
        spt = tm // seq_len
        seq_spec = pl.BlockSpec((spt, N_SHIFT), lambda i: (i, 0))
        init_spec = pl.BlockSpec((None, spt, N_SHIFT), lambda i: (layer, i, 0))
        init = p['state_shift']
        last_shape = jax.ShapeDtypeStruct((nseq, N_SHIFT), F32)
        scratch = [fold_scratch]
    g1, w_in, mu = p['norm1_g'], p['w_in'], p['shift_mu']
    u8, z, last = pl.pallas_call(
        functools.partial(_inproj_kernel, seq_len=seq_len, tm=tm, layer=layer),
        grid=(n // tm,),
        in_specs=[pl.BlockSpec((tm, D_MODEL), lambda i: (i, 0)),
                  _layer_spec(g1, layer), _layer_spec(w_in, layer), _layer_spec(mu, layer),
                  init_spec],
        out_specs=[pl.BlockSpec((tm // S5_STEPS, S5_ROW), lambda i: (i, 0)),
                   pl.BlockSpec((tm, N_SHIFT), lambda i: (i, 0)),
                   seq_spec],
        out_shape=[jax.ShapeDtypeStruct((n // S5_STEPS, S5_ROW), F32),
                   jax.ShapeDtypeStruct((n, N_SHIFT), BF16),
                   last_shape],
        scratch_shapes=scratch,
        compiler_params=pltpu.CompilerParams(dimension_semantics=("arbitrary",),
                                             vmem_limit_bytes=V7X_VMEM_LIMIT_BYTES),
        name="inproj",
    )(h, g1, w_in, mu, init)
    return u8, z, last.reshape(nseq, N_SHIFT)


def _s5_prep_kernel(lr_ref, li_ref, ldt_ref, br_ref, bi_ref, cr_ref, ci_ref, tile_ref, sel_ref,
                    wx_ref, wy_ref, apr_ref, api_ref):
    lr, li = lr_ref[...], li_ref[...]
    dt = jnp.exp(ldt_ref[...])
    mag = jnp.exp(lr * dt)
    ar = mag * jnp.cos(li * dt)
    ai = mag * jnp.sin(li * dt)
    den = lr * lr + li * li
    nr = ar - 1.0
    fr = (nr * lr + ai * li) / den
    fi = (ai * lr - nr * li) / den
    br, bi = br_ref[...], bi_ref[...]
    bbr = fr * br - fi * bi
    bbi = fr * bi + fi * br
    cr, ci = cr_ref[...], ci_ref[...]
    pr = [jnp.ones_like(ar)]
    pi = [jnp.zeros_like(ar)]
    for _ in range(S5_STEPS):
        pr.append(pr[-1] * ar - pi[-1] * ai)
        pi.append(pr[-2] * ai + pi[-1] * ar)
    nrow = lr.shape[0]
    blk = nrow // S5_LANE_BLOCKS
    half = S5_BLOCK_STATE // 2
    ri = lax.broadcasted_iota(jnp.int32, (nrow, nrow), 0)
    ci_ = lax.broadcasted_iota(jnp.int32, (nrow, nrow), 1)
    same_group = (ri // SSM_GROUP) == (ci_ // SSM_GROUP)
    keep_in = ((ri % blk) // SSM_GROUP) == (ci_ // SSM_STATE)
    keep_out = (ri // SSM_STATE) == ((ci_ % blk) // SSM_GROUP)
    tile, sel = tile_ref[...], sel_ref[...]
    nt = (((1,), (1,)), ((), ()))
    c_hi, c_lo = _split2(jnp.concatenate([cr, -ci], axis=1))
    wy_ref[...] = jnp.zeros(wy_ref.shape, wy_ref.dtype)
    for s in range(S5_STEPS):
        qr, qi = pr[S5_STEPS - 1 - s], pi[S5_STEPS - 1 - s]
        wr = qr * bbr - qi * bbi
        wi = qr * bbi + qi * bbr
        for h, w_ in enumerate((wr, wi)):
            t = jnp.dot(w_.astype(BF16), tile, preferred_element_type=F32)
            t = jnp.where(keep_in, t, 0.0).astype(BF16)
            for q in range(S5_LANE_BLOCKS):
                wx_ref[q, s * blk:(s + 1) * blk, h * half:(h + 1) * half] = t[q * blk:(q + 1) * blk, :]
        tau = S5_STEPS - 1 - s
        lhs_hi, lhs_lo = _split2(jnp.concatenate([wr, wi], axis=1))
        kt = (lax.dot_general(lhs_hi, c_hi, nt, preferred_element_type=F32)
              + lax.dot_general(lhs_hi, c_lo, nt, preferred_element_type=F32)
              + lax.dot_general(lhs_lo, c_hi, nt, preferred_element_type=F32))
        kt = jnp.where(same_group, kt, 0.0).astype(BF16)
        for q in range(S5_LANE_BLOCKS):
            kq = kt[q * blk:(q + 1) * blk, q * blk:(q + 1) * blk]
            for s_in in range(S5_STEPS - tau):
                s_out = s_in + tau
                r0 = S5_BLOCK_STATE + s_in * blk
                wy_ref[q, s_out // 2, r0:r0 + blk, (s_out % 2) * blk:(s_out % 2 + 1) * blk] = kq
        mr = cr * pr[s + 1] - ci * pi[s + 1]
        mi = cr * pi[s + 1] + ci * pr[s + 1]
        for h, m_ in enumerate((mr, -mi)):
            t = lax.dot_general(sel, m_.astype(BF16), nt, preferred_element_type=F32)
            t = jnp.where(keep_out, t, 0.0).astype(BF16)
            for q in range(S5_LANE_BLOCKS):
                wy_ref[q, s // 2, h * half:(h + 1) * half, (s % 2) * blk:(s % 2 + 1) * blk] = (
                    t[:, q * blk:(q + 1) * blk])
    a8r, a8i = pr[S5_STEPS], pi[S5_STEPS]
    er, ei = a8r, a8i
    for n in range(S5_STEPS):
        apr_ref[n] = er
        api_ref[n] = ei
        er, ei = er * a8r - ei * a8i, er * a8i + ei * a8r


def _s5_table_inputs(lam_re, lam_im, log_dt, b_re, b_im, c_re, c_im):
    depth = lam_re.shape[0]
    G, P, K = N_SSM_GROUPS, SSM_STATE, SSM_GROUP
    rep = lambda t: jnp.repeat(t, K, axis=1)
    rows = lambda t: t.reshape(depth, G * K, P)
    return (rep(lam_re), rep(lam_im), rep(jnp.broadcast_to(log_dt[:, :, None], (depth, G, P))),
            rows(jnp.swapaxes(b_re, 2, 3)), rows(jnp.swapaxes(b_im, 2, 3)), rows(c_re), rows(c_im))


def _s5_tables(stacked_inputs, layer):
    G, P, K = N_SSM_GROUPS, SSM_STATE, SSM_GROUP
    Q, GB = S5_LANE_BLOCKS, G // S5_LANE_BLOCKS
    tile = np.tile(np.eye(P, dtype=np.float32), (1, GB))
    consts = (jnp.asarray(tile, BF16), jnp.asarray(tile.T, BF16))
    t3 = jax.ShapeDtypeStruct((S5_STEPS, G * K, P), F32)
    out_shape = [jax.ShapeDtypeStruct((Q, S5_STEPS * GB * K, S5_BLOCK_STATE), BF16),
                 jax.ShapeDtypeStruct((Q, S5_STEPS // 2, S5_BLOCK_STATE + S5_STEPS * GB * K,
                                       2 * GB * K), BF16),
                 t3, t3]
    whole = lambda sds: pl.BlockSpec(sds.shape, lambda i: (0,) * len(sds.shape),
                                     pipeline_mode=pl.Buffered(1))
    wx, wy, apr, api = pl.pallas_call(
        _s5_prep_kernel,
        grid=(1,),
        in_specs=[_layer_spec(t, layer) for t in stacked_inputs]
        + [_const_spec(t.shape) for t in consts],
        out_specs=[whole(sds) for sds in out_shape],
        out_shape=out_shape,
        compiler_params=pltpu.CompilerParams(dimension_semantics=("arbitrary",),
                                             vmem_limit_bytes=V7X_VMEM_LIMIT_BYTES),
        name="s5_prep",
    )(*stacked_inputs, *consts)
    ap = jnp.concatenate([t[:, ::K, :].reshape(S5_STEPS, Q, GB * P) for t in (apr, api)], axis=-1)
    return wx, wy, ap.reshape(S5_STEPS, S5_STATE_ROW)


def _state_to_lanes(h_re, h_im):
    n = h_re.shape[0]
    parts = [t.reshape(n, S5_LANE_BLOCKS, -1) for t in (h_re, h_im)]
    return jnp.concatenate(parts, axis=-1).reshape(n, S5_STATE_ROW)


def _lanes_to_state(h):
    n = h.shape[0]
    h = h.reshape(n, S5_LANE_BLOCKS, 2, N_SSM_GROUPS // S5_LANE_BLOCKS, SSM_STATE)
    return (h[:, :, 0].reshape(n, N_SSM_GROUPS, SSM_STATE), h[:, :, 1].reshape(n, N_SSM_GROUPS, SSM_STATE))


def _s5_kernel(u_ref, h0_ref, wx_ref, wy_ref, ap_ref, y_ref, hl_ref, *scratch, rows, scan):
    half = S5_BLOCK_STATE // 2
    cw = D_SSM // S5_LANE_BLOCKS
    blocks = range(S5_LANE_BLOCKS)
    ub = u_ref[...].astype(BF16)
    ucat = [jnp.concatenate([ub[:, s * D_SSM + q * cw: s * D_SSM + (q + 1) * cw]
                             for s in range(S5_STEPS)], axis=1) for q in blocks]
    x = [jnp.dot(ucat[q], wx_ref[q], preferred_element_type=F32) for q in blocks]
    ap = [ap_ref[:, q * S5_BLOCK_STATE:(q + 1) * S5_BLOCK_STATE] for q in blocks]

    def emit_outputs(q, hprev):
        lhs = jnp.concatenate([hprev, ucat[q]], axis=1)
        for j in range(S5_STEPS // 2):
            kk = S5_BLOCK_STATE + cw * (2 * j + 2)
            y2 = jnp.dot(lhs[:, :kk], wy_ref[q, j, :kk, :], preferred_element_type=F32)
            c0 = (2 * j) * D_SSM + q * cw
            c1 = (2 * j + 1) * D_SSM + q * cw
            y_ref[:, c0:c0 + cw] = y2[:, :cw]
            y_ref[:, c1:c1 + cw] = y2[:, cw:]

    if scan:
        hs_ref, hp_ref = scratch
        sub3 = lax.broadcasted_iota(jnp.int32, (1, 8, 1), 1)
        sub2 = lax.broadcasted_iota(jnp.int32, (8, 1), 0)
        for q in blocks:
            xr = x[q][:, :half].reshape(rows // 8, 8, half)
            xi = x[q][:, half:].reshape(rows // 8, 8, half)
            for k in (1, 2, 4):
                er = jnp.where(sub3 >= k, ap[q][k - 1:k, :half].reshape(1, 1, half), 0.0)
                ei = jnp.where(sub3 >= k, ap[q][k - 1:k, half:].reshape(1, 1, half), 0.0)
                sr = pltpu.roll(xr, k, 1)
                si = pltpu.roll(xi, k, 1)
                xr, xi = xr + er * sr - ei * si, xi + er * si + ei * sr
            hs_ref[q, :, :half] = xr.reshape(rows, half)
            hs_ref[q, :, half:] = xi.reshape(rows, half)
        h0 = h0_ref[0]
        for q in blocks:
            lo = q * S5_BLOCK_STATE
            cr, ci = h0[:, lo:lo + half], h0[:, lo + half:lo + S5_BLOCK_STATE]
            pwr, pwi = ap[q][:, :half], ap[q][:, half:]
            for j in range(rows // 8):
                sl = slice(j * 8, (j + 1) * 8)
                hr = hs_ref[q, sl, :half] + pwr * cr - pwi * ci
                hi = hs_ref[q, sl, half:] + pwr * ci + pwi * cr
                hp_ref[q, sl, :half] = jnp.where(sub2 == 0, cr, pltpu.roll(hr, 1, 0))
                hp_ref[q, sl, half:] = jnp.where(sub2 == 0, ci, pltpu.roll(hi, 1, 0))
                cr, ci = hr[7:8, :], hi[7:8, :]
            hl_ref[0, :, lo:lo + half] = cr
            hl_ref[0, :, lo + half:lo + S5_BLOCK_STATE] = ci
            emit_outputs(q, hp_ref[q].astype(BF16))
    else:
        for q in blocks:
            lo = q * S5_BLOCK_STATE
            h0r = h0_ref[:, lo:lo + half]
            h0i = h0_ref[:, lo + half:lo + S5_BLOCK_STATE]
            er, ei = ap[q][0:1, :half], ap[q][0:1, half:]
            hl_ref[:, lo:lo + half] = er * h0r - ei * h0i + x[q][:, :half]
            hl_ref[:, lo + half:lo + S5_BLOCK_STATE] = er * h0i + ei * h0r + x[q][:, half:]
            emit_outputs(q, jnp.concatenate([h0r, h0i], axis=1).astype(BF16))


def _s5(u8, h_re, h_im, tables, seq_len):
    wx, wy, ap = tables
    n = u8.shape[0] * S5_STEPS
    nseq = n // seq_len
    h0 = _state_to_lanes(h_re, h_im)
    cps = seq_len // S5_STEPS
    scan = cps > 1
    if scan:
        rows = cps
        grid = (nseq,)
        h0 = h0.reshape(nseq, 1, S5_STATE_ROW)
        h_spec = pl.BlockSpec((1, 1, S5_STATE_ROW), lambda i: (i, 0, 0))
        h_shape = jax.ShapeDtypeStruct((nseq, 1, S5_STATE_ROW), F32)
    else:
        rows = min(nseq, 128)
        grid = (nseq // rows,)
        h_spec = pl.BlockSpec((rows, S5_STATE_ROW), lambda i: (i, 0))
        h_shape = jax.ShapeDtypeStruct((nseq, S5_STATE_ROW), F32)
    y8, hl = pl.pallas_call(
        functools.partial(_s5_kernel, rows=rows, scan=scan),
        grid=grid,
        in_specs=[pl.BlockSpec((rows, S5_ROW), lambda i: (i, 0)),
                  h_spec,
                  _const_spec(wx.shape), _const_spec(wy.shape), _const_spec(ap.shape)],
        out_specs=[pl.BlockSpec((rows, S5_ROW), lambda i: (i, 0)), h_spec],
        out_shape=[jax.ShapeDtypeStruct((n // S5_STEPS, S5_ROW), F32), h_shape],
        scratch_shapes=[pltpu.VMEM((S5_LANE_BLOCKS, rows, S5_BLOCK_STATE), F32)] * 2 if scan else [],
        compiler_params=pltpu.CompilerParams(dimension_semantics=("arbitrary",),
                                             vmem_limit_bytes=V7X_VMEM_LIMIT_BYTES),
        name="s5",
    )(u8, h0, wx, wy, ap)
    hre_new, him_new = _lanes_to_state(hl.reshape(nseq, S5_STATE_ROW))
    return y8, hre_new, him_new


RW_CHUNK = 64
RW_STEP_CHUNKS_LONG = 8
RW_STEP_CHUNKS_SHORT = 4
RW_GROUPS = 2
RW_GROUP_HEADS = N_RWKV_HEADS // RW_GROUPS
RW_GROUP_LANES = D_RWKV // RW_GROUPS


def _split3(x):
    hi = x.astype(BF16)
    r1 = x - hi.astype(F32)
    mid = r1.astype(BF16)
    lo = (r1 - mid.astype(F32)).astype(BF16)
    return hi, mid, lo


def _bd(x, hm_ref):
    xb = x.astype(BF16)
    return jnp.concatenate([xb * hm_ref[h] for h in range(RW_GROUP_HEADS)], axis=0)


def _unbd(f, hm_ref):
    n = RWKV_HEAD
    out = f[:n] * hm_ref[0].astype(F32)
    for h in range(1, RW_GROUP_HEADS):
        out = out + f[h * n:(h + 1) * n] * hm_ref[h].astype(F32)
    return out


def _rwkv_kernel(z_ref, s0_ref, w0_ref, w2_ref, a0_ref, a2_ref, g2_ref, kk_ref, ka_ref, rk_ref,
                 lg_ref, lb_ref, ones_ref, hm_ref, cm_ref, tri_ref, y_ref, sl_ref, ypre_ref,
                 *scratch, seg_len, step_chunks, layer):
    row_of = lambda ref: ref[layer:layer + 1, :]
    C = RW_CHUNK
    N = RWKV_HEAD
    GL = RW_GROUP_LANES
    chained = seg_len == C
    if chained:
        (st_ref,) = scratch
        b = pl.program_id(1)

        @pl.when(b == 0)
        def _():
            for g in range(RW_GROUPS):
                st_ref[g] = jnp.concatenate(
                    [s0_ref[0, h] for h in range(g * RW_GROUP_HEADS, (g + 1) * RW_GROUP_HEADS)], axis=1)

    ones_bd = ones_ref[...]
    z = z_ref[...].astype(F32)
    r = z[:, :D_RWKV]
    k = z[:, D_RWKV:2 * D_RWKV]
    v = z[:, 2 * D_RWKV:3 * D_RWKV]
    o = 3 * D_RWKV
    xw = z[:, o:o + DECAY_LORA]
    xa = z[:, o + DECAY_LORA:o + DECAY_LORA + AAA_LORA]
    xg = z[:, o + DECAY_LORA + AAA_LORA:]
    wd = -(row_of(w0_ref) + _mm(jnp.tanh(xw), w2_ref[...]))
    w = -(jnp.maximum(wd, 0.0) + jnp.log1p(jnp.exp(-jnp.abs(wd)))) - 0.5
    lw = -jnp.exp(w)
    a = _sigmoid(row_of(a0_ref) + _mm(xa, a2_ref[...]))
    out_gate = _mm(_sigmoid(xg), g2_ref[...])
    kk = k * row_of(kk_ref)
    kk = kk * lax.rsqrt(jnp.maximum(_seg_sum(kk * kk, ones_bd), L2_EPS * L2_EPS))
    kmod = k * (1.0 + (a - 1.0) * row_of(ka_ref))

    tri = tri_ref[...]
    tb = tri.shape[0]
    terms = _split3(lw)
    cw = jnp.concatenate([sum(jnp.dot(tri, t[i:i + tb], preferred_element_type=F32) for t in terms)
                          for i in range(0, lw.shape[0], tb)], axis=0)
    w_all = _expand_rows(jnp.exp(_step_rows(cw, seg_len, seg_len - 1)), seg_len)
    w_in = jnp.exp(cw)
    w_inv = jnp.exp(-cw)
    w_tail = w_all * w_inv
    ah = -kk * jnp.exp(cw - lw)
    bh = kk * a * w_inv
    kh = kmod * w_inv
    rh = r * w_in
    bt = kk * a * w_tail
    kt = kmod * w_tail
    strict = cm_ref[0] > 0.0
    incl = cm_ref[1] > 0.0
    eye_cat = cm_ref[2]
    n_fac = seg_len.bit_length() - 1

    items = [(c, g) for c in range(step_chunks) for g in range(RW_GROUPS)]
    sub = lambda t, c, g: t[c * C:(c + 1) * C, g * GL:(g + 1) * GL]
    bd = lambda t: _bd(t, hm_ref)
    A = [sub(ah, c, g) for c, g in items]
    R = [sub(rh, c, g) for c, g in items]
    V = [sub(v, c, g) for c, g in items]
    G = [_mm_nt(jnp.concatenate([A[i], R[i]], axis=0),
                jnp.concatenate([bd(sub(bh, c, g)), bd(sub(kh, c, g))], axis=0))
         for i, (c, g) in enumerate(items)]
    AB = [jnp.where(strict, t[:C, :GL], 0.0) for t in G]
    AK = [jnp.where(strict, t[:C, GL:], 0.0) for t in G]
    RB = [jnp.where(incl, t[C:, :GL], 0.0) for t in G]
    RK = [jnp.where(incl, t[C:, GL:], 0.0) for t in G]
    KV = [_mm(jnp.concatenate([AK[i], RK[i]], axis=0), bd(V[i])) for i in range(len(items))]
    T = [eye_cat + t for t in AB]
    P = [_mm(t, bd(t)) for t in AB]
    for _ in range(n_fac - 2):
        PT = [_mm(jnp.concatenate([P[i], T[i]], axis=0), bd(P[i])) for i in range(len(items))]
        P = [t[:C] for t in PT]
        T = [T[i] + PT[i][C:] for i in range(len(items))]
    T = [T[i] + _mm(T[i], bd(P[i])) for i in range(len(items))]
    RBT = [_mm(RB[i], bd(T[i])) for i in range(len(items))]
    X = [_mm(jnp.concatenate([T[i], RBT[i]], axis=0),
             jnp.concatenate([bd(A[i]), bd(KV[i][:C])], axis=1)) for i in range(len(items))]
    TA = [t[:C, :GL] for t in X]
    U0 = [t[:C, GL:] for t in X]
    Rt = [R[i] + X[i][C:, :GL] for i in range(len(items))]
    Y0 = [X[i][C:, GL:] + KV[i][C:] for i in range(len(items))]
    nseg = C // seg_len
    heads = lambda g: range(g * RW_GROUP_HEADS, (g + 1) * RW_GROUP_HEADS)
    head_lanes = lambda h: slice((h % RW_GROUP_HEADS) * N, (h % RW_GROUP_HEADS + 1) * N)
    if chained:
        PhiT, PsiT = [], []
        for i, (c, g) in enumerate(items):
            Bt, Kt, Wa = sub(bt, c, g), sub(kt, c, g), sub(w_all, c, g)
            PhiT.append(bd(eye_cat * Wa[0:1, :] + _unbd(_mm_tn(TA[i], Bt), hm_ref)))
            PsiT.append(_unbd(_mm_tn(jnp.concatenate([U0[i], V[i]], axis=0),
                                     jnp.concatenate([Bt, Kt], axis=0)), hm_ref))
        states = [st_ref[g] for g in range(RW_GROUPS)]
        for i, (c, g) in enumerate(items):
            S = states[g]
            ypre_ref[c * C:(c + 1) * C, g * GL:(g + 1) * GL] = _mm_nt(Rt[i], bd(S)) + Y0[i]
            states[g] = _mm(S, PhiT[i]) + PsiT[i]
        for g in range(RW_GROUPS):
            st_ref[g] = states[g]
            for h in heads(g):
                sl_ref[0, h] = states[g][:, head_lanes(h)]
    else:
        segs = [(i, c, g, s) for i, (c, g) in enumerate(items) for s in range(nseg)]
        rows_of = lambda s: slice(s * seg_len, (s + 1) * seg_len)
        S0 = [jnp.concatenate([s0_ref[c * nseg + s, h] for h in heads(g)], axis=1)
              for i, c, g, s in segs]
        RU = [_mm_nt(jnp.concatenate([Rt[i][rows_of(s)], TA[i][rows_of(s)]], axis=0), bd(S0[n]))
              for n, (i, c, g, s) in enumerate(segs)]
        for n, (i, c, g, s) in enumerate(segs):
            ypre_ref[c * C + s * seg_len:c * C + (s + 1) * seg_len, g * GL:(g + 1) * GL] = (
                RU[n][:seg_len] + Y0[i][rows_of(s)])
        UB = [_mm_tn(jnp.concatenate([RU[n][seg_len:] + U0[i][rows_of(s)], V[i][rows_of(s)]], axis=0),
                     jnp.concatenate([sub(bt, c, g)[rows_of(s)], sub(kt, c, g)[rows_of(s)]], axis=0))
              for n, (i, c, g, s) in enumerate(segs)]
        for n, (i, c, g, s) in enumerate(segs):
            S = S0[n] * sub(w_all, c, g)[s * seg_len:s * seg_len + 1, :] + _unbd(UB[n], hm_ref)
            for h in heads(g):
                sl_ref[c * nseg + s, h] = S[:, head_lanes(h)]
    y = ypre_ref[...]

    mu = _seg_sum(y, ones_bd) * (1.0 / N)
    d = y - mu
    var = _seg_sum(d * d, ones_bd) * (1.0 / N)
    yn = d * lax.rsqrt(var + GN_EPS) * row_of(lg_ref) + row_of(lb_ref)
    bonus = _seg_sum(r * kmod * row_of(rk_ref), ones_bd) * v
    y_ref[...] = (yn + bonus) * out_gate


def _rwkv_constants(seg_len, rows):
    n, gh, gl = RWKV_HEAD, RW_GROUP_HEADS, RW_GROUP_LANES
    lane_head = np.arange(gl) // n
    hm = (lane_head[None, None, :] == np.arange(gh)[:, None, None]) * np.ones((gh, n, gl))
    i = np.arange(RW_CHUNK)[:, None]
    j = (np.arange(gl) % n)[None, :]
    same = (i // seg_len) == (j // seg_len)
    cm = np.stack([(j < i) & same, (j <= i) & same, j == i]).astype(np.float32)
    tb = min(rows, 2 * LANES)
    ri = np.arange(tb)[:, None]
    rj = np.arange(tb)[None, :]
    same_r = (ri // seg_len) == (rj // seg_len)
    tri = ((rj <= ri) & same_r).astype(np.float32)
    ones_bd = np.kron(np.eye(gh), np.ones((n, n)))
    return (jnp.asarray(ones_bd, BF16), jnp.asarray(hm, BF16), jnp.asarray(cm, F32),
            jnp.asarray(tri, BF16))


def _rwkv(z, s0, p, layer, seq_len):
    n = z.shape[0]
    if seq_len >= RW_CHUNK:
        step_chunks = min(RW_STEP_CHUNKS_LONG, seq_len // RW_CHUNK)
    else:
        step_chunks = min(RW_STEP_CHUNKS_SHORT, n // RW_CHUNK)
    nseq = n // seq_len
    nh, hd = N_RWKV_HEADS, RWKV_HEAD
    seg_len = min(seq_len, RW_CHUNK)
    rows = RW_CHUNK * step_chunks
    chained = seq_len >= RW_CHUNK
    if chained:
        steps = seq_len // rows
        grid = (nseq, steps)
        row_map = lambda s, b: (s * steps + b, 0)
        state_spec = pl.BlockSpec((1, nh, hd, hd), lambda s, b: (s, 0, 0, 0))
        scratch = [pltpu.VMEM((rows, D_RWKV), F32), pltpu.VMEM((RW_GROUPS, hd, RW_GROUP_LANES), F32)]
    else:
        grid = (n // rows, 1)
        row_map = lambda s, b: (s, 0)
        state_spec = pl.BlockSpec((rows // seq_len, nh, hd, hd), lambda s, b: (s, 0, 0, 0))
        scratch = [pltpu.VMEM((rows, D_RWKV), F32)]
    if s0 is None:
        s0 = p['state_rwkv']
        s0_spec = pl.BlockSpec((None,) + state_spec.block_shape,
                               lambda s, b: (layer, s, 0, 0, 0))
    else:
        s0_spec = state_spec
    stacked = [p[k] for k in ('w0', 'w2', 'a0', 'a2', 'g2', 'k_k', 'k_a', 'r_k', 'lnx_g', 'lnx_b')]
    consts = list(_rwkv_constants(seg_len, rows))
    y, s_new = pl.pallas_call(
        functools.partial(_rwkv_kernel, seg_len=seg_len, step_chunks=step_chunks, layer=layer),
        grid=grid,
        in_specs=[pl.BlockSpec((rows, N_SHIFT), row_map), s0_spec]
        + [_layer_spec(t, layer) for t in stacked] + [_const_spec(t.shape) for t in consts],
        out_specs=[pl.BlockSpec((rows, D_RWKV), row_map), state_spec],
        out_shape=[jax.ShapeDtypeStruct((n, D_RWKV), F32),
                   jax.ShapeDtypeStruct((nseq, nh, hd, hd), F32)],
        scratch_shapes=scratch,
        compiler_params=pltpu.CompilerParams(dimension_semantics=("arbitrary", "arbitrary"),
                                             vmem_limit_bytes=V7X_VMEM_LIMIT_BYTES),
        name="rwkv7",
    )(z, s0, *stacked, *consts)
    return y, s_new


def _ffn_kernel(h_ref, y8_ref, u8_ref, yb_ref, pe_ref, c1_ref, c2_ref, d8_ref, wglu_ref, bglu_ref,
                wout_ref, g2_ref, wup_ref, cw_ref, cb_ref, wdn_ref, wple_ref, wpg_ref, gf_ref,
                *rest, seq_len, tm, final, layer):
    row_of = lambda ref: ref[layer:layer + 1, :]
    i = pl.program_id(0)
    long_seq = seq_len >= tm
    if long_seq:
        o_ref, ga_ref, gb_ref, fold_ref, act_ref, carry_ref = rest

        @pl.when(i == 0)
        def _():
            carry_ref[...] = jnp.zeros_like(carry_ref)
    else:
        o_ref, ga_ref, gb_ref, fold_ref, act_ref = rest
    ya = y8_ref[...] + row_of(d8_ref) * u8_ref[...]
    c_gelu = math.sqrt(2.0 / math.pi)
    ya = ya * (0.5 * (1.0 + jnp.tanh(c_gelu * (ya + 0.044715 * (ya * ya * ya)))))
    ya = _unfold_rows(ya, fold_ref)
    ya = ya * _sigmoid(_mm(ya, wglu_ref[...]) + row_of(bglu_ref))
    h1 = (h_ref[...] + _mm(ya, wout_ref[:D_SSM, :]) + _mm(yb_ref[...], wout_ref[D_SSM:, :]))
    x2 = _rmsnorm(h1, row_of(g2_ref)).astype(BF16)
    row = lax.broadcasted_iota(jnp.int32, (tm, 1), 0)
    if long_seq:
        first = (i % (seq_len // tm)) == 0
    else:
        t = row % seq_len
    def up(c):
        cs = slice(c * FF_CHUNK, (c + 1) * FF_CHUNK)
        gs = slice(D_FF + c * FF_CHUNK, D_FF + (c + 1) * FF_CHUNK)
        return (jnp.dot(x2, wup_ref[:, cs], preferred_element_type=F32),
                jnp.dot(x2, wup_ref[:, gs], preferred_element_type=F32))

    n_chunks = D_FF // FF_CHUNK
    nxt = up(0)
    for c in range(n_chunks):
        cs = slice(c * FF_CHUNK, (c + 1) * FF_CHUNK)
        val, gate = nxt
        if c + 1 < n_chunks:
            nxt = up(c + 1)
        r1 = pltpu.roll(gate, 1, 0)
        r2 = pltpu.roll(gate, 2, 0)
        if long_seq:
            m1 = jnp.where(first, c1_ref[0][:, cs], carry_ref[1:2, cs])
            m2 = jnp.where(first, c2_ref[0][:, cs], carry_ref[0:1, cs])
            p1 = jnp.where(row == 0, m1, r1)
            p2 = jnp.where(row == 0, m2, jnp.where(row == 1, m1, r2))
            carry_ref[:, cs] = gate[tm - 2:tm, :]
            ga_ref[0, :, cs] = gate[tm - 2:tm - 1, :]
            gb_ref[0, :, cs] = gate[tm - 1:tm, :]
        else:
            m1 = _expand_rows(c1_ref[:, cs], seq_len)
            m2 = _expand_rows(c2_ref[:, cs], seq_len)
            p1 = jnp.where(t == 0, m1, r1)
            p2 = jnp.where(t == 0, m2, jnp.where(t == 1, m1, r2))
            ga_ref[:, cs] = _step_rows(gate, seq_len, seq_len - 2)
            gb_ref[:, cs] = _step_rows(gate, seq_len, seq_len - 1)
        conv = (cb_ref[layer:layer + 1, cs] + cw_ref[2:3, cs] * gate + cw_ref[1:2, cs] * p1 + cw_ref[0:1, cs] * p2)
        act_ref[:, cs] = (conv * _sigmoid(conv) * val).astype(BF16)
    h2 = h1 + jnp.dot(act_ref[...], wdn_ref[...], preferred_element_type=F32)
    h3 = h2 + _mm(pe_ref[...], wple_ref[...]) * _sigmoid(_mm(h2, wpg_ref[...]))
    if final:
        h3 = _rmsnorm(h3, gf_ref[...])
    o_ref[...] = h3


def _ffn(h, y8, u8, yb, pe, conv0, p, layer, seq_len, tm, final):
    n = h.shape[0]
    nseq = n // seq_len
    row_spec = lambda w: pl.BlockSpec((tm, w), lambda i: (i, 0))
    pe_spec = pl.BlockSpec((None, tm, D_PLE), lambda i: (layer, i, 0))
    fold_spec = pl.BlockSpec((tm // S5_STEPS, S5_ROW), lambda i: (i, 0))
    scratch = [pltpu.VMEM((FOLD_LANE_BLOCKS, tm, LANES), F32), pltpu.VMEM((tm, D_FF), BF16)]
    if seq_len >= tm:
        tps = seq_len // tm
        seq_spec = pl.BlockSpec((1, 1, D_FF), lambda i: (i // tps, 0, 0))
        seq_shape = (nseq, 1, D_FF)
        scratch.append(pltpu.VMEM((2, D_FF), F32))
    else:
        seq_spec = pl.BlockSpec((tm // seq_len, D_FF), lambda i: (i, 0))
        seq_shape = (nseq, D_FF)
    stacked = [p[k] for k in ('d8', 'w_glu', 'b_glu', 'w_out', 'norm2_g', 'w_up', 'conv_w',
                              'conv_b', 'w_down', 'w_ple', 'w_pg')]
    consts = [p['final_g']]
    weights = stacked + consts
    out, ga, gb = pl.pallas_call(
        functools.partial(_ffn_kernel, seq_len=seq_len, tm=tm, final=final, layer=layer),
        grid=(n // tm,),
        in_specs=[row_spec(D_MODEL), fold_spec, fold_spec, row_spec(D_RWKV), pe_spec,
                  seq_spec, seq_spec] + [_layer_spec(t, layer) for t in stacked]
        + [_const_spec(t.shape) for t in consts],
        out_specs=[row_spec(D_MODEL), seq_spec, seq_spec],
        out_shape=[jax.ShapeDtypeStruct((n, D_MODEL), F32)]
        + [jax.ShapeDtypeStruct(seq_shape, F32)] * 2,
        scratch_shapes=scratch,
        compiler_params=pltpu.CompilerParams(dimension_semantics=("arbitrary",),
                                             vmem_limit_bytes=V7X_VMEM_LIMIT_BYTES),
        name="ffn",
    )(h, y8, u8, yb, pe, conv0[:, 1].reshape(seq_shape), conv0[:, 0].reshape(seq_shape), *weights)
    conv_new = jnp.stack([ga.reshape(nseq, D_FF), gb.reshape(nseq, D_FF)], axis=1)
    return out, conv_new


def _layer(h, pe, st, p, layer, s5_tables, seq_len, tm, final):
    ssm_re0, ssm_im0, rwkv0, shift0, conv0 = st
    tm_in, tm_ffn = tm
    u8, z, shift_new = _inproj(h, p, layer, shift0, seq_len, tm_in)
    y8, hre, him = _s5(u8, ssm_re0, ssm_im0, s5_tables, seq_len)
    yb, s_last = _rwkv(z, rwkv0, p, layer, seq_len)
    h, conv_new = _ffn(h, y8, u8, yb, pe, conv0, p, layer, seq_len, tm_ffn, final)
    return h, (hre, him, s_last, shift_new, conv_new)


def _stacked_params(w, state_rwkv, state_shift):
    depth = w['w_in'].shape[0]
    row = lambda t: t.reshape(depth, -1).astype(F32)
    bf = lambda t: t.astype(BF16)
    return {
        'norm1_g': row(w['norm1_g']), 'w_in': bf(w['w_in']), 'shift_mu': row(w['shift_mu']),
        'd8': jnp.tile(row(w['ssm_d']), (1, S5_STEPS)), 'w_glu': bf(w['ssm_w_glu']),
        'b_glu': row(w['ssm_b_glu']),
        'w0': row(w['rwkv_w0']), 'w2': bf(w['rwkv_w2']), 'a0': row(w['rwkv_a0']),
        'a2': bf(w['rwkv_a2']), 'g2': bf(w['rwkv_g2']), 'k_k': row(w['rwkv_k_k']),
        'k_a': row(w['rwkv_k_a']), 'r_k': row(w['rwkv_r_k']), 'lnx_g': row(w['rwkv_lnx_g']),
        'lnx_b': row(w['rwkv_lnx_b']), 'w_out': bf(w['w_out']), 'norm2_g': row(w['norm2_g']),
        'w_up': bf(w['w_up']), 'conv_w': w['conv_w'].astype(F32), 'conv_b': row(w['conv_b']),
        'w_down': bf(w['w_down']), 'w_ple': bf(w['w_ple']), 'w_pg': bf(w['w_pg']),
        'final_g': w['final_g'].reshape(1, -1).astype(F32),
        'state_rwkv': state_rwkv.astype(F32), 'state_shift': state_shift.astype(F32),
    }


def _forward(x_prompt, x_sample, p_prompt, p_sample, state_ssm_re, state_ssm_im, state_rwkv,
             state_shift, state_conv, w, tm_prompt, tm_sample):
    depth = w['w_in'].shape[0]
    bp, lp, _ = x_prompt.shape
    bs, ls, _ = x_sample.shape
    hp = x_prompt.reshape(bp * lp, D_MODEL).astype(F32)
    hs = x_sample.reshape(bs * ls, D_MODEL).astype(F32)
    pe_p = p_prompt.reshape(depth, bp * lp, D_PLE).astype(F32)
    pe_s = p_sample.reshape(depth, bs * ls, D_PLE).astype(F32)
    zero_st = (jnp.zeros((bp, N_SSM_GROUPS, SSM_STATE), F32),
               jnp.zeros((bp, N_SSM_GROUPS, SSM_STATE), F32),
               jnp.zeros((bp, N_RWKV_HEADS, RWKV_HEAD, RWKV_HEAD), F32),
               jnp.zeros((bp, N_SHIFT), F32),
               jnp.zeros((bp, 2, D_FF), F32))
    p = _stacked_params(w, state_rwkv, state_shift)
    s5_inputs = _s5_table_inputs(w['ssm_lam_re'], w['ssm_lam_im'], w['ssm_log_dt'], w['ssm_b_re'],
                                 w['ssm_b_im'], w['ssm_c_re'], w['ssm_c_im'])
    new_p = [[] for _ in range(5)]
    new_s = [[] for _ in range(5)]
    for i in range(depth):
        tables = _s5_tables(s5_inputs, i)
        final = i == depth - 1
        hp, stp = _layer(hp, pe_p, zero_st, p, i, tables, lp, tm_prompt, final)
        st_in = (state_ssm_re[i].astype(F32), state_ssm_im[i].astype(F32), None, None,
                 state_conv[i].astype(F32))
        hs, sts = _layer(hs, pe_s, st_in, p, i, tables, ls, tm_sample, final)
        for j in range(5):
            new_p[j].append(stp[j])
            new_s[j].append(sts[j])
    y_prompt = hp.reshape(bp, lp, D_MODEL).astype(x_prompt.dtype)
    y_sample = hs.reshape(bs, ls, D_MODEL).astype(x_sample.dtype)
    dts = (state_ssm_re.dtype, state_ssm_im.dtype, state_rwkv.dtype, state_shift.dtype,
           state_conv.dtype)
    outs_p = tuple(jnp.stack(new_p[j]).astype(dts[j]) for j in range(5))
    outs_s = tuple(jnp.stack(new_s[j]).astype(dts[j]) for j in range(5))
    return (y_prompt, y_sample) + outs_p + outs_s


def kernel(x_prompt, x_sample, p_prompt, p_sample, state_ssm_re, state_ssm_im, state_rwkv, state_shift, state_conv, norm1_g, w_in, shift_mu, ssm_lam_re, ssm_lam_im, ssm_log_dt, ssm_b_re, ssm_b_im, ssm_c_re, ssm_c_im, ssm_d, ssm_w_glu, ssm_b_glu, rwkv_w0, rwkv_w2, rwkv_a0, rwkv_a2, rwkv_g2, rwkv_k_k, rwkv_k_a, rwkv_r_k, rwkv_lnx_g, rwkv_lnx_b, w_out, norm2_g, w_up, conv_w, conv_b, w_down, w_ple, w_pg, final_g):
    w = dict(norm1_g=norm1_g, w_in=w_in, shift_mu=shift_mu, ssm_lam_re=ssm_lam_re,
             ssm_lam_im=ssm_lam_im, ssm_log_dt=ssm_log_dt, ssm_b_re=ssm_b_re, ssm_b_im=ssm_b_im,
             ssm_c_re=ssm_c_re, ssm_c_im=ssm_c_im, ssm_d=ssm_d, ssm_w_glu=ssm_w_glu,
             ssm_b_glu=ssm_b_glu, rwkv_w0=rwkv_w0, rwkv_w2=rwkv_w2, rwkv_a0=rwkv_a0,
             rwkv_a2=rwkv_a2, rwkv_g2=rwkv_g2, rwkv_k_k=rwkv_k_k, rwkv_k_a=rwkv_k_a,
             rwkv_r_k=rwkv_r_k, rwkv_lnx_g=rwkv_lnx_g, rwkv_lnx_b=rwkv_lnx_b, w_out=w_out,
             norm2_g=norm2_g, w_up=w_up, conv_w=conv_w, conv_b=conv_b, w_down=w_down,
             w_ple=w_ple, w_pg=w_pg, final_g=final_g)
    lp = x_prompt.shape[1]
    ns = x_sample.shape[0] * x_sample.shape[1]
    return _forward(x_prompt, x_sample, p_prompt, p_sample, state_ssm_re, state_ssm_im,
                    state_rwkv, state_shift, state_conv, w,
                    tm_prompt=(min(1024, lp), min(512, lp)),
                    tm_sample=(min(512, ns), min(512, ns)))
```

```python
import functools
import math

import numpy as np
import jax
import jax.numpy as jnp
from jax import lax
from jax.experimental import pallas as pl
from jax.experimental.pallas import tpu as pltpu

F32 = jnp.float32
BF16 = jnp.bfloat16

D_MODEL = 1024
D_SSM = 512
D_RWKV = 512
SSM_GROUP = 16
N_SSM_GROUPS = 32
SSM_STATE = 64
RWKV_HEAD = 64
N_RWKV_HEADS = 8
DECAY_LORA = 64
AAA_LORA = 64
GATE_LORA = 128
N_SHIFT = 3 * D_RWKV + DECAY_LORA + AAA_LORA + GATE_LORA
N_IN = D_SSM + N_SHIFT
D_FF = 2816
D_PLE = 256
RMS_EPS = 1e-6
GN_EPS = 64e-5
L2_EPS = 1e-12

LANES = 128
FOLD_LANE_BLOCKS = D_SSM // LANES
S5_STEPS = 8
S5_ROW = S5_STEPS * D_SSM
S5_LANE_BLOCKS = D_SSM // LANES
S5_BLOCK_GROUPS = N_SSM_GROUPS // S5_LANE_BLOCKS
S5_BLOCK_STATE = 2 * S5_BLOCK_GROUPS * SSM_STATE
S5_STATE_ROW = S5_LANE_BLOCKS * S5_BLOCK_STATE
FF_CHUNK = 2 * LANES
V7X_VMEM_BYTES = 64 * 1024 * 1024
V7X_VMEM_LIMIT_BYTES = V7X_VMEM_BYTES * 7 // 8


def _mm(a, b):
    return jnp.dot(a.astype(BF16), b.astype(BF16), preferred_element_type=F32)


def _mm_nt(a, b):
    return lax.dot_general(a.astype(BF16), b.astype(BF16), (((1,), (1,)), ((), ())),
                           preferred_element_type=F32)


def _mm_tn(a, b):
    return lax.dot_general(a.astype(BF16), b.astype(BF16), (((0,), (0,)), ((), ())),
                           preferred_element_type=F32)


def _split2(x):
    hi = x.astype(BF16)
    return hi, (x - hi.astype(F32)).astype(BF16)


def _seg_sum(x, ones_bd):
    xb = x.astype(BF16)
    gl = ones_bd.shape[0]
    return jnp.concatenate([jnp.dot(xb[:, g:g + gl], ones_bd, preferred_element_type=F32)
                            for g in range(0, x.shape[1], gl)], axis=1)


def _rmsnorm(x, g):
    return x * lax.rsqrt(jnp.mean(x * x, axis=-1, keepdims=True) + RMS_EPS) * g


def _sigmoid(x):
    return 1.0 / (1.0 + jnp.exp(-x))


def _const_spec(shape):
    nd = len(shape)
    return pl.BlockSpec(shape, lambda *_: (0,) * nd, pipeline_mode=pl.Buffered(1))


def _layer_spec(t, layer):
    if t.ndim == 2:
        return _const_spec(t.shape)
    nd = t.ndim - 1
    return pl.BlockSpec((None,) + t.shape[1:], lambda *_: (layer,) + (0,) * nd,
                        pipeline_mode=pl.Buffered(1))


def _expand_rows(x, seq_len):
    nseq, w = x.shape
    return jnp.broadcast_to(x[:, None, :], (nseq, seq_len, w)).reshape(nseq * seq_len, w)


def _step_rows(x, seq_len, t):
    return x.reshape(x.shape[0] // seq_len, seq_len, x.shape[1])[:, t, :]


def _fold_rows(x, scr_ref, out_ref):
    tm = x.shape[0]
    for q in range(FOLD_LANE_BLOCKS):
        scr_ref[q] = x[:, q * LANES:(q + 1) * LANES]
    for s in range(S5_STEPS):
        for q in range(FOLD_LANE_BLOCKS):
            c0 = s * D_SSM + q * LANES
            out_ref[:, c0:c0 + LANES] = scr_ref[q, pl.ds(s, tm // S5_STEPS, stride=S5_STEPS), :]


def _unfold_rows(x8, scr_ref):
    tm = x8.shape[0] * S5_STEPS
    for s in range(S5_STEPS):
        for q in range(FOLD_LANE_BLOCKS):
            c0 = s * D_SSM + q * LANES
            scr_ref[q, pl.ds(s, tm // S5_STEPS, stride=S5_STEPS), :] = x8[:, c0:c0 + LANES]
    return jnp.concatenate([scr_ref[q] for q in range(FOLD_LANE_BLOCKS)], axis=1)


def _inproj_kernel(h_ref, g_ref, w_ref, mu_ref, init_ref, *rest, seq_len, tm, layer):
    i = pl.program_id(0)
    row_of = lambda ref: ref[layer:layer + 1, :]
    long_seq = seq_len >= tm
    if long_seq:
        u8_ref, z_ref, last_ref, fold_ref, carry_ref = rest

        @pl.when(i == 0)
        def _():
            carry_ref[...] = jnp.zeros_like(carry_ref)
    else:
        u8_ref, z_ref, last_ref, fold_ref = rest
    xn = _rmsnorm(h_ref[...], row_of(g_ref))
    proj = jnp.dot(xn.astype(BF16), w_ref[...], preferred_element_type=F32)
    _fold_rows(proj[:, :D_SSM], fold_ref, u8_ref)
    zr = proj[:, D_SSM:]
    rolled = pltpu.roll(zr, 1, 0)
    row = lax.broadcasted_iota(jnp.int32, (tm, 1), 0)
    if long_seq:
        first = (i % (seq_len // tm)) == 0
        row0 = jnp.where(first, init_ref[0], carry_ref[...])
        prev = jnp.where(row == 0, row0, rolled)
        carry_ref[...] = zr[tm - 1:tm, :]
        last_ref[0] = zr[tm - 1:tm, :]
    else:
        prev = jnp.where(row % seq_len == 0, _expand_rows(init_ref[...], seq_len), rolled)
        last_ref[...] = _step_rows(zr, seq_len, seq_len - 1)
    z_ref[...] = (zr + (prev - zr) * row_of(mu_ref)).astype(z_ref.dtype)


def _inproj(h, p, layer, shift0, seq_len, tm):
    n = h.shape[0]
    nseq = n // seq_len
    fold_scratch = pltpu.VMEM((FOLD_LANE_BLOCKS, tm, LANES), F32)
    if seq_len >= tm:
        tps = seq_len // tm
        seq_spec = pl.BlockSpec((1, 1, N_SHIFT), lambda i: (i // tps, 0, 0))
        init_spec = seq_spec
        init = shift0.reshape(nseq, 1, N_SHIFT)
        last_shape = jax.ShapeDtypeStruct((nseq, 1, N_SHIFT), F32)
        scratch = [fold_scratch, pltpu.VMEM((1, N_SHIFT), F32)]
    else:
        spt = tm // seq_len
        seq_spec = pl.BlockSpec((spt, N_SHIFT), lambda i: (i, 0))
        init_spec = pl.BlockSpec((None, spt, N_SHIFT), lambda i: (layer, i, 0))
        init = p['state_shift']
        last_shape = jax.ShapeDtypeStruct((nseq, N_SHIFT), F32)
        scratch = [fold_scratch]
    g1, w_in, mu = p['norm1_g'], p['w_in'], p['shift_mu']
    u8, z, last = pl.pallas_call(
        functools.partial(_inproj_kernel, seq_len=seq_len, tm=tm, layer=layer),
        grid=(n // tm,),
        in_specs=[pl.BlockSpec((tm, D_MODEL), lambda i: (i, 0)),
                  _layer_spec(g1, layer), _layer_spec(w_in, layer), _layer_spec(mu, layer),
                  init_spec],
        out_specs=[pl.BlockSpec((tm // S5_STEPS, S5_ROW), lambda i: (i, 0)),
                   pl.BlockSpec((tm, N_SHIFT), lambda i: (i, 0)),
                   seq_spec],
        out_shape=[jax.ShapeDtypeStruct((n // S5_STEPS, S5_ROW), F32),
                   jax.ShapeDtypeStruct((n, N_SHIFT), BF16),
                   last_shape],
        scratch_shapes=scratch,
        compiler_params=pltpu.CompilerParams(dimension_semantics=("arbitrary",),
                                             vmem_limit_bytes=V7X_VMEM_LIMIT_BYTES),
        name="inproj",
    )(h, g1, w_in, mu, init)
    return u8, z, last.reshape(nseq, N_SHIFT)


def _s5_prep_kernel(lr_ref, li_ref, ldt_ref, br_ref, bi_ref, cr_ref, ci_ref, tile_ref, sel_ref,
                    wx_ref, wy_ref, apr_ref, api_ref):
    lr, li = lr_ref[...], li_ref[...]
    dt = jnp.exp(ldt_ref[...])
    mag = jnp.exp(lr * dt)
    ar = mag * jnp.cos(li * dt)
    ai = mag * jnp.sin(li * dt)
    den = lr * lr + li * li
    nr = ar - 1.0
    fr = (nr * lr + ai * li) / den
    fi = (ai * lr - nr * li) / den
    br, bi = br_ref[...], bi_ref[...]
    bbr = fr * br - fi * bi
    bbi = fr * bi + fi * br
    cr, ci = cr_ref[...], ci_ref[...]
    pr = [jnp.ones_like(ar)]
    pi = [jnp.zeros_like(ar)]
    for _ in range(S5_STEPS):
        pr.append(pr[-1] * ar - pi[-1] * ai)
        pi.append(pr[-2] * ai + pi[-1] * ar)
    nrow = lr.shape[0]
    blk = nrow // S5_LANE_BLOCKS
    half = S5_BLOCK_STATE // 2
    ri = lax.broadcasted_iota(jnp.int32, (nrow, nrow), 0)
    ci_ = lax.broadcasted_iota(jnp.int32, (nrow, nrow), 1)
    same_group = (ri // SSM_GROUP) == (ci_ // SSM_GROUP)
    keep_in = ((ri % blk) // SSM_GROUP) == (ci_ // SSM_STATE)
    keep_out = (ri // SSM_STATE) == ((ci_ % blk) // SSM_GROUP)
    tile, sel = tile_ref[...], sel_ref[...]
    nt = (((1,), (1,)), ((), ()))
    c_hi, c_lo = _split2(jnp.concatenate([cr, -ci], axis=1))
    wy_ref[...] = jnp.zeros(wy_ref.shape, wy_ref.dtype)
    for s in range(S5_STEPS):
        qr, qi = pr[S5_STEPS - 1 - s], pi[S5_STEPS - 1 - s]
        wr = qr * bbr - qi * bbi
        wi = qr * bbi + qi * bbr
        for h, w_ in enumerate((wr, wi)):
            t = jnp.dot(w_.astype(BF16), tile, preferred_element_type=F32)
            t = jnp.where(keep_in, t, 0.0).astype(BF16)
            for q in range(S5_LANE_BLOCKS):
                wx_ref[q, s * blk:(s + 1) * blk, h * half:(h + 1) * half] = t[q * blk:(q + 1) * blk, :]
        tau = S5_STEPS - 1 - s
        lhs_hi, lhs_lo = _split2(jnp.concatenate([wr, wi], axis=1))
        kt = (lax.dot_general(lhs_hi, c_hi, nt, preferred_element_type=F32)
              + lax.dot_general(lhs_hi, c_lo, nt, preferred_element_type=F32)
              + lax.dot_general(lhs_lo, c_hi, nt, preferred_element_type=F32))
        kt = jnp.where(same_group, kt, 0.0).astype(BF16)
        for q in range(S5_LANE_BLOCKS):
            kq = kt[q * blk:(q + 1) * blk, q * blk:(q + 1) * blk]
            for s_in in range(S5_STEPS - tau):
                s_out = s_in + tau
                r0 = S5_BLOCK_STATE + s_in * blk
                wy_ref[q, s_out // 2, r0:r0 + blk, (s_out % 2) * blk:(s_out % 2 + 1) * blk] = kq
        mr = cr * pr[s + 1] - ci * pi[s + 1]
        mi = cr * pi[s + 1] + ci * pr[s + 1]
        for h, m_ in enumerate((mr, -mi)):
            t = lax.dot_general(sel, m_.astype(BF16), nt, preferred_element_type=F32)
            t = jnp.where(keep_out, t, 0.0).astype(BF16)
            for q in range(S5_LANE_BLOCKS):
                wy_ref[q, s // 2, h * half:(h + 1) * half, (s % 2) * blk:(s % 2 + 1) * blk] = (
                    t[:, q * blk:(q + 1) * blk])
    a8r, a8i = pr[S5_STEPS], pi[S5_STEPS]
    er, ei = a8r, a8i
    for n in range(S5_STEPS):
        apr_ref[n] = er
        api_ref[n] = ei
        er, ei = er * a8r - ei * a8i, er * a8i + ei * a8r


def _s5_table_inputs(lam_re, lam_im, log_dt, b_re, b_im, c_re, c_im):
    depth = lam_re.shape[0]
    G, P, K = N_SSM_GROUPS, SSM_STATE, SSM_GROUP
    rep = lambda t: jnp.repeat(t, K, axis=1)
    rows = lambda t: t.reshape(depth, G * K, P)
    return (rep(lam_re), rep(lam_im), rep(jnp.broadcast_to(log_dt[:, :, None], (depth, G, P))),
            rows(jnp.swapaxes(b_re, 2, 3)), rows(jnp.swapaxes(b_im, 2, 3)), rows(c_re), rows(c_im))


def _s5_tables(stacked_inputs, layer):
    G, P, K = N_SSM_GROUPS, SSM_STATE, SSM_GROUP
    Q, GB = S5_LANE_BLOCKS, G // S5_LANE_BLOCKS
    tile = np.tile(np.eye(P, dtype=np.float32), (1, GB))
    consts = (jnp.asarray(tile, BF16), jnp.asarray(tile.T, BF16))
    t3 = jax.ShapeDtypeStruct((S5_STEPS, G * K, P), F32)
    out_shape = [jax.ShapeDtypeStruct((Q, S5_STEPS * GB * K, S5_BLOCK_STATE), BF16),
                 jax.ShapeDtypeStruct((Q, S5_STEPS // 2, S5_BLOCK_STATE + S5_STEPS * GB * K,
                                       2 * GB * K), BF16),
                 t3, t3]
    whole = lambda sds: pl.BlockSpec(sds.shape, lambda i: (0,) * len(sds.shape),
                                     pipeline_mode=pl.Buffered(1))
    wx, wy, apr, api = pl.pallas_call(
        _s5_prep_kernel,
        grid=(1,),
        in_specs=[_layer_spec(t, layer) for t in stacked_inputs]
        + [_const_spec(t.shape) for t in consts],
        out_specs=[whole(sds) for sds in out_shape],
        out_shape=out_shape,
        compiler_params=pltpu.CompilerParams(dimension_semantics=("arbitrary",),
                                             vmem_limit_bytes=V7X_VMEM_LIMIT_BYTES),
        name="s5_prep",
    )(*stacked_inputs, *consts)
    ap = jnp.concatenate([t[:, ::K, :].reshape(S5_STEPS, Q, GB * P) for t in (apr, api)], axis=-1)
    return wx, wy, ap.reshape(S5_STEPS, S5_STATE_ROW)


def _state_to_lanes(h_re, h_im):
    n = h_re.shape[0]
    parts = [t.reshape(n, S5_LANE_BLOCKS, -1) for t in (h_re, h_im)]
    return jnp.concatenate(parts, axis=-1).reshape(n, S5_STATE_ROW)


def _lanes_to_state(h):
    n = h.shape[0]
    h = h.reshape(n, S5_LANE_BLOCKS, 2, N_SSM_GROUPS // S5_LANE_BLOCKS, SSM_STATE)
    return (h[:, :, 0].reshape(n, N_SSM_GROUPS, SSM_STATE), h[:, :, 1].reshape(n, N_SSM_GROUPS, SSM_STATE))


def _s5_kernel(u_ref, h0_ref, wx_ref, wy_ref, ap_ref, y_ref, hl_ref, *scratch, rows, scan):
    half = S5_BLOCK_STATE // 2
    cw = D_SSM // S5_LANE_BLOCKS
    blocks = range(S5_LANE_BLOCKS)
    ub = u_ref[...].astype(BF16)
    ucat = [jnp.concatenate([ub[:, s * D_SSM + q * cw: s * D_SSM + (q + 1) * cw]
                             for s in range(S5_STEPS)], axis=1) for q in blocks]
    x = [jnp.dot(ucat[q], wx_ref[q], preferred_element_type=F32) for q in blocks]
    ap = [ap_ref[:, q * S5_BLOCK_STATE:(q + 1) * S5_BLOCK_STATE] for q in blocks]

    def emit_outputs(q, hprev):
        lhs = jnp.concatenate([hprev, ucat[q]], axis=1)
        for j in range(S5_STEPS // 2):
            kk = S5_BLOCK_STATE + cw * (2 * j + 2)
            y2 = jnp.dot(lhs[:, :kk], wy_ref[q, j, :kk, :], preferred_element_type=F32)
            c0 = (2 * j) * D_SSM + q * cw
            c1 = (2 * j + 1) * D_SSM + q * cw
            y_ref[:, c0:c0 + cw] = y2[:, :cw]
            y_ref[:, c1:c1 + cw] = y2[:, cw:]

    if scan:
        hs_ref, hp_ref = scratch
        sub3 = lax.broadcasted_iota(jnp.int32, (1, 8, 1), 1)
        sub2 = lax.broadcasted_iota(jnp.int32, (8, 1), 0)
        for q in blocks:
            xr = x[q][:, :half].reshape(rows // 8, 8, half)
            xi = x[q][:, half:].reshape(rows // 8, 8, half)
            for k in (1, 2, 4):
                er = jnp.where(sub3 >= k, ap[q][k - 1:k, :half].reshape(1, 1, half), 0.0)
                ei = jnp.where(sub3 >= k, ap[q][k - 1:k, half:].reshape(1, 1, half), 0.0)
                sr = pltpu.roll(xr, k, 1)
                si = pltpu.roll(xi, k, 1)
                xr, xi = xr + er * sr - ei * si, xi + er * si + ei * sr
            hs_ref[q, :, :half] = xr.reshape(rows, half)
            hs_ref[q, :, half:] = xi.reshape(rows, half)
        h0 = h0_ref[0]
        for q in blocks:
            lo = q * S5_BLOCK_STATE
            cr, ci = h0[:, lo:lo + half], h0[:, lo + half:lo + S5_BLOCK_STATE]
            pwr, pwi = ap[q][:, :half], ap[q][:, half:]
            for j in range(rows // 8):
                sl = slice(j * 8, (j + 1) * 8)
                hr = hs_ref[q, sl, :half] + pwr * cr - pwi * ci
                hi = hs_ref[q, sl, half:] + pwr * ci + pwi * cr
                hp_ref[q, sl, :half] = jnp.where(sub2 == 0, cr, pltpu.roll(hr, 1, 0))
                hp_ref[q, sl, half:] = jnp.where(sub2 == 0, ci, pltpu.roll(hi, 1, 0))
                cr, ci = hr[7:8, :], hi[7:8, :]
            hl_ref[0, :, lo:lo + half] = cr
            hl_ref[0, :, lo + half:lo + S5_BLOCK_STATE] = ci
            emit_outputs(q, hp_ref[q].astype(BF16))
    else:
        for q in blocks:
            lo = q * S5_BLOCK_STATE
            h0r = h0_ref[:, lo:lo + half]
            h0i = h0_ref[:, lo + half:lo + S5_BLOCK_STATE]
            er, ei = ap[q][0:1, :half], ap[q][0:1, half:]
            hl_ref[:, lo:lo + half] = er * h0r - ei * h0i + x[q][:, :half]
            hl_ref[:, lo + half:lo + S5_BLOCK_STATE] = er * h0i + ei * h0r + x[q][:, half:]
            emit_outputs(q, jnp.concatenate([h0r, h0i], axis=1).astype(BF16))


def _s5(u8, h_re, h_im, tables, seq_len):
    wx, wy, ap = tables
    n = u8.shape[0] * S5_STEPS
    nseq = n // seq_len
    h0 = _state_to_lanes(h_re, h_im)
    cps = seq_len // S5_STEPS
    scan = cps > 1
    if scan:
        rows = cps
        grid = (nseq,)
        h0 = h0.reshape(nseq, 1, S5_STATE_ROW)
        h_spec = pl.BlockSpec((1, 1, S5_STATE_ROW), lambda i: (i, 0, 0))
        h_shape = jax.ShapeDtypeStruct((nseq, 1, S5_STATE_ROW), F32)
    else:
        rows = min(nseq, 128)
        grid = (nseq // rows,)
        h_spec = pl.BlockSpec((rows, S5_STATE_ROW), lambda i: (i, 0))
        h_shape = jax.ShapeDtypeStruct((nseq, S5_STATE_ROW), F32)
    y8, hl = pl.pallas_call(
        functools.partial(_s5_kernel, rows=rows, scan=scan),
        grid=grid,
        in_specs=[pl.BlockSpec((rows, S5_ROW), lambda i: (i, 0)),
                  h_spec,
                  _const_spec(wx.shape), _const_spec(wy.shape), _const_spec(ap.shape)],
        out_specs=[pl.BlockSpec((rows, S5_ROW), lambda i: (i, 0)), h_spec],
        out_shape=[jax.ShapeDtypeStruct((n // S5_STEPS, S5_ROW), F32), h_shape],
        scratch_shapes=[pltpu.VMEM((S5_LANE_BLOCKS, rows, S5_BLOCK_STATE), F32)] * 2 if scan else [],
        compiler_params=pltpu.CompilerParams(dimension_semantics=("arbitrary",),
                                             vmem_limit_bytes=V7X_VMEM_LIMIT_BYTES),
        name="s5",
    )(u8, h0, wx, wy, ap)
    hre_new, him_new = _lanes_to_state(hl.reshape(nseq, S5_STATE_ROW))
    return y8, hre_new, him_new


RW_CHUNK = 64
RW_STEP_CHUNKS_LONG = 8
RW_STEP_CHUNKS_SHORT = 4
RW_GROUPS = 2
RW_GROUP_HEADS = N_RWKV_HEADS // RW_GROUPS
RW_GROUP_LANES = D_RWKV // RW_GROUPS


def _split3(x):
    hi = x.astype(BF16)
    r1 = x - hi.astype(F32)
    mid = r1.astype(BF16)
    lo = (r1 - mid.astype(F32)).astype(BF16)
    return hi, mid, lo


def _bd(x, hm_ref):
    xb = x.astype(BF16)
    return jnp.concatenate([xb * hm_ref[h] for h in range(RW_GROUP_HEADS)], axis=0)


def _unbd(f, hm_ref):
    n = RWKV_HEAD
    out = f[:n] * hm_ref[0].astype(F32)
    for h in range(1, RW_GROUP_HEADS):
        out = out + f[h * n:(h + 1) * n] * hm_ref[h].astype(F32)
    return out


def _rwkv_kernel(z_ref, s0_ref, w0_ref, w2_ref, a0_ref, a2_ref, g2_ref, kk_ref, ka_ref, rk_ref,
                 lg_ref, lb_ref, ones_ref, hm_ref, cm_ref, tri_ref, y_ref, sl_ref, ypre_ref,
                 *scratch, seg_len, step_chunks, layer):
    row_of = lambda ref: ref[layer:layer + 1, :]
    C = RW_CHUNK
    N = RWKV_HEAD
    GL = RW_GROUP_LANES
    chained = seg_len == C
    if chained:
        (st_ref,) = scratch
        b = pl.program_id(1)

        @pl.when(b == 0)
        def _():
            for g in range(RW_GROUPS):
                st_ref[g] = jnp.concatenate(
                    [s0_ref[0, h] for h in range(g * RW_GROUP_HEADS, (g + 1) * RW_GROUP_HEADS)], axis=1)

    ones_bd = ones_ref[...]
    z = z_ref[...].astype(F32)
    r = z[:, :D_RWKV]
    k = z[:, D_RWKV:2 * D_RWKV]
    v = z[:, 2 * D_RWKV:3 * D_RWKV]
    o = 3 * D_RWKV
    xw = z[:, o:o + DECAY_LORA]
    xa = z[:, o + DECAY_LORA:o + DECAY_LORA + AAA_LORA]
    xg = z[:, o + DECAY_LORA + AAA_LORA:]
    wd = -(row_of(w0_ref) + _mm(jnp.tanh(xw), w2_ref[...]))
    w = -(jnp.maximum(wd, 0.0) + jnp.log1p(jnp.exp(-jnp.abs(wd)))) - 0.5
    lw = -jnp.exp(w)
    a = _sigmoid(row_of(a0_ref) + _mm(xa, a2_ref[...]))
    out_gate = _mm(_sigmoid(xg), g2_ref[...])
    kk = k * row_of(kk_ref)
    kk = kk * lax.rsqrt(jnp.maximum(_seg_sum(kk * kk, ones_bd), L2_EPS * L2_EPS))
    kmod = k * (1.0 + (a - 1.0) * row_of(ka_ref))

    tri = tri_ref[...]
    tb = tri.shape[0]
    terms = _split3(lw)
    cw = jnp.concatenate([sum(jnp.dot(tri, t[i:i + tb], preferred_element_type=F32) for t in terms)
                          for i in range(0, lw.shape[0], tb)], axis=0)
    w_all = _expand_rows(jnp.exp(_step_rows(cw, seg_len, seg_len - 1)), seg_len)
    w_in = jnp.exp(cw)
    w_inv = jnp.exp(-cw)
    w_tail = w_all * w_inv
    ah = -kk * jnp.exp(cw - lw)
    bh = kk * a * w_inv
    kh = kmod * w_inv
    rh = r * w_in
    bt = kk * a * w_tail
    kt = kmod * w_tail
    strict = cm_ref[0] > 0.0
    incl = cm_ref[1] > 0.0
    eye_cat = cm_ref[2]
    n_fac = seg_len.bit_length() - 1

    items = [(c, g) for c in range(step_chunks) for g in range(RW_GROUPS)]
    sub = lambda t, c, g: t[c * C:(c + 1) * C, g * GL:(g + 1) * GL]
    bd = lambda t: _bd(t, hm_ref)
    A = [sub(ah, c, g) for c, g in items]
    R = [sub(rh, c, g) for c, g in items]
    V = [sub(v, c, g) for c, g in items]
    G = [_mm_nt(jnp.concatenate([A[i], R[i]], axis=0),
                jnp.concatenate([bd(sub(bh, c, g)), bd(sub(kh, c, g))], axis=0))
         for i, (c, g) in enumerate(items)]
    AB = [jnp.where(strict, t[:C, :GL], 0.0) for t in G]
    AK = [jnp.where(strict, t[:C, GL:], 0.0) for t in G]
    RB = [jnp.where(incl, t[C:, :GL], 0.0) for t in G]
    RK = [jnp.where(incl, t[C:, GL:], 0.0) for t in G]
    KV = [_mm(jnp.concatenate([AK[i], RK[i]], axis=0), bd(V[i])) for i in range(len(items))]
    T = [eye_cat + t for t in AB]
    P = [_mm(t, bd(t)) for t in AB]
    for _ in range(n_fac - 2):
        PT = [_mm(jnp.concatenate([P[i], T[i]], axis=0), bd(P[i])) for i in range(len(items))]
        P = [t[:C] for t in PT]
        T = [T[i] + PT[i][C:] for i in range(len(items))]
    T = [T[i] + _mm(T[i], bd(P[i])) for i in range(len(items))]
    RBT = [_mm(RB[i], bd(T[i])) for i in range(len(items))]
    X = [_mm(jnp.concatenate([T[i], RBT[i]], axis=0),
             jnp.concatenate([bd(A[i]), bd(KV[i][:C])], axis=1)) for i in range(len(items))]
    TA = [t[:C, :GL] for t in X]
    U0 = [t[:C, GL:] for t in X]
    Rt = [R[i] + X[i][C:, :GL] for i in range(len(items))]
    Y0 = [X[i][C:, GL:] + KV[i][C:] for i in range(len(items))]
    nseg = C // seg_len
    heads = lambda g: range(g * RW_GROUP_HEADS, (g + 1) * RW_GROUP_HEADS)
    head_lanes = lambda h: slice((h % RW_GROUP_HEADS) * N, (h % RW_GROUP_HEADS + 1) * N)
    if chained:
        PhiT, PsiT = [], []
        for i, (c, g) in enumerate(items):
            Bt, Kt, Wa = sub(bt, c, g), sub(kt, c, g), sub(w_all, c, g)
            PhiT.append(bd(eye_cat * Wa[0:1, :] + _unbd(_mm_tn(TA[i], Bt), hm_ref)))
            PsiT.append(_unbd(_mm_tn(jnp.concatenate([U0[i], V[i]], axis=0),
                                     jnp.concatenate([Bt, Kt], axis=0)), hm_ref))
        states = [st_ref[g] for g in range(RW_GROUPS)]
        for i, (c, g) in enumerate(items):
            S = states[g]
            ypre_ref[c * C:(c + 1) * C, g * GL:(g + 1) * GL] = _mm_nt(Rt[i], bd(S)) + Y0[i]
            states[g] = _mm(S, PhiT[i]) + PsiT[i]
        for g in range(RW_GROUPS):
            st_ref[g] = states[g]
            for h in heads(g):
                sl_ref[0, h] = states[g][:, head_lanes(h)]
    else:
        segs = [(i, c, g, s) for i, (c, g) in enumerate(items) for s in range(nseg)]
        rows_of = lambda s: slice(s * seg_len, (s + 1) * seg_len)
        S0 = [jnp.concatenate([s0_ref[c * nseg + s, h] for h in heads(g)], axis=1)
              for i, c, g, s in segs]
        RU = [_mm_nt(jnp.concatenate([Rt[i][rows_of(s)], TA[i][rows_of(s)]], axis=0), bd(S0[n]))
              for n, (i, c, g, s) in enumerate(segs)]
        for n, (i, c, g, s) in enumerate(segs):
            ypre_ref[c * C + s * seg_len:c * C + (s + 1) * seg_len, g * GL:(g + 1) * GL] = (
                RU[n][:seg_len] + Y0[i][rows_of(s)])
        UB = [_mm_tn(jnp.concatenate([RU[n][seg_len:] + U0[i][rows_of(s)], V[i][rows_of(s)]], axis=0),
                     jnp.concatenate([sub(bt, c, g)[rows_of(s)], sub(kt, c, g)[rows_of(s)]], axis=0))
              for n, (i, c, g, s) in enumerate(segs)]
        for n, (i, c, g, s) in enumerate(segs):
            S = S0[n] * sub(w_all, c, g)[s * seg_len:s * seg_len + 1, :] + _unbd(UB[n], hm_ref)
            for h in heads(g):
                sl_ref[c * nseg + s, h] = S[:, head_lanes(h)]
    y = ypre_ref[...]

    mu = _seg_sum(y, ones_bd) * (1.0 / N)
    d = y - mu
    var = _seg_sum(d * d, ones_bd) * (1.0 / N)
    yn = d * lax.rsqrt(var + GN_EPS) * row_of(lg_ref) + row_of(lb_ref)
    bonus = _seg_sum(r * kmod * row_of(rk_ref), ones_bd) * v
    y_ref[...] = ((yn + bonus) * out_gate).astype(y_ref.dtype)


def _rwkv_constants(seg_len, rows):
    n, gh, gl = RWKV_HEAD, RW_GROUP_HEADS, RW_GROUP_LANES
    lane_head = np.arange(gl) // n
    hm = (lane_head[None, None, :] == np.arange(gh)[:, None, None]) * np.ones((gh, n, gl))
    i = np.arange(RW_CHUNK)[:, None]
    j = (np.arange(gl) % n)[None, :]
    same = (i // seg_len) == (j // seg_len)
    cm = np.stack([(j < i) & same, (j <= i) & same, j == i]).astype(np.float32)
    tb = min(rows, 2 * LANES)
    ri = np.arange(tb)[:, None]
    rj = np.arange(tb)[None, :]
    same_r = (ri // seg_len) == (rj // seg_len)
    tri = ((rj <= ri) & same_r).astype(np.float32)
    ones_bd = np.kron(np.eye(gh), np.ones((n, n)))
    return (jnp.asarray(ones_bd, BF16), jnp.asarray(hm, BF16), jnp.asarray(cm, F32),
            jnp.asarray(tri, BF16))


def _rwkv(z, s0, p, layer, seq_len):
    n = z.shape[0]
    if seq_len >= RW_CHUNK:
        step_chunks = min(RW_STEP_CHUNKS_LONG, seq_len // RW_CHUNK)
    else:
        step_chunks = min(RW_STEP_CHUNKS_SHORT, n // RW_CHUNK)
    nseq = n // seq_len
    nh, hd = N_RWKV_HEADS, RWKV_HEAD
    seg_len = min(seq_len, RW_CHUNK)
    rows = RW_CHUNK * step_chunks
    chained = seq_len >= RW_CHUNK
    if chained:
        steps = seq_len // rows
        grid = (nseq, steps)
        row_map = lambda s, b: (s * steps + b, 0)
        state_spec = pl.BlockSpec((1, nh, hd, hd), lambda s, b: (s, 0, 0, 0))
        scratch = [pltpu.VMEM((rows, D_RWKV), F32), pltpu.VMEM((RW_GROUPS, hd, RW_GROUP_LANES), F32)]
    else:
        grid = (n // rows, 1)
        row_map = lambda s, b: (s, 0)
        state_spec = pl.BlockSpec((rows // seq_len, nh, hd, hd), lambda s, b: (s, 0, 0, 0))
        scratch = [pltpu.VMEM((rows, D_RWKV), F32)]
    if s0 is None:
        s0 = p['state_rwkv']
        s0_spec = pl.BlockSpec((None,) + state_spec.block_shape,
                               lambda s, b: (layer, s, 0, 0, 0))
    else:
        s0_spec = state_spec
    stacked = [p[k] for k in ('w0', 'w2', 'a0', 'a2', 'g2', 'k_k', 'k_a', 'r_k', 'lnx_g', 'lnx_b')]
    consts = list(_rwkv_constants(seg_len, rows))
    y, s_new = pl.pallas_call(
        functools.partial(_rwkv_kernel, seg_len=seg_len, step_chunks=step_chunks, layer=layer),
        grid=grid,
        in_specs=[pl.BlockSpec((rows, N_SHIFT), row_map), s0_spec]
        + [_layer_spec(t, layer) for t in stacked] + [_const_spec(t.shape) for t in consts],
        out_specs=[pl.BlockSpec((rows, D_RWKV), row_map), state_spec],
        out_shape=[jax.ShapeDtypeStruct((n, D_RWKV), BF16),
                   jax.ShapeDtypeStruct((nseq, nh, hd, hd), F32)],
        scratch_shapes=scratch,
        compiler_params=pltpu.CompilerParams(dimension_semantics=("arbitrary", "arbitrary"),
                                             vmem_limit_bytes=V7X_VMEM_LIMIT_BYTES),
        name="rwkv7",
    )(z, s0, *stacked, *consts)
    return y, s_new


def _ffn_kernel(h_ref, y8_ref, u8_ref, yb_ref, pe_ref, c1_ref, c2_ref, d8_ref, wglu_ref, bglu_ref,
                wout_ref, g2_ref, wup_ref, cw_ref, cb_ref, wdn_ref, wple_ref, wpg_ref, gf_ref,
                *rest, seq_len, tm, final, layer):
    row_of = lambda ref: ref[layer:layer + 1, :]
    i = pl.program_id(0)
    long_seq = seq_len >= tm
    if long_seq:
        o_ref, ga_ref, gb_ref, fold_ref, act_ref, carry_ref = rest

        @pl.when(i == 0)
        def _():
            carry_ref[...] = jnp.zeros_like(carry_ref)
    else:
        o_ref, ga_ref, gb_ref, fold_ref, act_ref = rest
    ya = y8_ref[...] + row_of(d8_ref) * u8_ref[...]
    c_gelu = math.sqrt(2.0 / math.pi)
    ya = ya * (0.5 * (1.0 + jnp.tanh(c_gelu * (ya + 0.044715 * (ya * ya * ya)))))
    ya = _unfold_rows(ya, fold_ref)
    ya = ya * _sigmoid(_mm(ya, wglu_ref[...]) + row_of(bglu_ref))
    h1 = (h_ref[...] + _mm(ya, wout_ref[:D_SSM, :]) + _mm(yb_ref[...], wout_ref[D_SSM:, :]))
    x2 = _rmsnorm(h1, row_of(g2_ref)).astype(BF16)
    row = lax.broadcasted_iota(jnp.int32, (tm, 1), 0)
    if long_seq:
        first = (i % (seq_len // tm)) == 0
    else:
        t = row % seq_len
    def up(c):
        cs = slice(c * FF_CHUNK, (c + 1) * FF_CHUNK)
        gs = slice(D_FF + c * FF_CHUNK, D_FF + (c + 1) * FF_CHUNK)
        return (jnp.dot(x2, wup_ref[:, cs], preferred_element_type=F32),
                jnp.dot(x2, wup_ref[:, gs], preferred_element_type=F32))

    n_chunks = D_FF // FF_CHUNK
    nxt = up(0)
    for c in range(n_chunks):
        cs = slice(c * FF_CHUNK, (c + 1) * FF_CHUNK)
        val, gate = nxt
        if c + 1 < n_chunks:
            nxt = up(c + 1)
        r1 = pltpu.roll(gate, 1, 0)
        r2 = pltpu.roll(gate, 2, 0)
        if long_seq:
            m1 = jnp.where(first, c1_ref[0][:, cs], carry_ref[1:2, cs])
            m2 = jnp.where(first, c2_ref[0][:, cs], carry_ref[0:1, cs])
            p1 = jnp.where(row == 0, m1, r1)
            p2 = jnp.where(row == 0, m2, jnp.where(row == 1, m1, r2))
            carry_ref[:, cs] = gate[tm - 2:tm, :]
            ga_ref[0, :, cs] = gate[tm - 2:tm - 1, :]
            gb_ref[0, :, cs] = gate[tm - 1:tm, :]
        else:
            m1 = _expand_rows(c1_ref[:, cs], seq_len)
            m2 = _expand_rows(c2_ref[:, cs], seq_len)
            p1 = jnp.where(t == 0, m1, r1)
            p2 = jnp.where(t == 0, m2, jnp.where(t == 1, m1, r2))
            ga_ref[:, cs] = _step_rows(gate, seq_len, seq_len - 2)
            gb_ref[:, cs] = _step_rows(gate, seq_len, seq_len - 1)
        conv = (cb_ref[layer:layer + 1, cs] + cw_ref[2:3, cs] * gate + cw_ref[1:2, cs] * p1 + cw_ref[0:1, cs] * p2)
        act_ref[:, cs] = (conv * _sigmoid(conv) * val).astype(BF16)
    h2 = h1 + jnp.dot(act_ref[...], wdn_ref[...], preferred_element_type=F32)
    h3 = h2 + _mm(pe_ref[...], wple_ref[...]) * _sigmoid(_mm(h2, wpg_ref[...]))
    if final:
        h3 = _rmsnorm(h3, gf_ref[...])
    o_ref[...] = h3


def _ffn(h, y8, u8, yb, pe, conv0, p, layer, seq_len, tm, final):
    n = h.shape[0]
    nseq = n // seq_len
    row_spec = lambda w: pl.BlockSpec((tm, w), lambda i: (i, 0))
    pe_spec = pl.BlockSpec((None, tm, D_PLE), lambda i: (layer, i, 0))
    fold_spec = pl.BlockSpec((tm // S5_STEPS, S5_ROW), lambda i: (i, 0))
    scratch = [pltpu.VMEM((FOLD_LANE_BLOCKS, tm, LANES), F32), pltpu.VMEM((tm, D_FF), BF16)]
    if seq_len >= tm:
        tps = seq_len // tm
        seq_spec = pl.BlockSpec((1, 1, D_FF), lambda i: (i // tps, 0, 0))
        seq_shape = (nseq, 1, D_FF)
        scratch.append(pltpu.VMEM((2, D_FF), F32))
    else:
        seq_spec = pl.BlockSpec((tm // seq_len, D_FF), lambda i: (i, 0))
        seq_shape = (nseq, D_FF)
    stacked = [p[k] for k in ('d8', 'w_glu', 'b_glu', 'w_out', 'norm2_g', 'w_up', 'conv_w',
                              'conv_b', 'w_down', 'w_ple', 'w_pg')]
    consts = [p['final_g']]
    weights = stacked + consts
    out, ga, gb = pl.pallas_call(
        functools.partial(_ffn_kernel, seq_len=seq_len, tm=tm, final=final, layer=layer),
        grid=(n // tm,),
        in_specs=[row_spec(D_MODEL), fold_spec, fold_spec, row_spec(D_RWKV), pe_spec,
                  seq_spec, seq_spec] + [_layer_spec(t, layer) for t in stacked]
        + [_const_spec(t.shape) for t in consts],
        out_specs=[row_spec(D_MODEL), seq_spec, seq_spec],
        out_shape=[jax.ShapeDtypeStruct((n, D_MODEL), F32)]
        + [jax.ShapeDtypeStruct(seq_shape, F32)] * 2,
        scratch_shapes=scratch,
        compiler_params=pltpu.CompilerParams(dimension_semantics=("arbitrary",),
                                             vmem_limit_bytes=V7X_VMEM_LIMIT_BYTES),
        name="ffn",
    )(h, y8, u8, yb, pe, conv0[:, 1].reshape(seq_shape), conv0[:, 0].reshape(seq_shape), *weights)
    conv_new = jnp.stack([ga.reshape(nseq, D_FF), gb.reshape(nseq, D_FF)], axis=1)
    return out, conv_new


def _layer(h, pe, st, p, layer, s5_tables, seq_len, tm, final):
    ssm_re0, ssm_im0, rwkv0, shift0, conv0 = st
    tm_in, tm_ffn = tm
    u8, z, shift_new = _inproj(h, p, layer, shift0, seq_len, tm_in)
    y8, hre, him = _s5(u8, ssm_re0, ssm_im0, s5_tables, seq_len)
    yb, s_last = _rwkv(z, rwkv0, p, layer, seq_len)
    h, conv_new = _ffn(h, y8, u8, yb, pe, conv0, p, layer, seq_len, tm_ffn, final)
    return h, (hre, him, s_last, shift_new, conv_new)


def _stacked_params(w, state_rwkv, state_shift):
    depth = w['w_in'].shape[0]
    row = lambda t: t.reshape(depth, -1).astype(F32)
    bf = lambda t: t.astype(BF16)
    return {
        'norm1_g': row(w['norm1_g']), 'w_in': bf(w['w_in']), 'shift_mu': row(w['shift_mu']),
        'd8': jnp.tile(row(w['ssm_d']), (1, S5_STEPS)), 'w_glu': bf(w['ssm_w_glu']),
        'b_glu': row(w['ssm_b_glu']),
        'w0': row(w['rwkv_w0']), 'w2': bf(w['rwkv_w2']), 'a0': row(w['rwkv_a0']),
        'a2': bf(w['rwkv_a2']), 'g2': bf(w['rwkv_g2']), 'k_k': row(w['rwkv_k_k']),
        'k_a': row(w['rwkv_k_a']), 'r_k': row(w['rwkv_r_k']), 'lnx_g': row(w['rwkv_lnx_g']),
        'lnx_b': row(w['rwkv_lnx_b']), 'w_out': bf(w['w_out']), 'norm2_g': row(w['norm2_g']),
        'w_up': bf(w['w_up']), 'conv_w': w['conv_w'].astype(F32), 'conv_b': row(w['conv_b']),
        'w_down': bf(w['w_down']), 'w_ple': bf(w['w_ple']), 'w_pg': bf(w['w_pg']),
        'final_g': w['final_g'].reshape(1, -1).astype(F32),
        'state_rwkv': state_rwkv.astype(F32), 'state_shift': state_shift.astype(F32),
    }


def _forward(x_prompt, x_sample, p_prompt, p_sample, state_ssm_re, state_ssm_im, state_rwkv,
             state_shift, state_conv, w, tm_prompt, tm_sample):
    depth = w['w_in'].shape[0]
    bp, lp, _ = x_prompt.shape
    bs, ls, _ = x_sample.shape
    hp = x_prompt.reshape(bp * lp, D_MODEL).astype(F32)
    hs = x_sample.reshape(bs * ls, D_MODEL).astype(F32)
    pe_p = p_prompt.reshape(depth, bp * lp, D_PLE).astype(F32)
    pe_s = p_sample.reshape(depth, bs * ls, D_PLE).astype(F32)
    zero_st = (jnp.zeros((bp, N_SSM_GROUPS, SSM_STATE), F32),
               jnp.zeros((bp, N_SSM_GROUPS, SSM_STATE), F32),
               jnp.zeros((bp, N_RWKV_HEADS, RWKV_HEAD, RWKV_HEAD), F32),
               jnp.zeros((bp, N_SHIFT), F32),
               jnp.zeros((bp, 2, D_FF), F32))
    p = _stacked_params(w, state_rwkv, state_shift)
    s5_inputs = _s5_table_inputs(w['ssm_lam_re'], w['ssm_lam_im'], w['ssm_log_dt'], w['ssm_b_re'],
                                 w['ssm_b_im'], w['ssm_c_re'], w['ssm_c_im'])
    new_p = [[] for _ in range(5)]
    new_s = [[] for _ in range(5)]
    for i in range(depth):
        tables = _s5_tables(s5_inputs, i)
        final = i == depth - 1
        hp, stp = _layer(hp, pe_p, zero_st, p, i, tables, lp, tm_prompt, final)
        st_in = (state_ssm_re[i].astype(F32), state_ssm_im[i].astype(F32), None, None,
                 state_conv[i].astype(F32))
        hs, sts = _layer(hs, pe_s, st_in, p, i, tables, ls, tm_sample, final)
        for j in range(5):
            new_p[j].append(stp[j])
            new_s[j].append(sts[j])
    y_prompt = hp.reshape(bp, lp, D_MODEL).astype(x_prompt.dtype)
    y_sample = hs.reshape(bs, ls, D_MODEL).astype(x_sample.dtype)
    dts = (state_ssm_re.dtype, state_ssm_im.dtype, state_rwkv.dtype, state_shift.dtype,
           state_conv.dtype)
    outs_p = tuple(jnp.stack(new_p[j]).astype(dts[j]) for j in range(5))
    outs_s = tuple(jnp.stack(new_s[j]).astype(dts[j]) for j in range(5))
    return (y_prompt, y_sample) + outs_p + outs_s


def kernel(x_prompt, x_sample, p_prompt, p_sample, state_ssm_re, state_ssm_im, state_rwkv, state_shift, state_conv, norm1_g, w_in, shift_mu, ssm_lam_re, ssm_lam_im, ssm_log_dt, ssm_b_re, ssm_b_im, ssm_c_re, ssm_c_im, ssm_d, ssm_w_glu, ssm_b_glu, rwkv_w0, rwkv_w2, rwkv_a0, rwkv_a2, rwkv_g2, rwkv_k_k, rwkv_k_a, rwkv_r_k, rwkv_lnx_g, rwkv_lnx_b, w_out, norm2_g, w_up, conv_w, conv_b, w_down, w_ple, w_pg, final_g):
    w = dict(norm1_g=norm1_g, w_in=w_in, shift_mu=shift_mu, ssm_lam_re=ssm_lam_re,
             ssm_lam_im=ssm_lam_im, ssm_log_dt=ssm_log_dt, ssm_b_re=ssm_b_re, ssm_b_im=ssm_b_im,
             ssm_c_re=ssm_c_re, ssm_c_im=ssm_c_im, ssm_d=ssm_d, ssm_w_glu=ssm_w_glu,
             ssm_b_glu=ssm_b_glu, rwkv_w0=rwkv_w0, rwkv_w2=rwkv_w2, rwkv_a0=rwkv_a0,
             rwkv_a2=rwkv_a2, rwkv_g2=rwkv_g2, rwkv_k_k=rwkv_k_k, rwkv_k_a=rwkv_k_a,
             rwkv_r_k=rwkv_r_k, rwkv_lnx_g=rwkv_lnx_g, rwkv_lnx_b=rwkv_lnx_b, w_out=w_out,
             norm2_g=norm2_g, w_up=w_up, conv_w=conv_w, conv_b=conv_b, w_down=w_down,
             w_ple=w_ple, w_pg=w_pg, final_g=final_g)
    lp = x_prompt.shape[1]
    ns = x_sample.shape[0] * x_sample.shape[1]
    return _forward(x_prompt, x_sample, p_prompt, p_sample, state_ssm_re, state_ssm_im,
                    state_rwkv, state_shift, state_conv, w,
                    tm_prompt=(min(1024, lp), min(512, lp)),
                    tm_sample=(min(512, ns), min(512, ns)))
```
